```python
import jax, jax.numpy as jnp
from jax import lax
import numpy as np

D_MODEL = 1024
BATCH = 4
SEQ = 4096
DEPTH = 2
DEC_BATCH = 128
DEC_SEQ = 4
PAST_LEN = 16384
PAGE_SIZE = 128

HEAD_DIM = 64
SWA_Q_HEADS = 8
SWA_KV_HEADS = 2
SWA_GROUP = SWA_Q_HEADS // SWA_KV_HEADS
WINDOW = 128
ROPE_THETA = 10000.0
GLA_HEADS = 4
GLA_DK = 64
GLA_DV = 128
GLA_LOWRANK = 16
GLA_GATE_TEMP = 16.0
GLA_CHUNK = 64
MEM_LEN = 256
XA_HEADS = 4
XA_HEAD_DIM = 128
XA_W = XA_HEADS * XA_HEAD_DIM
D_FF = 2816
EPS = 1e-6

SWA_Q_W = SWA_Q_HEADS * HEAD_DIM
SWA_KV_W = SWA_KV_HEADS * HEAD_DIM
GLA_K_W = GLA_HEADS * GLA_DK
GLA_V_W = GLA_HEADS * GLA_DV
MIX_W = SWA_Q_W + GLA_V_W
_IN_WIDTHS = (SWA_Q_W, SWA_KV_W, SWA_KV_W, GLA_K_W, GLA_K_W, GLA_V_W, GLA_V_W, GLA_LOWRANK)
IN_W = sum(_IN_WIDTHS)
IN_SPLITS = tuple(int(s) for s in np.cumsum(_IN_WIDTHS)[:-1])

kernel_name = "hymba_swa_gla_macaron_memxattn_step"


def rms_norm(x, g):
    xf = x.astype(jnp.float32)
    y = xf * lax.rsqrt(jnp.mean(xf * xf, axis=-1, keepdims=True) + EPS)
    return (y * g.astype(jnp.float32)).astype(x.dtype)


def rope(x, pos):
    half = x.shape[-1] // 2
    inv = ROPE_THETA ** (-jnp.arange(half, dtype=jnp.float32) / half)
    ang = pos.astype(jnp.float32)[:, None] * inv[None, :]
    cos = jnp.cos(ang)[None, :, None, :]
    sin = jnp.sin(ang)[None, :, None, :]
    xf = x.astype(jnp.float32)
    x1, x2 = xf[..., :half], xf[..., half:]
    return jnp.concatenate([x1 * cos - x2 * sin, x2 * cos + x1 * sin], axis=-1).astype(x.dtype)


def swiglu(x, wg, wu, wd):
    return (jax.nn.silu(x @ wg) * (x @ wu)) @ wd


def _swa_core(q, kc, vc, sinks, valid):
    s = jnp.einsum('bnqhgd,bnkhd->bnhgqk', q, kc, preferred_element_type=jnp.float32) * HEAD_DIM ** -0.5
    s = jnp.where(valid[None, :, None, None], s, -jnp.inf)
    sink = sinks.astype(jnp.float32).reshape(1, 1, SWA_KV_HEADS, SWA_GROUP, 1, 1)
    m = jnp.maximum(jnp.max(s, axis=-1, keepdims=True), sink)
    e = jnp.exp(s - m)
    pr = e / (jnp.sum(e, axis=-1, keepdims=True) + jnp.exp(sink - m))
    return jnp.einsum('bnhgqk,bnkhd->bnqhgd', pr.astype(vc.dtype), vc)


def swa_prompt(q, k, v, sinks):
    B, T, Hq, D = q.shape
    nb = T // WINDOW
    qb = q.reshape(B, nb, WINDOW, SWA_KV_HEADS, SWA_GROUP, D)

    def band(z):
        zb = z.reshape(B, nb, WINDOW, SWA_KV_HEADS, D)
        prev = jnp.concatenate([jnp.zeros_like(zb[:, :1]), zb[:, :-1]], axis=1)
        return jnp.concatenate([prev, zb], axis=2)

    i = jnp.arange(WINDOW)[:, None]
    j = jnp.arange(2 * WINDOW)[None, :]
    rel = WINDOW + i - j
    n = jnp.arange(nb)[:, None, None]
    valid = ((rel >= 0) & (rel < WINDOW))[None] & ((n > 0) | (j[None] >= WINDOW))
    o = _swa_core(qb, band(k), band(v), sinks, valid)
    return o.reshape(B, T, Hq * D)


def swa_decode(q, kcat, vcat, sinks):
    B, T, Hq, D = q.shape
    Tk = kcat.shape[1]
    i = jnp.arange(T)[:, None]
    j = jnp.arange(Tk)[None, :]
    rel = WINDOW + i - j
    valid = ((rel >= 0) & (rel < WINDOW))[None]
    o = _swa_core(q.reshape(B, 1, T, SWA_KV_HEADS, SWA_GROUP, D), kcat[:, None], vcat[:, None], sinks, valid)
    return o.reshape(B, T, Hq * D)


def gla_chunked(q, k, v, log_a, s0):
    B, T, H, DK = q.shape
    C = min(GLA_CHUNK, T)
    n = T // C

    def to_chunks(z):
        return z.reshape(B, n, C, H, z.shape[-1]).transpose(1, 0, 3, 2, 4)

    causal = jnp.tril(jnp.ones((C, C), dtype=bool))

    def step(S, inp):
        qc, kc, vc, gc = inp
        b = jnp.cumsum(gc, axis=-2)
        b_last = b[..., -1:, :]
        q_t = qc * jnp.exp(b)
        k_t = kc * jnp.exp(-b)
        A = jnp.where(causal, jnp.einsum('bhqd,bhkd->bhqk', q_t, k_t), 0.0)
        o = jnp.einsum('bhqd,bhdv->bhqv', q_t, S) + jnp.einsum('bhqk,bhkv->bhqv', A, vc)
        k_dec = kc * jnp.exp(b_last - b)
        S_new = jnp.exp(b_last)[..., 0, :, None] * S + jnp.einsum('bhkd,bhkv->bhdv', k_dec, vc)
        return S_new, o

    S, o = lax.scan(step, s0, (to_chunks(q), to_chunks(k), to_chunks(v), to_chunks(log_a)))
    o = o.transpose(1, 0, 3, 2, 4).reshape(B, T, H, v.shape[-1])
    return o, S


def hybrid_mix(h, pos, p, swa_cache, gla_state):
    B, T, _ = h.shape
    z = h @ p['w_in']
    q_s, k_s, v_s, q_g, k_g, v_g, g_g, lr = jnp.split(z, IN_SPLITS, axis=-1)
    q_s = rope(rms_norm(q_s.reshape(B, T, SWA_Q_HEADS, HEAD_DIM), p['swa_q_norm']), pos)
    k_s = rope(rms_norm(k_s.reshape(B, T, SWA_KV_HEADS, HEAD_DIM), p['swa_k_norm']), pos)
    v_s = v_s.reshape(B, T, SWA_KV_HEADS, HEAD_DIM)
    if swa_cache is None:
        a_out = swa_prompt(q_s, k_s, v_s, p['swa_sinks'])
        new_k, new_v = k_s[:, -WINDOW:], v_s[:, -WINDOW:]
    else:
        kcat = jnp.concatenate([swa_cache[0].astype(k_s.dtype), k_s], axis=1)
        vcat = jnp.concatenate([swa_cache[1].astype(v_s.dtype), v_s], axis=1)
        a_out = swa_decode(q_s, kcat, vcat, p['swa_sinks'])
        new_k, new_v = kcat[:, -WINDOW:], vcat[:, -WINDOW:]
    qf = q_g.reshape(B, T, GLA_HEADS, GLA_DK).astype(jnp.float32) * GLA_DK ** -0.5
    kf = k_g.reshape(B, T, GLA_HEADS, GLA_DK).astype(jnp.float32)
    vf = v_g.reshape(B, T, GLA_HEADS, GLA_DV).astype(jnp.float32)
    log_a = jax.nn.log_sigmoid((lr @ p['gla_w_gate'] + p['gla_b_gate']).astype(jnp.float32)) / GLA_GATE_TEMP
    log_a = log_a.reshape(B, T, GLA_HEADS, GLA_DK)
    if gla_state is None:
        s0 = jnp.zeros((B, GLA_HEADS, GLA_DK, GLA_DV), jnp.float32)
    else:
        s0 = gla_state.astype(jnp.float32)
    o, S = gla_chunked(qf, kf, vf, log_a, s0)
    o = rms_norm(o, p['gla_out_norm']).astype(h.dtype) * jax.nn.silu(g_g.reshape(B, T, GLA_HEADS, GLA_DV))
    mix = jnp.concatenate([a_out, o.reshape(B, T, GLA_V_W)], axis=-1) @ p['w_out']
    return mix, new_k, new_v, S.astype(h.dtype)


def mem_kv(mem, p):
    Bm, M, _ = mem.shape
    m = rms_norm(mem, p['mem_norm'])
    k = rms_norm((m @ p['xa_wk']).reshape(Bm, M, XA_HEADS, XA_HEAD_DIM), p['xa_k_norm'])
    v = (m @ p['xa_wv']).reshape(Bm, M, XA_HEADS, XA_HEAD_DIM)
    return k, v


def cross_attn(h, mk, mv, p):
    B, T, _ = h.shape
    q = rms_norm((h @ p['xa_wq']).reshape(B, T, XA_HEADS, XA_HEAD_DIM), p['xa_q_norm'])
    s = jnp.einsum('bqhd,bkhd->bhqk', q, mk.astype(q.dtype), preferred_element_type=jnp.float32) * XA_HEAD_DIM ** -0.5
    a = jax.nn.softmax(s, axis=-1).astype(h.dtype)
    o = jnp.einsum('bhqk,bkhd->bqhd', a, mv.astype(h.dtype)).reshape(B, T, XA_W)
    return o @ p['xa_wo']


def decoder_layer(x, pos, p, swa_cache, gla_state, mk, mv):
    x = x + 0.5 * swiglu(rms_norm(x, p['ffn1_norm']), p['ffn1_wg'], p['ffn1_wu'], p['ffn1_wd'])
    mix, nk, nv, S = hybrid_mix(rms_norm(x, p['mix_norm']), pos, p, swa_cache, gla_state)
    x = x + mix
    x = x + cross_attn(rms_norm(x, p['xa_norm']), mk, mv, p)
    x = x + 0.5 * swiglu(rms_norm(x, p['ffn2_norm']), p['ffn2_wg'], p['ffn2_wu'], p['ffn2_wd'])
    return x, nk, nv, S


def setup_inputs(seed: int = 0) -> dict:
    key = jax.random.key(seed)
    ks = iter(jax.random.split(key, 64))
    L = DEPTH

    def nrm(shape, scale=1.0):
        return jax.random.normal(next(ks), shape, jnp.float32) * scale

    def w(shape, fan_in):
        return nrm(shape, fan_in ** -0.5)

    def gain(shape):
        return 1.0 + nrm(shape, 0.02)

    return {
        'x_prompt': nrm((BATCH, SEQ, D_MODEL)),
        'x_sample': nrm((DEC_BATCH, DEC_SEQ, D_MODEL)),
        'cache_swa_k': nrm((L, DEC_BATCH, WINDOW, SWA_KV_HEADS, HEAD_DIM)),
        'cache_swa_v': nrm((L, DEC_BATCH, WINDOW, SWA_KV_HEADS, HEAD_DIM)),
        'state_gla': nrm((L, DEC_BATCH, GLA_HEADS, GLA_DK, GLA_DV)),
        'cache_mem_k': nrm((L, DEC_BATCH, MEM_LEN, XA_HEADS, XA_HEAD_DIM)),
        'cache_mem_v': nrm((L, DEC_BATCH, MEM_LEN, XA_HEADS, XA_HEAD_DIM)),
        'mem_prompt': nrm((BATCH, MEM_LEN, D_MODEL)),
        'ffn1_norm': gain((L, D_MODEL)),
        'ffn1_wg': w((L, D_MODEL, D_FF), D_MODEL),
        'ffn1_wu': w((L, D_MODEL, D_FF), D_MODEL),
        'ffn1_wd': w((L, D_FF, D_MODEL), D_FF),
        'mix_norm': gain((L, D_MODEL)),
        'w_in': w((L, D_MODEL, IN_W), D_MODEL),
        'swa_q_norm': gain((L, HEAD_DIM)),
        'swa_k_norm': gain((L, HEAD_DIM)),
        'swa_sinks': nrm((L, SWA_Q_HEADS), 0.5),
        'gla_w_gate': w((L, GLA_LOWRANK, GLA_K_W), GLA_LOWRANK),
        'gla_b_gate': nrm((L, GLA_K_W), 0.1),
        'gla_out_norm': gain((L, GLA_DV)),
        'w_out': w((L, MIX_W, D_MODEL), MIX_W),
        'xa_norm': gain((L, D_MODEL)),
        'mem_norm': gain((L, D_MODEL)),
        'xa_wq': w((L, D_MODEL, XA_W), D_MODEL),
        'xa_wk': w((L, D_MODEL, XA_W), D_MODEL),
        'xa_wv': w((L, D_MODEL, XA_W), D_MODEL),
        'xa_q_norm': gain((L, XA_HEAD_DIM)),
        'xa_k_norm': gain((L, XA_HEAD_DIM)),
        'xa_wo': w((L, XA_W, D_MODEL), XA_W),
        'ffn2_norm': gain((L, D_MODEL)),
        'ffn2_wg': w((L, D_MODEL, D_FF), D_MODEL),
        'ffn2_wu': w((L, D_MODEL, D_FF), D_MODEL),
        'ffn2_wd': w((L, D_FF, D_MODEL), D_FF),
    }


def reference(x_prompt, x_sample, cache_swa_k, cache_swa_v, state_gla, cache_mem_k, cache_mem_v,
              mem_prompt, ffn1_norm, ffn1_wg, ffn1_wu, ffn1_wd, mix_norm, w_in, swa_q_norm,
              swa_k_norm, swa_sinks, gla_w_gate, gla_b_gate, gla_out_norm, w_out, xa_norm, mem_norm,
              xa_wq, xa_wk, xa_wv, xa_q_norm, xa_k_norm, xa_wo, ffn2_norm, ffn2_wg, ffn2_wu, ffn2_wd):
    pos_p = jnp.arange(x_prompt.shape[1])
    pos_s = PAST_LEN + jnp.arange(x_sample.shape[1])
    yp, ys = x_prompt, x_sample
    kp_l, vp_l, sp_l, mkp_l, mvp_l, ks_l, vs_l, ss_l = [], [], [], [], [], [], [], []
    for l in range(DEPTH):
        p = {name: arr[l] for name, arr in (
            ('ffn1_norm', ffn1_norm), ('ffn1_wg', ffn1_wg), ('ffn1_wu', ffn1_wu), ('ffn1_wd', ffn1_wd),
            ('mix_norm', mix_norm), ('w_in', w_in), ('swa_q_norm', swa_q_norm), ('swa_k_norm', swa_k_norm),
            ('swa_sinks', swa_sinks), ('gla_w_gate', gla_w_gate), ('gla_b_gate', gla_b_gate),
            ('gla_out_norm', gla_out_norm), ('w_out', w_out), ('xa_norm', xa_norm), ('mem_norm', mem_norm),
            ('xa_wq', xa_wq), ('xa_wk', xa_wk), ('xa_wv', xa_wv), ('xa_q_norm', xa_q_norm),
            ('xa_k_norm', xa_k_norm), ('xa_wo', xa_wo), ('ffn2_norm', ffn2_norm), ('ffn2_wg', ffn2_wg),
            ('ffn2_wu', ffn2_wu), ('ffn2_wd', ffn2_wd))}
        mk_p, mv_p = mem_kv(mem_prompt, p)
        yp, kp, vp, sp = decoder_layer(yp, pos_p, p, None, None, mk_p, mv_p)
        ys, kss, vss, sss = decoder_layer(ys, pos_s, p, (cache_swa_k[l], cache_swa_v[l]), state_gla[l],
                                         cache_mem_k[l], cache_mem_v[l])
        kp_l.append(kp); vp_l.append(vp); sp_l.append(sp); mkp_l.append(mk_p); mvp_l.append(mv_p)
        ks_l.append(kss); vs_l.append(vss); ss_l.append(sss)
    return (yp, ys, jnp.stack(kp_l), jnp.stack(vp_l), jnp.stack(sp_l), jnp.stack(mkp_l), jnp.stack(mvp_l),
            jnp.stack(ks_l), jnp.stack(vs_l), jnp.stack(ss_l))
```

```python
import functools

import jax
import jax.numpy as jnp
from jax import lax
from jax.experimental import pallas as pl
from jax.experimental.pallas import tpu as pltpu

F32 = jnp.float32
BF16 = jnp.bfloat16

EPS = 1e-6
PAST_LEN = 16384
WINDOW = 128
ROPE_THETA = 10000.0
HEAD_DIM = 64
SWA_Q_HEADS = 8
SWA_KV_HEADS = 2
SWA_GROUP = SWA_Q_HEADS // SWA_KV_HEADS
GLA_HEADS = 4
GLA_DK = 64
GLA_DV = 128
GLA_LOWRANK = 16
GLA_GATE_TEMP = 16.0
GLA_CHUNK = 64
XA_HEADS = 4
XA_HEAD_DIM = 128

SWA_Q_W = SWA_Q_HEADS * HEAD_DIM
SWA_KV_W = SWA_KV_HEADS * HEAD_DIM
GLA_K_W = GLA_HEADS * GLA_DK
GLA_V_W = GLA_HEADS * GLA_DV
XA_W = XA_HEADS * XA_HEAD_DIM

LANES = 128
VMEM_LIMIT = 56 * 1024 * 1024

ROW_TILE = 512
FFN_CHUNK = 256
GLA_BLOCK = 256
DEC_GROUP = 4


def _dot(a, b):
    return jnp.dot(a, b, preferred_element_type=F32)


def _dot_tb(a, b):
    return lax.dot_general(a, b, (((1,), (1,)), ((), ())), preferred_element_type=F32)


def _dot_ta(a, b):
    return lax.dot_general(a, b, (((0,), (0,)), ((), ())), preferred_element_type=F32)


def _split_bf16(x):
    hi = x.astype(BF16)
    lo = (x - hi.astype(F32)).astype(BF16)
    return hi, lo


def _rms(x, g):
    ms = jnp.mean(x * x, axis=-1, keepdims=True)
    return x * lax.rsqrt(ms + EPS) * g


def _params(*sem):
    return pltpu.CompilerParams(dimension_semantics=sem, vmem_limit_bytes=VMEM_LIMIT)


def _resident(shape):
    nd = len(shape)
    return pl.BlockSpec(shape, lambda *_: (0,) * nd, pipeline_mode=pl.Buffered(1))


def _ffn_kernel(x_ref, g_ref, wg_ref, wu_ref, wd_ref, o_ref, a_scr):
    x = x_ref[...]
    h = _rms(x, g_ref[...]).astype(BF16)
    dff = wg_ref.shape[1]
    for c in range(dff // FFN_CHUNK):
        sl = slice(c * FFN_CHUNK, (c + 1) * FFN_CHUNK)
        g = _dot(h, wg_ref[:, sl])
        u = _dot(h, wu_ref[:, sl])
        a_scr[:, sl] = (g * jax.nn.sigmoid(g) * u).astype(BF16)
    o_ref[...] = x + 0.5 * _dot(a_scr[...], wd_ref[...])


def _ffn(x, g, wg, wu, wd):
    n, d = x.shape
    dff = wg.shape[1]
    tm = min(ROW_TILE, n)
    return pl.pallas_call(
        _ffn_kernel,
        grid=(n // tm,),
        in_specs=[
            pl.BlockSpec((tm, d), lambda i: (i, 0)),
            _resident((1, d)),
            _resident((d, dff)),
            _resident((d, dff)),
            _resident((dff, d)),
        ],
        out_specs=pl.BlockSpec((tm, d), lambda i: (i, 0)),
        out_shape=jax.ShapeDtypeStruct((n, d), F32),
        scratch_shapes=[pltpu.VMEM((tm, dff), BF16)],
        compiler_params=_params("parallel"),
        name="ffn",
    )(x, g, wg, wu, wd)


def _headnorm_rope(z, gain, bd, cos, sin, lane_lo):
    hi, lo = _split_bf16(z * z)
    ss = _dot(hi, bd) + _dot(lo, bd)
    y = z * lax.rsqrt(ss * (1.0 / HEAD_DIM) + EPS) * gain
    swapped = jnp.where(lane_lo, pltpu.roll(y, LANES - HEAD_DIM // 2, axis=1),
                        pltpu.roll(y, HEAD_DIM // 2, axis=1))
    return y * cos + swapped * sin


def _inproj_kernel(x_ref, g_ref, w_ref, wlr_ref, gw_ref, gb_ref, qn_ref, kn_ref, cos_ref, sin_ref,
                   bd_ref, qs_ref, ks_ref, vs_ref, qg_ref, kg_ref, vg_ref, gg_ref, la_ref):
    h = _rms(x_ref[...], g_ref[...]).astype(BF16)
    bd = bd_ref[...]
    cos = cos_ref[...]
    sin = sin_ref[...]
    lane = lax.broadcasted_iota(jnp.int32, cos.shape, 1)
    lane_lo = (lane % HEAD_DIM) < (HEAD_DIM // 2)
    qn = qn_ref[...]
    kn = kn_ref[...]
    o = 0
    for c in range(SWA_Q_W // LANES):
        z = _dot(h, w_ref[:, o + c * LANES:o + (c + 1) * LANES])
        y = _headnorm_rope(z, qn, bd, cos, sin, lane_lo) * (HEAD_DIM ** -0.5)
        qs_ref[:, c * LANES:(c + 1) * LANES] = y.astype(BF16)
    o += SWA_Q_W
    z = _dot(h, w_ref[:, o:o + SWA_KV_W])
    ks_ref[...] = _headnorm_rope(z, kn, bd, cos, sin, lane_lo)
    o += SWA_KV_W
    vs_ref[...] = _dot(h, w_ref[:, o:o + SWA_KV_W])
    o += SWA_KV_W
    qg_ref[...] = _dot(h, w_ref[:, o:o + GLA_K_W]) * (GLA_DK ** -0.5)
    o += GLA_K_W
    kg_ref[...] = _dot(h, w_ref[:, o:o + GLA_K_W])
    o += GLA_K_W
    vg_ref[...] = _dot(h, w_ref[:, o:o + GLA_V_W]).astype(BF16)
    o += GLA_V_W
    gg_ref[...] = _dot(h, w_ref[:, o:o + GLA_V_W])
    lr = _dot(h, wlr_ref[...]).astype(BF16)
    t = _dot(lr, gw_ref[...]) + gb_ref[...]
    log_sig = jnp.minimum(t, 0.0) - jnp.log(1.0 + jnp.exp(-jnp.abs(t)))
    la_ref[...] = log_sig * (1.0 / GLA_GATE_TEMP)


def _inproj(x, g, w_main, w_lr, gate_w, gate_b, qn, kn, cos, sin, bd):
    n, d = x.shape
    tm = min(ROW_TILE, n)
    pos_blocks = cos.shape[0] // tm
    row = lambda i: (i, 0)
    pos = lambda i: (i % pos_blocks, 0)
    widths = (SWA_Q_W, SWA_KV_W, SWA_KV_W, GLA_K_W, GLA_K_W, GLA_V_W, GLA_V_W, GLA_K_W)
    dtypes = (BF16, F32, F32, F32, F32, BF16, F32, F32)
    return pl.pallas_call(
        _inproj_kernel,
        grid=(n // tm,),
        in_specs=[
            pl.BlockSpec((tm, d), row),
            _resident((1, d)),
            _resident(w_main.shape),
            _resident(w_lr.shape),
            _resident(gate_w.shape),
            _resident(gate_b.shape),
            _resident(qn.shape),
            _resident(kn.shape),
            pl.BlockSpec((tm, LANES), pos),
            pl.BlockSpec((tm, LANES), pos),
            _resident(bd.shape),
        ],
        out_specs=[pl.BlockSpec((tm, w), row) for w in widths],
        out_shape=[jax.ShapeDtypeStruct((n, w), dt) for w, dt in zip(widths, dtypes)],
        compiler_params=_params("parallel"),
        name="inproj",
    )(x, g, w_main, w_lr, gate_w, gate_b, qn, kn, cos, sin, bd)


def _sink_softmax_pv(s, sink, v):
    m = jnp.maximum(jnp.max(s, axis=-1, keepdims=True), sink)
    e = jnp.exp(s - m)
    den = jnp.sum(e, axis=-1, keepdims=True) + jnp.exp(sink - m)
    return _dot(e.astype(BF16), v) / den


def _swa_prompt_kernel(sink_ref, q_ref, kp_ref, kc_ref, vp_ref, vc_ref, o_ref):
    j = pl.program_id(1)
    kcat = jnp.concatenate([kp_ref[...], kc_ref[...]], axis=0).astype(BF16)
    vcat = jnp.concatenate([vp_ref[...], vc_ref[...]], axis=0).astype(BF16)
    iq = lax.broadcasted_iota(jnp.int32, (WINDOW, 2 * WINDOW), 0)
    jk = lax.broadcasted_iota(jnp.int32, (WINDOW, 2 * WINDOW), 1)
    rel = WINDOW + iq - jk
    valid = (rel >= 0) & (rel < WINDOW) & ((j > 0) | (jk >= WINDOW))
    for hk in range(SWA_KV_HEADS):
        kh = kcat[:, hk * HEAD_DIM:(hk + 1) * HEAD_DIM]
        vh = vcat[:, hk * HEAD_DIM:(hk + 1) * HEAD_DIM]
        for g in range(SWA_GROUP):
            hq = hk * SWA_GROUP + g
            qh = q_ref[:, hq * HEAD_DIM:(hq + 1) * HEAD_DIM]
            s = jnp.where(valid, _dot_tb(qh, kh), -jnp.inf)
            o = _sink_softmax_pv(s, sink_ref[0, hq], vh)
            o_ref[:, hq * HEAD_DIM:(hq + 1) * HEAD_DIM] = o.astype(BF16)


def _swa_prompt(sinks, q, k, v, batch, seq):
    nb = seq // WINDOW
    cur = lambda b, j: (b * nb + j, 0)
    prev = lambda b, j: (b * nb + jnp.maximum(j - 1, 0), 0)
    return pl.pallas_call(
        _swa_prompt_kernel,
        grid=(batch, nb),
        in_specs=[
            pl.BlockSpec(memory_space=pltpu.SMEM),
            pl.BlockSpec((WINDOW, SWA_Q_W), cur),
            pl.BlockSpec((WINDOW, SWA_KV_W), prev),
            pl.BlockSpec((WINDOW, SWA_KV_W), cur),
            pl.BlockSpec((WINDOW, SWA_KV_W), prev),
            pl.BlockSpec((WINDOW, SWA_KV_W), cur),
        ],
        out_specs=pl.BlockSpec((WINDOW, SWA_Q_W), cur),
        out_shape=jax.ShapeDtypeStruct((batch * seq, SWA_Q_W), BF16),
        compiler_params=_params("parallel", "parallel"),
        name="swa_prompt",
    )(sinks, q, k, k, v, v)


def _swa_decode_kernel(sink_ref, q_ref, kn_ref, vn_ref, kc_ref, vc_ref, o_ref, ko_ref, vo_ref, *, steps):
    grp = kc_ref.shape[0]
    rows = grp * steps
    kn = kn_ref[...]
    vn = vn_ref[...]
    keep = WINDOW - steps
    for s in range(grp):
        ko_ref[s, 0:keep, :] = kc_ref[s, steps:WINDOW, :]
        ko_ref[s, keep:WINDOW, :] = kn[s * steps:(s + 1) * steps, :]
        vo_ref[s, 0:keep, :] = vc_ref[s, steps:WINDOW, :]
        vo_ref[s, keep:WINDOW, :] = vn[s * steps:(s + 1) * steps, :]
    kcat = jnp.concatenate([kc_ref[s] for s in range(grp)] + [kn], axis=0).astype(BF16)
    vcat = jnp.concatenate([vc_ref[s] for s in range(grp)] + [vn], axis=0).astype(BF16)
    nq = SWA_GROUP * rows
    nk = grp * WINDOW + rows
    r = lax.broadcasted_iota(jnp.int32, (nq, nk), 0) % rows
    q_seq, q_step = r // steps, r % steps
    c = lax.broadcasted_iota(jnp.int32, (nq, nk), 1)
    is_new = c >= grp * WINDOW
    cn = c - grp * WINDOW
    k_seq = jnp.where(is_new, cn // steps, c // WINDOW)
    k_idx = jnp.where(is_new, WINDOW + cn % steps, c % WINDOW)
    rel = WINDOW + q_step - k_idx
    valid = (q_seq == k_seq) & (rel >= 0) & (rel < WINDOW)
    for hk in range(SWA_KV_HEADS):
        kh = kcat[:, hk * HEAD_DIM:(hk + 1) * HEAD_DIM]
        vh = vcat[:, hk * HEAD_DIM:(hk + 1) * HEAD_DIM]
        qh = jnp.concatenate(
            [q_ref[:, (hk * SWA_GROUP + g) * HEAD_DIM:(hk * SWA_GROUP + g + 1) * HEAD_DIM]
             for g in range(SWA_GROUP)], axis=0)
        sink = jnp.concatenate(
            [jnp.full((rows, 1), sink_ref[0, hk * SWA_GROUP + g], F32) for g in range(SWA_GROUP)], axis=0)
        s = jnp.where(valid, _dot_tb(qh, kh), -jnp.inf)
        o = _sink_softmax_pv(s, sink, vh)
        for g in range(SWA_GROUP):
            hq = hk * SWA_GROUP + g
            o_ref[:, hq * HEAD_DIM:(hq + 1) * HEAD_DIM] = o[g * rows:(g + 1) * rows, :].astype(BF16)


def _swa_decode(sinks, q, kn, vn, kc, vc, steps):
    nseq = kc.shape[0]
    grp = DEC_GROUP
    rows = grp * steps
    row = lambda i: (i, 0)
    cache = lambda i: (i, 0, 0)
    return pl.pallas_call(
        functools.partial(_swa_decode_kernel, steps=steps),
        grid=(nseq // grp,),
        in_specs=[
            pl.BlockSpec(memory_space=pltpu.SMEM),
            pl.BlockSpec((rows, SWA_Q_W), row),
            pl.BlockSpec((rows, SWA_KV_W), row),
            pl.BlockSpec((rows, SWA_KV_W), row),
            pl.BlockSpec((grp, WINDOW, SWA_KV_W), cache),
            pl.BlockSpec((grp, WINDOW, SWA_KV_W), cache),
        ],
        out_specs=[
            pl.BlockSpec((rows, SWA_Q_W), row),
            pl.BlockSpec((grp, WINDOW, SWA_KV_W), cache),
            pl.BlockSpec((grp, WINDOW, SWA_KV_W), cache),
        ],
        out_shape=[
            jax.ShapeDtypeStruct((nseq * steps, SWA_Q_W), BF16),
            jax.ShapeDtypeStruct((nseq, WINDOW, SWA_KV_W), F32),
            jax.ShapeDtypeStruct((nseq, WINDOW, SWA_KV_W), F32),
        ],
        compiler_params=_params("parallel"),
        name="swa_decode",
    )(sinks, q, kn, vn, kc, vc)


def _gla_out(o, gain, gate):
    return _rms(o, gain) * (gate * jax.nn.sigmoid(gate))


def _gla_prompt_kernel(q_ref, k_ref, la_ref, v_ref, gg_ref, gn_ref, tril_ref, o_ref, s_ref, s_scr):
    tb = pl.program_id(1)

    @pl.when(tb == 0)
    def _():
        s_scr[...] = jnp.zeros_like(s_scr)

    c_len = tril_ref.shape[0]
    tril = tril_ref[...]
    ones = jnp.ones((c_len, GLA_DV), BF16)
    causal = (lax.broadcasted_iota(jnp.int32, (c_len, c_len), 0)
              >= lax.broadcasted_iota(jnp.int32, (c_len, c_len), 1))
    gain = gn_ref[...]
    for c in range(q_ref.shape[0] // c_len):
        rs = slice(c * c_len, (c + 1) * c_len)
        g_hi, g_lo = _split_bf16(la_ref[rs, :])
        b = _dot(tril, g_hi) + _dot(tril, g_lo)
        b_last = b[c_len - 1:c_len, :]
        k = k_ref[rs, :]
        q_t = (q_ref[rs, :] * jnp.exp(b)).astype(BF16)
        k_t = (k * jnp.exp(-b)).astype(BF16)
        k_dec = (k * jnp.exp(b_last - b)).astype(BF16)
        for h in range(GLA_HEADS):
            ks = slice(h * GLA_DK, (h + 1) * GLA_DK)
            vs = slice(h * GLA_DV, (h + 1) * GLA_DV)
            v = v_ref[rs, vs]
            state = s_scr[h]
            a = jnp.where(causal, _dot_tb(q_t[:, ks], k_t[:, ks]), 0.0)
            o = _dot(q_t[:, ks], state.astype(BF16)) + _dot(a.astype(BF16), v)
            o_ref[rs, vs] = _gla_out(o, gain, gg_ref[rs, vs]).astype(BF16)
            decay = jnp.exp(_dot_ta(g_hi[:, ks], ones) + _dot_ta(g_lo[:, ks], ones))
            s_scr[h] = decay * state + _dot_ta(k_dec[:, ks], v)

    @pl.when(tb == pl.num_programs(1) - 1)
    def _():
        s_ref[0] = s_scr[...]


def _gla_prompt(q, k, la, v, gg, gain, tril, batch, seq):
    tb = min(GLA_BLOCK, seq)
    nb = seq // tb
    row = lambda b, t: (b * nb + t, 0)
    return pl.pallas_call(
        _gla_prompt_kernel,
        grid=(batch, nb),
        in_specs=[
            pl.BlockSpec((tb, GLA_K_W), row),
            pl.BlockSpec((tb, GLA_K_W), row),
            pl.BlockSpec((tb, GLA_K_W), row),
            pl.BlockSpec((tb, GLA_V_W), row),
            pl.BlockSpec((tb, GLA_V_W), row),
            _resident(gain.shape),
            _resident(tril.shape),
        ],
        out_specs=[
            pl.BlockSpec((tb, GLA_V_W), row),
            pl.BlockSpec((1, GLA_HEADS, GLA_DK, GLA_DV), lambda b, t: (b, 0, 0, 0)),
        ],
        out_shape=[
            jax.ShapeDtypeStruct((batch * seq, GLA_V_W), BF16),
            jax.ShapeDtypeStruct((batch, GLA_HEADS, GLA_DK, GLA_DV), F32),
        ],
        scratch_shapes=[pltpu.VMEM((GLA_HEADS, GLA_DK, GLA_DV), F32)],
        compiler_params=_params("parallel", "arbitrary"),
        name="gla_prompt",
    )(q, k, la, v, gg, gain, tril)


def _gla_decode_kernel(q_ref, k_ref, la_ref, v_ref, gg_ref, gn_ref, tril_ref, s_ref, o_ref, so_ref, *, steps):
    grp = s_ref.shape[0]
    rows = grp * steps
    tril = tril_ref[...]
    ones = jnp.ones((steps, GLA_DV), BF16)
    g_hi, g_lo = _split_bf16(la_ref[...])
    b = _dot(tril, g_hi) + _dot(tril, g_lo)
    ri = lax.broadcasted_iota(jnp.int32, (rows, rows), 0)
    ci = lax.broadcasted_iota(jnp.int32, (rows, rows), 1)
    causal = (ri // steps == ci // steps) & (ri >= ci)
    last = (ri // steps == ci // steps) & (ci % steps == steps - 1)
    b_last = _dot(last.astype(BF16), _split_bf16(b)[0]) + _dot(last.astype(BF16), _split_bf16(b)[1])
    k = k_ref[...]
    q_t = (q_ref[...] * jnp.exp(b)).astype(BF16)
    k_t = (k * jnp.exp(-b)).astype(BF16)
    k_dec = (k * jnp.exp(b_last - b)).astype(BF16)
    gain = gn_ref[...]
    for h in range(GLA_HEADS):
        ks = slice(h * GLA_DK, (h + 1) * GLA_DK)
        vs = slice(h * GLA_DV, (h + 1) * GLA_DV)
        v = v_ref[:, vs]
        a = jnp.where(causal, _dot_tb(q_t[:, ks], k_t[:, ks]), 0.0)
        o_intra = _dot(a.astype(BF16), v)
        o_inter = []
        for s in range(grp):
            ss = slice(s * steps, (s + 1) * steps)
            state = s_ref[s, h]
            o_inter.append(_dot(q_t[ss, ks], state.astype(BF16)))
            decay = jnp.exp(_dot_ta(g_hi[ss, ks], ones) + _dot_ta(g_lo[ss, ks], ones))
            so_ref[s, h] = decay * state + _dot_ta(k_dec[ss, ks], v[ss, :])
        o = o_intra + jnp.concatenate(o_inter, axis=0)
        o_ref[:, vs] = _gla_out(o, gain, gg_ref[:, vs]).astype(BF16)


def _gla_decode(q, k, la, v, gg, gain, tril, state, steps):
    nseq = state.shape[0]
    grp = DEC_GROUP
    rows = grp * steps
    row = lambda i: (i, 0)
    st = lambda i: (i, 0, 0, 0)
    return pl.pallas_call(
        functools.partial(_gla_decode_kernel, steps=steps),
        grid=(nseq // grp,),
        in_specs=[
            pl.BlockSpec((rows, GLA_K_W), row),
            pl.BlockSpec((rows, GLA_K_W), row),
            pl.BlockSpec((rows, GLA_K_W), row),
            pl.BlockSpec((rows, GLA_V_W), row),
            pl.BlockSpec((rows, GLA_V_W), row),
            _resident(gain.shape),
            _resident(tril.shape),
            pl.BlockSpec((grp, GLA_HEADS, GLA_DK, GLA_DV), st),
        ],
        out_specs=[
            pl.BlockSpec((rows, GLA_V_W), row),
            pl.BlockSpec((grp, GLA_HEADS, GLA_DK, GLA_DV), st),
        ],
        out_shape=[
            jax.ShapeDtypeStruct((nseq * steps, GLA_V_W), BF16),
            jax.ShapeDtypeStruct(state.shape, F32),
        ],
        compiler_params=_params("parallel"),
        name="gla_decode",
    )(q, k, la, v, gg, gain, tril, state)


def _outproj_kernel(x_ref, a_ref, o_ref, wa_ref, wo_ref, g_ref, wq_ref, qn_ref, x1_ref, q_ref):
    x1 = x_ref[...] + _dot(a_ref[...], wa_ref[...]) + _dot(o_ref[...], wo_ref[...])
    x1_ref[...] = x1
    h = _rms(x1, g_ref[...]).astype(BF16)
    qn = qn_ref[...]
    for hd in range(XA_HEADS):
        sl = slice(hd * XA_HEAD_DIM, (hd + 1) * XA_HEAD_DIM)
        q_ref[:, sl] = _rms(_dot(h, wq_ref[:, sl]), qn).astype(BF16)


def _outproj(x, a, o, w_a, w_o, g, wq, qn):
    n, d = x.shape
    tm = min(ROW_TILE, n)
    row = lambda i: (i, 0)
    return pl.pallas_call(
        _outproj_kernel,
        grid=(n // tm,),
        in_specs=[
            pl.BlockSpec((tm, d), row),
            pl.BlockSpec((tm, SWA_Q_W), row),
            pl.BlockSpec((tm, GLA_V_W), row),
            _resident(w_a.shape),
            _resident(w_o.shape),
            _resident((1, d)),
            _resident(wq.shape),
            _resident(qn.shape),
        ],
        out_specs=[pl.BlockSpec((tm, d), row), pl.BlockSpec((tm, XA_W), row)],
        out_shape=[jax.ShapeDtypeStruct((n, d), F32), jax.ShapeDtypeStruct((n, XA_W), BF16)],
        compiler_params=_params("parallel"),
        name="outproj",
    )(x, a, o, w_a, w_o, g, wq, qn)


def _memkv_kernel(m_ref, g_ref, wk_ref, wv_ref, kn_ref, k_ref, v_ref):
    m = _rms(m_ref[...], g_ref[0]).astype(BF16)
    kn = kn_ref[0]
    for hd in range(XA_HEADS):
        sl = slice(hd * XA_HEAD_DIM, (hd + 1) * XA_HEAD_DIM)
        k_ref[0, :, sl] = _rms(_dot(m, wk_ref[0, :, sl]), kn)
    v_ref[0] = _dot(m, wv_ref[0])


def _memkv(mem, g, wk, wv, kn):
    depth = wk.shape[0]
    n, d = mem.shape
    tm = min(ROW_TILE, n)
    per_layer = lambda l, i: (l, 0, 0)
    out = lambda l, i: (l, i, 0)
    return pl.pallas_call(
        _memkv_kernel,
        grid=(depth, n // tm),
        in_specs=[
            pl.BlockSpec((tm, d), lambda l, i: (i, 0)),
            pl.BlockSpec((1, 1, d), per_layer),
            pl.BlockSpec((1, d, XA_W), per_layer),
            pl.BlockSpec((1, d, XA_W), per_layer),
            pl.BlockSpec((1, 1, XA_HEAD_DIM), per_layer),
        ],
        out_specs=[pl.BlockSpec((1, tm, XA_W), out), pl.BlockSpec((1, tm, XA_W), out)],
        out_shape=[jax.ShapeDtypeStruct((depth, n, XA_W), F32)] * 2,
        compiler_params=_params("parallel", "parallel"),
        name="memkv",
    )(mem, g, wk, wv, kn)


def _softmax_pv(s, v):
    m = jnp.max(s, axis=-1, keepdims=True)
    e = jnp.exp(s - m)
    return _dot(e.astype(BF16), v) / jnp.sum(e, axis=-1, keepdims=True)


def _xattn_prompt_kernel(x_ref, q_ref, mk_ref, mv_ref, wo_ref, o_ref, a_scr):
    mk = mk_ref[0, 0].astype(BF16)
    mv = mv_ref[0, 0].astype(BF16)
    for hd in range(XA_HEADS):
        sl = slice(hd * XA_HEAD_DIM, (hd + 1) * XA_HEAD_DIM)
        s = _dot_tb(q_ref[:, sl], mk[:, sl]) * (XA_HEAD_DIM ** -0.5)
        a_scr[:, sl] = _softmax_pv(s, mv[:, sl]).astype(BF16)
    o_ref[...] = x_ref[...] + _dot(a_scr[...], wo_ref[...])


def _xattn_prompt(x, q, mk, mv, layer, wo, seq):
    n, d = x.shape
    tm = min(ROW_TILE, seq)
    per_seq = seq // tm
    row = lambda i: (i, 0)
    mem = lambda i: (layer, i // per_seq, 0, 0)
    mem_len = mk.shape[2]
    return pl.pallas_call(
        _xattn_prompt_kernel,
        grid=(n // tm,),
        in_specs=[
            pl.BlockSpec((tm, d), row),
            pl.BlockSpec((tm, XA_W), row),
            pl.BlockSpec((1, 1, mem_len, XA_W), mem),
            pl.BlockSpec((1, 1, mem_len, XA_W), mem),
            _resident(wo.shape),
        ],
        out_specs=pl.BlockSpec((tm, d), row),
        out_shape=jax.ShapeDtypeStruct((n, d), F32),
        scratch_shapes=[pltpu.VMEM((tm, XA_W), BF16)],
        compiler_params=_params("parallel"),
        name="xattn_prompt",
    )(x, q, mk, mv, wo)


def _xattn_decode_kernel(q_ref, mk_ref, mv_ref, o_ref, *, steps):
    grp = mk_ref.shape[1]
    rows = grp * steps
    own = lax.broadcasted_iota(jnp.int32, (rows, 1), 0) // steps
    for hd in range(XA_HEADS):
        sl = slice(hd * XA_HEAD_DIM, (hd + 1) * XA_HEAD_DIM)
        q = q_ref[:, sl]
        s = None
        for j in range(grp):
            sj = _dot_tb(q, mk_ref[0, j, :, sl].astype(BF16))
            s = sj if s is None else jnp.where(own == j, sj, s)
        s = s * (XA_HEAD_DIM ** -0.5)
        m = jnp.max(s, axis=-1, keepdims=True)
        e = jnp.exp(s - m)
        p = e.astype(BF16)
        o = None
        for j in range(grp):
            oj = _dot(p, mv_ref[0, j, :, sl].astype(BF16))
            o = oj if o is None else jnp.where(own == j, oj, o)
        o_ref[:, sl] = (o / jnp.sum(e, axis=-1, keepdims=True)).astype(BF16)


def _xattn_decode(q, mk, mv, layer, steps):
    nseq, mem_len = mk.shape[1], mk.shape[2]
    grp = DEC_GROUP
    rows = grp * steps
    row = lambda i: (i, 0)
    mem = lambda i: (layer, i, 0, 0)
    return pl.pallas_call(
        functools.partial(_xattn_decode_kernel, steps=steps),
        grid=(nseq // grp,),
        in_specs=[
            pl.BlockSpec((rows, XA_W), row),
            pl.BlockSpec((1, grp, mem_len, XA_W), mem),
            pl.BlockSpec((1, grp, mem_len, XA_W), mem),
        ],
        out_specs=pl.BlockSpec((rows, XA_W), row),
        out_shape=jax.ShapeDtypeStruct((nseq * steps, XA_W), BF16),
        compiler_params=_params("parallel"),
        name="xattn_decode",
    )(q, mk, mv)


def _proj_res_kernel(x_ref, a_ref, w_ref, o_ref):
    o_ref[...] = x_ref[...] + _dot(a_ref[...], w_ref[...])


def _proj_res(x, a, w):
    n, d = x.shape
    tm = min(ROW_TILE, n)
    row = lambda i: (i, 0)
    return pl.pallas_call(
        _proj_res_kernel,
        grid=(n // tm,),
        in_specs=[pl.BlockSpec((tm, d), row), pl.BlockSpec((tm, a.shape[1]), row), _resident(w.shape)],
        out_specs=pl.BlockSpec((tm, d), row),
        out_shape=jax.ShapeDtypeStruct((n, d), F32),
        compiler_params=_params("parallel"),
        name="proj_res",
    )(x, a, w)


def _rope_tables(pos):
    half = HEAD_DIM // 2
    inv = ROPE_THETA ** (-jnp.arange(half, dtype=F32) / half)
    ang = pos.astype(F32)[:, None] * inv[None, :]
    cos, sin = jnp.cos(ang), jnp.sin(ang)
    reps = LANES // HEAD_DIM
    return (jnp.concatenate([cos, cos] * reps, axis=-1),
            jnp.concatenate([-sin, sin] * reps, axis=-1))


def _block_tril(n_blocks, size):
    i = jnp.arange(n_blocks * size)
    return ((i[:, None] // size == i[None, :] // size) & (i[:, None] >= i[None, :])).astype(BF16)


def kernel(x_prompt, x_sample, cache_swa_k, cache_swa_v, state_gla, cache_mem_k, cache_mem_v, mem_prompt, ffn1_norm, ffn1_wg, ffn1_wu, ffn1_wd, mix_norm, w_in, swa_q_norm, swa_k_norm, swa_sinks, gla_w_gate, gla_b_gate, gla_out_norm, w_out, xa_norm, mem_norm, xa_wq, xa_wk, xa_wv, xa_q_norm, xa_k_norm, xa_wo, ffn2_norm, ffn2_wg, ffn2_wu, ffn2_wd):
    batch, seq, d = x_prompt.shape
    nseq, steps, _ = x_sample.shape
    depth = w_in.shape[0]
    mem_len = mem_prompt.shape[1]
    main_w = SWA_Q_W + 2 * SWA_KV_W + 2 * GLA_K_W + 2 * GLA_V_W

    bf = lambda w: w.astype(BF16)
    ffn1 = (bf(ffn1_wg), bf(ffn1_wu), bf(ffn1_wd))
    ffn2 = (bf(ffn2_wg), bf(ffn2_wu), bf(ffn2_wd))
    w_main = bf(w_in[:, :, :main_w])
    w_lr = jnp.pad(bf(w_in[:, :, main_w:]), ((0, 0), (0, 0), (0, LANES - GLA_LOWRANK)))
    gate_w = jnp.pad(bf(gla_w_gate), ((0, 0), (0, LANES - GLA_LOWRANK), (0, 0)))
    w_out_b, wq_b, wk_b, wv_b, wo_b = bf(w_out), bf(xa_wq), bf(xa_wk), bf(xa_wv), bf(xa_wo)

    lane = jnp.arange(LANES)
    bd = (lane[:, None] // HEAD_DIM == lane[None, :] // HEAD_DIM).astype(BF16)
    cos_p, sin_p = _rope_tables(jnp.arange(seq))
    cos_s, sin_s = _rope_tables(PAST_LEN + jnp.arange(nseq * steps) % steps)
    tril_p = _block_tril(1, min(GLA_CHUNK, seq))
    tril_s = _block_tril(DEC_GROUP, steps)

    mk_p, mv_p = _memkv(mem_prompt.reshape(batch * mem_len, d), mem_norm[:, None, :], wk_b, wv_b,
                        xa_k_norm[:, None, :])
    mk_p = mk_p.reshape(depth, batch, mem_len, XA_W)
    mv_p = mv_p.reshape(depth, batch, mem_len, XA_W)
    mk_s = cache_mem_k.reshape(depth, nseq, mem_len, XA_W)
    mv_s = cache_mem_v.reshape(depth, nseq, mem_len, XA_W)

    xp = x_prompt.reshape(batch * seq, d)
    xs = x_sample.reshape(nseq * steps, d)
    kp_l, vp_l, sp_l, ks_l, vs_l, ss_l = [], [], [], [], [], []
    for l in range(depth):
        row = lambda a: a[l][None, :]
        qn = jnp.tile(swa_q_norm[l], LANES // HEAD_DIM)[None, :]
        kn = jnp.tile(swa_k_norm[l], LANES // HEAD_DIM)[None, :]
        sinks = swa_sinks[l][None, :]
        gla_gain = row(gla_out_norm)

        def mix_in(x, cos, sin):
            x = _ffn(x, row(ffn1_norm), ffn1[0][l], ffn1[1][l], ffn1[2][l])
            return (x,) + tuple(_inproj(x, row(mix_norm), w_main[l], w_lr[l], gate_w[l], row(gla_b_gate),
                                        qn, kn, cos, sin, bd))

        def mix_out(x, a, o, cross):
            x, q = _outproj(x, a, o, w_out_b[l, :SWA_Q_W], w_out_b[l, SWA_Q_W:], row(xa_norm), wq_b[l],
                            row(xa_q_norm))
            x = cross(x, q)
            return _ffn(x, row(ffn2_norm), ffn2[0][l], ffn2[1][l], ffn2[2][l])

        xp, q_s, k_s, v_s, q_g, k_g, v_g, g_g, la = mix_in(xp, cos_p, sin_p)
        a_p = _swa_prompt(sinks, q_s, k_s, v_s, batch, seq)
        o_p, s_p = _gla_prompt(q_g, k_g, la, v_g, g_g, gla_gain, tril_p, batch, seq)
        kp_l.append(k_s.reshape(batch, seq, SWA_KV_HEADS, HEAD_DIM)[:, seq - WINDOW:])
        vp_l.append(v_s.reshape(batch, seq, SWA_KV_HEADS, HEAD_DIM)[:, seq - WINDOW:])
        sp_l.append(s_p)
        xp = mix_out(xp, a_p, o_p, lambda x, q: _xattn_prompt(x, q, mk_p, mv_p, l, wo_b[l], seq))

        xs, q_s, k_s, v_s, q_g, k_g, v_g, g_g, la = mix_in(xs, cos_s, sin_s)
        a_s, k_new, v_new = _swa_decode(sinks, q_s, k_s, v_s,
                                        cache_swa_k[l].reshape(nseq, WINDOW, SWA_KV_W),
                                        cache_swa_v[l].reshape(nseq, WINDOW, SWA_KV_W), steps)
        o_s, s_s = _gla_decode(q_g, k_g, la, v_g, g_g, gla_gain, tril_s, state_gla[l], steps)
        ks_l.append(k_new.reshape(nseq, WINDOW, SWA_KV_HEADS, HEAD_DIM))
        vs_l.append(v_new.reshape(nseq, WINDOW, SWA_KV_HEADS, HEAD_DIM))
        ss_l.append(s_s)
        xs = mix_out(xs, a_s, o_s,
                     lambda x, q: _proj_res(x, _xattn_decode(q, mk_s, mv_s, l, steps), wo_b[l]))

    return (xp.reshape(batch, seq, d), xs.reshape(nseq, steps, d),
            jnp.stack(kp_l), jnp.stack(vp_l), jnp.stack(sp_l),
            mk_p.reshape(depth, batch, mem_len, XA_HEADS, XA_HEAD_DIM),
            mv_p.reshape(depth, batch, mem_len, XA_HEADS, XA_HEAD_DIM),
            jnp.stack(ks_l), jnp.stack(vs_l), jnp.stack(ss_l))
```

```python
import functools

import jax
import jax.numpy as jnp
from jax import lax
from jax.experimental import pallas as pl
from jax.experimental.pallas import tpu as pltpu

F32 = jnp.float32
BF16 = jnp.bfloat16

EPS = 1e-6
PAST_LEN = 16384
WINDOW = 128
ROPE_THETA = 10000.0
HEAD_DIM = 64
SWA_Q_HEADS = 8
SWA_KV_HEADS = 2
SWA_GROUP = SWA_Q_HEADS // SWA_KV_HEADS
GLA_HEADS = 4
GLA_DK = 64
GLA_DV = 128
GLA_LOWRANK = 16
GLA_GATE_TEMP = 16.0
GLA_CHUNK = 64
XA_HEADS = 4
XA_HEAD_DIM = 128

SWA_Q_W = SWA_Q_HEADS * HEAD_DIM
SWA_KV_W = SWA_KV_HEADS * HEAD_DIM
GLA_K_W = GLA_HEADS * GLA_DK
GLA_V_W = GLA_HEADS * GLA_DV
XA_W = XA_HEADS * XA_HEAD_DIM
MAIN_W = SWA_Q_W + 2 * SWA_KV_W + 2 * GLA_K_W + 2 * GLA_V_W

LANES = 128
MXU_TILE = 256
VMEM_LIMIT = 56 * 1024 * 1024

ROW_TILE = 512
FFN_CHUNK = 256
SWA_BLOCK = 256
GLA_BLOCK = 256
DEC_GROUP = 4

assert SWA_KV_HEADS * HEAD_DIM == LANES
SWA_HEAD_ORDER = tuple(kv * SWA_GROUP + g for g in range(SWA_GROUP) for kv in range(SWA_KV_HEADS))


def _dot(a, b):
    return jnp.dot(a, b, preferred_element_type=F32)


def _dot_tb(a, b):
    return lax.dot_general(a, b, (((1,), (1,)), ((), ())), preferred_element_type=F32)


def _dot_ta(a, b):
    return lax.dot_general(a, b, (((0,), (0,)), ((), ())), preferred_element_type=F32)


def _split_bf16(x):
    hi = x.astype(BF16)
    lo = (x - hi.astype(F32)).astype(BF16)
    return hi, lo


def _rms(x, g):
    ms = jnp.mean(x * x, axis=-1, keepdims=True)
    return x * lax.rsqrt(ms + EPS) * g


def _params(*sem):
    return pltpu.CompilerParams(dimension_semantics=sem, vmem_limit_bytes=VMEM_LIMIT)


def _resident(shape):
    nd = len(shape)
    return pl.BlockSpec(shape, lambda *_: (0,) * nd, pipeline_mode=pl.Buffered(1))


def _ffn_kernel(x_ref, g_ref, wg_ref, wu_ref, wd_ref, o_ref, a_scr):
    x = x_ref[...]
    h = _rms(x, g_ref[...]).astype(BF16)
    dff = wg_ref.shape[1]
    for c in range(dff // FFN_CHUNK):
        sl = slice(c * FFN_CHUNK, (c + 1) * FFN_CHUNK)
        g = _dot(h, wg_ref[:, sl])
        u = _dot(h, wu_ref[:, sl])
        a_scr[:, sl] = (g * jax.nn.sigmoid(g) * u).astype(BF16)
    o_ref[...] = x + 0.5 * _dot(a_scr[...], wd_ref[...])


def _ffn(x, g, wg, wu, wd):
    n, d = x.shape
    dff = wg.shape[1]
    tm = min(ROW_TILE, n)
    return pl.pallas_call(
        _ffn_kernel,
        grid=(n // tm,),
        in_specs=[
            pl.BlockSpec((tm, d), lambda i: (i, 0)),
            _resident((1, d)),
            _resident((d, dff)),
            _resident((d, dff)),
            _resident((dff, d)),
        ],
        out_specs=pl.BlockSpec((tm, d), lambda i: (i, 0)),
        out_shape=jax.ShapeDtypeStruct((n, d), F32),
        scratch_shapes=[pltpu.VMEM((tm, dff), BF16)],
        compiler_params=_params("parallel"),
        name="ffn",
    )(x, g, wg, wu, wd)


def _inproj_kernel(x_ref, g_ref, w_ref, gw_ref, gb_ref, qn_ref, kn_ref, cos_ref, sin_ref, bd_ref,
                   qs_ref, ks_ref, vs_ref, qg_ref, kg_ref, vg_ref, gg_ref, la_ref, z_scr):
    h = _rms(x_ref[...], g_ref[...]).astype(BF16)
    z_scr[...] = _dot(h, w_ref[...])
    bd = bd_ref[...]
    cos = cos_ref[...]
    sin = sin_ref[...]
    lane = lax.broadcasted_iota(jnp.int32, cos.shape, 1)
    lane_lo = (lane % HEAD_DIM) < (HEAD_DIM // 2)

    def head_scale(o):
        z = z_scr[:, o:o + MXU_TILE]
        ss = _dot((z * z).astype(BF16), bd)
        return lax.rsqrt(ss * (1.0 / HEAD_DIM) + EPS)

    def rope(y):
        swapped = jnp.where(lane_lo, pltpu.roll(y, LANES - HEAD_DIM // 2, axis=1),
                            pltpu.roll(y, HEAD_DIM // 2, axis=1))
        return y * cos + swapped * sin

    qn = qn_ref[...]
    for t in range(SWA_Q_W // MXU_TILE):
        inv = head_scale(t * MXU_TILE)
        for c in range(MXU_TILE // LANES):
            sl = slice(t * MXU_TILE + c * LANES, t * MXU_TILE + (c + 1) * LANES)
            y = z_scr[:, sl] * inv[:, c * LANES:(c + 1) * LANES] * qn
            qs_ref[:, sl] = rope(y).astype(BF16)
    o = SWA_Q_W
    inv = head_scale(o)
    ks_ref[...] = rope(z_scr[:, o:o + SWA_KV_W] * inv[:, :SWA_KV_W] * kn_ref[...])
    o += SWA_KV_W
    vs_ref[...] = z_scr[:, o:o + SWA_KV_W]
    o += SWA_KV_W
    qg_ref[...] = z_scr[:, o:o + GLA_K_W]
    o += GLA_K_W
    kg_ref[...] = z_scr[:, o:o + GLA_K_W]
    o += GLA_K_W
    vg_ref[...] = z_scr[:, o:o + GLA_V_W].astype(BF16)
    o += GLA_V_W
    gg_ref[...] = z_scr[:, o:o + GLA_V_W]
    o += GLA_V_W
    t = _dot(z_scr[:, o:o + LANES].astype(BF16), gw_ref[...]) + gb_ref[...]
    log_sig = jnp.minimum(t, 0.0) - jnp.log(1.0 + jnp.exp(-jnp.abs(t)))
    la_ref[...] = log_sig * (1.0 / GLA_GATE_TEMP)


def _inproj(x, g, w_all, gate_w, gate_b, qn, kn, cos, sin, bd):
    n, d = x.shape
    tm = min(ROW_TILE, n)
    pos_blocks = cos.shape[0] // tm
    row = lambda i: (i, 0)
    pos = lambda i: (i % pos_blocks, 0)
    widths = (SWA_Q_W, SWA_KV_W, SWA_KV_W, GLA_K_W, GLA_K_W, GLA_V_W, GLA_V_W, GLA_K_W)
    dtypes = (BF16, F32, F32, F32, F32, BF16, F32, F32)
    return pl.pallas_call(
        _inproj_kernel,
        grid=(n // tm,),
        in_specs=[
            pl.BlockSpec((tm, d), row),
            _resident((1, d)),
            _resident(w_all.shape),
            _resident(gate_w.shape),
            _resident(gate_b.shape),
            _resident(qn.shape),
            _resident(kn.shape),
            pl.BlockSpec((tm, LANES), pos),
            pl.BlockSpec((tm, LANES), pos),
            _resident(bd.shape),
        ],
        out_specs=[pl.BlockSpec((tm, w), row) for w in widths],
        out_shape=[jax.ShapeDtypeStruct((n, w), dt) for w, dt in zip(widths, dtypes)],
        scratch_shapes=[pltpu.VMEM((tm, w_all.shape[1]), F32)],
        compiler_params=_params("parallel"),
        name="inproj",
    )(x, g, w_all, gate_w, gate_b, qn, kn, cos, sin, bd)


def _swa_attend(q, k, v, valid, sinks):
    rows, keys = valid.shape
    low = lax.broadcasted_iota(jnp.int32, (1, LANES), 1) < HEAD_DIM
    zero = jnp.zeros((), q.dtype)
    pieces = []
    for c in range(SWA_Q_W // LANES):
        qc = q[:, c * LANES:(c + 1) * LANES]
        pieces += [jnp.where(low, qc, zero), jnp.where(low, zero, qc)]
    s = _dot_tb(jnp.concatenate(pieces, axis=0), k).reshape(SWA_Q_HEADS, rows, keys)
    s = jnp.where(valid[None], s, -jnp.inf)
    m = jnp.maximum(jnp.max(s, axis=-1, keepdims=True), sinks)
    e = jnp.exp(s - m)
    den = jnp.sum(e, axis=-1, keepdims=True) + jnp.exp(sinks - m)
    o = _dot(e.astype(BF16).reshape(SWA_Q_HEADS * rows, keys), v).reshape(SWA_Q_HEADS, rows, LANES) / den
    return [jnp.where(low, o[2 * c], o[2 * c + 1]) for c in range(SWA_Q_W // LANES)]


def _sink_column(sink_ref):
    idx = lax.broadcasted_iota(jnp.int32, (SWA_Q_HEADS, 1, 1), 0)
    col = jnp.zeros((SWA_Q_HEADS, 1, 1), F32)
    for p, head in enumerate(SWA_HEAD_ORDER):
        col = jnp.where(idx == p, sink_ref[0, head], col)
    return col


def _swa_prompt_kernel(sink_ref, q_ref, kp_ref, kc_ref, vp_ref, vc_ref, o_ref):
    j = pl.program_id(1)
    kall = jnp.concatenate([kp_ref[...], kc_ref[...]], axis=0).astype(BF16)
    vall = jnp.concatenate([vp_ref[...], vc_ref[...]], axis=0).astype(BF16)
    sinks = _sink_column(sink_ref)
    iq = lax.broadcasted_iota(jnp.int32, (WINDOW, 2 * WINDOW), 0)
    jk = lax.broadcasted_iota(jnp.int32, (WINDOW, 2 * WINDOW), 1)
    rel = WINDOW + iq - jk
    band = (rel >= 0) & (rel < WINDOW)
    for w in range(q_ref.shape[0] // WINDOW):
        rows = slice(w * WINDOW, (w + 1) * WINDOW)
        keys = slice(w * WINDOW, (w + 2) * WINDOW)
        valid = band if w > 0 else band & ((j > 0) | (jk >= WINDOW))
        cols = _swa_attend(q_ref[rows, :], kall[keys], vall[keys], valid, sinks)
        for c, col in enumerate(cols):
            o_ref[rows, c * LANES:(c + 1) * LANES] = col.astype(BF16)


def _swa_prompt(sinks, q, k, v, batch, seq):
    qb = min(SWA_BLOCK, seq)
    nb = seq // qb
    per = qb // WINDOW
    cur = lambda b, j: (b * nb + j, 0)
    prev = lambda b, j: ((b * nb + j) * per - jnp.minimum(j, 1), 0)
    return pl.pallas_call(
        _swa_prompt_kernel,
        grid=(batch, nb),
        in_specs=[
            pl.BlockSpec(memory_space=pltpu.SMEM),
            pl.BlockSpec((qb, SWA_Q_W), cur),
            pl.BlockSpec((WINDOW, SWA_KV_W), prev),
            pl.BlockSpec((qb, SWA_KV_W), cur),
            pl.BlockSpec((WINDOW, SWA_KV_W), prev),
            pl.BlockSpec((qb, SWA_KV_W), cur),
        ],
        out_specs=pl.BlockSpec((qb, SWA_Q_W), cur),
        out_shape=jax.ShapeDtypeStruct((batch * seq, SWA_Q_W), BF16),
        compiler_params=_params("parallel", "parallel"),
        name="swa_prompt",
    )(sinks, q, k, k, v, v)


def _swa_decode_kernel(sink_ref, q_ref, kn_ref, vn_ref, kc_ref, vc_ref, o_ref, ko_ref, vo_ref, *, steps):
    grp = kc_ref.shape[0]
    rows = grp * steps
    kn = kn_ref[...]
    vn = vn_ref[...]
    keep = WINDOW - steps
    for s in range(grp):
        ko_ref[s, 0:keep, :] = kc_ref[s, steps:WINDOW, :]
        ko_ref[s, keep:WINDOW, :] = kn[s * steps:(s + 1) * steps, :]
        vo_ref[s, 0:keep, :] = vc_ref[s, steps:WINDOW, :]
        vo_ref[s, keep:WINDOW, :] = vn[s * steps:(s + 1) * steps, :]
    kcat = jnp.concatenate([kc_ref[s] for s in range(grp)] + [kn], axis=0).astype(BF16)
    vcat = jnp.concatenate([vc_ref[s] for s in range(grp)] + [vn], axis=0).astype(BF16)
    nk = grp * WINDOW + rows
    r = lax.broadcasted_iota(jnp.int32, (rows, nk), 0)
    q_seq, q_step = r // steps, r % steps
    c = lax.broadcasted_iota(jnp.int32, (rows, nk), 1)
    is_new = c >= grp * WINDOW
    cn = c - grp * WINDOW
    k_seq = jnp.where(is_new, cn // steps, c // WINDOW)
    k_idx = jnp.where(is_new, WINDOW + cn % steps, c % WINDOW)
    rel = WINDOW + q_step - k_idx
    valid = (q_seq == k_seq) & (rel >= 0) & (rel < WINDOW)
    cols = _swa_attend(q_ref[...], kcat, vcat, valid, _sink_column(sink_ref))
    for c, col in enumerate(cols):
        o_ref[:, c * LANES:(c + 1) * LANES] = col.astype(BF16)


def _swa_decode(sinks, q, kn, vn, kc, vc, steps):
    nseq = kc.shape[0]
    grp = DEC_GROUP
    rows = grp * steps
    row = lambda i: (i, 0)
    cache = lambda i: (i, 0, 0)
    return pl.pallas_call(
        functools.partial(_swa_decode_kernel, steps=steps),
        grid=(nseq // grp,),
        in_specs=[
            pl.BlockSpec(memory_space=pltpu.SMEM),
            pl.BlockSpec((rows, SWA_Q_W), row),
            pl.BlockSpec((rows, SWA_KV_W), row),
            pl.BlockSpec((rows, SWA_KV_W), row),
            pl.BlockSpec((grp, WINDOW, SWA_KV_W), cache),
            pl.BlockSpec((grp, WINDOW, SWA_KV_W), cache),
        ],
        out_specs=[
            pl.BlockSpec((rows, SWA_Q_W), row),
            pl.BlockSpec((grp, WINDOW, SWA_KV_W), cache),
            pl.BlockSpec((grp, WINDOW, SWA_KV_W), cache),
        ],
        out_shape=[
            jax.ShapeDtypeStruct((nseq * steps, SWA_Q_W), BF16),
            jax.ShapeDtypeStruct((nseq, WINDOW, SWA_KV_W), F32),
            jax.ShapeDtypeStruct((nseq, WINDOW, SWA_KV_W), F32),
        ],
        compiler_params=_params("parallel"),
        name="swa_decode",
    )(sinks, q, kn, vn, kc, vc)


def _gla_out(o, gain, gate):
    return _rms(o, gain) * (gate * jax.nn.sigmoid(gate))


def _head_stack(x, width):
    return jnp.concatenate([x[:, h * width:(h + 1) * width] for h in range(GLA_HEADS)], axis=0)


def _head_masked_stack(x, head_of_lane):
    zero = jnp.zeros((), x.dtype)
    return jnp.concatenate([jnp.where(head_of_lane == h, x, zero) for h in range(GLA_HEADS)], axis=0)


def _gla_prompt_kernel(q_ref, k_ref, la_ref, v_ref, gg_ref, gn_ref, tril_ref, o_ref, s_ref, st_scr):
    tb = pl.program_id(1)

    @pl.when(tb == 0)
    def _():
        st_scr[...] = jnp.zeros_like(st_scr)

    c_len = min(GLA_CHUNK, q_ref.shape[0])
    tril = tril_ref[...]
    g_hi, g_lo = _split_bf16(la_ref[...])
    b_all = _dot(tril, g_hi) + _dot(tril, g_lo)
    head_of_lane = lax.broadcasted_iota(jnp.int32, (1, GLA_K_W), 1) // GLA_DK
    ri = lax.broadcasted_iota(jnp.int32, (GLA_HEADS * c_len, c_len), 0) % c_len
    ci = lax.broadcasted_iota(jnp.int32, (GLA_HEADS * c_len, c_len), 1)
    causal = ri >= ci
    gain = gn_ref[...]
    st = st_scr[...]
    for c in range(q_ref.shape[0] // c_len):
        rs = slice(c * c_len, (c + 1) * c_len)
        b = b_all[rs, :]
        dec = jnp.exp(b[c_len - 1:c_len, :])
        q_t = q_ref[rs, :] * jnp.exp(b)
        k_t = k_ref[rs, :] * jnp.exp(-b)
        qm = _head_masked_stack(q_t, head_of_lane).astype(BF16)
        kdm = _head_masked_stack(k_t * dec, head_of_lane).astype(BF16)
        a = jnp.where(causal, _dot_tb(qm, k_t.astype(BF16)), 0.0).astype(BF16)
        v = v_ref[rs, :]
        o = _dot_tb(qm, st.astype(BF16)) + jnp.concatenate(
            [_dot(a[h * c_len:(h + 1) * c_len, :], v[:, h * GLA_DV:(h + 1) * GLA_DV])
             for h in range(GLA_HEADS)], axis=0)
        y = _gla_out(o, gain, _head_stack(gg_ref[rs, :], GLA_DV)).astype(BF16)
        for h in range(GLA_HEADS):
            o_ref[rs, h * GLA_DV:(h + 1) * GLA_DV] = y[h * c_len:(h + 1) * c_len, :]
        st = dec * st + _dot_ta(_head_stack(v, GLA_DV), kdm)
    st_scr[...] = st

    @pl.when(tb == pl.num_programs(1) - 1)
    def _():
        s_ref[0] = st.T


def _gla_prompt(q, k, la, v, gg, gain, tril, batch, seq):
    tb = tril.shape[0]
    nb = seq // tb
    row = lambda b, t: (b * nb + t, 0)
    return pl.pallas_call(
        _gla_prompt_kernel,
        grid=(batch, nb),
        in_specs=[
            pl.BlockSpec((tb, GLA_K_W), row),
            pl.BlockSpec((tb, GLA_K_W), row),
            pl.BlockSpec((tb, GLA_K_W), row),
            pl.BlockSpec((tb, GLA_V_W), row),
            pl.BlockSpec((tb, GLA_V_W), row),
            _resident(gain.shape),
            _resident(tril.shape),
        ],
        out_specs=[
            pl.BlockSpec((tb, GLA_V_W), row),
            pl.BlockSpec((1, GLA_K_W, GLA_DV), lambda b, t: (b, 0, 0)),
        ],
        out_shape=[
            jax.ShapeDtypeStruct((batch * seq, GLA_V_W), BF16),
            jax.ShapeDtypeStruct((batch, GLA_K_W, GLA_DV), F32),
        ],
        scratch_shapes=[pltpu.VMEM((GLA_DV, GLA_K_W), F32)],
        compiler_params=_params("parallel", "arbitrary"),
        name="gla_prompt",
    )(q, k, la, v, gg, gain, tril)


def _gla_decode_kernel(q_ref, k_ref, la_ref, v_ref, gg_ref, gn_ref, tril_ref, s_ref, o_ref, so_ref, *, steps):
    grp = s_ref.shape[0]
    rows = grp * steps
    tril = tril_ref[...]
    ones = jnp.ones((steps, GLA_DV), BF16)
    g_hi, g_lo = _split_bf16(la_ref[...])
    b = _dot(tril, g_hi) + _dot(tril, g_lo)
    ri = lax.broadcasted_iota(jnp.int32, (rows, rows), 0)
    ci = lax.broadcasted_iota(jnp.int32, (rows, rows), 1)
    causal = (ri // steps == ci // steps) & (ri >= ci)
    last = ((ri // steps == ci // steps) & (ci % steps == steps - 1)).astype(BF16)
    b_hi, b_lo = _split_bf16(b)
    b_last = _dot(last, b_hi) + _dot(last, b_lo)
    k = k_ref[...]
    q_t = (q_ref[...] * jnp.exp(b)).astype(BF16)
    k_t = (k * jnp.exp(-b)).astype(BF16)
    k_dec = (k * jnp.exp(b_last - b)).astype(BF16)
    gain = gn_ref[...]
    for h in range(GLA_HEADS):
        ks = slice(h * GLA_DK, (h + 1) * GLA_DK)
        vs = slice(h * GLA_DV, (h + 1) * GLA_DV)
        v = v_ref[:, vs]
        a = jnp.where(causal, _dot_tb(q_t[:, ks], k_t[:, ks]), 0.0)
        o_intra = _dot(a.astype(BF16), v)
        o_inter = []
        for s in range(grp):
            ss = slice(s * steps, (s + 1) * steps)
            state = s_ref[s, h]
            o_inter.append(_dot(q_t[ss, ks], state.astype(BF16)))
            decay = jnp.exp(_dot_ta(g_hi[ss, ks], ones) + _dot_ta(g_lo[ss, ks], ones))
            so_ref[s, h] = decay * state + _dot_ta(k_dec[ss, ks], v[ss, :])
        o = o_intra + jnp.concatenate(o_inter, axis=0)
        o_ref[:, vs] = _gla_out(o, gain, gg_ref[:, vs]).astype(BF16)


def _gla_decode(q, k, la, v, gg, gain, tril, state, steps):
    nseq = state.shape[0]
    grp = DEC_GROUP
    rows = grp * steps
    row = lambda i: (i, 0)
    st = lambda i: (i, 0, 0, 0)
    return pl.pallas_call(
        functools.partial(_gla_decode_kernel, steps=steps),
        grid=(nseq // grp,),
        in_specs=[
            pl.BlockSpec((rows, GLA_K_W), row),
            pl.BlockSpec((rows, GLA_K_W), row),
            pl.BlockSpec((rows, GLA_K_W), row),
            pl.BlockSpec((rows, GLA_V_W), row),
            pl.BlockSpec((rows, GLA_V_W), row),
            _resident(gain.shape),
            _resident(tril.shape),
            pl.BlockSpec((grp, GLA_HEADS, GLA_DK, GLA_DV), st),
        ],
        out_specs=[
            pl.BlockSpec((rows, GLA_V_W), row),
            pl.BlockSpec((grp, GLA_HEADS, GLA_DK, GLA_DV), st),
        ],
        out_shape=[
            jax.ShapeDtypeStruct((nseq * steps, GLA_V_W), BF16),
            jax.ShapeDtypeStruct(state.shape, F32),
        ],
        compiler_params=_params("parallel"),
        name="gla_decode",
    )(q, k, la, v, gg, gain, tril, state)


def _outproj_kernel(x_ref, a_ref, o_ref, wa_ref, wo_ref, g_ref, wq_ref, qn_ref, x1_ref, q_ref):
    x1 = x_ref[...] + _dot(a_ref[...], wa_ref[...]) + _dot(o_ref[...], wo_ref[...])
    x1_ref[...] = x1
    h = _rms(x1, g_ref[...]).astype(BF16)
    qn = qn_ref[...]
    for hd in range(XA_HEADS):
        sl = slice(hd * XA_HEAD_DIM, (hd + 1) * XA_HEAD_DIM)
        q_ref[:, sl] = _rms(_dot(h, wq_ref[:, sl]), qn).astype(BF16)


def _outproj(x, a, o, w_a, w_o, g, wq, qn):
    n, d = x.shape
    tm = min(ROW_TILE, n)
    row = lambda i: (i, 0)
    return pl.pallas_call(
        _outproj_kernel,
        grid=(n // tm,),
        in_specs=[
            pl.BlockSpec((tm, d), row),
            pl.BlockSpec((tm, SWA_Q_W), row),
            pl.BlockSpec((tm, GLA_V_W), row),
            _resident(w_a.shape),
            _resident(w_o.shape),
            _resident((1, d)),
            _resident(wq.shape),
            _resident(qn.shape),
        ],
        out_specs=[pl.BlockSpec((tm, d), row), pl.BlockSpec((tm, XA_W), row)],
        out_shape=[jax.ShapeDtypeStruct((n, d), F32), jax.ShapeDtypeStruct((n, XA_W), BF16)],
        compiler_params=_params("parallel"),
        name="outproj",
    )(x, a, o, w_a, w_o, g, wq, qn)


def _memkv_kernel(m_ref, g_ref, wk_ref, wv_ref, kn_ref, k_ref, v_ref):
    m = _rms(m_ref[...], g_ref[0]).astype(BF16)
    kn = kn_ref[0]
    for hd in range(XA_HEADS):
        sl = slice(hd * XA_HEAD_DIM, (hd + 1) * XA_HEAD_DIM)
        k_ref[0, :, sl] = _rms(_dot(m, wk_ref[0, :, sl]), kn)
    v_ref[0] = _dot(m, wv_ref[0])


def _memkv(mem, g, wk, wv, kn):
    depth = wk.shape[0]
    n, d = mem.shape
    tm = min(ROW_TILE, n)
    per_layer = lambda l, i: (l, 0, 0)
    out = lambda l, i: (l, i, 0)
    return pl.pallas_call(
        _memkv_kernel,
        grid=(depth, n // tm),
        in_specs=[
            pl.BlockSpec((tm, d), lambda l, i: (i, 0)),
            pl.BlockSpec((1, 1, d), per_layer),
            pl.BlockSpec((1, d, XA_W), per_layer),
            pl.BlockSpec((1, d, XA_W), per_layer),
            pl.BlockSpec((1, 1, XA_HEAD_DIM), per_layer),
        ],
        out_specs=[pl.BlockSpec((1, tm, XA_W), out), pl.BlockSpec((1, tm, XA_W), out)],
        out_shape=[jax.ShapeDtypeStruct((depth, n, XA_W), F32)] * 2,
        compiler_params=_params("parallel", "parallel"),
        name="memkv",
    )(mem, g, wk, wv, kn)


def _softmax_pv(s, v):
    m = jnp.max(s, axis=-1, keepdims=True)
    e = jnp.exp(s - m)
    return _dot(e.astype(BF16), v) / jnp.sum(e, axis=-1, keepdims=True)


def _xattn_prompt_kernel(x_ref, q_ref, mk_ref, mv_ref, wo_ref, o_ref, a_scr):
    mk = mk_ref[0, 0].astype(BF16)
    mv = mv_ref[0, 0].astype(BF16)
    for hd in range(XA_HEADS):
        sl = slice(hd * XA_HEAD_DIM, (hd + 1) * XA_HEAD_DIM)
        s = _dot_tb(q_ref[:, sl], mk[:, sl]) * (XA_HEAD_DIM ** -0.5)
        a_scr[:, sl] = _softmax_pv(s, mv[:, sl]).astype(BF16)
    o_ref[...] = x_ref[...] + _dot(a_scr[...], wo_ref[...])


def _xattn_prompt(x, q, mk, mv, layer, wo, seq):
    n, d = x.shape
    tm = min(ROW_TILE, seq)
    per_seq = seq // tm
    row = lambda i: (i, 0)
    mem = lambda i: (layer, i // per_seq, 0, 0)
    mem_len = mk.shape[2]
    return pl.pallas_call(
        _xattn_prompt_kernel,
        grid=(n // tm,),
        in_specs=[
            pl.BlockSpec((tm, d), row),
            pl.BlockSpec((tm, XA_W), row),
            pl.BlockSpec((1, 1, mem_len, XA_W), mem),
            pl.BlockSpec((1, 1, mem_len, XA_W), mem),
            _resident(wo.shape),
        ],
        out_specs=pl.BlockSpec((tm, d), row),
        out_shape=jax.ShapeDtypeStruct((n, d), F32),
        scratch_shapes=[pltpu.VMEM((tm, XA_W), BF16)],
        compiler_params=_params("parallel"),
        name="xattn_prompt",
    )(x, q, mk, mv, wo)


def _xattn_decode_kernel(q_ref, mk_ref, mv_ref, o_ref, *, steps):
    grp, nkeys = mk_ref.shape[1], mk_ref.shape[2]
    rows = grp * steps
    q = jnp.concatenate([q_ref[:, hd * XA_HEAD_DIM:(hd + 1) * XA_HEAD_DIM] for hd in range(XA_HEADS)], axis=0)
    r = lax.broadcasted_iota(jnp.int32, (XA_HEADS * rows, 1), 0)
    own = (r % rows) // steps
    same_head = (r // rows) == (lax.broadcasted_iota(jnp.int32, (1, nkeys), 1) % XA_HEADS)
    s = None
    for j in range(grp):
        sj = _dot_tb(q, mk_ref[0, j].astype(BF16))
        s = sj if s is None else jnp.where(own == j, sj, s)
    s = jnp.where(same_head, s * (XA_HEAD_DIM ** -0.5), -jnp.inf)
    m = jnp.max(s, axis=-1, keepdims=True)
    e = jnp.exp(s - m)
    p = e.astype(BF16)
    o = None
    for j in range(grp):
        oj = _dot(p, mv_ref[0, j].astype(BF16))
        o = oj if o is None else jnp.where(own == j, oj, o)
    o = o / jnp.sum(e, axis=-1, keepdims=True)
    for hd in range(XA_HEADS):
        o_ref[:, hd * XA_HEAD_DIM:(hd + 1) * XA_HEAD_DIM] = o[hd * rows:(hd + 1) * rows, :].astype(BF16)


def _xattn_decode(q, mk, mv, layer, steps):
    nseq, nkeys = mk.shape[1], mk.shape[2]
    grp = DEC_GROUP
    rows = grp * steps
    row = lambda i: (i, 0)
    mem = lambda i: (layer, i, 0, 0)
    blk = (1, grp, nkeys, XA_HEAD_DIM)
    return pl.pallas_call(
        functools.partial(_xattn_decode_kernel, steps=steps),
        grid=(nseq // grp,),
        in_specs=[pl.BlockSpec((rows, XA_W), row), pl.BlockSpec(blk, mem), pl.BlockSpec(blk, mem)],
        out_specs=pl.BlockSpec((rows, XA_W), row),
        out_shape=jax.ShapeDtypeStruct((nseq * steps, XA_W), BF16),
        compiler_params=_params("parallel"),
        name="xattn_decode",
    )(q, mk, mv)


def _proj_res_kernel(x_ref, a_ref, w_ref, o_ref):
    o_ref[...] = x_ref[...] + _dot(a_ref[...], w_ref[...])


def _proj_res(x, a, w):
    n, d = x.shape
    tm = min(ROW_TILE, n)
    row = lambda i: (i, 0)
    return pl.pallas_call(
        _proj_res_kernel,
        grid=(n // tm,),
        in_specs=[pl.BlockSpec((tm, d), row), pl.BlockSpec((tm, a.shape[1]), row), _resident(w.shape)],
        out_specs=pl.BlockSpec((tm, d), row),
        out_shape=jax.ShapeDtypeStruct((n, d), F32),
        compiler_params=_params("parallel"),
        name="proj_res",
    )(x, a, w)


def _rope_tables(pos):
    half = HEAD_DIM // 2
    inv = ROPE_THETA ** (-jnp.arange(half, dtype=F32) / half)
    ang = pos.astype(F32)[:, None] * inv[None, :]
    cos, sin = jnp.cos(ang), jnp.sin(ang)
    reps = LANES // HEAD_DIM
    return (jnp.concatenate([cos, cos] * reps, axis=-1),
            jnp.concatenate([-sin, sin] * reps, axis=-1))


def _block_tril(n_blocks, size):
    i = jnp.arange(n_blocks * size)
    return ((i[:, None] // size == i[None, :] // size) & (i[:, None] >= i[None, :])).astype(BF16)


def _permute_heads(w, axis):
    blocks = jnp.split(w, SWA_Q_HEADS, axis=axis)
    return jnp.concatenate([blocks[h] for h in SWA_HEAD_ORDER], axis=axis)


def kernel(x_prompt, x_sample, cache_swa_k, cache_swa_v, state_gla, cache_mem_k, cache_mem_v, mem_prompt, ffn1_norm, ffn1_wg, ffn1_wu, ffn1_wd, mix_norm, w_in, swa_q_norm, swa_k_norm, swa_sinks, gla_w_gate, gla_b_gate, gla_out_norm, w_out, xa_norm, mem_norm, xa_wq, xa_wk, xa_wv, xa_q_norm, xa_k_norm, xa_wo, ffn2_norm, ffn2_wg, ffn2_wu, ffn2_wd):
    batch, seq, d = x_prompt.shape
    nseq, steps, _ = x_sample.shape
    depth = w_in.shape[0]
    mem_len = mem_prompt.shape[1]

    bf = lambda w: w.astype(BF16)
    ffn1 = (bf(ffn1_wg), bf(ffn1_wu), bf(ffn1_wd))
    ffn2 = (bf(ffn2_wg), bf(ffn2_wu), bf(ffn2_wd))
    o_qg = SWA_Q_W + 2 * SWA_KV_W
    w_all = bf(jnp.concatenate([
        _permute_heads(w_in[:, :, :SWA_Q_W], 2),
        w_in[:, :, SWA_Q_W:o_qg],
        w_in[:, :, o_qg:o_qg + GLA_K_W] * (GLA_DK ** -0.5),
        w_in[:, :, o_qg + GLA_K_W:],
        jnp.zeros((depth, d, MXU_TILE - GLA_LOWRANK), F32)], axis=2))
    gate_w = jnp.pad(bf(gla_w_gate), ((0, 0), (0, LANES - GLA_LOWRANK), (0, 0)))
    w_a = bf(_permute_heads(w_out[:, :SWA_Q_W], 1))
    w_o = bf(w_out[:, SWA_Q_W:])
    wq_b, wk_b, wv_b, wo_b = bf(xa_wq), bf(xa_wk), bf(xa_wv), bf(xa_wo)

    lane = jnp.arange(MXU_TILE)
    bd = (lane[:, None] // HEAD_DIM == lane[None, :] // HEAD_DIM).astype(BF16)
    cos_p, sin_p = _rope_tables(jnp.arange(seq))
    cos_s, sin_s = _rope_tables(PAST_LEN + jnp.arange(nseq * steps) % steps)
    tril_p = _block_tril(min(GLA_BLOCK, seq) // min(GLA_CHUNK, seq), min(GLA_CHUNK, seq))
    tril_s = _block_tril(DEC_GROUP, steps)

    mk_p, mv_p = _memkv(mem_prompt.reshape(batch * mem_len, d), mem_norm[:, None, :], wk_b, wv_b,
                        xa_k_norm[:, None, :])
    mk_p = mk_p.reshape(depth, batch, mem_len, XA_W)
    mv_p = mv_p.reshape(depth, batch, mem_len, XA_W)
    mk_s = cache_mem_k.reshape(depth, nseq, mem_len * XA_HEADS, XA_HEAD_DIM)
    mv_s = cache_mem_v.reshape(depth, nseq, mem_len * XA_HEADS, XA_HEAD_DIM)

    xp = x_prompt.reshape(batch * seq, d)
    xs = x_sample.reshape(nseq * steps, d)
    kp_l, vp_l, sp_l, ks_l, vs_l, ss_l = [], [], [], [], [], []
    for l in range(depth):
        row = lambda a: a[l][None, :]
        qn = jnp.tile(swa_q_norm[l] * (HEAD_DIM ** -0.5), LANES // HEAD_DIM)[None, :]
        kn = jnp.tile(swa_k_norm[l], LANES // HEAD_DIM)[None, :]
        sinks = swa_sinks[l][None, :]
        gla_gain = row(gla_out_norm)

        def mix_in(x, cos, sin):
            x = _ffn(x, row(ffn1_norm), ffn1[0][l], ffn1[1][l], ffn1[2][l])
            return (x,) + tuple(_inproj(x, row(mix_norm), w_all[l], gate_w[l], row(gla_b_gate),
                                        qn, kn, cos, sin, bd))

        def mix_out(x, a, o, cross):
            x, q = _outproj(x, a, o, w_a[l], w_o[l], row(xa_norm), wq_b[l], row(xa_q_norm))
            x = cross(x, q)
            return _ffn(x, row(ffn2_norm), ffn2[0][l], ffn2[1][l], ffn2[2][l])

        xp, q_s, k_s, v_s, q_g, k_g, v_g, g_g, la = mix_in(xp, cos_p, sin_p)
        a_p = _swa_prompt(sinks, q_s, k_s, v_s, batch, seq)
        o_p, s_p = _gla_prompt(q_g, k_g, la, v_g, g_g, gla_gain, tril_p, batch, seq)
        kp_l.append(k_s.reshape(batch, seq, SWA_KV_HEADS, HEAD_DIM)[:, seq - WINDOW:])
        vp_l.append(v_s.reshape(batch, seq, SWA_KV_HEADS, HEAD_DIM)[:, seq - WINDOW:])
        sp_l.append(s_p.reshape(batch, GLA_HEADS, GLA_DK, GLA_DV))
        xp = mix_out(xp, a_p, o_p, lambda x, q: _xattn_prompt(x, q, mk_p, mv_p, l, wo_b[l], seq))

        xs, q_s, k_s, v_s, q_g, k_g, v_g, g_g, la = mix_in(xs, cos_s, sin_s)
        a_s, k_new, v_new = _swa_decode(sinks, q_s, k_s, v_s,
                                        cache_swa_k[l].reshape(nseq, WINDOW, SWA_KV_W),
                                        cache_swa_v[l].reshape(nseq, WINDOW, SWA_KV_W), steps)
        o_s, s_s = _gla_decode(q_g, k_g, la, v_g, g_g, gla_gain, tril_s, state_gla[l], steps)
        ks_l.append(k_new.reshape(nseq, WINDOW, SWA_KV_HEADS, HEAD_DIM))
        vs_l.append(v_new.reshape(nseq, WINDOW, SWA_KV_HEADS, HEAD_DIM))
        ss_l.append(s_s)
        xs = mix_out(xs, a_s, o_s,
                     lambda x, q: _proj_res(x, _xattn_decode(q, mk_s, mv_s, l, steps), wo_b[l]))

    return (xp.reshape(batch, seq, d), xs.reshape(nseq, steps, d),
            jnp.stack(kp_l), jnp.stack(vp_l), jnp.stack(sp_l),
            mk_p.reshape(depth, batch, mem_len, XA_HEADS, XA_HEAD_DIM),
            mv_p.reshape(depth, batch, mem_len, XA_HEADS, XA_HEAD_DIM),
            jnp.stack(ks_l), jnp.stack(vs_l), jnp.stack(ss_l))
```

```python
import functools

import jax
import jax.numpy as jnp
from jax import lax
from jax.experimental import pallas as pl
from jax.experimental.pallas import tpu as pltpu

F32 = jnp.float32
BF16 = jnp.bfloat16

EPS = 1e-6
PAST_LEN = 16384
WINDOW = 128
ROPE_THETA = 10000.0
HEAD_DIM = 64
SWA_Q_HEADS = 8
SWA_KV_HEADS = 2
SWA_GROUP = SWA_Q_HEADS // SWA_KV_HEADS
GLA_HEADS = 4
GLA_DK = 64
GLA_DV = 128
GLA_LOWRANK = 16
GLA_GATE_TEMP = 16.0
GLA_CHUNK = 64
XA_HEADS = 4
XA_HEAD_DIM = 128

SWA_Q_W = SWA_Q_HEADS * HEAD_DIM
SWA_KV_W = SWA_KV_HEADS * HEAD_DIM
GLA_K_W = GLA_HEADS * GLA_DK
GLA_V_W = GLA_HEADS * GLA_DV
XA_W = XA_HEADS * XA_HEAD_DIM
MAIN_W = SWA_Q_W + 2 * SWA_KV_W + 2 * GLA_K_W + 2 * GLA_V_W

LANES = 128
MXU_TILE = 256
VMEM_LIMIT = 56 * 1024 * 1024

ROW_TILE = 512
FFN_CHUNK = 256
SWA_BLOCK = 256
GLA_BLOCK = 256
DEC_GROUP = 4
XA_DEC_GROUP = 8

assert SWA_KV_HEADS * HEAD_DIM == LANES
SWA_HEAD_ORDER = tuple(kv * SWA_GROUP + g for g in range(SWA_GROUP) for kv in range(SWA_KV_HEADS))


def _dot(a, b):
    return jnp.dot(a, b, preferred_element_type=F32)


def _dot_tb(a, b):
    return lax.dot_general(a, b, (((1,), (1,)), ((), ())), preferred_element_type=F32)


def _dot_ta(a, b):
    return lax.dot_general(a, b, (((0,), (0,)), ((), ())), preferred_element_type=F32)


def _split_bf16(x):
    hi = x.astype(BF16)
    lo = (x - hi.astype(F32)).astype(BF16)
    return hi, lo


def _rms(x, g):
    ms = jnp.mean(x * x, axis=-1, keepdims=True)
    return x * lax.rsqrt(ms + EPS) * g


def _params(*sem):
    return pltpu.CompilerParams(dimension_semantics=sem, vmem_limit_bytes=VMEM_LIMIT)


def _resident(arr):
    nd = arr.ndim
    return pl.BlockSpec(arr.shape, lambda *_: (0,) * nd, pipeline_mode=pl.Buffered(1))


def _layer_resident(arr, layer):
    nd = arr.ndim
    return pl.BlockSpec((None,) + arr.shape[1:], lambda *_: (layer,) + (0,) * (nd - 1),
                        pipeline_mode=pl.Buffered(1))


def _ffn_kernel(x_ref, g_ref, wg_ref, wu_ref, wd_ref, o_ref, a_scr):
    x = x_ref[...]
    h = _rms(x, g_ref[...]).astype(BF16)
    dff = wg_ref.shape[1]
    for c in range(dff // FFN_CHUNK):
        sl = slice(c * FFN_CHUNK, (c + 1) * FFN_CHUNK)
        g = _dot(h, wg_ref[:, sl])
        u = _dot(h, wu_ref[:, sl])
        a_scr[:, sl] = (g * jax.nn.sigmoid(g) * u).astype(BF16)
    o_ref[...] = x + 0.5 * _dot(a_scr[...], wd_ref[...])


def _ffn(x, layer, g, wg, wu, wd):
    n, d = x.shape
    dff = wg.shape[2]
    tm = min(ROW_TILE, n)
    return pl.pallas_call(
        _ffn_kernel,
        grid=(n // tm,),
        in_specs=[pl.BlockSpec((tm, d), lambda i: (i, 0))] + [_layer_resident(a, layer) for a in (g, wg, wu, wd)],
        out_specs=pl.BlockSpec((tm, d), lambda i: (i, 0)),
        out_shape=jax.ShapeDtypeStruct((n, d), F32),
        scratch_shapes=[pltpu.VMEM((tm, dff), BF16)],
        compiler_params=_params("parallel"),
        name="ffn",
    )(x, g, wg, wu, wd)


def _inproj_kernel(x_ref, g_ref, w_ref, gw_ref, gb_ref, qn_ref, kn_ref, cos_ref, sin_ref, bd_ref,
                   qs_ref, ks_ref, vs_ref, qg_ref, kg_ref, vg_ref, gg_ref, la_ref):
    h = _rms(x_ref[...], g_ref[...]).astype(BF16)
    bd = bd_ref[...]
    cos = cos_ref[...]
    sin = sin_ref[...]
    lane = lax.broadcasted_iota(jnp.int32, cos.shape, 1)
    lane_lo = (lane % HEAD_DIM) < (HEAD_DIM // 2)

    def head_scale(z):
        ss = _dot((z * z).astype(BF16), bd)
        return lax.rsqrt(ss * (1.0 / HEAD_DIM) + EPS)

    def rope(y):
        swapped = jnp.where(lane_lo, pltpu.roll(y, LANES - HEAD_DIM // 2, axis=1),
                            pltpu.roll(y, HEAD_DIM // 2, axis=1))
        return y * cos + swapped * sin

    split = SWA_Q_W + 2 * SWA_KV_W + 2 * GLA_K_W
    z1 = _dot(h, w_ref[:, :split])
    z2 = _dot(h, w_ref[:, split:])
    qn = qn_ref[...]
    for t in range(SWA_Q_W // MXU_TILE):
        z = z1[:, t * MXU_TILE:(t + 1) * MXU_TILE]
        y = z * head_scale(z)
        for c in range(MXU_TILE // LANES):
            sl = slice(c * LANES, (c + 1) * LANES)
            qs_ref[:, t * MXU_TILE + c * LANES:t * MXU_TILE + (c + 1) * LANES] = rope(y[:, sl] * qn).astype(BF16)
    o = SWA_Q_W
    z = z1[:, o:o + 2 * SWA_KV_W]
    ks_ref[...] = rope(z[:, :SWA_KV_W] * head_scale(z)[:, :SWA_KV_W] * kn_ref[...])
    vs_ref[...] = z[:, SWA_KV_W:]
    o += 2 * SWA_KV_W
    qg_ref[...] = z1[:, o:o + GLA_K_W]
    o += GLA_K_W
    kg_ref[...] = z1[:, o:o + GLA_K_W]
    vg_ref[...] = z2[:, :GLA_V_W].astype(BF16)
    gg_ref[...] = z2[:, GLA_V_W:2 * GLA_V_W]
    lr = z2[:, 2 * GLA_V_W:2 * GLA_V_W + LANES].astype(BF16)
    t = _dot(lr, gw_ref[...]) + gb_ref[...]
    log_sig = jnp.minimum(t, 0.0) - jnp.log(1.0 + jnp.exp(-jnp.abs(t)))
    la_ref[...] = log_sig * (1.0 / GLA_GATE_TEMP)


def _inproj(x, layer, g, w_all, gate_w, gate_b, qn, kn, cos, sin, bd):
    n, d = x.shape
    tm = min(ROW_TILE, n)
    pos_blocks = cos.shape[0] // tm
    row = lambda i: (i, 0)
    pos = lambda i: (i % pos_blocks, 0)
    widths = (SWA_Q_W, SWA_KV_W, SWA_KV_W, GLA_K_W, GLA_K_W, GLA_V_W, GLA_V_W, GLA_K_W)
    dtypes = (BF16, F32, F32, F32, F32, BF16, F32, F32)
    return pl.pallas_call(
        _inproj_kernel,
        grid=(n // tm,),
        in_specs=([pl.BlockSpec((tm, d), row)]
                  + [_layer_resident(a, layer) for a in (g, w_all, gate_w, gate_b, qn, kn)]
                  + [pl.BlockSpec((tm, LANES), pos), pl.BlockSpec((tm, LANES), pos), _resident(bd)]),
        out_specs=[pl.BlockSpec((tm, w), row) for w in widths],
        out_shape=[jax.ShapeDtypeStruct((n, w), dt) for w, dt in zip(widths, dtypes)],
        compiler_params=_params("parallel"),
        name="inproj",
    )(x, g, w_all, gate_w, gate_b, qn, kn, cos, sin, bd)


def _swa_attend(q, k, v, valid, sinks):
    rows, keys = valid.shape
    low = lax.broadcasted_iota(jnp.int32, (1, LANES), 1) < HEAD_DIM
    zero = jnp.zeros((), q.dtype)
    pieces = []
    for c in range(SWA_Q_W // LANES):
        qc = q[:, c * LANES:(c + 1) * LANES]
        pieces += [jnp.where(low, qc, zero), jnp.where(low, zero, qc)]
    s = _dot_tb(jnp.concatenate(pieces, axis=0), k).reshape(SWA_Q_HEADS, rows, keys)
    s = jnp.where(valid[None], s, -jnp.inf)
    m = jnp.maximum(jnp.max(s, axis=-1, keepdims=True), sinks)
    e = jnp.exp(s - m)
    den = jnp.sum(e, axis=-1, keepdims=True) + jnp.exp(sinks - m)
    o = _dot(e.astype(BF16).reshape(SWA_Q_HEADS * rows, keys), v).reshape(SWA_Q_HEADS, rows, LANES) / den
    return [jnp.where(low, o[2 * c], o[2 * c + 1]) for c in range(SWA_Q_W // LANES)]


def _sink_column(sink_ref, layer):
    idx = lax.broadcasted_iota(jnp.int32, (SWA_Q_HEADS, 1, 1), 0)
    col = jnp.zeros((SWA_Q_HEADS, 1, 1), F32)
    for p, head in enumerate(SWA_HEAD_ORDER):
        col = jnp.where(idx == p, sink_ref[layer, head], col)
    return col


def _swa_prompt_kernel(sink_ref, q_ref, kp_ref, kc_ref, vp_ref, vc_ref, o_ref, *, layer):
    j = pl.program_id(1)
    kall = jnp.concatenate([kp_ref[...], kc_ref[...]], axis=0).astype(BF16)
    vall = jnp.concatenate([vp_ref[...], vc_ref[...]], axis=0).astype(BF16)
    sinks = _sink_column(sink_ref, layer)
    iq = lax.broadcasted_iota(jnp.int32, (WINDOW, 2 * WINDOW), 0)
    jk = lax.broadcasted_iota(jnp.int32, (WINDOW, 2 * WINDOW), 1)
    rel = WINDOW + iq - jk
    band = (rel >= 0) & (rel < WINDOW)
    for w in range(q_ref.shape[0] // WINDOW):
        rows = slice(w * WINDOW, (w + 1) * WINDOW)
        keys = slice(w * WINDOW, (w + 2) * WINDOW)
        valid = band if w > 0 else band & ((j > 0) | (jk >= WINDOW))
        cols = _swa_attend(q_ref[rows, :], kall[keys], vall[keys], valid, sinks)
        for c, col in enumerate(cols):
            o_ref[rows, c * LANES:(c + 1) * LANES] = col.astype(BF16)


def _swa_prompt(sinks, layer, q, k, v, batch, seq):
    qb = min(SWA_BLOCK, seq)
    nb = seq // qb
    per = qb // WINDOW
    cur = lambda b, j: (b * nb + j, 0)
    prev = lambda b, j: ((b * nb + j) * per - jnp.minimum(j, 1), 0)
    return pl.pallas_call(
        functools.partial(_swa_prompt_kernel, layer=layer),
        grid=(batch, nb),
        in_specs=[
            pl.BlockSpec(memory_space=pltpu.SMEM),
            pl.BlockSpec((qb, SWA_Q_W), cur),
            pl.BlockSpec((WINDOW, SWA_KV_W), prev),
            pl.BlockSpec((qb, SWA_KV_W), cur),
            pl.BlockSpec((WINDOW, SWA_KV_W), prev),
            pl.BlockSpec((qb, SWA_KV_W), cur),
        ],
        out_specs=pl.BlockSpec((qb, SWA_Q_W), cur),
        out_shape=jax.ShapeDtypeStruct((batch * seq, SWA_Q_W), BF16),
        compiler_params=_params("parallel", "parallel"),
        name="swa_prompt",
    )(sinks, q, k, k, v, v)


def _layer_view(ref, layer, first):
    if not first:
        return ref
    for other in range(ref.shape[0]):
        if other != layer:
            ref[other] = jnp.zeros(ref.shape[1:], ref.dtype)
    return ref.at[layer]


def _swa_decode_kernel(sink_ref, q_ref, kn_ref, vn_ref, kc_ref, vc_ref, *rest, steps, layer):
    o_ref, ko_ref, vo_ref = rest[-3:]
    first = len(rest) == 3
    ko_ref = _layer_view(ko_ref, layer, first)
    vo_ref = _layer_view(vo_ref, layer, first)
    grp = kc_ref.shape[0]
    rows = grp * steps
    kn = kn_ref[...]
    vn = vn_ref[...]
    keep = WINDOW - steps
    for s in range(grp):
        ko_ref[s, 0:keep, :] = kc_ref[s, steps:WINDOW, :]
        ko_ref[s, keep:WINDOW, :] = kn[s * steps:(s + 1) * steps, :]
        vo_ref[s, 0:keep, :] = vc_ref[s, steps:WINDOW, :]
        vo_ref[s, keep:WINDOW, :] = vn[s * steps:(s + 1) * steps, :]
    kcat = jnp.concatenate([kc_ref[s] for s in range(grp)] + [kn], axis=0).astype(BF16)
    vcat = jnp.concatenate([vc_ref[s] for s in range(grp)] + [vn], axis=0).astype(BF16)
    nk = grp * WINDOW + rows
    r = lax.broadcasted_iota(jnp.int32, (rows, nk), 0)
    q_seq, q_step = r // steps, r % steps
    c = lax.broadcasted_iota(jnp.int32, (rows, nk), 1)
    is_new = c >= grp * WINDOW
    cn = c - grp * WINDOW
    k_seq = jnp.where(is_new, cn // steps, c // WINDOW)
    k_idx = jnp.where(is_new, WINDOW + cn % steps, c % WINDOW)
    rel = WINDOW + q_step - k_idx
    valid = (q_seq == k_seq) & (rel >= 0) & (rel < WINDOW)
    cols = _swa_attend(q_ref[...], kcat, vcat, valid, _sink_column(sink_ref, layer))
    for c, col in enumerate(cols):
        o_ref[:, c * LANES:(c + 1) * LANES] = col.astype(BF16)


def _stacked_out(shape, block, layer, prev):
    tail = (0,) * (len(block) - 1)
    if prev is None:
        spec = pl.BlockSpec((shape[0],) + block, lambda i: (0, i) + tail)
        extra_inputs = []
    else:
        spec = pl.BlockSpec((None,) + block, lambda i: (layer, i) + tail)
        extra_inputs = list(prev)
    extra_specs = [pl.BlockSpec(memory_space=pl.ANY) for _ in extra_inputs]
    return extra_inputs, extra_specs, spec, jax.ShapeDtypeStruct(shape, F32)


def _swa_decode(sinks, layer, q, kn, vn, kc, vc, steps, prev):
    depth, nseq = kc.shape[:2]
    grp = DEC_GROUP
    rows = grp * steps
    row = lambda i: (i, 0)
    cache = lambda i: (layer, i, 0, 0)
    cache_spec = pl.BlockSpec((None, grp, WINDOW, SWA_KV_W), cache)
    extra_in, extra_specs, out_spec, stacked = _stacked_out(kc.shape, (grp, WINDOW, SWA_KV_W), layer, prev)
    n_in = 6
    return pl.pallas_call(
        functools.partial(_swa_decode_kernel, steps=steps, layer=layer),
        grid=(nseq // grp,),
        in_specs=[
            pl.BlockSpec(memory_space=pltpu.SMEM),
            pl.BlockSpec((rows, SWA_Q_W), row),
            pl.BlockSpec((rows, SWA_KV_W), row),
            pl.BlockSpec((rows, SWA_KV_W), row),
            cache_spec,
            cache_spec,
        ] + extra_specs,
        out_specs=[pl.BlockSpec((rows, SWA_Q_W), row), out_spec, out_spec],
        out_shape=[jax.ShapeDtypeStruct((nseq * steps, SWA_Q_W), BF16), stacked, stacked],
        input_output_aliases={n_in + i: 1 + i for i in range(len(extra_in))},
        compiler_params=_params("parallel"),
        name="swa_decode",
    )(sinks, q, kn, vn, kc, vc, *extra_in)


def _gla_out(o, gain, gate):
    return _rms(o, gain) * (gate * jax.nn.sigmoid(gate))


def _head_stack(x, width):
    return jnp.concatenate([x[:, h * width:(h + 1) * width] for h in range(GLA_HEADS)], axis=0)


def _head_masked_stack(x, head_of_lane):
    zero = jnp.zeros((), x.dtype)
    return jnp.concatenate([jnp.where(head_of_lane == h, x, zero) for h in range(GLA_HEADS)], axis=0)


def _gla_prompt_kernel(q_ref, k_ref, la_ref, v_ref, gg_ref, gn_ref, tril_ref, o_ref, s_ref, st_scr):
    tb = pl.program_id(1)

    @pl.when(tb == 0)
    def _():
        st_scr[...] = jnp.zeros_like(st_scr)

    c_len = min(GLA_CHUNK, q_ref.shape[0])
    tril = tril_ref[...]
    g_hi, g_lo = _split_bf16(la_ref[...])
    b_all = _dot(tril, g_hi) + _dot(tril, g_lo)
    head_of_lane = lax.broadcasted_iota(jnp.int32, (1, GLA_K_W), 1) // GLA_DK
    ri = lax.broadcasted_iota(jnp.int32, (GLA_HEADS * c_len, c_len), 0) % c_len
    ci = lax.broadcasted_iota(jnp.int32, (GLA_HEADS * c_len, c_len), 1)
    causal = ri >= ci
    gain = gn_ref[...]
    st = st_scr[...]
    for c in range(q_ref.shape[0] // c_len):
        rs = slice(c * c_len, (c + 1) * c_len)
        b = b_all[rs, :]
        dec = jnp.exp(b[c_len - 1:c_len, :])
        q_t = q_ref[rs, :] * jnp.exp(b)
        k_t = k_ref[rs, :] * jnp.exp(-b)
        qm = _head_masked_stack(q_t, head_of_lane).astype(BF16)
        kdm = _head_masked_stack(k_t * dec, head_of_lane).astype(BF16)
        a = jnp.where(causal, _dot_tb(qm, k_t.astype(BF16)), 0.0).astype(BF16)
        v = v_ref[rs, :]
        o = _dot_tb(qm, st.astype(BF16)) + jnp.concatenate(
            [_dot(a[h * c_len:(h + 1) * c_len, :], v[:, h * GLA_DV:(h + 1) * GLA_DV])
             for h in range(GLA_HEADS)], axis=0)
        y = _gla_out(o, gain, _head_stack(gg_ref[rs, :], GLA_DV)).astype(BF16)
        for h in range(GLA_HEADS):
            o_ref[rs, h * GLA_DV:(h + 1) * GLA_DV] = y[h * c_len:(h + 1) * c_len, :]
        st = dec * st + _dot_ta(_head_stack(v, GLA_DV), kdm)
    st_scr[...] = st

    @pl.when(tb == pl.num_programs(1) - 1)
    def _():
        s_ref[0] = st.T


def _gla_prompt(layer, q, k, la, v, gg, gain, tril, batch, seq):
    tb = tril.shape[0]
    nb = seq // tb
    row = lambda b, t: (b * nb + t, 0)
    return pl.pallas_call(
        _gla_prompt_kernel,
        grid=(batch, nb),
        in_specs=[
            pl.BlockSpec((tb, GLA_K_W), row),
            pl.BlockSpec((tb, GLA_K_W), row),
            pl.BlockSpec((tb, GLA_K_W), row),
            pl.BlockSpec((tb, GLA_V_W), row),
            pl.BlockSpec((tb, GLA_V_W), row),
            _layer_resident(gain, layer),
            _resident(tril),
        ],
        out_specs=[
            pl.BlockSpec((tb, GLA_V_W), row),
            pl.BlockSpec((1, GLA_K_W, GLA_DV), lambda b, t: (b, 0, 0)),
        ],
        out_shape=[
            jax.ShapeDtypeStruct((batch * seq, GLA_V_W), BF16),
            jax.ShapeDtypeStruct((batch, GLA_K_W, GLA_DV), F32),
        ],
        scratch_shapes=[pltpu.VMEM((GLA_DV, GLA_K_W), F32)],
        compiler_params=_params("parallel", "arbitrary"),
        name="gla_prompt",
    )(q, k, la, v, gg, gain, tril)


def _gla_decode_kernel(q_ref, k_ref, la_ref, v_ref, gg_ref, gn_ref, tril_ref, s_ref, *rest, steps, layer):
    o_ref, so_ref = rest[-2:]
    so_ref = _layer_view(so_ref, layer, len(rest) == 2)
    grp = s_ref.shape[0]
    rows = grp * steps
    tril = tril_ref[...]
    ones = jnp.ones((steps, GLA_DV), BF16)
    g_hi, g_lo = _split_bf16(la_ref[...])
    b = _dot(tril, g_hi) + _dot(tril, g_lo)
    ri = lax.broadcasted_iota(jnp.int32, (rows, rows), 0)
    ci = lax.broadcasted_iota(jnp.int32, (rows, rows), 1)
    causal = (ri // steps == ci // steps) & (ri >= ci)
    last = ((ri // steps == ci // steps) & (ci % steps == steps - 1)).astype(BF16)
    b_hi, b_lo = _split_bf16(b)
    b_last = _dot(last, b_hi) + _dot(last, b_lo)
    k = k_ref[...]
    q_t = (q_ref[...] * jnp.exp(b)).astype(BF16)
    k_t = (k * jnp.exp(-b)).astype(BF16)
    k_dec = (k * jnp.exp(b_last - b)).astype(BF16)
    gain = gn_ref[...]
    for h in range(GLA_HEADS):
        ks = slice(h * GLA_DK, (h + 1) * GLA_DK)
        vs = slice(h * GLA_DV, (h + 1) * GLA_DV)
        v = v_ref[:, vs]
        a = jnp.where(causal, _dot_tb(q_t[:, ks], k_t[:, ks]), 0.0)
        o_intra = _dot(a.astype(BF16), v)
        o_inter = []
        for s in range(grp):
            ss = slice(s * steps, (s + 1) * steps)
            state = s_ref[s, h]
            o_inter.append(_dot(q_t[ss, ks], state.astype(BF16)))
            decay = jnp.exp(_dot_ta(g_hi[ss, ks], ones) + _dot_ta(g_lo[ss, ks], ones))
            so_ref[s, h] = decay * state + _dot_ta(k_dec[ss, ks], v[ss, :])
        o = o_intra + jnp.concatenate(o_inter, axis=0)
        o_ref[:, vs] = _gla_out(o, gain, gg_ref[:, vs]).astype(BF16)


def _gla_decode(layer, q, k, la, v, gg, gain, tril, state, steps, prev):
    nseq = state.shape[1]
    grp = DEC_GROUP
    rows = grp * steps
    row = lambda i: (i, 0)
    st_spec = pl.BlockSpec((None, grp, GLA_HEADS, GLA_DK, GLA_DV), lambda i: (layer, i, 0, 0, 0))
    extra_in, extra_specs, out_spec, stacked = _stacked_out(
        state.shape, (grp, GLA_HEADS, GLA_DK, GLA_DV), layer, prev)
    n_in = 8
    return pl.pallas_call(
        functools.partial(_gla_decode_kernel, steps=steps, layer=layer),
        grid=(nseq // grp,),
        in_specs=[
            pl.BlockSpec((rows, GLA_K_W), row),
            pl.BlockSpec((rows, GLA_K_W), row),
            pl.BlockSpec((rows, GLA_K_W), row),
            pl.BlockSpec((rows, GLA_V_W), row),
            pl.BlockSpec((rows, GLA_V_W), row),
            _layer_resident(gain, layer),
            _resident(tril),
            st_spec,
        ] + extra_specs,
        out_specs=[pl.BlockSpec((rows, GLA_V_W), row), out_spec],
        out_shape=[jax.ShapeDtypeStruct((nseq * steps, GLA_V_W), BF16), stacked],
        input_output_aliases={n_in + i: 1 + i for i in range(len(extra_in))},
        compiler_params=_params("parallel"),
        name="gla_decode",
    )(q, k, la, v, gg, gain, tril, state, *extra_in)


def _outproj_kernel(x_ref, a_ref, o_ref, wa_ref, wo_ref, g_ref, wq_ref, qn_ref, x1_ref, q_ref):
    x1 = x_ref[...] + _dot(a_ref[...], wa_ref[...]) + _dot(o_ref[...], wo_ref[...])
    x1_ref[...] = x1
    h = _rms(x1, g_ref[...]).astype(BF16)
    qn = qn_ref[...]
    for hd in range(XA_HEADS):
        sl = slice(hd * XA_HEAD_DIM, (hd + 1) * XA_HEAD_DIM)
        q_ref[:, sl] = _rms(_dot(h, wq_ref[:, sl]), qn).astype(BF16)


def _outproj(x, layer, a, o, w_a, w_o, g, wq, qn):
    n, d = x.shape
    tm = min(ROW_TILE, n)
    row = lambda i: (i, 0)
    return pl.pallas_call(
        _outproj_kernel,
        grid=(n // tm,),
        in_specs=[
            pl.BlockSpec((tm, d), row),
            pl.BlockSpec((tm, SWA_Q_W), row),
            pl.BlockSpec((tm, GLA_V_W), row),
        ] + [_layer_resident(p, layer) for p in (w_a, w_o, g, wq, qn)],
        out_specs=[pl.BlockSpec((tm, d), row), pl.BlockSpec((tm, XA_W), row)],
        out_shape=[jax.ShapeDtypeStruct((n, d), F32), jax.ShapeDtypeStruct((n, XA_W), BF16)],
        compiler_params=_params("parallel"),
        name="outproj",
    )(x, a, o, w_a, w_o, g, wq, qn)


def _memkv_kernel(m_ref, g_ref, wk_ref, wv_ref, kn_ref, k_ref, v_ref):
    m = _rms(m_ref[...], g_ref[...]).astype(BF16)
    kn = kn_ref[...]
    for hd in range(XA_HEADS):
        sl = slice(hd * XA_HEAD_DIM, (hd + 1) * XA_HEAD_DIM)
        k_ref[:, sl] = _rms(_dot(m, wk_ref[:, sl]), kn)
    v_ref[...] = _dot(m, wv_ref[...])


def _memkv(mem, g, wk, wv, kn):
    depth = wk.shape[0]
    n, d = mem.shape
    tm = min(ROW_TILE, n)
    per_layer = lambda l, i: (l, 0, 0)
    out = lambda l, i: (l, i, 0)
    return pl.pallas_call(
        _memkv_kernel,
        grid=(depth, n // tm),
        in_specs=[
            pl.BlockSpec((tm, d), lambda l, i: (i, 0)),
            pl.BlockSpec((None, 1, d), per_layer),
            pl.BlockSpec((None, d, XA_W), per_layer),
            pl.BlockSpec((None, d, XA_W), per_layer),
            pl.BlockSpec((None, 1, XA_HEAD_DIM), per_layer),
        ],
        out_specs=[pl.BlockSpec((None, tm, XA_W), out), pl.BlockSpec((None, tm, XA_W), out)],
        out_shape=[jax.ShapeDtypeStruct((depth, n, XA_W), F32)] * 2,
        compiler_params=_params("parallel", "parallel"),
        name="memkv",
    )(mem, g, wk, wv, kn)


def _softmax_pv(s, v):
    m = jnp.max(s, axis=-1, keepdims=True)
    e = jnp.exp(s - m)
    return _dot(e.astype(BF16), v) / jnp.sum(e, axis=-1, keepdims=True)


def _xattn_prompt_kernel(x_ref, q_ref, mk_ref, mv_ref, wo_ref, o_ref, a_scr):
    mk = mk_ref[...].astype(BF16)
    mv = mv_ref[...].astype(BF16)
    for hd in range(XA_HEADS):
        sl = slice(hd * XA_HEAD_DIM, (hd + 1) * XA_HEAD_DIM)
        s = _dot_tb(q_ref[:, sl], mk[:, sl]) * (XA_HEAD_DIM ** -0.5)
        a_scr[:, sl] = _softmax_pv(s, mv[:, sl]).astype(BF16)
    o_ref[...] = x_ref[...] + _dot(a_scr[...], wo_ref[...])


def _xattn_prompt(x, layer, q, mk, mv, wo, seq):
    n, d = x.shape
    tm = min(ROW_TILE, seq)
    per_seq = seq // tm
    row = lambda i: (i, 0)
    mem = lambda i: (layer, i // per_seq, 0, 0)
    mem_len = mk.shape[2]
    return pl.pallas_call(
        _xattn_prompt_kernel,
        grid=(n // tm,),
        in_specs=[
            pl.BlockSpec((tm, d), row),
            pl.BlockSpec((tm, XA_W), row),
            pl.BlockSpec((None, None, mem_len, XA_W), mem),
            pl.BlockSpec((None, None, mem_len, XA_W), mem),
            _layer_resident(wo, layer),
        ],
        out_specs=pl.BlockSpec((tm, d), row),
        out_shape=jax.ShapeDtypeStruct((n, d), F32),
        scratch_shapes=[pltpu.VMEM((tm, XA_W), BF16)],
        compiler_params=_params("parallel"),
        name="xattn_prompt",
    )(x, q, mk, mv, wo)


def _xattn_decode_kernel(q_ref, mk_ref, mv_ref, o_ref, *, steps):
    grp, nkeys = mk_ref.shape[0], mk_ref.shape[1]
    rows = grp * steps
    q = jnp.concatenate([q_ref[:, hd * XA_HEAD_DIM:(hd + 1) * XA_HEAD_DIM] for hd in range(XA_HEADS)], axis=0)
    r = lax.broadcasted_iota(jnp.int32, (XA_HEADS * rows, 1), 0)
    own = (r % rows) // steps
    same_head = (r // rows) == (lax.broadcasted_iota(jnp.int32, (1, nkeys), 1) % XA_HEADS)
    s = None
    for j in range(grp):
        sj = _dot_tb(q, mk_ref[j].astype(BF16))
        s = sj if s is None else jnp.where(own == j, sj, s)
    s = jnp.where(same_head, s * (XA_HEAD_DIM ** -0.5), -jnp.inf)
    m = jnp.max(s, axis=-1, keepdims=True)
    e = jnp.exp(s - m)
    p = e.astype(BF16)
    o = None
    for j in range(grp):
        oj = _dot(p, mv_ref[j].astype(BF16))
        o = oj if o is None else jnp.where(own == j, oj, o)
    o = o / jnp.sum(e, axis=-1, keepdims=True)
    for hd in range(XA_HEADS):
        o_ref[:, hd * XA_HEAD_DIM:(hd + 1) * XA_HEAD_DIM] = o[hd * rows:(hd + 1) * rows, :].astype(BF16)


def _xattn_decode(q, layer, mk, mv, steps):
    nseq, nkeys = mk.shape[1], mk.shape[2]
    grp = XA_DEC_GROUP
    rows = grp * steps
    row = lambda i: (i, 0)
    mem_spec = pl.BlockSpec((None, grp, nkeys, XA_HEAD_DIM), lambda i: (layer, i, 0, 0))
    return pl.pallas_call(
        functools.partial(_xattn_decode_kernel, steps=steps),
        grid=(nseq // grp,),
        in_specs=[pl.BlockSpec((rows, XA_W), row), mem_spec, mem_spec],
        out_specs=pl.BlockSpec((rows, XA_W), row),
        out_shape=jax.ShapeDtypeStruct((nseq * steps, XA_W), BF16),
        compiler_params=_params("parallel"),
        name="xattn_decode",
    )(q, mk, mv)


def _proj_res_kernel(x_ref, a_ref, w_ref, o_ref):
    o_ref[...] = x_ref[...] + _dot(a_ref[...], w_ref[...])


def _proj_res(x, layer, a, w):
    n, d = x.shape
    tm = min(ROW_TILE, n)
    row = lambda i: (i, 0)
    return pl.pallas_call(
        _proj_res_kernel,
        grid=(n // tm,),
        in_specs=[pl.BlockSpec((tm, d), row), pl.BlockSpec((tm, a.shape[1]), row), _layer_resident(w, layer)],
        out_specs=pl.BlockSpec((tm, d), row),
        out_shape=jax.ShapeDtypeStruct((n, d), F32),
        compiler_params=_params("parallel"),
        name="proj_res",
    )(x, a, w)


def _rope_tables(pos):
    half = HEAD_DIM // 2
    inv = ROPE_THETA ** (-jnp.arange(half, dtype=F32) / half)
    ang = pos.astype(F32)[:, None] * inv[None, :]
    cos, sin = jnp.cos(ang), jnp.sin(ang)
    reps = LANES // HEAD_DIM
    return (jnp.concatenate([cos, cos] * reps, axis=-1),
            jnp.concatenate([-sin, sin] * reps, axis=-1))


def _block_tril(n_blocks, size):
    i = jnp.arange(n_blocks * size)
    return ((i[:, None] // size == i[None, :] // size) & (i[:, None] >= i[None, :])).astype(BF16)


def _permute_heads(w, axis):
    blocks = jnp.split(w, SWA_Q_HEADS, axis=axis)
    return jnp.concatenate([blocks[h] for h in SWA_HEAD_ORDER], axis=axis)


def kernel(x_prompt, x_sample, cache_swa_k, cache_swa_v, state_gla, cache_mem_k, cache_mem_v, mem_prompt, ffn1_norm, ffn1_wg, ffn1_wu, ffn1_wd, mix_norm, w_in, swa_q_norm, swa_k_norm, swa_sinks, gla_w_gate, gla_b_gate, gla_out_norm, w_out, xa_norm, mem_norm, xa_wq, xa_wk, xa_wv, xa_q_norm, xa_k_norm, xa_wo, ffn2_norm, ffn2_wg, ffn2_wu, ffn2_wd):
    batch, seq, d = x_prompt.shape
    nseq, steps, _ = x_sample.shape
    depth = w_in.shape[0]
    mem_len = mem_prompt.shape[1]

    bf = lambda w: w.astype(BF16)
    vec = lambda p: p[:, None, :]
    ffn1 = (vec(ffn1_norm), bf(ffn1_wg), bf(ffn1_wu), bf(ffn1_wd))
    ffn2 = (vec(ffn2_norm), bf(ffn2_wg), bf(ffn2_wu), bf(ffn2_wd))
    o_qg = SWA_Q_W + 2 * SWA_KV_W
    w_all = bf(jnp.concatenate([
        _permute_heads(w_in[:, :, :SWA_Q_W], 2),
        w_in[:, :, SWA_Q_W:o_qg],
        w_in[:, :, o_qg:o_qg + GLA_K_W] * (GLA_DK ** -0.5),
        w_in[:, :, o_qg + GLA_K_W:],
        jnp.zeros((depth, d, MXU_TILE - GLA_LOWRANK), F32)], axis=2))
    gate_w = jnp.pad(bf(gla_w_gate), ((0, 0), (0, LANES - GLA_LOWRANK), (0, 0)))
    w_a = bf(_permute_heads(w_out[:, :SWA_Q_W], 1))
    w_o = bf(w_out[:, SWA_Q_W:])
    wq_b, wk_b, wv_b, wo_b = bf(xa_wq), bf(xa_wk), bf(xa_wv), bf(xa_wo)
    qn = vec(jnp.tile(swa_q_norm * (HEAD_DIM ** -0.5), (1, LANES // HEAD_DIM)))
    kn = vec(jnp.tile(swa_k_norm, (1, LANES // HEAD_DIM)))
    inproj_params = (vec(mix_norm), w_all, gate_w, vec(gla_b_gate), qn, kn)
    outproj_params = (w_a, w_o, vec(xa_norm), wq_b, vec(xa_q_norm))
    gla_gain = vec(gla_out_norm)

    lane = jnp.arange(MXU_TILE)
    bd = (lane[:, None] // HEAD_DIM == lane[None, :] // HEAD_DIM).astype(BF16)
    cos_p, sin_p = _rope_tables(jnp.arange(seq))
    cos_s, sin_s = _rope_tables(PAST_LEN + jnp.arange(nseq * steps) % steps)
    tril_p = _block_tril(min(GLA_BLOCK, seq) // min(GLA_CHUNK, seq), min(GLA_CHUNK, seq))
    tril_s = _block_tril(DEC_GROUP, steps)

    mk_p, mv_p = _memkv(mem_prompt.reshape(batch * mem_len, d), vec(mem_norm), wk_b, wv_b, vec(xa_k_norm))
    mk_p = mk_p.reshape(depth, batch, mem_len, XA_W)
    mv_p = mv_p.reshape(depth, batch, mem_len, XA_W)
    mk_s = cache_mem_k.reshape(depth, nseq, mem_len * XA_HEADS, XA_HEAD_DIM)
    mv_s = cache_mem_v.reshape(depth, nseq, mem_len * XA_HEADS, XA_HEAD_DIM)
    kc_s = cache_swa_k.reshape(depth, nseq, WINDOW, SWA_KV_W)
    vc_s = cache_swa_v.reshape(depth, nseq, WINDOW, SWA_KV_W)

    xp = x_prompt.reshape(batch * seq, d)
    xs = x_sample.reshape(nseq * steps, d)
    kp_l, vp_l, sp_l = [], [], []
    swa_new = gla_new = None
    for l in range(depth):
        def mix_in(x, cos, sin):
            x = _ffn(x, l, *ffn1)
            return (x,) + tuple(_inproj(x, l, *inproj_params, cos, sin, bd))

        def mix_out(x, a, o, cross):
            x, q = _outproj(x, l, a, o, *outproj_params)
            return _ffn(cross(x, q), l, *ffn2)

        xp, q_s, k_s, v_s, q_g, k_g, v_g, g_g, la = mix_in(xp, cos_p, sin_p)
        a_p = _swa_prompt(swa_sinks, l, q_s, k_s, v_s, batch, seq)
        o_p, s_p = _gla_prompt(l, q_g, k_g, la, v_g, g_g, gla_gain, tril_p, batch, seq)
        kp_l.append(k_s.reshape(batch, seq, SWA_KV_HEADS, HEAD_DIM)[:, seq - WINDOW:])
        vp_l.append(v_s.reshape(batch, seq, SWA_KV_HEADS, HEAD_DIM)[:, seq - WINDOW:])
        sp_l.append(s_p.reshape(batch, GLA_HEADS, GLA_DK, GLA_DV))
        xp = mix_out(xp, a_p, o_p, lambda x, q: _xattn_prompt(x, l, q, mk_p, mv_p, wo_b, seq))

        xs, q_s, k_s, v_s, q_g, k_g, v_g, g_g, la = mix_in(xs, cos_s, sin_s)
        a_s, *swa_new = _swa_decode(swa_sinks, l, q_s, k_s, v_s, kc_s, vc_s, steps, swa_new)
        o_s, *gla_new = _gla_decode(l, q_g, k_g, la, v_g, g_g, gla_gain, tril_s, state_gla, steps, gla_new)
        xs = mix_out(xs, a_s, o_s,
                     lambda x, q: _proj_res(x, l, _xattn_decode(q, l, mk_s, mv_s, steps), wo_b))

    new_shape = (depth, nseq, WINDOW, SWA_KV_HEADS, HEAD_DIM)
    return (xp.reshape(batch, seq, d), xs.reshape(nseq, steps, d),
            jnp.stack(kp_l), jnp.stack(vp_l), jnp.stack(sp_l),
            mk_p.reshape(depth, batch, mem_len, XA_HEADS, XA_HEAD_DIM),
            mv_p.reshape(depth, batch, mem_len, XA_HEADS, XA_HEAD_DIM),
            swa_new[0].reshape(new_shape), swa_new[1].reshape(new_shape), gla_new[0])
```

```python
import functools

import jax
import jax.numpy as jnp
from jax import lax
from jax.experimental import pallas as pl
from jax.experimental.pallas import tpu as pltpu

F32 = jnp.float32
BF16 = jnp.bfloat16

EPS = 1e-6
LOG2E = 1.4426950408889634
PAST_LEN = 16384
WINDOW = 128
ROPE_THETA = 10000.0
HEAD_DIM = 64
SWA_Q_HEADS = 8
SWA_KV_HEADS = 2
SWA_GROUP = SWA_Q_HEADS // SWA_KV_HEADS
GLA_HEADS = 4
GLA_DK = 64
GLA_DV = 128
GLA_LOWRANK = 16
GLA_GATE_TEMP = 16.0
GLA_CHUNK = 64
XA_HEADS = 4
XA_HEAD_DIM = 128

SWA_Q_W = SWA_Q_HEADS * HEAD_DIM
SWA_KV_W = SWA_KV_HEADS * HEAD_DIM
GLA_K_W = GLA_HEADS * GLA_DK
GLA_V_W = GLA_HEADS * GLA_DV
XA_W = XA_HEADS * XA_HEAD_DIM
MAIN_W = SWA_Q_W + 2 * SWA_KV_W + 2 * GLA_K_W + 2 * GLA_V_W

LANES = 128
MXU_TILE = 256
VMEM_LIMIT = 56 * 1024 * 1024

ROW_TILE = 512
FFN_CHUNK = 256
SWA_BLOCK = 256
GLA_BLOCK = 256
GLA_SEQS = 4
DEC_GROUP = 4
SWA_DEC_SUBGROUPS = 4
XA_DEC_GROUP = 8

assert SWA_KV_HEADS * HEAD_DIM == LANES
SWA_HEAD_ORDER = tuple(kv * SWA_GROUP + g for g in range(SWA_GROUP) for kv in range(SWA_KV_HEADS))


def _dot(a, b):
    return jnp.dot(a, b, preferred_element_type=F32)


def _dot_tb(a, b):
    return lax.dot_general(a, b, (((1,), (1,)), ((), ())), preferred_element_type=F32)


def _dot_ta(a, b):
    return lax.dot_general(a, b, (((0,), (0,)), ((), ())), preferred_element_type=F32)


def _split_bf16(x):
    hi = x.astype(BF16)
    lo = (x - hi.astype(F32)).astype(BF16)
    return hi, lo


def _rms(x, g):
    ms = jnp.mean(x * x, axis=-1, keepdims=True)
    return x * lax.rsqrt(ms + EPS) * g


def _params(*sem):
    return pltpu.CompilerParams(dimension_semantics=sem, vmem_limit_bytes=VMEM_LIMIT)


def _resident(arr):
    nd = arr.ndim
    return pl.BlockSpec(arr.shape, lambda *_: (0,) * nd, pipeline_mode=pl.Buffered(1))


def _layer_resident(arr, layer):
    nd = arr.ndim
    return pl.BlockSpec((None,) + arr.shape[1:], lambda *_: (layer,) + (0,) * (nd - 1),
                        pipeline_mode=pl.Buffered(1))


def _ffn_kernel(x_ref, g_ref, wg_ref, wu_ref, wd_ref, o_ref, a_scr):
    x = x_ref[...]
    h = _rms(x, g_ref[...]).astype(BF16)
    dff = wg_ref.shape[1]
    for c in range(dff // FFN_CHUNK):
        sl = slice(c * FFN_CHUNK, (c + 1) * FFN_CHUNK)
        g = _dot(h, wg_ref[:, sl])
        u = _dot(h, wu_ref[:, sl])
        a_scr[:, sl] = (g * jax.nn.sigmoid(g) * u).astype(BF16)
    o_ref[...] = x + 0.5 * _dot(a_scr[...], wd_ref[...])


def _ffn(x, layer, g, wg, wu, wd):
    n, d = x.shape
    dff = wg.shape[2]
    tm = min(ROW_TILE, n)
    return pl.pallas_call(
        _ffn_kernel,
        grid=(n // tm,),
        in_specs=[pl.BlockSpec((tm, d), lambda i: (i, 0))] + [_layer_resident(a, layer) for a in (g, wg, wu, wd)],
        out_specs=pl.BlockSpec((tm, d), lambda i: (i, 0)),
        out_shape=jax.ShapeDtypeStruct((n, d), F32),
        scratch_shapes=[pltpu.VMEM((tm, dff), BF16)],
        compiler_params=_params("parallel"),
        name="ffn",
    )(x, g, wg, wu, wd)


def _inproj_kernel(x_ref, g_ref, w_ref, gw_ref, gb_ref, qn_ref, kn_ref, cos_ref, sin_ref, bd_ref,
                   qs_ref, ks_ref, vs_ref, qg_ref, kg_ref, vg_ref, gg_ref, la_ref):
    h = _rms(x_ref[...], g_ref[...]).astype(BF16)
    bd = bd_ref[...]
    cos = cos_ref[...]
    sin = sin_ref[...]
    lane = lax.broadcasted_iota(jnp.int32, cos.shape, 1)
    lane_lo = (lane % HEAD_DIM) < (HEAD_DIM // 2)

    def head_scale(z):
        ss = _dot((z * z).astype(BF16), bd)
        return lax.rsqrt(ss * (1.0 / HEAD_DIM) + EPS)

    def rope(y):
        swapped = jnp.where(lane_lo, pltpu.roll(y, LANES - HEAD_DIM // 2, axis=1),
                            pltpu.roll(y, HEAD_DIM // 2, axis=1))
        return y * cos + swapped * sin

    split = SWA_Q_W + 2 * SWA_KV_W + 2 * GLA_K_W
    z1 = _dot(h, w_ref[:, :split])
    z2 = _dot(h, w_ref[:, split:])
    qn = qn_ref[...]
    for t in range(SWA_Q_W // MXU_TILE):
        z = z1[:, t * MXU_TILE:(t + 1) * MXU_TILE]
        y = z * head_scale(z)
        for c in range(MXU_TILE // LANES):
            sl = slice(c * LANES, (c + 1) * LANES)
            qs_ref[:, t * MXU_TILE + c * LANES:t * MXU_TILE + (c + 1) * LANES] = rope(y[:, sl] * qn).astype(BF16)
    o = SWA_Q_W
    z = z1[:, o:o + 2 * SWA_KV_W]
    ks_ref[...] = rope(z[:, :SWA_KV_W] * head_scale(z)[:, :SWA_KV_W] * kn_ref[...])
    vs_ref[...] = z[:, SWA_KV_W:]
    o += 2 * SWA_KV_W
    qg_ref[...] = z1[:, o:o + GLA_K_W]
    o += GLA_K_W
    kg_ref[...] = z1[:, o:o + GLA_K_W]
    vg_ref[...] = z2[:, :GLA_V_W].astype(BF16)
    gg_ref[...] = z2[:, GLA_V_W:2 * GLA_V_W]
    lr = z2[:, 2 * GLA_V_W:2 * GLA_V_W + LANES].astype(BF16)
    t = _dot(lr, gw_ref[...]) + gb_ref[...]
    log_sig = jnp.minimum(t, 0.0) - jnp.log(1.0 + jnp.exp(-jnp.abs(t)))
    la_ref[...] = log_sig * (1.0 / GLA_GATE_TEMP)


def _inproj(x, layer, g, w_all, gate_w, gate_b, qn, kn, cos, sin, bd):
    n, d = x.shape
    tm = min(ROW_TILE, n)
    pos_blocks = cos.shape[0] // tm
    row = lambda i: (i, 0)
    pos = lambda i: (i % pos_blocks, 0)
    widths = (SWA_Q_W, SWA_KV_W, SWA_KV_W, GLA_K_W, GLA_K_W, GLA_V_W, GLA_V_W, GLA_K_W)
    dtypes = (BF16, F32, F32, F32, F32, BF16, F32, F32)
    return pl.pallas_call(
        _inproj_kernel,
        grid=(n // tm,),
        in_specs=([pl.BlockSpec((tm, d), row)]
                  + [_layer_resident(a, layer) for a in (g, w_all, gate_w, gate_b, qn, kn)]
                  + [pl.BlockSpec((tm, LANES), pos), pl.BlockSpec((tm, LANES), pos), _resident(bd)]),
        out_specs=[pl.BlockSpec((tm, w), row) for w in widths],
        out_shape=[jax.ShapeDtypeStruct((n, w), dt) for w, dt in zip(widths, dtypes)],
        compiler_params=_params("parallel"),
        name="inproj",
    )(x, g, w_all, gate_w, gate_b, qn, kn, cos, sin, bd)


def _swa_attend(q, k, v, valid, sinks):
    return _swa_out(*_swa_probs(_swa_scores(q, k), valid, sinks), v)


def _low_half():
    return lax.broadcasted_iota(jnp.int32, (1, LANES), 1) < HEAD_DIM


def _swa_scores(q, k):
    low = _low_half()
    zero = jnp.zeros((), q.dtype)
    pieces = []
    for c in range(SWA_Q_W // LANES):
        qc = q[:, c * LANES:(c + 1) * LANES]
        pieces += [jnp.where(low, qc, zero), jnp.where(low, zero, qc)]
    return _dot_tb(jnp.concatenate(pieces, axis=0), k)


def _swa_probs(s, valid, sinks):
    rows, keys = valid.shape
    s = jnp.where(valid[None], s.reshape(SWA_Q_HEADS, rows, keys), -jnp.inf)
    m = jnp.maximum(jnp.max(s, axis=-1, keepdims=True), sinks)
    e = jnp.exp2(s - m)
    den = jnp.sum(e, axis=-1, keepdims=True) + jnp.exp2(sinks - m)
    return e.astype(BF16).reshape(SWA_Q_HEADS * rows, keys), den


def _swa_out(e, den, v):
    low = _low_half()
    o = _dot(e, v).reshape(den.shape[0], den.shape[1], LANES) / den
    return [jnp.where(low, o[2 * c], o[2 * c + 1]) for c in range(SWA_Q_W // LANES)]


def _sink_column(sink_ref, layer):
    idx = lax.broadcasted_iota(jnp.int32, (SWA_Q_HEADS, 1, 1), 0)
    col = jnp.zeros((SWA_Q_HEADS, 1, 1), F32)
    for p, head in enumerate(SWA_HEAD_ORDER):
        col = jnp.where(idx == p, sink_ref[layer, head] * LOG2E, col)
    return col


def _swa_prompt_kernel(sink_ref, q_ref, kp_ref, kc_ref, vp_ref, vc_ref, o_ref, *, layer):
    j = pl.program_id(1)
    kall = jnp.concatenate([kp_ref[...], kc_ref[...]], axis=0).astype(BF16)
    vall = jnp.concatenate([vp_ref[...], vc_ref[...]], axis=0).astype(BF16)
    sinks = _sink_column(sink_ref, layer)
    iq = lax.broadcasted_iota(jnp.int32, (WINDOW, 2 * WINDOW), 0)
    jk = lax.broadcasted_iota(jnp.int32, (WINDOW, 2 * WINDOW), 1)
    rel = WINDOW + iq - jk
    band = (rel >= 0) & (rel < WINDOW)
    windows = range(q_ref.shape[0] // WINDOW)
    rows = lambda w: slice(w * WINDOW, (w + 1) * WINDOW)
    keys = lambda w: slice(w * WINDOW, (w + 2) * WINDOW)
    scores = [_swa_scores(q_ref[rows(w), :], kall[keys(w)]) for w in windows]
    probs = [_swa_probs(scores[w], band if w > 0 else band & ((j > 0) | (jk >= WINDOW)), sinks)
             for w in windows]
    for w in windows:
        for c, col in enumerate(_swa_out(*probs[w], vall[keys(w)])):
            o_ref[rows(w), c * LANES:(c + 1) * LANES] = col.astype(BF16)


def _swa_prompt(sinks, layer, q, k, v, batch, seq):
    qb = min(SWA_BLOCK, seq)
    nb = seq // qb
    per = qb // WINDOW
    cur = lambda b, j: (b * nb + j, 0)
    prev = lambda b, j: ((b * nb + j) * per - jnp.minimum(j, 1), 0)
    return pl.pallas_call(
        functools.partial(_swa_prompt_kernel, layer=layer),
        grid=(batch, nb),
        in_specs=[
            pl.BlockSpec(memory_space=pltpu.SMEM),
            pl.BlockSpec((qb, SWA_Q_W), cur),
            pl.BlockSpec((WINDOW, SWA_KV_W), prev),
            pl.BlockSpec((qb, SWA_KV_W), cur),
            pl.BlockSpec((WINDOW, SWA_KV_W), prev),
            pl.BlockSpec((qb, SWA_KV_W), cur),
        ],
        out_specs=pl.BlockSpec((qb, SWA_Q_W), cur),
        out_shape=jax.ShapeDtypeStruct((batch * seq, SWA_Q_W), BF16),
        compiler_params=_params("parallel", "parallel"),
        name="swa_prompt",
    )(sinks, q, k, k, v, v)


def _layer_view(ref, layer, first):
    if not first:
        return ref
    for other in range(ref.shape[0]):
        if other != layer:
            ref[other] = jnp.zeros(ref.shape[1:], ref.dtype)
    return ref.at[layer]


def _swa_decode_kernel(sink_ref, q_ref, kn_ref, vn_ref, kc_ref, vc_ref, *rest, steps, layer):
    o_ref, ko_ref, vo_ref = rest[-3:]
    first = len(rest) == 3
    ko_ref = _layer_view(ko_ref, layer, first)
    vo_ref = _layer_view(vo_ref, layer, first)
    n_seq = kc_ref.shape[0]
    grp = DEC_GROUP
    rows = grp * steps
    keep = WINDOW - steps
    for s in range(n_seq):
        new = slice(s * steps, (s + 1) * steps)
        ko_ref[s, 0:keep, :] = kc_ref[s, steps:WINDOW, :]
        ko_ref[s, keep:WINDOW, :] = kn_ref[new, :]
        vo_ref[s, 0:keep, :] = vc_ref[s, steps:WINDOW, :]
        vo_ref[s, keep:WINDOW, :] = vn_ref[new, :]
    nk = grp * WINDOW + rows
    r = lax.broadcasted_iota(jnp.int32, (rows, nk), 0)
    q_seq, q_step = r // steps, r % steps
    c = lax.broadcasted_iota(jnp.int32, (rows, nk), 1)
    is_new = c >= grp * WINDOW
    cn = c - grp * WINDOW
    k_seq = jnp.where(is_new, cn // steps, c // WINDOW)
    k_idx = jnp.where(is_new, WINDOW + cn % steps, c % WINDOW)
    rel = WINDOW + q_step - k_idx
    valid = (q_seq == k_seq) & (rel >= 0) & (rel < WINDOW)
    sinks = _sink_column(sink_ref, layer)
    groups = range(n_seq // grp)
    new_rows = lambda g: slice(g * rows, (g + 1) * rows)

    def keys(cache_ref, new_ref, g):
        return jnp.concatenate([cache_ref[g * grp + s] for s in range(grp)] + [new_ref[new_rows(g), :]],
                               axis=0).astype(BF16)

    scores = [_swa_scores(q_ref[new_rows(g), :], keys(kc_ref, kn_ref, g)) for g in groups]
    probs = [_swa_probs(scores[g], valid, sinks) for g in groups]
    for g in groups:
        for c, col in enumerate(_swa_out(*probs[g], keys(vc_ref, vn_ref, g))):
            o_ref[new_rows(g), c * LANES:(c + 1) * LANES] = col.astype(BF16)


def _stacked_out(shape, block, layer, prev):
    tail = (0,) * (len(block) - 1)
    if prev is None:
        spec = pl.BlockSpec((shape[0],) + block, lambda i: (0, i) + tail)
        extra_inputs = []
    else:
        spec = pl.BlockSpec((None,) + block, lambda i: (layer, i) + tail)
        extra_inputs = list(prev)
    extra_specs = [pl.BlockSpec(memory_space=pl.ANY) for _ in extra_inputs]
    return extra_inputs, extra_specs, spec, jax.ShapeDtypeStruct(shape, F32)


def _swa_decode(sinks, layer, q, kn, vn, kc, vc, steps, prev):
    depth, nseq = kc.shape[:2]
    grp = min(DEC_GROUP * SWA_DEC_SUBGROUPS, nseq)
    rows = grp * steps
    row = lambda i: (i, 0)
    cache = lambda i: (layer, i, 0, 0)
    cache_spec = pl.BlockSpec((None, grp, WINDOW, SWA_KV_W), cache)
    extra_in, extra_specs, out_spec, stacked = _stacked_out(kc.shape, (grp, WINDOW, SWA_KV_W), layer, prev)
    n_in = 6
    return pl.pallas_call(
        functools.partial(_swa_decode_kernel, steps=steps, layer=layer),
        grid=(nseq // grp,),
        in_specs=[
            pl.BlockSpec(memory_space=pltpu.SMEM),
            pl.BlockSpec((rows, SWA_Q_W), row),
            pl.BlockSpec((rows, SWA_KV_W), row),
            pl.BlockSpec((rows, SWA_KV_W), row),
            cache_spec,
            cache_spec,
        ] + extra_specs,
        out_specs=[pl.BlockSpec((rows, SWA_Q_W), row), out_spec, out_spec],
        out_shape=[jax.ShapeDtypeStruct((nseq * steps, SWA_Q_W), BF16), stacked, stacked],
        input_output_aliases={n_in + i: 1 + i for i in range(len(extra_in))},
        compiler_params=_params("parallel"),
        name="swa_decode",
    )(sinks, q, kn, vn, kc, vc, *extra_in)


def _gla_out(o, gain, gate):
    return _rms(o, gain) * (gate * jax.nn.sigmoid(gate))


def _head_stack(x, width):
    return jnp.concatenate([x[:, h * width:(h + 1) * width] for h in range(GLA_HEADS)], axis=0)


def _head_masked_stack(x, head_of_lane):
    zero = jnp.zeros((), x.dtype)
    return jnp.concatenate([jnp.where(head_of_lane == h, x, zero) for h in range(GLA_HEADS)], axis=0)


def _gla_prompt_kernel(q_ref, k_ref, la_ref, v_ref, gg_ref, gn_ref, tril_ref, o_ref, s_ref, st_scr):
    tb = pl.program_id(1)

    @pl.when(tb == 0)
    def _():
        st_scr[...] = jnp.zeros_like(st_scr)

    n_tok = q_ref.shape[1]
    c_len = min(GLA_CHUNK, n_tok)
    tril = tril_ref[...]
    head_of_lane = lax.broadcasted_iota(jnp.int32, (1, GLA_K_W), 1) // GLA_DK
    ri = lax.broadcasted_iota(jnp.int32, (GLA_HEADS * c_len, c_len), 0) % c_len
    ci = lax.broadcasted_iota(jnp.int32, (GLA_HEADS * c_len, c_len), 1)
    causal = ri >= ci
    gain = gn_ref[...]
    n_seq = q_ref.shape[0]
    items = [(i, c) for i in range(n_seq) for c in range(n_tok // c_len)]
    rows = lambda c: slice(c * c_len, (c + 1) * c_len)
    b_all = []
    for i in range(n_seq):
        g_hi, g_lo = _split_bf16(la_ref[i])
        b_all.append(_dot(tril, g_hi) + _dot(tril, g_lo))
    qm, kdm, dec, a_raw = {}, {}, {}, {}
    for it in items:
        i, c = it
        b = b_all[i][rows(c), :]
        dec[it] = jnp.exp(b[c_len - 1:c_len, :])
        q_t = q_ref[i, rows(c), :] * jnp.exp(b)
        k_t = k_ref[i, rows(c), :] * jnp.exp(-b)
        qm[it] = _head_masked_stack(q_t, head_of_lane).astype(BF16)
        kdm[it] = _head_masked_stack(k_t * dec[it], head_of_lane).astype(BF16)
        a_raw[it] = _dot_tb(qm[it], k_t.astype(BF16))
    upd = {it: _dot_ta(_head_stack(v_ref[it[0], rows(it[1]), :], GLA_DV), kdm[it]) for it in items}
    intra = {}
    for it in items:
        i, c = it
        a = jnp.where(causal, a_raw[it], 0.0).astype(BF16)
        intra[it] = jnp.concatenate(
            [_dot(a[h * c_len:(h + 1) * c_len, :], v_ref[i, rows(c), h * GLA_DV:(h + 1) * GLA_DV])
             for h in range(GLA_HEADS)], axis=0)
    inter = {}
    for i in range(n_seq):
        st = st_scr[i]
        for c in range(n_tok // c_len):
            inter[(i, c)] = _dot_tb(qm[(i, c)], st.astype(BF16))
            st = dec[(i, c)] * st + upd[(i, c)]
        st_scr[i] = st
    for it in items:
        i, c = it
        y = _gla_out(inter[it] + intra[it], gain, _head_stack(gg_ref[i, rows(c), :], GLA_DV)).astype(BF16)
        for h in range(GLA_HEADS):
            o_ref[i, rows(c), h * GLA_DV:(h + 1) * GLA_DV] = y[h * c_len:(h + 1) * c_len, :]

    @pl.when(tb == pl.num_programs(1) - 1)
    def _():
        for i in range(q_ref.shape[0]):
            s_ref[i] = st_scr[i].T


def _gla_prompt(layer, q, k, la, v, gg, gain, tril, batch, seq):
    tb = tril.shape[0]
    per = min(GLA_SEQS, batch)
    blk = lambda b, t: (b, t, 0)
    kw = pl.BlockSpec((per, tb, GLA_K_W), blk)
    vw = pl.BlockSpec((per, tb, GLA_V_W), blk)
    return pl.pallas_call(
        _gla_prompt_kernel,
        grid=(batch // per, seq // tb),
        in_specs=[kw, kw, kw, vw, vw, _layer_resident(gain, layer), _resident(tril)],
        out_specs=[vw, pl.BlockSpec((per, GLA_K_W, GLA_DV), lambda b, t: (b, 0, 0))],
        out_shape=[
            jax.ShapeDtypeStruct((batch, seq, GLA_V_W), BF16),
            jax.ShapeDtypeStruct((batch, GLA_K_W, GLA_DV), F32),
        ],
        scratch_shapes=[pltpu.VMEM((per, GLA_DV, GLA_K_W), F32)],
        compiler_params=_params("parallel", "arbitrary"),
        name="gla_prompt",
    )(q, k, la, v, gg, gain, tril)


def _gla_decode_kernel(q_ref, k_ref, la_ref, v_ref, gg_ref, gn_ref, s_ref, *rest, steps, layer):
    o_ref, so_ref = rest[-2:]
    so_ref = _layer_view(so_ref, layer, len(rest) == 2)
    grp = s_ref.shape[0]
    rows = grp * steps
    stacked = GLA_HEADS * rows
    ri = lax.broadcasted_iota(jnp.int32, (rows, rows), 0)
    ci = lax.broadcasted_iota(jnp.int32, (rows, rows), 1)
    same_seq = ri // steps == ci // steps
    g_hi, g_lo = _split_bf16(la_ref[...])
    tril = (same_seq & (ri >= ci)).astype(BF16)
    total = same_seq.astype(BF16)
    b = _dot(tril, g_hi) + _dot(tril, g_lo)
    b_last = _dot(total, g_hi) + _dot(total, g_lo)
    head_of_lane = lax.broadcasted_iota(jnp.int32, (1, GLA_K_W), 1) // GLA_DK
    k_t = k_ref[...] * jnp.exp(-b)
    qm = _head_masked_stack(q_ref[...] * jnp.exp(b), head_of_lane).astype(BF16)
    km = _head_masked_stack(k_t, head_of_lane).astype(BF16)
    kdm = _head_masked_stack(k_t * jnp.exp(b_last), head_of_lane).astype(BF16)
    v_st = _head_stack(v_ref[...], GLA_DV)
    seq_of_row = (lax.broadcasted_iota(jnp.int32, (stacked, 1), 0) % rows) // steps
    seq_of_g = lax.broadcasted_iota(jnp.int32, (rows, 1), 0) // steps
    zero = jnp.zeros((), BF16)
    rhs = jnp.concatenate([
        jnp.concatenate([v_st, jnp.zeros((stacked, GLA_DV), BF16)], axis=1),
        jnp.concatenate([jnp.zeros((2 * rows, GLA_DV), BF16), jnp.ones((2 * rows, GLA_DV), BF16)], axis=1),
    ], axis=0)
    a_raw = _dot_tb(qm, km)
    states = [s_ref[s].reshape(GLA_K_W, GLA_DV) for s in range(grp)]
    inter_all = [_dot(qm, st.astype(BF16)) for st in states]
    upd_all = []
    for s in range(grp):
        lhs = jnp.concatenate([jnp.where(seq_of_row == s, kdm, zero),
                               jnp.where(seq_of_g == s, g_hi, zero),
                               jnp.where(seq_of_g == s, g_lo, zero)], axis=0)
        upd_all.append(_dot_ta(lhs, rhs))
    rr = lax.broadcasted_iota(jnp.int32, (stacked, stacked), 0) % rows
    cc = lax.broadcasted_iota(jnp.int32, (stacked, stacked), 1) % rows
    causal = (rr // steps == cc // steps) & (rr >= cc)
    o = _dot(jnp.where(causal, a_raw, 0.0).astype(BF16), v_st)
    inter = inter_all[0]
    for s in range(1, grp):
        inter = jnp.where(seq_of_row == s, inter_all[s], inter)
    y = _gla_out(o + inter, gn_ref[...], _head_stack(gg_ref[...], GLA_DV)).astype(BF16)
    for h in range(GLA_HEADS):
        o_ref[:, h * GLA_DV:(h + 1) * GLA_DV] = y[h * rows:(h + 1) * rows, :]
    for s in range(grp):
        new = jnp.exp(upd_all[s][:, GLA_DV:]) * states[s] + upd_all[s][:, :GLA_DV]
        so_ref[s] = new.reshape(GLA_HEADS, GLA_DK, GLA_DV)


def _gla_decode(layer, q, k, la, v, gg, gain, state, steps, prev):
    nseq = state.shape[1]
    grp = DEC_GROUP
    rows = grp * steps
    row = lambda i: (i, 0)
    st_spec = pl.BlockSpec((None, grp, GLA_HEADS, GLA_DK, GLA_DV), lambda i: (layer, i, 0, 0, 0))
    extra_in, extra_specs, out_spec, stacked = _stacked_out(
        state.shape, (grp, GLA_HEADS, GLA_DK, GLA_DV), layer, prev)
    n_in = 7
    return pl.pallas_call(
        functools.partial(_gla_decode_kernel, steps=steps, layer=layer),
        grid=(nseq // grp,),
        in_specs=[
            pl.BlockSpec((rows, GLA_K_W), row),
            pl.BlockSpec((rows, GLA_K_W), row),
            pl.BlockSpec((rows, GLA_K_W), row),
            pl.BlockSpec((rows, GLA_V_W), row),
            pl.BlockSpec((rows, GLA_V_W), row),
            _layer_resident(gain, layer),
            st_spec,
        ] + extra_specs,
        out_specs=[pl.BlockSpec((rows, GLA_V_W), row), out_spec],
        out_shape=[jax.ShapeDtypeStruct((nseq * steps, GLA_V_W), BF16), stacked],
        input_output_aliases={n_in + i: 1 + i for i in range(len(extra_in))},
        compiler_params=_params("parallel"),
        name="gla_decode",
    )(q, k, la, v, gg, gain, state, *extra_in)


def _outproj_kernel(x_ref, a_ref, o_ref, wa_ref, wo_ref, g_ref, wq_ref, qn_ref, x1_ref, q_ref):
    x1 = x_ref[...] + _dot(a_ref[...], wa_ref[...]) + _dot(o_ref[...], wo_ref[...])
    x1_ref[...] = x1
    h = _rms(x1, g_ref[...]).astype(BF16)
    qn = qn_ref[...]
    for hd in range(XA_HEADS):
        sl = slice(hd * XA_HEAD_DIM, (hd + 1) * XA_HEAD_DIM)
        q_ref[:, sl] = _rms(_dot(h, wq_ref[:, sl]), qn).astype(BF16)


def _outproj(x, layer, a, o, w_a, w_o, g, wq, qn):
    n, d = x.shape
    tm = min(ROW_TILE, n)
    row = lambda i: (i, 0)
    return pl.pallas_call(
        _outproj_kernel,
        grid=(n // tm,),
        in_specs=[
            pl.BlockSpec((tm, d), row),
            pl.BlockSpec((tm, SWA_Q_W), row),
            pl.BlockSpec((tm, GLA_V_W), row),
        ] + [_layer_resident(p, layer) for p in (w_a, w_o, g, wq, qn)],
        out_specs=[pl.BlockSpec((tm, d), row), pl.BlockSpec((tm, XA_W), row)],
        out_shape=[jax.ShapeDtypeStruct((n, d), F32), jax.ShapeDtypeStruct((n, XA_W), BF16)],
        compiler_params=_params("parallel"),
        name="outproj",
    )(x, a, o, w_a, w_o, g, wq, qn)


def _memkv_kernel(m_ref, g_ref, wk_ref, wv_ref, kn_ref, k_ref, v_ref):
    m = _rms(m_ref[...], g_ref[...]).astype(BF16)
    kn = kn_ref[...]
    for hd in range(XA_HEADS):
        sl = slice(hd * XA_HEAD_DIM, (hd + 1) * XA_HEAD_DIM)
        k_ref[:, sl] = _rms(_dot(m, wk_ref[:, sl]), kn)
    v_ref[...] = _dot(m, wv_ref[...])


def _memkv(mem, g, wk, wv, kn):
    depth = wk.shape[0]
    n, d = mem.shape
    tm = min(ROW_TILE, n)
    per_layer = lambda l, i: (l, 0, 0)
    out = lambda l, i: (l, i, 0)
    return pl.pallas_call(
        _memkv_kernel,
        grid=(depth, n // tm),
        in_specs=[
            pl.BlockSpec((tm, d), lambda l, i: (i, 0)),
            pl.BlockSpec((None, 1, d), per_layer),
            pl.BlockSpec((None, d, XA_W), per_layer),
            pl.BlockSpec((None, d, XA_W), per_layer),
            pl.BlockSpec((None, 1, XA_HEAD_DIM), per_layer),
        ],
        out_specs=[pl.BlockSpec((None, tm, XA_W), out), pl.BlockSpec((None, tm, XA_W), out)],
        out_shape=[jax.ShapeDtypeStruct((depth, n, XA_W), F32)] * 2,
        compiler_params=_params("parallel", "parallel"),
        name="memkv",
    )(mem, g, wk, wv, kn)


def _softmax_pv(s, v):
    m = jnp.max(s, axis=-1, keepdims=True)
    e = jnp.exp2(s - m)
    return _dot(e.astype(BF16), v) / jnp.sum(e, axis=-1, keepdims=True)


def _xattn_prompt_kernel(x_ref, q_ref, mk_ref, mv_ref, wo_ref, o_ref, a_scr):
    mk = mk_ref[...].astype(BF16)
    mv = mv_ref[...].astype(BF16)
    for hd in range(XA_HEADS):
        sl = slice(hd * XA_HEAD_DIM, (hd + 1) * XA_HEAD_DIM)
        s = _dot_tb(q_ref[:, sl], mk[:, sl]) * (XA_HEAD_DIM ** -0.5 * LOG2E)
        a_scr[:, sl] = _softmax_pv(s, mv[:, sl]).astype(BF16)
    o_ref[...] = x_ref[...] + _dot(a_scr[...], wo_ref[...])


def _xattn_prompt(x, layer, q, mk, mv, wo, seq):
    n, d = x.shape
    tm = min(ROW_TILE, seq)
    per_seq = seq // tm
    row = lambda i: (i, 0)
    mem = lambda i: (layer, i // per_seq, 0, 0)
    mem_len = mk.shape[2]
    return pl.pallas_call(
        _xattn_prompt_kernel,
        grid=(n // tm,),
        in_specs=[
            pl.BlockSpec((tm, d), row),
            pl.BlockSpec((tm, XA_W), row),
            pl.BlockSpec((None, None, mem_len, XA_W), mem),
            pl.BlockSpec((None, None, mem_len, XA_W), mem),
            _layer_resident(wo, layer),
        ],
        out_specs=pl.BlockSpec((tm, d), row),
        out_shape=jax.ShapeDtypeStruct((n, d), F32),
        scratch_shapes=[pltpu.VMEM((tm, XA_W), BF16)],
        compiler_params=_params("parallel"),
        name="xattn_prompt",
    )(x, q, mk, mv, wo)


def _xattn_decode_kernel(q_ref, mk_ref, mv_ref, o_ref, *, steps):
    grp, nkeys = mk_ref.shape[0], mk_ref.shape[1]
    rows = grp * steps
    q = jnp.concatenate([q_ref[:, hd * XA_HEAD_DIM:(hd + 1) * XA_HEAD_DIM] for hd in range(XA_HEADS)], axis=0)
    r = lax.broadcasted_iota(jnp.int32, (XA_HEADS * rows, 1), 0)
    own = (r % rows) // steps
    same_head = (r // rows) == (lax.broadcasted_iota(jnp.int32, (1, nkeys), 1) % XA_HEADS)
    s = None
    for j in range(grp):
        sj = _dot_tb(q, mk_ref[j].astype(BF16))
        s = sj if s is None else jnp.where(own == j, sj, s)
    s = jnp.where(same_head, s * (XA_HEAD_DIM ** -0.5 * LOG2E), -jnp.inf)
    m = jnp.max(s, axis=-1, keepdims=True)
    e = jnp.exp2(s - m)
    p = e.astype(BF16)
    o = None
    for j in range(grp):
        oj = _dot(p, mv_ref[j].astype(BF16))
        o = oj if o is None else jnp.where(own == j, oj, o)
    o = o / jnp.sum(e, axis=-1, keepdims=True)
    for hd in range(XA_HEADS):
        o_ref[:, hd * XA_HEAD_DIM:(hd + 1) * XA_HEAD_DIM] = o[hd * rows:(hd + 1) * rows, :].astype(BF16)


def _xattn_decode(q, layer, mk, mv, steps):
    nseq, nkeys = mk.shape[1], mk.shape[2]
    grp = XA_DEC_GROUP
    rows = grp * steps
    row = lambda i: (i, 0)
    mem_spec = pl.BlockSpec((None, grp, nkeys, XA_HEAD_DIM), lambda i: (layer, i, 0, 0))
    return pl.pallas_call(
        functools.partial(_xattn_decode_kernel, steps=steps),
        grid=(nseq // grp,),
        in_specs=[pl.BlockSpec((rows, XA_W), row), mem_spec, mem_spec],
        out_specs=pl.BlockSpec((rows, XA_W), row),
        out_shape=jax.ShapeDtypeStruct((nseq * steps, XA_W), BF16),
        compiler_params=_params("parallel"),
        name="xattn_decode",
    )(q, mk, mv)


def _proj_res_kernel(x_ref, a_ref, w_ref, o_ref):
    o_ref[...] = x_ref[...] + _dot(a_ref[...], w_ref[...])


def _proj_res(x, layer, a, w):
    n, d = x.shape
    tm = min(ROW_TILE, n)
    row = lambda i: (i, 0)
    return pl.pallas_call(
        _proj_res_kernel,
        grid=(n // tm,),
        in_specs=[pl.BlockSpec((tm, d), row), pl.BlockSpec((tm, a.shape[1]), row), _layer_resident(w, layer)],
        out_specs=pl.BlockSpec((tm, d), row),
        out_shape=jax.ShapeDtypeStruct((n, d), F32),
        compiler_params=_params("parallel"),
        name="proj_res",
    )(x, a, w)


def _rope_tables(pos):
    half = HEAD_DIM // 2
    inv = ROPE_THETA ** (-jnp.arange(half, dtype=F32) / half)
    ang = pos.astype(F32)[:, None] * inv[None, :]
    cos, sin = jnp.cos(ang), jnp.sin(ang)
    reps = LANES // HEAD_DIM
    return (jnp.concatenate([cos, cos] * reps, axis=-1),
            jnp.concatenate([-sin, sin] * reps, axis=-1))


def _block_tril(n_blocks, size):
    i = jnp.arange(n_blocks * size)
    return ((i[:, None] // size == i[None, :] // size) & (i[:, None] >= i[None, :])).astype(BF16)


def _permute_heads(w, axis):
    blocks = jnp.split(w, SWA_Q_HEADS, axis=axis)
    return jnp.concatenate([blocks[h] for h in SWA_HEAD_ORDER], axis=axis)


def kernel(x_prompt, x_sample, cache_swa_k, cache_swa_v, state_gla, cache_mem_k, cache_mem_v, mem_prompt, ffn1_norm, ffn1_wg, ffn1_wu, ffn1_wd, mix_norm, w_in, swa_q_norm, swa_k_norm, swa_sinks, gla_w_gate, gla_b_gate, gla_out_norm, w_out, xa_norm, mem_norm, xa_wq, xa_wk, xa_wv, xa_q_norm, xa_k_norm, xa_wo, ffn2_norm, ffn2_wg, ffn2_wu, ffn2_wd):
    batch, seq, d = x_prompt.shape
    nseq, steps, _ = x_sample.shape
    depth = w_in.shape[0]
    mem_len = mem_prompt.shape[1]

    bf = lambda w: w.astype(BF16)
    vec = lambda p: p[:, None, :]
    ffn1 = (vec(ffn1_norm), bf(ffn1_wg), bf(ffn1_wu), bf(ffn1_wd))
    ffn2 = (vec(ffn2_norm), bf(ffn2_wg), bf(ffn2_wu), bf(ffn2_wd))
    o_qg = SWA_Q_W + 2 * SWA_KV_W
    w_all = bf(jnp.concatenate([
        _permute_heads(w_in[:, :, :SWA_Q_W], 2),
        w_in[:, :, SWA_Q_W:o_qg],
        w_in[:, :, o_qg:o_qg + GLA_K_W] * (GLA_DK ** -0.5),
        w_in[:, :, o_qg + GLA_K_W:],
        jnp.zeros((depth, d, MXU_TILE - GLA_LOWRANK), F32)], axis=2))
    gate_w = jnp.pad(bf(gla_w_gate), ((0, 0), (0, LANES - GLA_LOWRANK), (0, 0)))
    w_a = bf(_permute_heads(w_out[:, :SWA_Q_W], 1))
    w_o = bf(w_out[:, SWA_Q_W:])
    wq_b, wk_b, wv_b, wo_b = bf(xa_wq), bf(xa_wk), bf(xa_wv), bf(xa_wo)
    qn = vec(jnp.tile(swa_q_norm * (HEAD_DIM ** -0.5 * LOG2E), (1, LANES // HEAD_DIM)))
    kn = vec(jnp.tile(swa_k_norm, (1, LANES // HEAD_DIM)))
    inproj_params = (vec(mix_norm), w_all, gate_w, vec(gla_b_gate), qn, kn)
    outproj_params = (w_a, w_o, vec(xa_norm), wq_b, vec(xa_q_norm))
    gla_gain = vec(gla_out_norm)

    lane = jnp.arange(MXU_TILE)
    bd = (lane[:, None] // HEAD_DIM == lane[None, :] // HEAD_DIM).astype(BF16)
    cos_p, sin_p = _rope_tables(jnp.arange(seq))
    cos_s, sin_s = _rope_tables(PAST_LEN + jnp.arange(nseq * steps) % steps)
    tril_p = _block_tril(min(GLA_BLOCK, seq) // min(GLA_CHUNK, seq), min(GLA_CHUNK, seq))

    mk_p, mv_p = _memkv(mem_prompt.reshape(batch * mem_len, d), vec(mem_norm), wk_b, wv_b, vec(xa_k_norm))
    mk_p = mk_p.reshape(depth, batch, mem_len, XA_W)
    mv_p = mv_p.reshape(depth, batch, mem_len, XA_W)
    mk_s = cache_mem_k.reshape(depth, nseq, mem_len * XA_HEADS, XA_HEAD_DIM)
    mv_s = cache_mem_v.reshape(depth, nseq, mem_len * XA_HEADS, XA_HEAD_DIM)
    kc_s = cache_swa_k.reshape(depth, nseq, WINDOW, SWA_KV_W)
    vc_s = cache_swa_v.reshape(depth, nseq, WINDOW, SWA_KV_W)

    xp = x_prompt.reshape(batch * seq, d)
    xs = x_sample.reshape(nseq * steps, d)
    kp_l, vp_l, sp_l = [], [], []
    swa_new = gla_new = None
    for l in range(depth):
        def mix_in(x, cos, sin):
            x = _ffn(x, l, *ffn1)
            return (x,) + tuple(_inproj(x, l, *inproj_params, cos, sin, bd))

        def mix_out(x, a, o, cross):
            x, q = _outproj(x, l, a, o, *outproj_params)
            return _ffn(cross(x, q), l, *ffn2)

        xp, q_s, k_s, v_s, q_g, k_g, v_g, g_g, la = mix_in(xp, cos_p, sin_p)
        a_p = _swa_prompt(swa_sinks, l, q_s, k_s, v_s, batch, seq)
        seqs = lambda a: a.reshape(batch, seq, a.shape[-1])
        o_p, s_p = _gla_prompt(l, seqs(q_g), seqs(k_g), seqs(la), seqs(v_g), seqs(g_g), gla_gain, tril_p,
                               batch, seq)
        o_p = o_p.reshape(batch * seq, GLA_V_W)
        last = lambda a: seqs(a)[:, seq - WINDOW:].reshape(batch, WINDOW, SWA_KV_HEADS, HEAD_DIM)
        kp_l.append(last(k_s))
        vp_l.append(last(v_s))
        sp_l.append(s_p.reshape(batch, GLA_HEADS, GLA_DK, GLA_DV))
        xp = mix_out(xp, a_p, o_p, lambda x, q: _xattn_prompt(x, l, q, mk_p, mv_p, wo_b, seq))

        xs, q_s, k_s, v_s, q_g, k_g, v_g, g_g, la = mix_in(xs, cos_s, sin_s)
        a_s, *swa_new = _swa_decode(swa_sinks, l, q_s, k_s, v_s, kc_s, vc_s, steps, swa_new)
        o_s, *gla_new = _gla_decode(l, q_g, k_g, la, v_g, g_g, gla_gain, state_gla, steps, gla_new)
        xs = mix_out(xs, a_s, o_s,
                     lambda x, q: _proj_res(x, l, _xattn_decode(q, l, mk_s, mv_s, steps), wo_b))

    new_shape = (depth, nseq, WINDOW, SWA_KV_HEADS, HEAD_DIM)
    return (xp.reshape(batch, seq, d), xs.reshape(nseq, steps, d),
            jnp.stack(kp_l), jnp.stack(vp_l), jnp.stack(sp_l),
            mk_p.reshape(depth, batch, mem_len, XA_HEADS, XA_HEAD_DIM),
            mv_p.reshape(depth, batch, mem_len, XA_HEADS, XA_HEAD_DIM),
            swa_new[0].reshape(new_shape), swa_new[1].reshape(new_shape), gla_new[0])
```

```python
import functools

import jax
import jax.numpy as jnp
from jax import lax
from jax.experimental import pallas as pl
from jax.experimental.pallas import tpu as pltpu

F32 = jnp.float32
BF16 = jnp.bfloat16

EPS = 1e-6
LOG2E = 1.4426950408889634
PAST_LEN = 16384
WINDOW = 128
ROPE_THETA = 10000.0
HEAD_DIM = 64
SWA_Q_HEADS = 8
SWA_KV_HEADS = 2
SWA_GROUP = SWA_Q_HEADS // SWA_KV_HEADS
GLA_HEADS = 4
GLA_DK = 64
GLA_DV = 128
GLA_LOWRANK = 16
GLA_GATE_TEMP = 16.0
GLA_CHUNK = 64
XA_HEADS = 4
XA_HEAD_DIM = 128

SWA_Q_W = SWA_Q_HEADS * HEAD_DIM
SWA_KV_W = SWA_KV_HEADS * HEAD_DIM
GLA_K_W = GLA_HEADS * GLA_DK
GLA_V_W = GLA_HEADS * GLA_DV
XA_W = XA_HEADS * XA_HEAD_DIM
MAIN_W = SWA_Q_W + 2 * SWA_KV_W + 2 * GLA_K_W + 2 * GLA_V_W

LANES = 128
MXU_TILE = 256
VMEM_LIMIT = 56 * 1024 * 1024

ROW_TILE = 512
FFN_CHUNK = 256
SWA_BLOCK = 256
GLA_BLOCK = 256
GLA_SEQS = 4
DEC_GROUP = 4
SWA_DEC_SUBGROUPS = 4
XA_DEC_GROUP = 8

assert SWA_KV_HEADS * HEAD_DIM == LANES
SWA_HEAD_ORDER = tuple(kv * SWA_GROUP + g for g in range(SWA_GROUP) for kv in range(SWA_KV_HEADS))


def _dot(a, b):
    return jnp.dot(a, b, preferred_element_type=F32)


def _dot_tb(a, b):
    return lax.dot_general(a, b, (((1,), (1,)), ((), ())), preferred_element_type=F32)


def _dot_ta(a, b):
    return lax.dot_general(a, b, (((0,), (0,)), ((), ())), preferred_element_type=F32)


def _split_bf16(x):
    hi = x.astype(BF16)
    lo = (x - hi.astype(F32)).astype(BF16)
    return hi, lo


def _rms(x, g):
    ms = jnp.mean(x * x, axis=-1, keepdims=True)
    return x * lax.rsqrt(ms + EPS) * g


def _params(*sem):
    return pltpu.CompilerParams(dimension_semantics=sem, vmem_limit_bytes=VMEM_LIMIT)


def _resident(arr):
    nd = arr.ndim
    return pl.BlockSpec(arr.shape, lambda *_: (0,) * nd, pipeline_mode=pl.Buffered(1))


def _layer_resident(arr, layer):
    nd = arr.ndim
    return pl.BlockSpec((None,) + arr.shape[1:], lambda *_: (layer,) + (0,) * (nd - 1),
                        pipeline_mode=pl.Buffered(1))


def _ffn_apply(x, g_ref, wg_ref, wu_ref, wd_ref, a_scr):
    h = _rms(x, g_ref[...]).astype(BF16)
    dff = wg_ref.shape[1]
    for c in range(dff // FFN_CHUNK):
        sl = slice(c * FFN_CHUNK, (c + 1) * FFN_CHUNK)
        g = _dot(h, wg_ref[:, sl])
        u = _dot(h, wu_ref[:, sl])
        a_scr[:, sl] = (g * jax.nn.sigmoid(g) * u).astype(BF16)
    return x + 0.5 * _dot(a_scr[...], wd_ref[...])


def _ffn_kernel(x_ref, g_ref, wg_ref, wu_ref, wd_ref, o_ref, a_scr):
    o_ref[...] = _ffn_apply(x_ref[...], g_ref, wg_ref, wu_ref, wd_ref, a_scr)


def _ffn(x, layer, g, wg, wu, wd):
    n, d = x.shape
    dff = wg.shape[2]
    tm = min(ROW_TILE, n)
    return pl.pallas_call(
        _ffn_kernel,
        grid=(n // tm,),
        in_specs=[pl.BlockSpec((tm, d), lambda i: (i, 0))] + [_layer_resident(a, layer) for a in (g, wg, wu, wd)],
        out_specs=pl.BlockSpec((tm, d), lambda i: (i, 0)),
        out_shape=jax.ShapeDtypeStruct((n, d), F32),
        scratch_shapes=[pltpu.VMEM((tm, dff), BF16)],
        compiler_params=_params("parallel"),
        name="ffn",
    )(x, g, wg, wu, wd)


def _ffn_inproj_kernel(x_ref, fg_ref, wg_ref, wu_ref, wd_ref, *rest):
    xo_ref, a_scr = rest[9], rest[-1]
    x = _ffn_apply(x_ref[...], fg_ref, wg_ref, wu_ref, wd_ref, a_scr)
    xo_ref[...] = x
    _inproj_apply(x, *rest[:9], *rest[10:-1])


def _inproj_apply(x, g_ref, w_ref, gw_ref, gb_ref, qn_ref, kn_ref, cos_ref, sin_ref, bd_ref,
                  qs_ref, ks_ref, vs_ref, qg_ref, kg_ref, vg_ref, gg_ref, la_ref):
    h = _rms(x, g_ref[...]).astype(BF16)
    bd = bd_ref[...]
    cos = cos_ref[...]
    sin = sin_ref[...]
    lane = lax.broadcasted_iota(jnp.int32, cos.shape, 1)
    lane_lo = (lane % HEAD_DIM) < (HEAD_DIM // 2)

    def head_scale(z):
        ss = _dot((z * z).astype(BF16), bd)
        return lax.rsqrt(ss * (1.0 / HEAD_DIM) + EPS)

    def rope(y):
        swapped = jnp.where(lane_lo, pltpu.roll(y, LANES - HEAD_DIM // 2, axis=1),
                            pltpu.roll(y, HEAD_DIM // 2, axis=1))
        return y * cos + swapped * sin

    split = SWA_Q_W + 2 * SWA_KV_W + 2 * GLA_K_W
    z1 = _dot(h, w_ref[:, :split])
    z2 = _dot(h, w_ref[:, split:])
    qn = qn_ref[...]
    for t in range(SWA_Q_W // MXU_TILE):
        z = z1[:, t * MXU_TILE:(t + 1) * MXU_TILE]
        y = z * head_scale(z)
        for c in range(MXU_TILE // LANES):
            sl = slice(c * LANES, (c + 1) * LANES)
            qs_ref[:, t * MXU_TILE + c * LANES:t * MXU_TILE + (c + 1) * LANES] = rope(y[:, sl] * qn).astype(BF16)
    o = SWA_Q_W
    z = z1[:, o:o + 2 * SWA_KV_W]
    ks_ref[...] = rope(z[:, :SWA_KV_W] * head_scale(z)[:, :SWA_KV_W] * kn_ref[...])
    vs_ref[...] = z[:, SWA_KV_W:]
    o += 2 * SWA_KV_W
    qg_ref[...] = z1[:, o:o + GLA_K_W]
    o += GLA_K_W
    kg_ref[...] = z1[:, o:o + GLA_K_W]
    vg_ref[...] = z2[:, :GLA_V_W].astype(BF16)
    gg_ref[...] = z2[:, GLA_V_W:2 * GLA_V_W]
    lr = z2[:, 2 * GLA_V_W:2 * GLA_V_W + LANES].astype(BF16)
    t = _dot(lr, gw_ref[...]) + gb_ref[...]
    log_sig = jnp.minimum(t, 0.0) - jnp.log(1.0 + jnp.exp(-jnp.abs(t)))
    la_ref[...] = log_sig * (1.0 / GLA_GATE_TEMP)


def _ffn_inproj(x, layer, ffn, inproj, cos, sin, bd):
    n, d = x.shape
    dff = ffn[1].shape[2]
    tm = min(ROW_TILE, n)
    pos_blocks = cos.shape[0] // tm
    row = lambda i: (i, 0)
    pos = lambda i: (i % pos_blocks, 0)
    widths = (d, SWA_Q_W, SWA_KV_W, SWA_KV_W, GLA_K_W, GLA_K_W, GLA_V_W, GLA_V_W, GLA_K_W)
    dtypes = (F32, BF16, F32, F32, F32, F32, BF16, F32, F32)
    return pl.pallas_call(
        _ffn_inproj_kernel,
        grid=(n // tm,),
        in_specs=([pl.BlockSpec((tm, d), row)]
                  + [_layer_resident(a, layer) for a in ffn + inproj]
                  + [pl.BlockSpec((tm, LANES), pos), pl.BlockSpec((tm, LANES), pos), _resident(bd)]),
        out_specs=[pl.BlockSpec((tm, w), row) for w in widths],
        out_shape=[jax.ShapeDtypeStruct((n, w), dt) for w, dt in zip(widths, dtypes)],
        scratch_shapes=[pltpu.VMEM((tm, dff), BF16)],
        compiler_params=_params("parallel"),
        name="ffn_inproj",
    )(x, *ffn, *inproj, cos, sin, bd)


def _swa_attend(q, k, v, valid, sinks):
    return _swa_out(*_swa_probs(_swa_scores(q, k), valid, sinks), v)


def _low_half():
    return lax.broadcasted_iota(jnp.int32, (1, LANES), 1) < HEAD_DIM


def _swa_scores(q, k):
    low = _low_half()
    zero = jnp.zeros((), q.dtype)
    pieces = []
    for c in range(SWA_Q_W // LANES):
        qc = q[:, c * LANES:(c + 1) * LANES]
        pieces += [jnp.where(low, qc, zero), jnp.where(low, zero, qc)]
    return _dot_tb(jnp.concatenate(pieces, axis=0), k)


def _swa_probs(s, valid, sinks):
    rows, keys = valid.shape
    s = jnp.where(valid[None], s.reshape(SWA_Q_HEADS, rows, keys), -jnp.inf)
    m = jnp.maximum(jnp.max(s, axis=-1, keepdims=True), sinks)
    e = jnp.exp2(s - m)
    den = jnp.sum(e, axis=-1, keepdims=True) + jnp.exp2(sinks - m)
    return e.astype(BF16).reshape(SWA_Q_HEADS * rows, keys), den


def _swa_out(e, den, v):
    low = _low_half()
    o = _dot(e, v).reshape(den.shape[0], den.shape[1], LANES) / den
    return [jnp.where(low, o[2 * c], o[2 * c + 1]) for c in range(SWA_Q_W // LANES)]


def _sink_column(sink_ref, layer):
    idx = lax.broadcasted_iota(jnp.int32, (SWA_Q_HEADS, 1, 1), 0)
    col = jnp.zeros((SWA_Q_HEADS, 1, 1), F32)
    for p, head in enumerate(SWA_HEAD_ORDER):
        col = jnp.where(idx == p, sink_ref[layer, head] * LOG2E, col)
    return col


def _swa_prompt_kernel(sink_ref, q_ref, kp_ref, kc_ref, vp_ref, vc_ref, o_ref, *, layer):
    j = pl.program_id(1)
    kall = jnp.concatenate([kp_ref[...], kc_ref[...]], axis=0).astype(BF16)
    vall = jnp.concatenate([vp_ref[...], vc_ref[...]], axis=0).astype(BF16)
    sinks = _sink_column(sink_ref, layer)
    iq = lax.broadcasted_iota(jnp.int32, (WINDOW, 2 * WINDOW), 0)
    jk = lax.broadcasted_iota(jnp.int32, (WINDOW, 2 * WINDOW), 1)
    rel = WINDOW + iq - jk
    band = (rel >= 0) & (rel < WINDOW)
    windows = range(q_ref.shape[0] // WINDOW)
    rows = lambda w: slice(w * WINDOW, (w + 1) * WINDOW)
    keys = lambda w: slice(w * WINDOW, (w + 2) * WINDOW)
    scores = [_swa_scores(q_ref[rows(w), :], kall[keys(w)]) for w in windows]
    probs = [_swa_probs(scores[w], band if w > 0 else band & ((j > 0) | (jk >= WINDOW)), sinks)
             for w in windows]
    for w in windows:
        for c, col in enumerate(_swa_out(*probs[w], vall[keys(w)])):
            o_ref[rows(w), c * LANES:(c + 1) * LANES] = col.astype(BF16)


def _swa_prompt(sinks, layer, q, k, v, batch, seq):
    qb = min(SWA_BLOCK, seq)
    nb = seq // qb
    per = qb // WINDOW
    cur = lambda b, j: (b * nb + j, 0)
    prev = lambda b, j: ((b * nb + j) * per - jnp.minimum(j, 1), 0)
    return pl.pallas_call(
        functools.partial(_swa_prompt_kernel, layer=layer),
        grid=(batch, nb),
        in_specs=[
            pl.BlockSpec(memory_space=pltpu.SMEM),
            pl.BlockSpec((qb, SWA_Q_W), cur),
            pl.BlockSpec((WINDOW, SWA_KV_W), prev),
            pl.BlockSpec((qb, SWA_KV_W), cur),
            pl.BlockSpec((WINDOW, SWA_KV_W), prev),
            pl.BlockSpec((qb, SWA_KV_W), cur),
        ],
        out_specs=pl.BlockSpec((qb, SWA_Q_W), cur),
        out_shape=jax.ShapeDtypeStruct((batch * seq, SWA_Q_W), BF16),
        compiler_params=_params("parallel", "parallel"),
        name="swa_prompt",
    )(sinks, q, k, k, v, v)


def _layer_view(ref, layer, first):
    if not first:
        return ref
    for other in range(ref.shape[0]):
        if other != layer:
            ref[other] = jnp.zeros(ref.shape[1:], ref.dtype)
    return ref.at[layer]


def _swa_decode_kernel(sink_ref, q_ref, kn_ref, vn_ref, kc_ref, vc_ref, *rest, steps, layer):
    o_ref, ko_ref, vo_ref = rest[-3:]
    first = len(rest) == 3
    ko_ref = _layer_view(ko_ref, layer, first)
    vo_ref = _layer_view(vo_ref, layer, first)
    n_seq = kc_ref.shape[0]
    grp = DEC_GROUP
    rows = grp * steps
    keep = WINDOW - steps
    for s in range(n_seq):
        new = slice(s * steps, (s + 1) * steps)
        ko_ref[s, 0:keep, :] = kc_ref[s, steps:WINDOW, :]
        ko_ref[s, keep:WINDOW, :] = kn_ref[new, :]
        vo_ref[s, 0:keep, :] = vc_ref[s, steps:WINDOW, :]
        vo_ref[s, keep:WINDOW, :] = vn_ref[new, :]
    nk = grp * WINDOW + rows
    r = lax.broadcasted_iota(jnp.int32, (rows, nk), 0)
    q_seq, q_step = r // steps, r % steps
    c = lax.broadcasted_iota(jnp.int32, (rows, nk), 1)
    is_new = c >= grp * WINDOW
    cn = c - grp * WINDOW
    k_seq = jnp.where(is_new, cn // steps, c // WINDOW)
    k_idx = jnp.where(is_new, WINDOW + cn % steps, c % WINDOW)
    rel = WINDOW + q_step - k_idx
    valid = (q_seq == k_seq) & (rel >= 0) & (rel < WINDOW)
    sinks = _sink_column(sink_ref, layer)
    groups = range(n_seq // grp)
    new_rows = lambda g: slice(g * rows, (g + 1) * rows)

    def keys(cache_ref, new_ref, g):
        return jnp.concatenate([cache_ref[g * grp + s] for s in range(grp)] + [new_ref[new_rows(g), :]],
                               axis=0).astype(BF16)

    scores = [_swa_scores(q_ref[new_rows(g), :], keys(kc_ref, kn_ref, g)) for g in groups]
    probs = [_swa_probs(scores[g], valid, sinks) for g in groups]
    for g in groups:
        for c, col in enumerate(_swa_out(*probs[g], keys(vc_ref, vn_ref, g))):
            o_ref[new_rows(g), c * LANES:(c + 1) * LANES] = col.astype(BF16)


def _stacked_out(shape, block, layer, prev):
    tail = (0,) * (len(block) - 1)
    if prev is None:
        spec = pl.BlockSpec((shape[0],) + block, lambda i: (0, i) + tail)
        extra_inputs = []
    else:
        spec = pl.BlockSpec((None,) + block, lambda i: (layer, i) + tail)
        extra_inputs = list(prev)
    extra_specs = [pl.BlockSpec(memory_space=pl.ANY) for _ in extra_inputs]
    return extra_inputs, extra_specs, spec, jax.ShapeDtypeStruct(shape, F32)


def _swa_decode(sinks, layer, q, kn, vn, kc, vc, steps, prev):
    depth, nseq = kc.shape[:2]
    grp = min(DEC_GROUP * SWA_DEC_SUBGROUPS, nseq)
    rows = grp * steps
    row = lambda i: (i, 0)
    cache = lambda i: (layer, i, 0, 0)
    cache_spec = pl.BlockSpec((None, grp, WINDOW, SWA_KV_W), cache)
    extra_in, extra_specs, out_spec, stacked = _stacked_out(kc.shape, (grp, WINDOW, SWA_KV_W), layer, prev)
    n_in = 6
    return pl.pallas_call(
        functools.partial(_swa_decode_kernel, steps=steps, layer=layer),
        grid=(nseq // grp,),
        in_specs=[
            pl.BlockSpec(memory_space=pltpu.SMEM),
            pl.BlockSpec((rows, SWA_Q_W), row),
            pl.BlockSpec((rows, SWA_KV_W), row),
            pl.BlockSpec((rows, SWA_KV_W), row),
            cache_spec,
            cache_spec,
        ] + extra_specs,
        out_specs=[pl.BlockSpec((rows, SWA_Q_W), row), out_spec, out_spec],
        out_shape=[jax.ShapeDtypeStruct((nseq * steps, SWA_Q_W), BF16), stacked, stacked],
        input_output_aliases={n_in + i: 1 + i for i in range(len(extra_in))},
        compiler_params=_params("parallel"),
        name="swa_decode",
    )(sinks, q, kn, vn, kc, vc, *extra_in)


def _gla_out(o, gain, gate):
    return _rms(o, gain) * (gate * jax.nn.sigmoid(gate))


def _head_stack(x, width):
    return jnp.concatenate([x[:, h * width:(h + 1) * width] for h in range(GLA_HEADS)], axis=0)


def _head_masked_stack(x, head_of_lane):
    zero = jnp.zeros((), x.dtype)
    return jnp.concatenate([jnp.where(head_of_lane == h, x, zero) for h in range(GLA_HEADS)], axis=0)


def _gla_prompt_kernel(q_ref, k_ref, la_ref, v_ref, gg_ref, gn_ref, tril_ref, o_ref, s_ref, st_scr):
    tb = pl.program_id(1)

    @pl.when(tb == 0)
    def _():
        st_scr[...] = jnp.zeros_like(st_scr)

    n_tok = q_ref.shape[1]
    c_len = min(GLA_CHUNK, n_tok)
    tril = tril_ref[...]
    head_of_lane = lax.broadcasted_iota(jnp.int32, (1, GLA_K_W), 1) // GLA_DK
    ri = lax.broadcasted_iota(jnp.int32, (GLA_HEADS * c_len, c_len), 0) % c_len
    ci = lax.broadcasted_iota(jnp.int32, (GLA_HEADS * c_len, c_len), 1)
    causal = ri >= ci
    gain = gn_ref[...]
    n_seq = q_ref.shape[0]
    items = [(i, c) for i in range(n_seq) for c in range(n_tok // c_len)]
    rows = lambda c: slice(c * c_len, (c + 1) * c_len)
    b_all = []
    for i in range(n_seq):
        g_hi, g_lo = _split_bf16(la_ref[i])
        b_all.append(_dot(tril, g_hi) + _dot(tril, g_lo))
    qm, kdm, dec, a_raw = {}, {}, {}, {}
    for it in items:
        i, c = it
        b = b_all[i][rows(c), :]
        dec[it] = jnp.exp(b[c_len - 1:c_len, :])
        q_t = q_ref[i, rows(c), :] * jnp.exp(b)
        k_t = k_ref[i, rows(c), :] * jnp.exp(-b)
        qm[it] = _head_masked_stack(q_t, head_of_lane).astype(BF16)
        kdm[it] = _head_masked_stack(k_t * dec[it], head_of_lane).astype(BF16)
        a_raw[it] = _dot_tb(qm[it], k_t.astype(BF16))
    upd = {it: _dot_ta(_head_stack(v_ref[it[0], rows(it[1]), :], GLA_DV), kdm[it]) for it in items}
    intra = {}
    for it in items:
        i, c = it
        a = jnp.where(causal, a_raw[it], 0.0).astype(BF16)
        intra[it] = jnp.concatenate(
            [_dot(a[h * c_len:(h + 1) * c_len, :], v_ref[i, rows(c), h * GLA_DV:(h + 1) * GLA_DV])
             for h in range(GLA_HEADS)], axis=0)
    inter = {}
    for i in range(n_seq):
        st = st_scr[i]
        for c in range(n_tok // c_len):
            inter[(i, c)] = _dot_tb(qm[(i, c)], st.astype(BF16))
            st = dec[(i, c)] * st + upd[(i, c)]
        st_scr[i] = st
    for it in items:
        i, c = it
        y = _gla_out(inter[it] + intra[it], gain, _head_stack(gg_ref[i, rows(c), :], GLA_DV)).astype(BF16)
        for h in range(GLA_HEADS):
            o_ref[i, rows(c), h * GLA_DV:(h + 1) * GLA_DV] = y[h * c_len:(h + 1) * c_len, :]

    @pl.when(tb == pl.num_programs(1) - 1)
    def _():
        for i in range(q_ref.shape[0]):
            s_ref[i] = st_scr[i].T


def _gla_prompt(layer, q, k, la, v, gg, gain, tril, batch, seq):
    tb = tril.shape[0]
    per = min(GLA_SEQS, batch)
    blk = lambda b, t: (b, t, 0)
    kw = pl.BlockSpec((per, tb, GLA_K_W), blk)
    vw = pl.BlockSpec((per, tb, GLA_V_W), blk)
    return pl.pallas_call(
        _gla_prompt_kernel,
        grid=(batch // per, seq // tb),
        in_specs=[kw, kw, kw, vw, vw, _layer_resident(gain, layer), _resident(tril)],
        out_specs=[vw, pl.BlockSpec((per, GLA_K_W, GLA_DV), lambda b, t: (b, 0, 0))],
        out_shape=[
            jax.ShapeDtypeStruct((batch, seq, GLA_V_W), BF16),
            jax.ShapeDtypeStruct((batch, GLA_K_W, GLA_DV), F32),
        ],
        scratch_shapes=[pltpu.VMEM((per, GLA_DV, GLA_K_W), F32)],
        compiler_params=_params("parallel", "arbitrary"),
        name="gla_prompt",
    )(q, k, la, v, gg, gain, tril)


def _gla_decode_kernel(q_ref, k_ref, la_ref, v_ref, gg_ref, gn_ref, s_ref, *rest, steps, layer):
    o_ref, so_ref = rest[-2:]
    so_ref = _layer_view(so_ref, layer, len(rest) == 2)
    grp = s_ref.shape[0]
    rows = grp * steps
    stacked = GLA_HEADS * rows
    ri = lax.broadcasted_iota(jnp.int32, (rows, rows), 0)
    ci = lax.broadcasted_iota(jnp.int32, (rows, rows), 1)
    same_seq = ri // steps == ci // steps
    g_hi, g_lo = _split_bf16(la_ref[...])
    tril = (same_seq & (ri >= ci)).astype(BF16)
    total = same_seq.astype(BF16)
    b = _dot(tril, g_hi) + _dot(tril, g_lo)
    b_last = _dot(total, g_hi) + _dot(total, g_lo)
    head_of_lane = lax.broadcasted_iota(jnp.int32, (1, GLA_K_W), 1) // GLA_DK
    k_t = k_ref[...] * jnp.exp(-b)
    qm = _head_masked_stack(q_ref[...] * jnp.exp(b), head_of_lane).astype(BF16)
    km = _head_masked_stack(k_t, head_of_lane).astype(BF16)
    kdm = _head_masked_stack(k_t * jnp.exp(b_last), head_of_lane).astype(BF16)
    v_st = _head_stack(v_ref[...], GLA_DV)
    seq_of_row = (lax.broadcasted_iota(jnp.int32, (stacked, 1), 0) % rows) // steps
    seq_of_g = lax.broadcasted_iota(jnp.int32, (rows, 1), 0) // steps
    zero = jnp.zeros((), BF16)
    rhs = jnp.concatenate([
        jnp.concatenate([v_st, jnp.zeros((stacked, GLA_DV), BF16)], axis=1),
        jnp.concatenate([jnp.zeros((2 * rows, GLA_DV), BF16), jnp.ones((2 * rows, GLA_DV), BF16)], axis=1),
    ], axis=0)
    a_raw = _dot_tb(qm, km)
    states = [s_ref[s].reshape(GLA_K_W, GLA_DV) for s in range(grp)]
    inter_all = [_dot(qm, st.astype(BF16)) for st in states]
    upd_all = []
    for s in range(grp):
        lhs = jnp.concatenate([jnp.where(seq_of_row == s, kdm, zero),
                               jnp.where(seq_of_g == s, g_hi, zero),
                               jnp.where(seq_of_g == s, g_lo, zero)], axis=0)
        upd_all.append(_dot_ta(lhs, rhs))
    rr = lax.broadcasted_iota(jnp.int32, (stacked, stacked), 0) % rows
    cc = lax.broadcasted_iota(jnp.int32, (stacked, stacked), 1) % rows
    causal = (rr // steps == cc // steps) & (rr >= cc)
    o = _dot(jnp.where(causal, a_raw, 0.0).astype(BF16), v_st)
    inter = inter_all[0]
    for s in range(1, grp):
        inter = jnp.where(seq_of_row == s, inter_all[s], inter)
    y = _gla_out(o + inter, gn_ref[...], _head_stack(gg_ref[...], GLA_DV)).astype(BF16)
    for h in range(GLA_HEADS):
        o_ref[:, h * GLA_DV:(h + 1) * GLA_DV] = y[h * rows:(h + 1) * rows, :]
    for s in range(grp):
        new = jnp.exp(upd_all[s][:, GLA_DV:]) * states[s] + upd_all[s][:, :GLA_DV]
        so_ref[s] = new.reshape(GLA_HEADS, GLA_DK, GLA_DV)


def _gla_decode(layer, q, k, la, v, gg, gain, state, steps, prev):
    nseq = state.shape[1]
    grp = DEC_GROUP
    rows = grp * steps
    row = lambda i: (i, 0)
    st_spec = pl.BlockSpec((None, grp, GLA_HEADS, GLA_DK, GLA_DV), lambda i: (layer, i, 0, 0, 0))
    extra_in, extra_specs, out_spec, stacked = _stacked_out(
        state.shape, (grp, GLA_HEADS, GLA_DK, GLA_DV), layer, prev)
    n_in = 7
    return pl.pallas_call(
        functools.partial(_gla_decode_kernel, steps=steps, layer=layer),
        grid=(nseq // grp,),
        in_specs=[
            pl.BlockSpec((rows, GLA_K_W), row),
            pl.BlockSpec((rows, GLA_K_W), row),
            pl.BlockSpec((rows, GLA_K_W), row),
            pl.BlockSpec((rows, GLA_V_W), row),
            pl.BlockSpec((rows, GLA_V_W), row),
            _layer_resident(gain, layer),
            st_spec,
        ] + extra_specs,
        out_specs=[pl.BlockSpec((rows, GLA_V_W), row), out_spec],
        out_shape=[jax.ShapeDtypeStruct((nseq * steps, GLA_V_W), BF16), stacked],
        input_output_aliases={n_in + i: 1 + i for i in range(len(extra_in))},
        compiler_params=_params("parallel"),
        name="gla_decode",
    )(q, k, la, v, gg, gain, state, *extra_in)


def _outproj_apply(x, a_ref, o_ref, wa_ref, wo_ref, g_ref, wq_ref, qn_ref):
    x1 = x + _dot(a_ref[...], wa_ref[...]) + _dot(o_ref[...], wo_ref[...])
    h = _rms(x1, g_ref[...]).astype(BF16)
    qn = qn_ref[...]
    q = [_rms(_dot(h, wq_ref[:, hd * XA_HEAD_DIM:(hd + 1) * XA_HEAD_DIM]), qn).astype(BF16)
         for hd in range(XA_HEADS)]
    return x1, q


def _outproj_kernel(x_ref, a_ref, o_ref, wa_ref, wo_ref, g_ref, wq_ref, qn_ref, x1_ref, q_ref):
    x1, q = _outproj_apply(x_ref[...], a_ref, o_ref, wa_ref, wo_ref, g_ref, wq_ref, qn_ref)
    x1_ref[...] = x1
    for hd in range(XA_HEADS):
        q_ref[:, hd * XA_HEAD_DIM:(hd + 1) * XA_HEAD_DIM] = q[hd]


def _outproj(x, layer, a, o, w_a, w_o, g, wq, qn):
    n, d = x.shape
    tm = min(ROW_TILE, n)
    row = lambda i: (i, 0)
    return pl.pallas_call(
        _outproj_kernel,
        grid=(n // tm,),
        in_specs=[
            pl.BlockSpec((tm, d), row),
            pl.BlockSpec((tm, SWA_Q_W), row),
            pl.BlockSpec((tm, GLA_V_W), row),
        ] + [_layer_resident(p, layer) for p in (w_a, w_o, g, wq, qn)],
        out_specs=[pl.BlockSpec((tm, d), row), pl.BlockSpec((tm, XA_W), row)],
        out_shape=[jax.ShapeDtypeStruct((n, d), F32), jax.ShapeDtypeStruct((n, XA_W), BF16)],
        compiler_params=_params("parallel"),
        name="outproj",
    )(x, a, o, w_a, w_o, g, wq, qn)


def _memkv_kernel(m_ref, g_ref, wk_ref, wv_ref, kn_ref, k_ref, v_ref):
    m = _rms(m_ref[...], g_ref[...]).astype(BF16)
    kn = kn_ref[...]
    for hd in range(XA_HEADS):
        sl = slice(hd * XA_HEAD_DIM, (hd + 1) * XA_HEAD_DIM)
        k_ref[:, sl] = _rms(_dot(m, wk_ref[:, sl]), kn)
    v_ref[...] = _dot(m, wv_ref[...])


def _memkv(mem, g, wk, wv, kn):
    depth = wk.shape[0]
    n, d = mem.shape
    tm = min(ROW_TILE, n)
    per_layer = lambda l, i: (l, 0, 0)
    out = lambda l, i: (l, i, 0)
    return pl.pallas_call(
        _memkv_kernel,
        grid=(depth, n // tm),
        in_specs=[
            pl.BlockSpec((tm, d), lambda l, i: (i, 0)),
            pl.BlockSpec((None, 1, d), per_layer),
            pl.BlockSpec((None, d, XA_W), per_layer),
            pl.BlockSpec((None, d, XA_W), per_layer),
            pl.BlockSpec((None, 1, XA_HEAD_DIM), per_layer),
        ],
        out_specs=[pl.BlockSpec((None, tm, XA_W), out), pl.BlockSpec((None, tm, XA_W), out)],
        out_shape=[jax.ShapeDtypeStruct((depth, n, XA_W), F32)] * 2,
        compiler_params=_params("parallel", "parallel"),
        name="memkv",
    )(mem, g, wk, wv, kn)


def _mixout_prompt_kernel(x_ref, a_ref, o_ref, wa_ref, wo_ref, g_ref, wq_ref, qn_ref, mk_ref, mv_ref,
                          xwo_ref, fg_ref, wg_ref, wu_ref, wd_ref, out_ref, att_scr, a_scr):
    x1, q = _outproj_apply(x_ref[...], a_ref, o_ref, wa_ref, wo_ref, g_ref, wq_ref, qn_ref)
    mk = mk_ref[...].astype(BF16)
    mv = mv_ref[...].astype(BF16)
    heads = range(XA_HEADS)
    cols = lambda hd: slice(hd * XA_HEAD_DIM, (hd + 1) * XA_HEAD_DIM)
    scores = [_dot_tb(q[hd], mk[:, cols(hd)]) * (XA_HEAD_DIM ** -0.5 * LOG2E) for hd in heads]
    probs = []
    for s in scores:
        e = jnp.exp2(s - jnp.max(s, axis=-1, keepdims=True))
        probs.append((e.astype(BF16), jnp.sum(e, axis=-1, keepdims=True)))
    for hd in heads:
        att_scr[:, cols(hd)] = (_dot(probs[hd][0], mv[:, cols(hd)]) / probs[hd][1]).astype(BF16)
    x2 = x1 + _dot(att_scr[...], xwo_ref[...])
    out_ref[...] = _ffn_apply(x2, fg_ref, wg_ref, wu_ref, wd_ref, a_scr)


def _mixout_prompt(x, layer, a, o, outproj, mk, mv, wo, ffn, seq):
    n, d = x.shape
    dff = ffn[1].shape[2]
    tm = min(ROW_TILE, seq)
    per_seq = seq // tm
    row = lambda i: (i, 0)
    mem_spec = pl.BlockSpec((None, None, mk.shape[2], XA_W), lambda i: (layer, i // per_seq, 0, 0))
    return pl.pallas_call(
        _mixout_prompt_kernel,
        grid=(n // tm,),
        in_specs=([pl.BlockSpec((tm, d), row), pl.BlockSpec((tm, SWA_Q_W), row), pl.BlockSpec((tm, GLA_V_W), row)]
                  + [_layer_resident(p, layer) for p in outproj]
                  + [mem_spec, mem_spec]
                  + [_layer_resident(p, layer) for p in (wo,) + ffn]),
        out_specs=pl.BlockSpec((tm, d), row),
        out_shape=jax.ShapeDtypeStruct((n, d), F32),
        scratch_shapes=[pltpu.VMEM((tm, XA_W), BF16), pltpu.VMEM((tm, dff), BF16)],
        compiler_params=_params("parallel"),
        name="mixout_prompt",
    )(x, a, o, *outproj, mk, mv, wo, *ffn)


def _xattn_decode_kernel(q_ref, mk_ref, mv_ref, o_ref, *, steps):
    grp, nkeys = mk_ref.shape[0], mk_ref.shape[1]
    rows = grp * steps
    q = jnp.concatenate([q_ref[:, hd * XA_HEAD_DIM:(hd + 1) * XA_HEAD_DIM] for hd in range(XA_HEADS)], axis=0)
    r = lax.broadcasted_iota(jnp.int32, (XA_HEADS * rows, 1), 0)
    own = (r % rows) // steps
    same_head = (r // rows) == (lax.broadcasted_iota(jnp.int32, (1, nkeys), 1) % XA_HEADS)
    s = None
    for j in range(grp):
        sj = _dot_tb(q, mk_ref[j].astype(BF16))
        s = sj if s is None else jnp.where(own == j, sj, s)
    s = jnp.where(same_head, s * (XA_HEAD_DIM ** -0.5 * LOG2E), -jnp.inf)
    m = jnp.max(s, axis=-1, keepdims=True)
    e = jnp.exp2(s - m)
    p = e.astype(BF16)
    o = None
    for j in range(grp):
        oj = _dot(p, mv_ref[j].astype(BF16))
        o = oj if o is None else jnp.where(own == j, oj, o)
    o = o / jnp.sum(e, axis=-1, keepdims=True)
    for hd in range(XA_HEADS):
        o_ref[:, hd * XA_HEAD_DIM:(hd + 1) * XA_HEAD_DIM] = o[hd * rows:(hd + 1) * rows, :].astype(BF16)


def _xattn_decode(q, layer, mk, mv, steps):
    nseq, nkeys = mk.shape[1], mk.shape[2]
    grp = XA_DEC_GROUP
    rows = grp * steps
    row = lambda i: (i, 0)
    mem_spec = pl.BlockSpec((None, grp, nkeys, XA_HEAD_DIM), lambda i: (layer, i, 0, 0))
    return pl.pallas_call(
        functools.partial(_xattn_decode_kernel, steps=steps),
        grid=(nseq // grp,),
        in_specs=[pl.BlockSpec((rows, XA_W), row), mem_spec, mem_spec],
        out_specs=pl.BlockSpec((rows, XA_W), row),
        out_shape=jax.ShapeDtypeStruct((nseq * steps, XA_W), BF16),
        compiler_params=_params("parallel"),
        name="xattn_decode",
    )(q, mk, mv)


def _proj_res_kernel(x_ref, a_ref, w_ref, o_ref):
    o_ref[...] = x_ref[...] + _dot(a_ref[...], w_ref[...])


def _proj_res(x, layer, a, w):
    n, d = x.shape
    tm = min(ROW_TILE, n)
    row = lambda i: (i, 0)
    return pl.pallas_call(
        _proj_res_kernel,
        grid=(n // tm,),
        in_specs=[pl.BlockSpec((tm, d), row), pl.BlockSpec((tm, a.shape[1]), row), _layer_resident(w, layer)],
        out_specs=pl.BlockSpec((tm, d), row),
        out_shape=jax.ShapeDtypeStruct((n, d), F32),
        compiler_params=_params("parallel"),
        name="proj_res",
    )(x, a, w)


def _rope_tables(pos):
    half = HEAD_DIM // 2
    inv = ROPE_THETA ** (-jnp.arange(half, dtype=F32) / half)
    ang = pos.astype(F32)[:, None] * inv[None, :]
    cos, sin = jnp.cos(ang), jnp.sin(ang)
    reps = LANES // HEAD_DIM
    return (jnp.concatenate([cos, cos] * reps, axis=-1),
            jnp.concatenate([-sin, sin] * reps, axis=-1))


def _block_tril(n_blocks, size):
    i = jnp.arange(n_blocks * size)
    return ((i[:, None] // size == i[None, :] // size) & (i[:, None] >= i[None, :])).astype(BF16)


def _permute_heads(w, axis):
    blocks = jnp.split(w, SWA_Q_HEADS, axis=axis)
    return jnp.concatenate([blocks[h] for h in SWA_HEAD_ORDER], axis=axis)


def kernel(x_prompt, x_sample, cache_swa_k, cache_swa_v, state_gla, cache_mem_k, cache_mem_v, mem_prompt, ffn1_norm, ffn1_wg, ffn1_wu, ffn1_wd, mix_norm, w_in, swa_q_norm, swa_k_norm, swa_sinks, gla_w_gate, gla_b_gate, gla_out_norm, w_out, xa_norm, mem_norm, xa_wq, xa_wk, xa_wv, xa_q_norm, xa_k_norm, xa_wo, ffn2_norm, ffn2_wg, ffn2_wu, ffn2_wd):
    batch, seq, d = x_prompt.shape
    nseq, steps, _ = x_sample.shape
    depth = w_in.shape[0]
    mem_len = mem_prompt.shape[1]

    bf = lambda w: w.astype(BF16)
    vec = lambda p: p[:, None, :]
    ffn1 = (vec(ffn1_norm), bf(ffn1_wg), bf(ffn1_wu), bf(ffn1_wd))
    ffn2 = (vec(ffn2_norm), bf(ffn2_wg), bf(ffn2_wu), bf(ffn2_wd))
    o_qg = SWA_Q_W + 2 * SWA_KV_W
    w_all = bf(jnp.concatenate([
        _permute_heads(w_in[:, :, :SWA_Q_W], 2),
        w_in[:, :, SWA_Q_W:o_qg],
        w_in[:, :, o_qg:o_qg + GLA_K_W] * (GLA_DK ** -0.5),
        w_in[:, :, o_qg + GLA_K_W:],
        jnp.zeros((depth, d, MXU_TILE - GLA_LOWRANK), F32)], axis=2))
    gate_w = jnp.pad(bf(gla_w_gate), ((0, 0), (0, LANES - GLA_LOWRANK), (0, 0)))
    w_a = bf(_permute_heads(w_out[:, :SWA_Q_W], 1))
    w_o = bf(w_out[:, SWA_Q_W:])
    wq_b, wk_b, wv_b, wo_b = bf(xa_wq), bf(xa_wk), bf(xa_wv), bf(xa_wo)
    qn = vec(jnp.tile(swa_q_norm * (HEAD_DIM ** -0.5 * LOG2E), (1, LANES // HEAD_DIM)))
    kn = vec(jnp.tile(swa_k_norm, (1, LANES // HEAD_DIM)))
    inproj_params = (vec(mix_norm), w_all, gate_w, vec(gla_b_gate), qn, kn)
    outproj_params = (w_a, w_o, vec(xa_norm), wq_b, vec(xa_q_norm))
    gla_gain = vec(gla_out_norm)

    lane = jnp.arange(MXU_TILE)
    bd = (lane[:, None] // HEAD_DIM == lane[None, :] // HEAD_DIM).astype(BF16)
    cos_p, sin_p = _rope_tables(jnp.arange(seq))
    cos_s, sin_s = _rope_tables(PAST_LEN + jnp.arange(nseq * steps) % steps)
    tril_p = _block_tril(min(GLA_BLOCK, seq) // min(GLA_CHUNK, seq), min(GLA_CHUNK, seq))

    mk_p, mv_p = _memkv(mem_prompt.reshape(batch * mem_len, d), vec(mem_norm), wk_b, wv_b, vec(xa_k_norm))
    mk_p = mk_p.reshape(depth, batch, mem_len, XA_W)
    mv_p = mv_p.reshape(depth, batch, mem_len, XA_W)
    mk_s = cache_mem_k.reshape(depth, nseq, mem_len * XA_HEADS, XA_HEAD_DIM)
    mv_s = cache_mem_v.reshape(depth, nseq, mem_len * XA_HEADS, XA_HEAD_DIM)
    kc_s = cache_swa_k.reshape(depth, nseq, WINDOW, SWA_KV_W)
    vc_s = cache_swa_v.reshape(depth, nseq, WINDOW, SWA_KV_W)

    xp = x_prompt.reshape(batch * seq, d)
    xs = x_sample.reshape(nseq * steps, d)
    kp_l, vp_l, sp_l = [], [], []
    swa_new = gla_new = None
    for l in range(depth):
        xp, q_s, k_s, v_s, q_g, k_g, v_g, g_g, la = _ffn_inproj(xp, l, ffn1, inproj_params, cos_p, sin_p, bd)
        a_p = _swa_prompt(swa_sinks, l, q_s, k_s, v_s, batch, seq)
        seqs = lambda a: a.reshape(batch, seq, a.shape[-1])
        o_p, s_p = _gla_prompt(l, seqs(q_g), seqs(k_g), seqs(la), seqs(v_g), seqs(g_g), gla_gain, tril_p,
                               batch, seq)
        o_p = o_p.reshape(batch * seq, GLA_V_W)
        last = lambda a: seqs(a)[:, seq - WINDOW:].reshape(batch, WINDOW, SWA_KV_HEADS, HEAD_DIM)
        kp_l.append(last(k_s))
        vp_l.append(last(v_s))
        sp_l.append(s_p.reshape(batch, GLA_HEADS, GLA_DK, GLA_DV))
        xp = _mixout_prompt(xp, l, a_p, o_p, outproj_params, mk_p, mv_p, wo_b, ffn2, seq)

        xs, q_s, k_s, v_s, q_g, k_g, v_g, g_g, la = _ffn_inproj(xs, l, ffn1, inproj_params, cos_s, sin_s, bd)
        a_s, *swa_new = _swa_decode(swa_sinks, l, q_s, k_s, v_s, kc_s, vc_s, steps, swa_new)
        o_s, *gla_new = _gla_decode(l, q_g, k_g, la, v_g, g_g, gla_gain, state_gla, steps, gla_new)
        xs, q_x = _outproj(xs, l, a_s, o_s, *outproj_params)
        xs = _proj_res(xs, l, _xattn_decode(q_x, l, mk_s, mv_s, steps), wo_b)
        xs = _ffn(xs, l, *ffn2)

    new_shape = (depth, nseq, WINDOW, SWA_KV_HEADS, HEAD_DIM)
    return (xp.reshape(batch, seq, d), xs.reshape(nseq, steps, d),
            jnp.stack(kp_l), jnp.stack(vp_l), jnp.stack(sp_l),
            mk_p.reshape(depth, batch, mem_len, XA_HEADS, XA_HEAD_DIM),
            mv_p.reshape(depth, batch, mem_len, XA_HEADS, XA_HEAD_DIM),
            swa_new[0].reshape(new_shape), swa_new[1].reshape(new_shape), gla_new[0])
```

```python
import functools

import jax
import jax.numpy as jnp
from jax import lax
from jax.experimental import pallas as pl
from jax.experimental.pallas import tpu as pltpu

F32 = jnp.float32
BF16 = jnp.bfloat16

EPS = 1e-6
LOG2E = 1.4426950408889634
PAST_LEN = 16384
WINDOW = 128
ROPE_THETA = 10000.0
HEAD_DIM = 64
SWA_Q_HEADS = 8
SWA_KV_HEADS = 2
SWA_GROUP = SWA_Q_HEADS // SWA_KV_HEADS
GLA_HEADS = 4
GLA_DK = 64
GLA_DV = 128
GLA_LOWRANK = 16
GLA_GATE_TEMP = 16.0
GLA_CHUNK = 64
XA_HEADS = 4
XA_HEAD_DIM = 128

SWA_Q_W = SWA_Q_HEADS * HEAD_DIM
SWA_KV_W = SWA_KV_HEADS * HEAD_DIM
GLA_K_W = GLA_HEADS * GLA_DK
GLA_V_W = GLA_HEADS * GLA_DV
XA_W = XA_HEADS * XA_HEAD_DIM
MAIN_W = SWA_Q_W + 2 * SWA_KV_W + 2 * GLA_K_W + 2 * GLA_V_W

LANES = 128
MXU_TILE = 256
VMEM_LIMIT = 56 * 1024 * 1024

ROW_TILE = 512
FFN_CHUNK = 256
SWA_BLOCK = 256
GLA_BLOCK = 256
GLA_SEQS = 4
DEC_GROUP = 4
SWA_DEC_SUBGROUPS = 8
XA_DEC_GROUP = 8
GLA_DEC_GROUP = 8

assert SWA_KV_HEADS * HEAD_DIM == LANES
SWA_HEAD_ORDER = tuple(kv * SWA_GROUP + g for g in range(SWA_GROUP) for kv in range(SWA_KV_HEADS))


def _dot(a, b):
    return jnp.dot(a, b, preferred_element_type=F32)


def _dot_tb(a, b):
    return lax.dot_general(a, b, (((1,), (1,)), ((), ())), preferred_element_type=F32)


def _dot_ta(a, b):
    return lax.dot_general(a, b, (((0,), (0,)), ((), ())), preferred_element_type=F32)


def _split_bf16(x):
    hi = x.astype(BF16)
    lo = (x - hi.astype(F32)).astype(BF16)
    return hi, lo


def _rms(x, g):
    ms = jnp.mean(x * x, axis=-1, keepdims=True)
    return x * lax.rsqrt(ms + EPS) * g


def _params(*sem):
    return pltpu.CompilerParams(dimension_semantics=sem, vmem_limit_bytes=VMEM_LIMIT)


def _resident(arr):
    nd = arr.ndim
    return pl.BlockSpec(arr.shape, lambda *_: (0,) * nd, pipeline_mode=pl.Buffered(1))


def _layer_resident(arr, layer):
    nd = arr.ndim
    return pl.BlockSpec((None,) + arr.shape[1:], lambda *_: (layer,) + (0,) * (nd - 1),
                        pipeline_mode=pl.Buffered(1))


def _ffn_apply(x, g_ref, wg_ref, wu_ref, wd_ref, a_scr):
    h = _rms(x, g_ref[...]).astype(BF16)
    dff = wg_ref.shape[1]
    for c in range(dff // FFN_CHUNK):
        sl = slice(c * FFN_CHUNK, (c + 1) * FFN_CHUNK)
        g = _dot(h, wg_ref[:, sl])
        u = _dot(h, wu_ref[:, sl])
        a_scr[:, sl] = (g * jax.nn.sigmoid(g) * u).astype(BF16)
    return x + 0.5 * _dot(a_scr[...], wd_ref[...])


def _ffn_kernel(x_ref, g_ref, wg_ref, wu_ref, wd_ref, o_ref, a_scr):
    o_ref[...] = _ffn_apply(x_ref[...], g_ref, wg_ref, wu_ref, wd_ref, a_scr)


def _ffn(x, layer, g, wg, wu, wd):
    n, d = x.shape
    dff = wg.shape[2]
    tm = min(ROW_TILE, n)
    return pl.pallas_call(
        _ffn_kernel,
        grid=(n // tm,),
        in_specs=[pl.BlockSpec((tm, d), lambda i: (i, 0))] + [_layer_resident(a, layer) for a in (g, wg, wu, wd)],
        out_specs=pl.BlockSpec((tm, d), lambda i: (i, 0)),
        out_shape=jax.ShapeDtypeStruct((n, d), F32),
        scratch_shapes=[pltpu.VMEM((tm, dff), BF16)],
        compiler_params=_params("parallel"),
        name="ffn",
    )(x, g, wg, wu, wd)


def _ffn_inproj_kernel(x_ref, fg_ref, wg_ref, wu_ref, wd_ref, *rest):
    xo_ref, a_scr = rest[9], rest[-1]
    x = _ffn_apply(x_ref[...], fg_ref, wg_ref, wu_ref, wd_ref, a_scr)
    xo_ref[...] = x
    _inproj_apply(x, *rest[:9], *rest[10:-1])


def _inproj_apply(x, g_ref, w_ref, gw_ref, gb_ref, qn_ref, kn_ref, cos_ref, sin_ref, bd_ref,
                  qs_ref, ks_ref, vs_ref, qg_ref, kg_ref, vg_ref, gg_ref, la_ref, kt_ref=None, vt_ref=None):
    h = _rms(x, g_ref[...]).astype(BF16)
    bd = bd_ref[...]
    cos = cos_ref[...]
    sin = sin_ref[...]
    lane = lax.broadcasted_iota(jnp.int32, cos.shape, 1)
    lane_lo = (lane % HEAD_DIM) < (HEAD_DIM // 2)

    def head_scale(z):
        ss = _dot((z * z).astype(BF16), bd)
        return lax.rsqrt(ss * (1.0 / HEAD_DIM) + EPS)

    def rope(y):
        swapped = jnp.where(lane_lo, pltpu.roll(y, LANES - HEAD_DIM // 2, axis=1),
                            pltpu.roll(y, HEAD_DIM // 2, axis=1))
        return y * cos + swapped * sin

    split = SWA_Q_W + 2 * SWA_KV_W + 2 * GLA_K_W
    z1 = _dot(h, w_ref[:, :split])
    z2 = _dot(h, w_ref[:, split:])
    qn = qn_ref[...]
    for t in range(SWA_Q_W // MXU_TILE):
        z = z1[:, t * MXU_TILE:(t + 1) * MXU_TILE]
        y = z * head_scale(z)
        for c in range(MXU_TILE // LANES):
            sl = slice(c * LANES, (c + 1) * LANES)
            qs_ref[:, t * MXU_TILE + c * LANES:t * MXU_TILE + (c + 1) * LANES] = rope(y[:, sl] * qn).astype(BF16)
    o = SWA_Q_W
    z = z1[:, o:o + 2 * SWA_KV_W]
    k = rope(z[:, :SWA_KV_W] * head_scale(z)[:, :SWA_KV_W] * kn_ref[...])
    ks_ref[...] = k
    vs_ref[...] = z[:, SWA_KV_W:]
    if kt_ref is not None:
        kt_ref[...] = k.T
        vt_ref[...] = z[:, SWA_KV_W:].T
    o += 2 * SWA_KV_W
    qg_ref[...] = z1[:, o:o + GLA_K_W]
    o += GLA_K_W
    kg_ref[...] = z1[:, o:o + GLA_K_W]
    vg_ref[...] = z2[:, :GLA_V_W].astype(BF16)
    gg_ref[...] = z2[:, GLA_V_W:2 * GLA_V_W]
    lr = z2[:, 2 * GLA_V_W:2 * GLA_V_W + LANES].astype(BF16)
    t = _dot(lr, gw_ref[...]) + gb_ref[...]
    log_sig = jnp.minimum(t, 0.0) - jnp.log(1.0 + jnp.exp(-jnp.abs(t)))
    la_ref[...] = log_sig * (1.0 / GLA_GATE_TEMP)


def _ffn_inproj(x, layer, ffn, inproj, cos, sin, bd, transposed_kv=False):
    n, d = x.shape
    dff = ffn[1].shape[2]
    tm = min(ROW_TILE, n)
    pos_blocks = cos.shape[0] // tm
    row = lambda i: (i, 0)
    pos = lambda i: (i % pos_blocks, 0)
    widths = (d, SWA_Q_W, SWA_KV_W, SWA_KV_W, GLA_K_W, GLA_K_W, GLA_V_W, GLA_V_W, GLA_K_W)
    dtypes = (F32, BF16, F32, F32, F32, F32, BF16, F32, F32)
    out_specs = [pl.BlockSpec((tm, w), row) for w in widths]
    out_shape = [jax.ShapeDtypeStruct((n, w), dt) for w, dt in zip(widths, dtypes)]
    if transposed_kv:
        out_specs += [pl.BlockSpec((SWA_KV_W, tm), lambda i: (0, i))] * 2
        out_shape += [jax.ShapeDtypeStruct((SWA_KV_W, n), F32)] * 2
    return pl.pallas_call(
        _ffn_inproj_kernel,
        grid=(n // tm,),
        in_specs=([pl.BlockSpec((tm, d), row)]
                  + [_layer_resident(a, layer) for a in ffn + inproj]
                  + [pl.BlockSpec((tm, LANES), pos), pl.BlockSpec((tm, LANES), pos), _resident(bd)]),
        out_specs=out_specs,
        out_shape=out_shape,
        scratch_shapes=[pltpu.VMEM((tm, dff), BF16)],
        compiler_params=_params("parallel"),
        name="ffn_inproj",
    )(x, *ffn, *inproj, cos, sin, bd)


def _swa_attend(q, k, v, valid, sinks):
    return _swa_out(*_swa_probs(_swa_scores(q, k), valid, sinks), v)


def _low_half():
    return lax.broadcasted_iota(jnp.int32, (1, LANES), 1) < HEAD_DIM


def _swa_head_rows(q):
    low = _low_half()
    zero = jnp.zeros((), q.dtype)
    pieces = []
    for c in range(SWA_Q_W // LANES):
        qc = q[:, c * LANES:(c + 1) * LANES]
        pieces += [jnp.where(low, qc, zero), jnp.where(low, zero, qc)]
    return jnp.concatenate(pieces, axis=0)


def _swa_scores(q, k):
    return _dot_tb(_swa_head_rows(q), k)


def _swa_probs(s, valid, sinks):
    rows, keys = valid.shape
    s = jnp.where(valid[None], s.reshape(SWA_Q_HEADS, rows, keys), -jnp.inf)
    m = jnp.maximum(jnp.max(s, axis=-1, keepdims=True), sinks)
    e = jnp.exp2(s - m)
    den = jnp.sum(e, axis=-1, keepdims=True) + jnp.exp2(sinks - m)
    return e.astype(BF16).reshape(SWA_Q_HEADS * rows, keys), den


def _swa_out(e, den, v):
    return _swa_finish(_dot(e, v), den)


def _swa_finish(o, den):
    low = _low_half()
    o = o.reshape(den.shape[0], den.shape[1], LANES) / den
    return [jnp.where(low, o[2 * c], o[2 * c + 1]) for c in range(SWA_Q_W // LANES)]


def _sink_column(sink_ref, layer):
    idx = lax.broadcasted_iota(jnp.int32, (SWA_Q_HEADS, 1, 1), 0)
    col = jnp.zeros((SWA_Q_HEADS, 1, 1), F32)
    for p, head in enumerate(SWA_HEAD_ORDER):
        col = jnp.where(idx == p, sink_ref[layer, head] * LOG2E, col)
    return col


def _swa_prompt_kernel(sink_ref, q_ref, kp_ref, kc_ref, vp_ref, vc_ref, o_ref, *, layer):
    j = pl.program_id(1)
    kall = jnp.concatenate([kp_ref[...], kc_ref[...]], axis=0).astype(BF16)
    vall = jnp.concatenate([vp_ref[...], vc_ref[...]], axis=0).astype(BF16)
    sinks = _sink_column(sink_ref, layer)
    iq = lax.broadcasted_iota(jnp.int32, (WINDOW, 2 * WINDOW), 0)
    jk = lax.broadcasted_iota(jnp.int32, (WINDOW, 2 * WINDOW), 1)
    rel = WINDOW + iq - jk
    band = (rel >= 0) & (rel < WINDOW)
    windows = range(q_ref.shape[0] // WINDOW)
    rows = lambda w: slice(w * WINDOW, (w + 1) * WINDOW)
    keys = lambda w: slice(w * WINDOW, (w + 2) * WINDOW)
    scores = [_swa_scores(q_ref[rows(w), :], kall[keys(w)]) for w in windows]
    probs = [_swa_probs(scores[w], band if w > 0 else band & ((j > 0) | (jk >= WINDOW)), sinks)
             for w in windows]
    for w in windows:
        for c, col in enumerate(_swa_out(*probs[w], vall[keys(w)])):
            o_ref[rows(w), c * LANES:(c + 1) * LANES] = col.astype(BF16)


def _swa_prompt(sinks, layer, q, k, v, batch, seq):
    qb = min(SWA_BLOCK, seq)
    nb = seq // qb
    per = qb // WINDOW
    cur = lambda b, j: (b * nb + j, 0)
    prev = lambda b, j: ((b * nb + j) * per - jnp.minimum(j, 1), 0)
    return pl.pallas_call(
        functools.partial(_swa_prompt_kernel, layer=layer),
        grid=(batch, nb),
        in_specs=[
            pl.BlockSpec(memory_space=pltpu.SMEM),
            pl.BlockSpec((qb, SWA_Q_W), cur),
            pl.BlockSpec((WINDOW, SWA_KV_W), prev),
            pl.BlockSpec((qb, SWA_KV_W), cur),
            pl.BlockSpec((WINDOW, SWA_KV_W), prev),
            pl.BlockSpec((qb, SWA_KV_W), cur),
        ],
        out_specs=pl.BlockSpec((qb, SWA_Q_W), cur),
        out_shape=jax.ShapeDtypeStruct((batch * seq, SWA_Q_W), BF16),
        compiler_params=_params("parallel", "parallel"),
        name="swa_prompt",
    )(sinks, q, k, k, v, v)


def _layer_view(ref, layer, first):
    if not first:
        return ref
    for other in range(ref.shape[0]):
        if other != layer:
            ref[other] = jnp.zeros(ref.shape[1:], ref.dtype)
    return ref.at[layer]


def _swa_decode_kernel(sink_ref, q_ref, kn_ref, vn_ref, knt_ref, vnt_ref, kc_ref, vc_ref, *rest, steps, layer):
    o_ref, ko_ref, vo_ref = rest[-3:]
    first = len(rest) == 3
    ko_ref = _layer_view(ko_ref, layer, first)
    vo_ref = _layer_view(vo_ref, layer, first)
    n_seq = kc_ref.shape[0]
    grp = DEC_GROUP
    rows = grp * steps
    keep = WINDOW - steps
    lane = lax.broadcasted_iota(jnp.int32, (1, WINDOW), 1)
    for s in range(n_seq):
        tile = slice((s * steps) // LANES * LANES, (s * steps) // LANES * LANES + LANES)
        shift = (keep - s * steps) % LANES
        for cache_ref, new_ref, out_ref in ((kc_ref, knt_ref, ko_ref), (vc_ref, vnt_ref, vo_ref)):
            out_ref[s] = jnp.where(lane >= keep, pltpu.roll(new_ref[:, tile], shift, axis=1),
                                   pltpu.roll(cache_ref[s], keep, axis=1))
    nk = grp * WINDOW + rows
    r = lax.broadcasted_iota(jnp.int32, (rows, nk), 0)
    q_seq, q_step = r // steps, r % steps
    c = lax.broadcasted_iota(jnp.int32, (rows, nk), 1)
    is_new = c >= grp * WINDOW
    cn = c - grp * WINDOW
    k_seq = jnp.where(is_new, cn // steps, c // WINDOW)
    k_idx = jnp.where(is_new, WINDOW + cn % steps, c % WINDOW)
    rel = WINDOW + q_step - k_idx
    valid = (q_seq == k_seq) & (rel >= 0) & (rel < WINDOW)
    sinks = _sink_column(sink_ref, layer)
    groups = range(n_seq // grp)
    new_rows = lambda g: slice(g * rows, (g + 1) * rows)
    cached = lambda ref, g: jnp.concatenate([ref[g * grp + s] for s in range(grp)], axis=1).astype(BF16)
    scores = []
    for g in groups:
        qm = _swa_head_rows(q_ref[new_rows(g), :])
        scores.append(jnp.concatenate(
            [_dot(qm, cached(kc_ref, g)), _dot_tb(qm, kn_ref[new_rows(g), :].astype(BF16))], axis=1))
    probs = [_swa_probs(scores[g], valid, sinks) for g in groups]
    for g in groups:
        e, den = probs[g]
        o = (_dot_tb(e[:, :grp * WINDOW], cached(vc_ref, g))
             + _dot(e[:, grp * WINDOW:], vn_ref[new_rows(g), :].astype(BF16)))
        for c, col in enumerate(_swa_finish(o, den)):
            o_ref[new_rows(g), c * LANES:(c + 1) * LANES] = col.astype(BF16)


def _stacked_out(shape, block, layer, prev):
    tail = (0,) * (len(block) - 1)
    if prev is None:
        spec = pl.BlockSpec((shape[0],) + block, lambda i: (0, i) + tail)
        extra_inputs = []
    else:
        spec = pl.BlockSpec((None,) + block, lambda i: (layer, i) + tail)
        extra_inputs = list(prev)
    extra_specs = [pl.BlockSpec(memory_space=pl.ANY) for _ in extra_inputs]
    return extra_inputs, extra_specs, spec, jax.ShapeDtypeStruct(shape, F32)


def _swa_decode(sinks, layer, q, kn, vn, knt, vnt, kc, vc, steps, prev):
    depth, nseq = kc.shape[:2]
    grp = min(DEC_GROUP * SWA_DEC_SUBGROUPS, nseq)
    rows = grp * steps
    assert rows % LANES == 0, "a grid step's new keys must fill whole 128-lane tiles"
    row = lambda i: (i, 0)
    cache_spec = pl.BlockSpec((None, grp, SWA_KV_W, WINDOW), lambda i: (layer, i, 0, 0))
    new_t = pl.BlockSpec((SWA_KV_W, rows), lambda i: (0, i))
    extra_in, extra_specs, out_spec, stacked = _stacked_out(kc.shape, (grp, SWA_KV_W, WINDOW), layer, prev)
    n_in = 8
    return pl.pallas_call(
        functools.partial(_swa_decode_kernel, steps=steps, layer=layer),
        grid=(nseq // grp,),
        in_specs=[
            pl.BlockSpec(memory_space=pltpu.SMEM),
            pl.BlockSpec((rows, SWA_Q_W), row),
            pl.BlockSpec((rows, SWA_KV_W), row),
            pl.BlockSpec((rows, SWA_KV_W), row),
            new_t,
            new_t,
            cache_spec,
            cache_spec,
        ] + extra_specs,
        out_specs=[pl.BlockSpec((rows, SWA_Q_W), row), out_spec, out_spec],
        out_shape=[jax.ShapeDtypeStruct((nseq * steps, SWA_Q_W), BF16), stacked, stacked],
        input_output_aliases={n_in + i: 1 + i for i in range(len(extra_in))},
        compiler_params=_params("parallel"),
        name="swa_decode",
    )(sinks, q, kn, vn, knt, vnt, kc, vc, *extra_in)


def _gla_out(o, gain, gate):
    return _rms(o, gain) * (gate * jax.nn.sigmoid(gate))


def _head_stack(x, width):
    return jnp.concatenate([x[:, h * width:(h + 1) * width] for h in range(GLA_HEADS)], axis=0)


def _head_masked_stack(x, head_of_lane):
    zero = jnp.zeros((), x.dtype)
    return jnp.concatenate([jnp.where(head_of_lane == h, x, zero) for h in range(GLA_HEADS)], axis=0)


def _gla_prompt_kernel(q_ref, k_ref, la_ref, v_ref, gg_ref, gn_ref, tril_ref, o_ref, s_ref, st_scr):
    tb = pl.program_id(1)

    @pl.when(tb == 0)
    def _():
        st_scr[...] = jnp.zeros_like(st_scr)

    n_tok = q_ref.shape[1]
    c_len = min(GLA_CHUNK, n_tok)
    tril = tril_ref[...]
    head_of_lane = lax.broadcasted_iota(jnp.int32, (1, GLA_K_W), 1) // GLA_DK
    ri = lax.broadcasted_iota(jnp.int32, (GLA_HEADS * c_len, c_len), 0) % c_len
    ci = lax.broadcasted_iota(jnp.int32, (GLA_HEADS * c_len, c_len), 1)
    causal = ri >= ci
    gain = gn_ref[...]
    n_seq = q_ref.shape[0]
    items = [(i, c) for i in range(n_seq) for c in range(n_tok // c_len)]
    rows = lambda c: slice(c * c_len, (c + 1) * c_len)
    b_all = []
    for i in range(n_seq):
        g_hi, g_lo = _split_bf16(la_ref[i])
        b_all.append(_dot(tril, g_hi) + _dot(tril, g_lo))
    qm, kdm, dec, a_raw = {}, {}, {}, {}
    for it in items:
        i, c = it
        b = b_all[i][rows(c), :]
        dec[it] = jnp.exp(b[c_len - 1:c_len, :])
        q_t = q_ref[i, rows(c), :] * jnp.exp(b)
        k_t = k_ref[i, rows(c), :] * jnp.exp(-b)
        qm[it] = _head_masked_stack(q_t, head_of_lane).astype(BF16)
        kdm[it] = _head_masked_stack(k_t * dec[it], head_of_lane).astype(BF16)
        a_raw[it] = _dot_tb(qm[it], k_t.astype(BF16))
    upd = {it: _dot_ta(_head_stack(v_ref[it[0], rows(it[1]), :], GLA_DV), kdm[it]) for it in items}
    intra = {}
    for it in items:
        i, c = it
        a = jnp.where(causal, a_raw[it], 0.0).astype(BF16)
        intra[it] = jnp.concatenate(
            [_dot(a[h * c_len:(h + 1) * c_len, :], v_ref[i, rows(c), h * GLA_DV:(h + 1) * GLA_DV])
             for h in range(GLA_HEADS)], axis=0)
    inter = {}
    for i in range(n_seq):
        st = st_scr[i]
        for c in range(n_tok // c_len):
            inter[(i, c)] = _dot_tb(qm[(i, c)], st.astype(BF16))
            st = dec[(i, c)] * st + upd[(i, c)]
        st_scr[i] = st
    for it in items:
        i, c = it
        y = _gla_out(inter[it] + intra[it], gain, _head_stack(gg_ref[i, rows(c), :], GLA_DV)).astype(BF16)
        for h in range(GLA_HEADS):
            o_ref[i, rows(c), h * GLA_DV:(h + 1) * GLA_DV] = y[h * c_len:(h + 1) * c_len, :]

    @pl.when(tb == pl.num_programs(1) - 1)
    def _():
        for i in range(q_ref.shape[0]):
            s_ref[i] = st_scr[i].T


def _gla_prompt(layer, q, k, la, v, gg, gain, tril, batch, seq):
    tb = tril.shape[0]
    per = min(GLA_SEQS, batch)
    blk = lambda b, t: (b, t, 0)
    kw = pl.BlockSpec((per, tb, GLA_K_W), blk)
    vw = pl.BlockSpec((per, tb, GLA_V_W), blk)
    return pl.pallas_call(
        _gla_prompt_kernel,
        grid=(batch // per, seq // tb),
        in_specs=[kw, kw, kw, vw, vw, _layer_resident(gain, layer), _resident(tril)],
        out_specs=[vw, pl.BlockSpec((per, GLA_K_W, GLA_DV), lambda b, t: (b, 0, 0))],
        out_shape=[
            jax.ShapeDtypeStruct((batch, seq, GLA_V_W), BF16),
            jax.ShapeDtypeStruct((batch, GLA_K_W, GLA_DV), F32),
        ],
        scratch_shapes=[pltpu.VMEM((per, GLA_DV, GLA_K_W), F32)],
        compiler_params=_params("parallel", "arbitrary"),
        name="gla_prompt",
    )(q, k, la, v, gg, gain, tril)


def _gla_decode_kernel(q_ref, k_ref, la_ref, v_ref, gg_ref, gn_ref, s_ref, *rest, steps, layer):
    o_ref, so_ref = rest[-2:]
    so_ref = _layer_view(so_ref, layer, len(rest) == 2)
    grp = s_ref.shape[0]
    rows = grp * steps
    stacked = GLA_HEADS * rows
    ri = lax.broadcasted_iota(jnp.int32, (rows, rows), 0)
    ci = lax.broadcasted_iota(jnp.int32, (rows, rows), 1)
    same_seq = ri // steps == ci // steps
    g_hi, g_lo = _split_bf16(la_ref[...])
    tril = (same_seq & (ri >= ci)).astype(BF16)
    total = same_seq.astype(BF16)
    b = _dot(tril, g_hi) + _dot(tril, g_lo)
    b_last = _dot(total, g_hi) + _dot(total, g_lo)
    head_of_lane = lax.broadcasted_iota(jnp.int32, (1, GLA_K_W), 1) // GLA_DK
    k_t = k_ref[...] * jnp.exp(-b)
    qm = _head_masked_stack(q_ref[...] * jnp.exp(b), head_of_lane).astype(BF16)
    km = _head_masked_stack(k_t, head_of_lane).astype(BF16)
    kdm = _head_masked_stack(k_t * jnp.exp(b_last), head_of_lane).astype(BF16)
    v_st = _head_stack(v_ref[...], GLA_DV)
    seq_of_row = (lax.broadcasted_iota(jnp.int32, (stacked, 1), 0) % rows) // steps
    seq_of_g = lax.broadcasted_iota(jnp.int32, (rows, 1), 0) // steps
    zero = jnp.zeros((), BF16)
    rhs = jnp.concatenate([
        jnp.concatenate([v_st, jnp.zeros((stacked, GLA_DV), BF16)], axis=1),
        jnp.concatenate([jnp.zeros((2 * rows, GLA_DV), BF16), jnp.ones((2 * rows, GLA_DV), BF16)], axis=1),
    ], axis=0)
    a_raw = _dot_tb(qm, km)
    states = [s_ref[s].reshape(GLA_K_W, GLA_DV) for s in range(grp)]
    inter_all = [_dot(qm, st.astype(BF16)) for st in states]
    upd_all = []
    for s in range(grp):
        lhs = jnp.concatenate([jnp.where(seq_of_row == s, kdm, zero),
                               jnp.where(seq_of_g == s, g_hi, zero),
                               jnp.where(seq_of_g == s, g_lo, zero)], axis=0)
        upd_all.append(_dot_ta(lhs, rhs))
    rr = lax.broadcasted_iota(jnp.int32, (stacked, stacked), 0) % rows
    cc = lax.broadcasted_iota(jnp.int32, (stacked, stacked), 1) % rows
    causal = (rr // steps == cc // steps) & (rr >= cc)
    o = _dot(jnp.where(causal, a_raw, 0.0).astype(BF16), v_st)
    inter = inter_all[0]
    for s in range(1, grp):
        inter = jnp.where(seq_of_row == s, inter_all[s], inter)
    y = _gla_out(o + inter, gn_ref[...], _head_stack(gg_ref[...], GLA_DV)).astype(BF16)
    for h in range(GLA_HEADS):
        o_ref[:, h * GLA_DV:(h + 1) * GLA_DV] = y[h * rows:(h + 1) * rows, :]
    for s in range(grp):
        new = jnp.exp(upd_all[s][:, GLA_DV:]) * states[s] + upd_all[s][:, :GLA_DV]
        so_ref[s] = new.reshape(GLA_HEADS, GLA_DK, GLA_DV)


def _gla_decode(layer, q, k, la, v, gg, gain, state, steps, prev):
    nseq = state.shape[1]
    grp = min(GLA_DEC_GROUP, nseq)
    rows = grp * steps
    row = lambda i: (i, 0)
    st_spec = pl.BlockSpec((None, grp, GLA_HEADS, GLA_DK, GLA_DV), lambda i: (layer, i, 0, 0, 0))
    extra_in, extra_specs, out_spec, stacked = _stacked_out(
        state.shape, (grp, GLA_HEADS, GLA_DK, GLA_DV), layer, prev)
    n_in = 7
    return pl.pallas_call(
        functools.partial(_gla_decode_kernel, steps=steps, layer=layer),
        grid=(nseq // grp,),
        in_specs=[
            pl.BlockSpec((rows, GLA_K_W), row),
            pl.BlockSpec((rows, GLA_K_W), row),
            pl.BlockSpec((rows, GLA_K_W), row),
            pl.BlockSpec((rows, GLA_V_W), row),
            pl.BlockSpec((rows, GLA_V_W), row),
            _layer_resident(gain, layer),
            st_spec,
        ] + extra_specs,
        out_specs=[pl.BlockSpec((rows, GLA_V_W), row), out_spec],
        out_shape=[jax.ShapeDtypeStruct((nseq * steps, GLA_V_W), BF16), stacked],
        input_output_aliases={n_in + i: 1 + i for i in range(len(extra_in))},
        compiler_params=_params("parallel"),
        name="gla_decode",
    )(q, k, la, v, gg, gain, state, *extra_in)


def _outproj_apply(x, a_ref, o_ref, wa_ref, wo_ref, g_ref, wq_ref, qn_ref):
    x1 = x + _dot(a_ref[...], wa_ref[...]) + _dot(o_ref[...], wo_ref[...])
    h = _rms(x1, g_ref[...]).astype(BF16)
    qn = qn_ref[...]
    q = [_rms(_dot(h, wq_ref[:, hd * XA_HEAD_DIM:(hd + 1) * XA_HEAD_DIM]), qn).astype(BF16)
         for hd in range(XA_HEADS)]
    return x1, q


def _outproj_kernel(x_ref, a_ref, o_ref, wa_ref, wo_ref, g_ref, wq_ref, qn_ref, x1_ref, q_ref):
    x1, q = _outproj_apply(x_ref[...], a_ref, o_ref, wa_ref, wo_ref, g_ref, wq_ref, qn_ref)
    x1_ref[...] = x1
    for hd in range(XA_HEADS):
        q_ref[:, hd * XA_HEAD_DIM:(hd + 1) * XA_HEAD_DIM] = q[hd]


def _outproj(x, layer, a, o, w_a, w_o, g, wq, qn):
    n, d = x.shape
    tm = min(ROW_TILE, n)
    row = lambda i: (i, 0)
    return pl.pallas_call(
        _outproj_kernel,
        grid=(n // tm,),
        in_specs=[
            pl.BlockSpec((tm, d), row),
            pl.BlockSpec((tm, SWA_Q_W), row),
            pl.BlockSpec((tm, GLA_V_W), row),
        ] + [_layer_resident(p, layer) for p in (w_a, w_o, g, wq, qn)],
        out_specs=[pl.BlockSpec((tm, d), row), pl.BlockSpec((tm, XA_W), row)],
        out_shape=[jax.ShapeDtypeStruct((n, d), F32), jax.ShapeDtypeStruct((n, XA_W), BF16)],
        compiler_params=_params("parallel"),
        name="outproj",
    )(x, a, o, w_a, w_o, g, wq, qn)


def _memkv_kernel(m_ref, g_ref, wk_ref, wv_ref, kn_ref, k_ref, v_ref):
    m = _rms(m_ref[...], g_ref[...]).astype(BF16)
    kn = kn_ref[...]
    for hd in range(XA_HEADS):
        sl = slice(hd * XA_HEAD_DIM, (hd + 1) * XA_HEAD_DIM)
        k_ref[:, sl] = _rms(_dot(m, wk_ref[:, sl]), kn)
    v_ref[...] = _dot(m, wv_ref[...])


def _memkv(mem, g, wk, wv, kn):
    depth = wk.shape[0]
    n, d = mem.shape
    tm = min(ROW_TILE, n)
    per_layer = lambda l, i: (l, 0, 0)
    out = lambda l, i: (l, i, 0)
    return pl.pallas_call(
        _memkv_kernel,
        grid=(depth, n // tm),
        in_specs=[
            pl.BlockSpec((tm, d), lambda l, i: (i, 0)),
            pl.BlockSpec((None, 1, d), per_layer),
            pl.BlockSpec((None, d, XA_W), per_layer),
            pl.BlockSpec((None, d, XA_W), per_layer),
            pl.BlockSpec((None, 1, XA_HEAD_DIM), per_layer),
        ],
        out_specs=[pl.BlockSpec((None, tm, XA_W), out), pl.BlockSpec((None, tm, XA_W), out)],
        out_shape=[jax.ShapeDtypeStruct((depth, n, XA_W), F32)] * 2,
        compiler_params=_params("parallel", "parallel"),
        name="memkv",
    )(mem, g, wk, wv, kn)


def _mixout_prompt_kernel(x_ref, a_ref, o_ref, wa_ref, wo_ref, g_ref, wq_ref, qn_ref, mk_ref, mv_ref,
                          xwo_ref, fg_ref, wg_ref, wu_ref, wd_ref, out_ref, att_scr, a_scr):
    x1, q = _outproj_apply(x_ref[...], a_ref, o_ref, wa_ref, wo_ref, g_ref, wq_ref, qn_ref)
    mk = mk_ref[...].astype(BF16)
    mv = mv_ref[...].astype(BF16)
    heads = range(XA_HEADS)
    cols = lambda hd: slice(hd * XA_HEAD_DIM, (hd + 1) * XA_HEAD_DIM)
    scores = [_dot_tb(q[hd], mk[:, cols(hd)]) * (XA_HEAD_DIM ** -0.5 * LOG2E) for hd in heads]
    probs = []
    for s in scores:
        e = jnp.exp2(s - jnp.max(s, axis=-1, keepdims=True))
        probs.append((e.astype(BF16), jnp.sum(e, axis=-1, keepdims=True)))
    for hd in heads:
        att_scr[:, cols(hd)] = (_dot(probs[hd][0], mv[:, cols(hd)]) / probs[hd][1]).astype(BF16)
    x2 = x1 + _dot(att_scr[...], xwo_ref[...])
    out_ref[...] = _ffn_apply(x2, fg_ref, wg_ref, wu_ref, wd_ref, a_scr)


def _mixout_prompt(x, layer, a, o, outproj, mk, mv, wo, ffn, seq):
    n, d = x.shape
    dff = ffn[1].shape[2]
    tm = min(ROW_TILE, seq)
    per_seq = seq // tm
    row = lambda i: (i, 0)
    mem_spec = pl.BlockSpec((None, None, mk.shape[2], XA_W), lambda i: (layer, i // per_seq, 0, 0))
    return pl.pallas_call(
        _mixout_prompt_kernel,
        grid=(n // tm,),
        in_specs=([pl.BlockSpec((tm, d), row), pl.BlockSpec((tm, SWA_Q_W), row), pl.BlockSpec((tm, GLA_V_W), row)]
                  + [_layer_resident(p, layer) for p in outproj]
                  + [mem_spec, mem_spec]
                  + [_layer_resident(p, layer) for p in (wo,) + ffn]),
        out_specs=pl.BlockSpec((tm, d), row),
        out_shape=jax.ShapeDtypeStruct((n, d), F32),
        scratch_shapes=[pltpu.VMEM((tm, XA_W), BF16), pltpu.VMEM((tm, dff), BF16)],
        compiler_params=_params("parallel"),
        name="mixout_prompt",
    )(x, a, o, *outproj, mk, mv, wo, *ffn)


def _xattn_decode_kernel(q_ref, mk_ref, mv_ref, o_ref, *, steps):
    grp, nkeys = mk_ref.shape[0], mk_ref.shape[1]
    rows = grp * steps
    q = jnp.concatenate([q_ref[:, hd * XA_HEAD_DIM:(hd + 1) * XA_HEAD_DIM] for hd in range(XA_HEADS)], axis=0)
    r = lax.broadcasted_iota(jnp.int32, (XA_HEADS * rows, 1), 0)
    own = (r % rows) // steps
    same_head = (r // rows) == (lax.broadcasted_iota(jnp.int32, (1, nkeys), 1) % XA_HEADS)
    s = None
    for j in range(grp):
        sj = _dot_tb(q, mk_ref[j].astype(BF16))
        s = sj if s is None else jnp.where(own == j, sj, s)
    s = jnp.where(same_head, s * (XA_HEAD_DIM ** -0.5 * LOG2E), -jnp.inf)
    m = jnp.max(s, axis=-1, keepdims=True)
    e = jnp.exp2(s - m)
    p = e.astype(BF16)
    o = None
    for j in range(grp):
        oj = _dot(p, mv_ref[j].astype(BF16))
        o = oj if o is None else jnp.where(own == j, oj, o)
    o = o / jnp.sum(e, axis=-1, keepdims=True)
    for hd in range(XA_HEADS):
        o_ref[:, hd * XA_HEAD_DIM:(hd + 1) * XA_HEAD_DIM] = o[hd * rows:(hd + 1) * rows, :].astype(BF16)


def _xattn_decode(q, layer, mk, mv, steps):
    nseq, nkeys = mk.shape[1], mk.shape[2]
    grp = XA_DEC_GROUP
    rows = grp * steps
    row = lambda i: (i, 0)
    mem_spec = pl.BlockSpec((None, grp, nkeys, XA_HEAD_DIM), lambda i: (layer, i, 0, 0))
    return pl.pallas_call(
        functools.partial(_xattn_decode_kernel, steps=steps),
        grid=(nseq // grp,),
        in_specs=[pl.BlockSpec((rows, XA_W), row), mem_spec, mem_spec],
        out_specs=pl.BlockSpec((rows, XA_W), row),
        out_shape=jax.ShapeDtypeStruct((nseq * steps, XA_W), BF16),
        compiler_params=_params("parallel"),
        name="xattn_decode",
    )(q, mk, mv)


def _proj_res_kernel(x_ref, a_ref, w_ref, o_ref):
    o_ref[...] = x_ref[...] + _dot(a_ref[...], w_ref[...])


def _proj_res(x, layer, a, w):
    n, d = x.shape
    tm = min(ROW_TILE, n)
    row = lambda i: (i, 0)
    return pl.pallas_call(
        _proj_res_kernel,
        grid=(n // tm,),
        in_specs=[pl.BlockSpec((tm, d), row), pl.BlockSpec((tm, a.shape[1]), row), _layer_resident(w, layer)],
        out_specs=pl.BlockSpec((tm, d), row),
        out_shape=jax.ShapeDtypeStruct((n, d), F32),
        compiler_params=_params("parallel"),
        name="proj_res",
    )(x, a, w)


def _rope_tables(pos):
    half = HEAD_DIM // 2
    inv = ROPE_THETA ** (-jnp.arange(half, dtype=F32) / half)
    ang = pos.astype(F32)[:, None] * inv[None, :]
    cos, sin = jnp.cos(ang), jnp.sin(ang)
    reps = LANES // HEAD_DIM
    return (jnp.concatenate([cos, cos] * reps, axis=-1),
            jnp.concatenate([-sin, sin] * reps, axis=-1))


def _block_tril(n_blocks, size):
    i = jnp.arange(n_blocks * size)
    return ((i[:, None] // size == i[None, :] // size) & (i[:, None] >= i[None, :])).astype(BF16)


def _permute_heads(w, axis):
    blocks = jnp.split(w, SWA_Q_HEADS, axis=axis)
    return jnp.concatenate([blocks[h] for h in SWA_HEAD_ORDER], axis=axis)


def kernel(x_prompt, x_sample, cache_swa_k, cache_swa_v, state_gla, cache_mem_k, cache_mem_v, mem_prompt, ffn1_norm, ffn1_wg, ffn1_wu, ffn1_wd, mix_norm, w_in, swa_q_norm, swa_k_norm, swa_sinks, gla_w_gate, gla_b_gate, gla_out_norm, w_out, xa_norm, mem_norm, xa_wq, xa_wk, xa_wv, xa_q_norm, xa_k_norm, xa_wo, ffn2_norm, ffn2_wg, ffn2_wu, ffn2_wd):
    batch, seq, d = x_prompt.shape
    nseq, steps, _ = x_sample.shape
    depth = w_in.shape[0]
    mem_len = mem_prompt.shape[1]

    bf = lambda w: w.astype(BF16)
    vec = lambda p: p[:, None, :]
    ffn1 = (vec(ffn1_norm), bf(ffn1_wg), bf(ffn1_wu), bf(ffn1_wd))
    ffn2 = (vec(ffn2_norm), bf(ffn2_wg), bf(ffn2_wu), bf(ffn2_wd))
    o_qg = SWA_Q_W + 2 * SWA_KV_W
    w_all = bf(jnp.concatenate([
        _permute_heads(w_in[:, :, :SWA_Q_W], 2),
        w_in[:, :, SWA_Q_W:o_qg],
        w_in[:, :, o_qg:o_qg + GLA_K_W] * (GLA_DK ** -0.5),
        w_in[:, :, o_qg + GLA_K_W:],
        jnp.zeros((depth, d, MXU_TILE - GLA_LOWRANK), F32)], axis=2))
    gate_w = jnp.pad(bf(gla_w_gate), ((0, 0), (0, LANES - GLA_LOWRANK), (0, 0)))
    w_a = bf(_permute_heads(w_out[:, :SWA_Q_W], 1))
    w_o = bf(w_out[:, SWA_Q_W:])
    wq_b, wk_b, wv_b, wo_b = bf(xa_wq), bf(xa_wk), bf(xa_wv), bf(xa_wo)
    qn = vec(jnp.tile(swa_q_norm * (HEAD_DIM ** -0.5 * LOG2E), (1, LANES // HEAD_DIM)))
    kn = vec(jnp.tile(swa_k_norm, (1, LANES // HEAD_DIM)))
    inproj_params = (vec(mix_norm), w_all, gate_w, vec(gla_b_gate), qn, kn)
    outproj_params = (w_a, w_o, vec(xa_norm), wq_b, vec(xa_q_norm))
    gla_gain = vec(gla_out_norm)

    lane = jnp.arange(MXU_TILE)
    bd = (lane[:, None] // HEAD_DIM == lane[None, :] // HEAD_DIM).astype(BF16)
    cos_p, sin_p = _rope_tables(jnp.arange(seq))
    cos_s, sin_s = _rope_tables(PAST_LEN + jnp.arange(nseq * steps) % steps)
    tril_p = _block_tril(min(GLA_BLOCK, seq) // min(GLA_CHUNK, seq), min(GLA_CHUNK, seq))

    mk_p, mv_p = _memkv(mem_prompt.reshape(batch * mem_len, d), vec(mem_norm), wk_b, wv_b, vec(xa_k_norm))
    mk_p = mk_p.reshape(depth, batch, mem_len, XA_W)
    mv_p = mv_p.reshape(depth, batch, mem_len, XA_W)
    mk_s = cache_mem_k.reshape(depth, nseq, mem_len * XA_HEADS, XA_HEAD_DIM)
    mv_s = cache_mem_v.reshape(depth, nseq, mem_len * XA_HEADS, XA_HEAD_DIM)
    native = lambda c: jnp.transpose(c, (0, 1, 3, 4, 2)).reshape(depth, nseq, SWA_KV_W, WINDOW)
    kc_s, vc_s = native(cache_swa_k), native(cache_swa_v)

    xp = x_prompt.reshape(batch * seq, d)
    xs = x_sample.reshape(nseq * steps, d)
    kp_l, vp_l, sp_l = [], [], []
    swa_new = gla_new = None
    for l in range(depth):
        xp, q_s, k_s, v_s, q_g, k_g, v_g, g_g, la = _ffn_inproj(xp, l, ffn1, inproj_params, cos_p, sin_p, bd)
        a_p = _swa_prompt(swa_sinks, l, q_s, k_s, v_s, batch, seq)
        seqs = lambda a: a.reshape(batch, seq, a.shape[-1])
        o_p, s_p = _gla_prompt(l, seqs(q_g), seqs(k_g), seqs(la), seqs(v_g), seqs(g_g), gla_gain, tril_p,
                               batch, seq)
        o_p = o_p.reshape(batch * seq, GLA_V_W)
        last = lambda a: seqs(a)[:, seq - WINDOW:].reshape(batch, WINDOW, SWA_KV_HEADS, HEAD_DIM)
        kp_l.append(last(k_s))
        vp_l.append(last(v_s))
        sp_l.append(s_p.reshape(batch, GLA_HEADS, GLA_DK, GLA_DV))
        xp = _mixout_prompt(xp, l, a_p, o_p, outproj_params, mk_p, mv_p, wo_b, ffn2, seq)

        xs, q_s, k_s, v_s, q_g, k_g, v_g, g_g, la, k_t, v_t = _ffn_inproj(
            xs, l, ffn1, inproj_params, cos_s, sin_s, bd, transposed_kv=True)
        a_s, *swa_new = _swa_decode(swa_sinks, l, q_s, k_s, v_s, k_t, v_t, kc_s, vc_s, steps, swa_new)
        o_s, *gla_new = _gla_decode(l, q_g, k_g, la, v_g, g_g, gla_gain, state_gla, steps, gla_new)
        xs, q_x = _outproj(xs, l, a_s, o_s, *outproj_params)
        xs = _proj_res(xs, l, _xattn_decode(q_x, l, mk_s, mv_s, steps), wo_b)
        xs = _ffn(xs, l, *ffn2)

    unnative = lambda c: jnp.transpose(c.reshape(depth, nseq, SWA_KV_HEADS, HEAD_DIM, WINDOW), (0, 1, 4, 2, 3))
    return (xp.reshape(batch, seq, d), xs.reshape(nseq, steps, d),
            jnp.stack(kp_l), jnp.stack(vp_l), jnp.stack(sp_l),
            mk_p.reshape(depth, batch, mem_len, XA_HEADS, XA_HEAD_DIM),
            mv_p.reshape(depth, batch, mem_len, XA_HEADS, XA_HEAD_DIM),
            unnative(swa_new[0]), unnative(swa_new[1]), gla_new[0])
```

```python
import functools

import jax
import jax.numpy as jnp
from jax import lax
from jax.experimental import pallas as pl
from jax.experimental.pallas import tpu as pltpu

F32 = jnp.float32
BF16 = jnp.bfloat16

EPS = 1e-6
LOG2E = 1.4426950408889634
PAST_LEN = 16384
WINDOW = 128
ROPE_THETA = 10000.0
HEAD_DIM = 64
SWA_Q_HEADS = 8
SWA_KV_HEADS = 2
SWA_GROUP = SWA_Q_HEADS // SWA_KV_HEADS
GLA_HEADS = 4
GLA_DK = 64
GLA_DV = 128
GLA_LOWRANK = 16
GLA_GATE_TEMP = 16.0
GLA_CHUNK = 64
XA_HEADS = 4
XA_HEAD_DIM = 128

SWA_Q_W = SWA_Q_HEADS * HEAD_DIM
SWA_KV_W = SWA_KV_HEADS * HEAD_DIM
GLA_K_W = GLA_HEADS * GLA_DK
GLA_V_W = GLA_HEADS * GLA_DV
XA_W = XA_HEADS * XA_HEAD_DIM
MAIN_W = SWA_Q_W + 2 * SWA_KV_W + 2 * GLA_K_W + 2 * GLA_V_W

LANES = 128
MXU_TILE = 256
VMEM_LIMIT = 56 * 1024 * 1024

ROW_TILE = 512
ROW_SPLIT = 2
FFN_CHUNK = 256
SWA_BLOCK = 256
GLA_BLOCK = 256
GLA_SEQS = 4
DEC_GROUP = 4
SWA_DEC_SUBGROUPS = 8
XA_DEC_GROUP = 8
GLA_DEC_GROUP = 8

assert SWA_KV_HEADS * HEAD_DIM == LANES
SWA_HEAD_ORDER = tuple(kv * SWA_GROUP + g for g in range(SWA_GROUP) for kv in range(SWA_KV_HEADS))


def _dot(a, b):
    return jnp.dot(a, b, preferred_element_type=F32)


def _dot_tb(a, b):
    return lax.dot_general(a, b, (((1,), (1,)), ((), ())), preferred_element_type=F32)


def _dot_ta(a, b):
    return lax.dot_general(a, b, (((0,), (0,)), ((), ())), preferred_element_type=F32)


def _split_bf16(x):
    hi = x.astype(BF16)
    lo = (x - hi.astype(F32)).astype(BF16)
    return hi, lo


def _rms(x, g):
    ms = jnp.mean(x * x, axis=-1, keepdims=True)
    return x * lax.rsqrt(ms + EPS) * g


def _params(*sem):
    return pltpu.CompilerParams(dimension_semantics=sem, vmem_limit_bytes=VMEM_LIMIT)


def _resident(arr):
    nd = arr.ndim
    return pl.BlockSpec(arr.shape, lambda *_: (0,) * nd, pipeline_mode=pl.Buffered(1))


def _layer_resident(arr, layer):
    nd = arr.ndim
    return pl.BlockSpec((None,) + arr.shape[1:], lambda *_: (layer,) + (0,) * (nd - 1),
                        pipeline_mode=pl.Buffered(1))


def _ffn_hidden(x, g_ref, wg_ref, wu_ref, a_scr):
    h = _rms(x, g_ref[...]).astype(BF16)
    dff = wg_ref.shape[1]
    for c in range(dff // FFN_CHUNK):
        sl = slice(c * FFN_CHUNK, (c + 1) * FFN_CHUNK)
        g = _dot(h, wg_ref[:, sl])
        u = _dot(h, wu_ref[:, sl])
        a_scr[:, sl] = (g * jax.nn.sigmoid(g) * u).astype(BF16)


def _ffn_apply(x, g_ref, wg_ref, wu_ref, wd_ref, a_scr):
    _ffn_hidden(x, g_ref, wg_ref, wu_ref, a_scr)
    return x + 0.5 * _dot(a_scr[...], wd_ref[...])


def _ffn_kernel(x_ref, g_ref, wg_ref, wu_ref, wd_ref, o_ref, a_scr):
    o_ref[...] = _ffn_apply(x_ref[...], g_ref, wg_ref, wu_ref, wd_ref, a_scr)


def _ffn(x, layer, g, wg, wu, wd):
    n, d = x.shape
    dff = wg.shape[2]
    tm = min(ROW_TILE, n)
    return pl.pallas_call(
        _ffn_kernel,
        grid=(n // tm,),
        in_specs=[pl.BlockSpec((tm, d), lambda i: (i, 0))] + [_layer_resident(a, layer) for a in (g, wg, wu, wd)],
        out_specs=pl.BlockSpec((tm, d), lambda i: (i, 0)),
        out_shape=jax.ShapeDtypeStruct((n, d), F32),
        scratch_shapes=[pltpu.VMEM((tm, dff), BF16)],
        compiler_params=_params("parallel"),
        name="ffn",
    )(x, g, wg, wu, wd)


def _ffn_inproj_kernel(x_ref, fg_ref, wg_ref, wu_ref, wd_ref, g_ref, w_ref, *rest):
    xo_ref, a_scr = rest[7], rest[-1]
    x = x_ref[...]
    _ffn_hidden(x, fg_ref, wg_ref, wu_ref, a_scr)
    n = x.shape[0] // ROW_SPLIT
    groups = [slice(r * n, (r + 1) * n) for r in range(ROW_SPLIT)]
    mid = [x[rs] + 0.5 * _dot(a_scr[rs, :], wd_ref[...]) for rs in groups]
    for rs, xm in zip(groups, mid):
        xo_ref[rs, :] = xm
    z = [_inproj_matmuls(xm, g_ref, w_ref) for xm in mid]
    for rs, (z1, z2) in zip(groups, z):
        _inproj_finish(z1, z2, rs, *rest[:7], *rest[8:-1])


def _inproj_matmuls(x, g_ref, w_ref):
    h = _rms(x, g_ref[...]).astype(BF16)
    split = SWA_Q_W + 2 * SWA_KV_W + 2 * GLA_K_W
    return _dot(h, w_ref[:, :split]), _dot(h, w_ref[:, split:])


def _inproj_finish(z1, z2, rs, gw_ref, gb_ref, qn_ref, kn_ref, cos_ref, sin_ref, bd_ref,
                   qs_ref, ks_ref, vs_ref, qg_ref, kg_ref, vg_ref, gg_ref, la_ref, kt_ref=None, vt_ref=None):
    bd = bd_ref[...]
    cos = cos_ref[rs, :]
    sin = sin_ref[rs, :]
    lane = lax.broadcasted_iota(jnp.int32, cos.shape, 1)
    lane_lo = (lane % HEAD_DIM) < (HEAD_DIM // 2)

    def head_scale(z):
        ss = _dot((z * z).astype(BF16), bd)
        return lax.rsqrt(ss * (1.0 / HEAD_DIM) + EPS)

    def rope(y):
        swapped = jnp.where(lane_lo, pltpu.roll(y, LANES - HEAD_DIM // 2, axis=1),
                            pltpu.roll(y, HEAD_DIM // 2, axis=1))
        return y * cos + swapped * sin

    qn = qn_ref[...]
    nat = []
    for t in range(SWA_Q_W // MXU_TILE):
        z = z1[:, t * MXU_TILE:(t + 1) * MXU_TILE]
        y = z * head_scale(z)
        nat += [rope(y[:, c * LANES:(c + 1) * LANES] * qn) for c in range(MXU_TILE // LANES)]
    low = _low_half()
    per_col = LANES // HEAD_DIM
    for c in range(SWA_Q_W // LANES):
        halves = []
        for half, head in enumerate(SWA_HEAD_ORDER[per_col * c:per_col * (c + 1)]):
            col = nat[head // per_col]
            halves.append(col if head % per_col == half else pltpu.roll(col, HEAD_DIM, axis=1))
        qs_ref[rs, c * LANES:(c + 1) * LANES] = jnp.where(low, halves[0], halves[1]).astype(BF16)
    o = SWA_Q_W
    z = z1[:, o:o + 2 * SWA_KV_W]
    k = rope(z[:, :SWA_KV_W] * head_scale(z)[:, :SWA_KV_W] * kn_ref[...])
    ks_ref[rs, :] = k
    vs_ref[rs, :] = z[:, SWA_KV_W:]
    if kt_ref is not None:
        kt_ref[:, rs] = k.T
        vt_ref[:, rs] = z[:, SWA_KV_W:].T
    o += 2 * SWA_KV_W
    qg_ref[rs, :] = z1[:, o:o + GLA_K_W] * (GLA_DK ** -0.5)
    o += GLA_K_W
    kg_ref[rs, :] = z1[:, o:o + GLA_K_W]
    vg_ref[rs, :] = z2[:, :GLA_V_W].astype(BF16)
    gg_ref[rs, :] = z2[:, GLA_V_W:2 * GLA_V_W]
    lr = z2[:, 2 * GLA_V_W:2 * GLA_V_W + LANES].astype(BF16)
    t = _dot(lr, gw_ref[...]) + gb_ref[...]
    log_sig = jnp.minimum(t, 0.0) - jnp.log(1.0 + jnp.exp(-jnp.abs(t)))
    la_ref[rs, :] = log_sig * (1.0 / GLA_GATE_TEMP)


def _ffn_inproj(x, layer, ffn, inproj, cos, sin, bd, transposed_kv=False):
    n, d = x.shape
    dff = ffn[1].shape[2]
    tm = min(ROW_TILE, n)
    pos_blocks = cos.shape[0] // tm
    row = lambda i: (i, 0)
    pos = lambda i: (i % pos_blocks, 0)
    widths = (d, SWA_Q_W, SWA_KV_W, SWA_KV_W, GLA_K_W, GLA_K_W, GLA_V_W, GLA_V_W, GLA_K_W)
    dtypes = (F32, BF16, F32, F32, F32, F32, BF16, F32, F32)
    out_specs = [pl.BlockSpec((tm, w), row) for w in widths]
    out_shape = [jax.ShapeDtypeStruct((n, w), dt) for w, dt in zip(widths, dtypes)]
    if transposed_kv:
        out_specs += [pl.BlockSpec((SWA_KV_W, tm), lambda i: (0, i))] * 2
        out_shape += [jax.ShapeDtypeStruct((SWA_KV_W, n), F32)] * 2
    return pl.pallas_call(
        _ffn_inproj_kernel,
        grid=(n // tm,),
        in_specs=([pl.BlockSpec((tm, d), row)]
                  + [_layer_resident(a, layer) for a in ffn + inproj]
                  + [pl.BlockSpec((tm, LANES), pos), pl.BlockSpec((tm, LANES), pos), _resident(bd)]),
        out_specs=out_specs,
        out_shape=out_shape,
        scratch_shapes=[pltpu.VMEM((tm, dff), BF16)],
        compiler_params=_params("parallel"),
        name="ffn_inproj",
    )(x, *ffn, *inproj, cos, sin, bd)


def _swa_attend(q, k, v, valid, sinks):
    return _swa_out(*_swa_probs(_swa_scores(q, k), valid, sinks), v)


def _low_half():
    return lax.broadcasted_iota(jnp.int32, (1, LANES), 1) < HEAD_DIM


def _swa_head_rows(q):
    low = _low_half()
    zero = jnp.zeros((), q.dtype)
    pieces = []
    for c in range(SWA_Q_W // LANES):
        qc = q[:, c * LANES:(c + 1) * LANES]
        pieces += [jnp.where(low, qc, zero), jnp.where(low, zero, qc)]
    return jnp.concatenate(pieces, axis=0)


def _swa_scores(q, k):
    return _dot_tb(_swa_head_rows(q), k)


def _swa_probs(s, valid, sinks):
    rows, keys = valid.shape
    s = jnp.where(valid[None], s.reshape(SWA_Q_HEADS, rows, keys), -jnp.inf)
    m = jnp.maximum(jnp.max(s, axis=-1, keepdims=True), sinks)
    e = jnp.exp2(s - m)
    den = jnp.sum(e, axis=-1, keepdims=True) + jnp.exp2(sinks - m)
    return e.astype(BF16).reshape(SWA_Q_HEADS * rows, keys), den


def _swa_out(e, den, v):
    return _swa_finish(_dot(e, v), den)


def _swa_finish(o, den):
    low = _low_half()
    o = o.reshape(den.shape[0], den.shape[1], LANES) / den
    return [jnp.where(low, o[2 * c], o[2 * c + 1]) for c in range(SWA_Q_W // LANES)]


def _sink_column(sink_ref, layer):
    idx = lax.broadcasted_iota(jnp.int32, (SWA_Q_HEADS, 1, 1), 0)
    col = jnp.zeros((SWA_Q_HEADS, 1, 1), F32)
    for p, head in enumerate(SWA_HEAD_ORDER):
        col = jnp.where(idx == p, sink_ref[layer, head] * LOG2E, col)
    return col


def _swa_prompt_kernel(sink_ref, q_ref, kp_ref, kc_ref, vp_ref, vc_ref, o_ref, *, layer):
    j = pl.program_id(1)
    kall = jnp.concatenate([kp_ref[...], kc_ref[...]], axis=0).astype(BF16)
    vall = jnp.concatenate([vp_ref[...], vc_ref[...]], axis=0).astype(BF16)
    sinks = _sink_column(sink_ref, layer)
    upper = (lax.broadcasted_iota(jnp.int32, (WINDOW, WINDOW), 1)
             > lax.broadcasted_iota(jnp.int32, (WINDOW, WINDOW), 0))[None]

    def banded_probs(s, has_prev):
        s = s.reshape(SWA_Q_HEADS, WINDOW, 2 * WINDOW)
        prev = s[:, :, :WINDOW]
        if has_prev is not True:
            prev = jnp.where(has_prev, prev, -jnp.inf)
        logits = jnp.where(upper, prev, s[:, :, WINDOW:])
        m = jnp.maximum(jnp.max(logits, axis=-1, keepdims=True), sinks)
        e = jnp.exp2(logits - m)
        den = jnp.sum(e, axis=-1, keepdims=True) + jnp.exp2(sinks - m)
        e = e.astype(BF16)
        zero = jnp.zeros((), BF16)
        e = jnp.concatenate([jnp.where(upper, e, zero), jnp.where(upper, zero, e)], axis=-1)
        return e.reshape(SWA_Q_HEADS * WINDOW, 2 * WINDOW), den

    windows = range(q_ref.shape[0] // WINDOW)
    rows = lambda w: slice(w * WINDOW, (w + 1) * WINDOW)
    keys = lambda w: slice(w * WINDOW, (w + 2) * WINDOW)
    scores = [_swa_scores(q_ref[rows(w), :], kall[keys(w)]) for w in windows]
    probs = [banded_probs(scores[w], True if w > 0 else j > 0) for w in windows]
    for w in windows:
        for c, col in enumerate(_swa_out(*probs[w], vall[keys(w)])):
            o_ref[rows(w), c * LANES:(c + 1) * LANES] = col.astype(BF16)


def _swa_prompt(sinks, layer, q, k, v, batch, seq):
    qb = min(SWA_BLOCK, seq)
    nb = seq // qb
    per = qb // WINDOW
    cur = lambda b, j: (b * nb + j, 0)
    prev = lambda b, j: ((b * nb + j) * per - jnp.minimum(j, 1), 0)
    return pl.pallas_call(
        functools.partial(_swa_prompt_kernel, layer=layer),
        grid=(batch, nb),
        in_specs=[
            pl.BlockSpec(memory_space=pltpu.SMEM),
            pl.BlockSpec((qb, SWA_Q_W), cur),
            pl.BlockSpec((WINDOW, SWA_KV_W), prev),
            pl.BlockSpec((qb, SWA_KV_W), cur),
            pl.BlockSpec((WINDOW, SWA_KV_W), prev),
            pl.BlockSpec((qb, SWA_KV_W), cur),
        ],
        out_specs=pl.BlockSpec((qb, SWA_Q_W), cur),
        out_shape=jax.ShapeDtypeStruct((batch * seq, SWA_Q_W), BF16),
        compiler_params=_params("parallel", "parallel"),
        name="swa_prompt",
    )(sinks, q, k, k, v, v)


def _layer_view(ref, layer, first):
    if not first:
        return ref
    for other in range(ref.shape[0]):
        if other != layer:
            ref[other] = jnp.zeros(ref.shape[1:], ref.dtype)
    return ref.at[layer]


def _swa_decode_kernel(sink_ref, q_ref, kn_ref, vn_ref, knt_ref, vnt_ref, kc_ref, vc_ref, *rest, steps, layer):
    o_ref, ko_ref, vo_ref = rest[-3:]
    first = len(rest) == 3
    ko_ref = _layer_view(ko_ref, layer, first)
    vo_ref = _layer_view(vo_ref, layer, first)
    n_seq = kc_ref.shape[0]
    grp = DEC_GROUP
    rows = grp * steps
    keep = WINDOW - steps
    lane = lax.broadcasted_iota(jnp.int32, (1, WINDOW), 1)
    for s in range(n_seq):
        tile = slice((s * steps) // LANES * LANES, (s * steps) // LANES * LANES + LANES)
        shift = (keep - s * steps) % LANES
        for cache_ref, new_ref, out_ref in ((kc_ref, knt_ref, ko_ref), (vc_ref, vnt_ref, vo_ref)):
            out_ref[s] = jnp.where(lane >= keep, pltpu.roll(new_ref[:, tile], shift, axis=1),
                                   pltpu.roll(cache_ref[s], keep, axis=1))
    nk = grp * WINDOW + rows
    r = lax.broadcasted_iota(jnp.int32, (rows, nk), 0)
    q_seq, q_step = r // steps, r % steps
    c = lax.broadcasted_iota(jnp.int32, (rows, nk), 1)
    is_new = c >= grp * WINDOW
    cn = c - grp * WINDOW
    k_seq = jnp.where(is_new, cn // steps, c // WINDOW)
    k_idx = jnp.where(is_new, WINDOW + cn % steps, c % WINDOW)
    rel = WINDOW + q_step - k_idx
    valid = (q_seq == k_seq) & (rel >= 0) & (rel < WINDOW)
    sinks = _sink_column(sink_ref, layer)
    groups = range(n_seq // grp)
    new_rows = lambda g: slice(g * rows, (g + 1) * rows)
    cached = lambda ref, g: jnp.concatenate([ref[g * grp + s] for s in range(grp)], axis=1).astype(BF16)
    scores = []
    for g in groups:
        qm = _swa_head_rows(q_ref[new_rows(g), :])
        scores.append(jnp.concatenate(
            [_dot(qm, cached(kc_ref, g)), _dot_tb(qm, kn_ref[new_rows(g), :].astype(BF16))], axis=1))
    probs = [_swa_probs(scores[g], valid, sinks) for g in groups]
    for g in groups:
        e, den = probs[g]
        o = (_dot_tb(e[:, :grp * WINDOW], cached(vc_ref, g))
             + _dot(e[:, grp * WINDOW:], vn_ref[new_rows(g), :].astype(BF16)))
        for c, col in enumerate(_swa_finish(o, den)):
            o_ref[new_rows(g), c * LANES:(c + 1) * LANES] = col.astype(BF16)


def _stacked_out(shape, block, layer, prev):
    tail = (0,) * (len(block) - 1)
    if prev is None:
        spec = pl.BlockSpec((shape[0],) + block, lambda i: (0, i) + tail)
        extra_inputs = []
    else:
        spec = pl.BlockSpec((None,) + block, lambda i: (layer, i) + tail)
        extra_inputs = list(prev)
    extra_specs = [pl.BlockSpec(memory_space=pl.ANY) for _ in extra_inputs]
    return extra_inputs, extra_specs, spec, jax.ShapeDtypeStruct(shape, F32)


def _swa_decode(sinks, layer, q, kn, vn, knt, vnt, kc, vc, steps, prev):
    depth, nseq = kc.shape[:2]
    grp = min(DEC_GROUP * SWA_DEC_SUBGROUPS, nseq)
    rows = grp * steps
    assert rows % LANES == 0, "a grid step's new keys must fill whole 128-lane tiles"
    row = lambda i: (i, 0)
    cache_spec = pl.BlockSpec((None, grp, SWA_KV_W, WINDOW), lambda i: (layer, i, 0, 0))
    new_t = pl.BlockSpec((SWA_KV_W, rows), lambda i: (0, i))
    extra_in, extra_specs, out_spec, stacked = _stacked_out(kc.shape, (grp, SWA_KV_W, WINDOW), layer, prev)
    n_in = 8
    return pl.pallas_call(
        functools.partial(_swa_decode_kernel, steps=steps, layer=layer),
        grid=(nseq // grp,),
        in_specs=[
            pl.BlockSpec(memory_space=pltpu.SMEM),
            pl.BlockSpec((rows, SWA_Q_W), row),
            pl.BlockSpec((rows, SWA_KV_W), row),
            pl.BlockSpec((rows, SWA_KV_W), row),
            new_t,
            new_t,
            cache_spec,
            cache_spec,
        ] + extra_specs,
        out_specs=[pl.BlockSpec((rows, SWA_Q_W), row), out_spec, out_spec],
        out_shape=[jax.ShapeDtypeStruct((nseq * steps, SWA_Q_W), BF16), stacked, stacked],
        input_output_aliases={n_in + i: 1 + i for i in range(len(extra_in))},
        compiler_params=_params("parallel"),
        name="swa_decode",
    )(sinks, q, kn, vn, knt, vnt, kc, vc, *extra_in)


def _gla_out(o, gain, gate):
    return _rms(o, gain) * (gate * jax.nn.sigmoid(gate))


def _head_stack(x, width):
    return jnp.concatenate([x[:, h * width:(h + 1) * width] for h in range(GLA_HEADS)], axis=0)


def _head_masked_stack(x, head_of_lane):
    zero = jnp.zeros((), x.dtype)
    return jnp.concatenate([jnp.where(head_of_lane == h, x, zero) for h in range(GLA_HEADS)], axis=0)


def _gla_prompt_kernel(q_ref, k_ref, la_ref, v_ref, gg_ref, gn_ref, tril_ref, o_ref, s_ref, st_scr):
    tb = pl.program_id(1)

    @pl.when(tb == 0)
    def _():
        st_scr[...] = jnp.zeros_like(st_scr)

    n_tok = q_ref.shape[1]
    c_len = min(GLA_CHUNK, n_tok)
    tril = tril_ref[...]
    head_of_lane = lax.broadcasted_iota(jnp.int32, (1, GLA_K_W), 1) // GLA_DK
    ri = lax.broadcasted_iota(jnp.int32, (GLA_HEADS * c_len, c_len), 0) % c_len
    ci = lax.broadcasted_iota(jnp.int32, (GLA_HEADS * c_len, c_len), 1)
    causal = ri >= ci
    gain = gn_ref[...]
    n_seq = q_ref.shape[0]
    items = [(i, c) for i in range(n_seq) for c in range(n_tok // c_len)]
    rows = lambda c: slice(c * c_len, (c + 1) * c_len)
    b_all = []
    for i in range(n_seq):
        g_hi, g_lo = _split_bf16(la_ref[i])
        b_all.append(_dot(tril, g_hi) + _dot(tril, g_lo))
    qm, kdm, dec, a_raw = {}, {}, {}, {}
    for it in items:
        i, c = it
        b = b_all[i][rows(c), :]
        dec[it] = jnp.exp(b[c_len - 1:c_len, :])
        q_t = q_ref[i, rows(c), :] * jnp.exp(b)
        k_t = k_ref[i, rows(c), :] * jnp.exp(-b)
        qm[it] = _head_masked_stack(q_t, head_of_lane).astype(BF16)
        kdm[it] = _head_masked_stack(k_t * dec[it], head_of_lane).astype(BF16)
        a_raw[it] = _dot_tb(qm[it], k_t.astype(BF16))
    upd = {it: _dot_ta(_head_stack(v_ref[it[0], rows(it[1]), :], GLA_DV), kdm[it]) for it in items}
    intra = {}
    for it in items:
        i, c = it
        a = jnp.where(causal, a_raw[it], 0.0).astype(BF16)
        intra[it] = jnp.concatenate(
            [_dot(a[h * c_len:(h + 1) * c_len, :], v_ref[i, rows(c), h * GLA_DV:(h + 1) * GLA_DV])
             for h in range(GLA_HEADS)], axis=0)
    inter = {}
    for i in range(n_seq):
        st = st_scr[i]
        for c in range(n_tok // c_len):
            inter[(i, c)] = _dot_tb(qm[(i, c)], st.astype(BF16))
            st = dec[(i, c)] * st + upd[(i, c)]
        st_scr[i] = st
    for it in items:
        i, c = it
        y = _gla_out(inter[it] + intra[it], gain, _head_stack(gg_ref[i, rows(c), :], GLA_DV)).astype(BF16)
        for h in range(GLA_HEADS):
            o_ref[i, rows(c), h * GLA_DV:(h + 1) * GLA_DV] = y[h * c_len:(h + 1) * c_len, :]

    @pl.when(tb == pl.num_programs(1) - 1)
    def _():
        for i in range(q_ref.shape[0]):
            s_ref[i] = st_scr[i].T


def _gla_prompt(layer, q, k, la, v, gg, gain, tril, batch, seq):
    tb = tril.shape[0]
    per = min(GLA_SEQS, batch)
    blk = lambda b, t: (b, t, 0)
    kw = pl.BlockSpec((per, tb, GLA_K_W), blk)
    vw = pl.BlockSpec((per, tb, GLA_V_W), blk)
    return pl.pallas_call(
        _gla_prompt_kernel,
        grid=(batch // per, seq // tb),
        in_specs=[kw, kw, kw, vw, vw, _layer_resident(gain, layer), _resident(tril)],
        out_specs=[vw, pl.BlockSpec((per, GLA_K_W, GLA_DV), lambda b, t: (b, 0, 0))],
        out_shape=[
            jax.ShapeDtypeStruct((batch, seq, GLA_V_W), BF16),
            jax.ShapeDtypeStruct((batch, GLA_K_W, GLA_DV), F32),
        ],
        scratch_shapes=[pltpu.VMEM((per, GLA_DV, GLA_K_W), F32)],
        compiler_params=_params("parallel", "arbitrary"),
        name="gla_prompt",
    )(q, k, la, v, gg, gain, tril)


def _gla_decode_kernel(q_ref, k_ref, la_ref, v_ref, gg_ref, gn_ref, s_ref, *rest, steps, layer):
    o_ref, so_ref = rest[-2:]
    so_ref = _layer_view(so_ref, layer, len(rest) == 2)
    grp = s_ref.shape[0]
    rows = grp * steps
    stacked = GLA_HEADS * rows
    ri = lax.broadcasted_iota(jnp.int32, (rows, rows), 0)
    ci = lax.broadcasted_iota(jnp.int32, (rows, rows), 1)
    same_seq = ri // steps == ci // steps
    g_hi, g_lo = _split_bf16(la_ref[...])
    tril = (same_seq & (ri >= ci)).astype(BF16)
    total = same_seq.astype(BF16)
    b = _dot(tril, g_hi) + _dot(tril, g_lo)
    b_last = _dot(total, g_hi) + _dot(total, g_lo)
    head_of_lane = lax.broadcasted_iota(jnp.int32, (1, GLA_K_W), 1) // GLA_DK
    k_t = k_ref[...] * jnp.exp(-b)
    qm = _head_masked_stack(q_ref[...] * jnp.exp(b), head_of_lane).astype(BF16)
    km = _head_masked_stack(k_t, head_of_lane).astype(BF16)
    kdm = _head_masked_stack(k_t * jnp.exp(b_last), head_of_lane).astype(BF16)
    v_st = _head_stack(v_ref[...], GLA_DV)
    seq_of_row = (lax.broadcasted_iota(jnp.int32, (stacked, 1), 0) % rows) // steps
    seq_of_g = lax.broadcasted_iota(jnp.int32, (rows, 1), 0) // steps
    zero = jnp.zeros((), BF16)
    rhs = jnp.concatenate([
        jnp.concatenate([v_st, jnp.zeros((stacked, GLA_DV), BF16)], axis=1),
        jnp.concatenate([jnp.zeros((2 * rows, GLA_DV), BF16), jnp.ones((2 * rows, GLA_DV), BF16)], axis=1),
    ], axis=0)
    a_raw = _dot_tb(qm, km)
    states = [s_ref[s].reshape(GLA_K_W, GLA_DV) for s in range(grp)]
    inter_all = [_dot(qm, st.astype(BF16)) for st in states]
    upd_all = []
    for s in range(grp):
        lhs = jnp.concatenate([jnp.where(seq_of_row == s, kdm, zero),
                               jnp.where(seq_of_g == s, g_hi, zero),
                               jnp.where(seq_of_g == s, g_lo, zero)], axis=0)
        upd_all.append(_dot_ta(lhs, rhs))
    rr = lax.broadcasted_iota(jnp.int32, (stacked, stacked), 0) % rows
    cc = lax.broadcasted_iota(jnp.int32, (stacked, stacked), 1) % rows
    causal = (rr // steps == cc // steps) & (rr >= cc)
    o = _dot(jnp.where(causal, a_raw, 0.0).astype(BF16), v_st)
    inter = inter_all[0]
    for s in range(1, grp):
        inter = jnp.where(seq_of_row == s, inter_all[s], inter)
    y = _gla_out(o + inter, gn_ref[...], _head_stack(gg_ref[...], GLA_DV)).astype(BF16)
    for h in range(GLA_HEADS):
        o_ref[:, h * GLA_DV:(h + 1) * GLA_DV] = y[h * rows:(h + 1) * rows, :]
    for s in range(grp):
        new = jnp.exp(upd_all[s][:, GLA_DV:]) * states[s] + upd_all[s][:, :GLA_DV]
        so_ref[s] = new.reshape(GLA_HEADS, GLA_DK, GLA_DV)


def _gla_decode(layer, q, k, la, v, gg, gain, state, steps, prev):
    nseq = state.shape[1]
    grp = min(GLA_DEC_GROUP, nseq)
    rows = grp * steps
    row = lambda i: (i, 0)
    st_spec = pl.BlockSpec((None, grp, GLA_HEADS, GLA_DK, GLA_DV), lambda i: (layer, i, 0, 0, 0))
    extra_in, extra_specs, out_spec, stacked = _stacked_out(
        state.shape, (grp, GLA_HEADS, GLA_DK, GLA_DV), layer, prev)
    n_in = 7
    return pl.pallas_call(
        functools.partial(_gla_decode_kernel, steps=steps, layer=layer),
        grid=(nseq // grp,),
        in_specs=[
            pl.BlockSpec((rows, GLA_K_W), row),
            pl.BlockSpec((rows, GLA_K_W), row),
            pl.BlockSpec((rows, GLA_K_W), row),
            pl.BlockSpec((rows, GLA_V_W), row),
            pl.BlockSpec((rows, GLA_V_W), row),
            _layer_resident(gain, layer),
            st_spec,
        ] + extra_specs,
        out_specs=[pl.BlockSpec((rows, GLA_V_W), row), out_spec],
        out_shape=[jax.ShapeDtypeStruct((nseq * steps, GLA_V_W), BF16), stacked],
        input_output_aliases={n_in + i: 1 + i for i in range(len(extra_in))},
        compiler_params=_params("parallel"),
        name="gla_decode",
    )(q, k, la, v, gg, gain, state, *extra_in)


def _outproj_apply(x, a_ref, o_ref, wa_ref, wo_ref, g_ref, wq_ref, qn_ref):
    x1 = x + _dot(a_ref[...], wa_ref[...]) + _dot(o_ref[...], wo_ref[...])
    h = _rms(x1, g_ref[...]).astype(BF16)
    qn = qn_ref[...]
    q = [_rms(_dot(h, wq_ref[:, hd * XA_HEAD_DIM:(hd + 1) * XA_HEAD_DIM]), qn).astype(BF16)
         for hd in range(XA_HEADS)]
    return x1, q


def _outproj_kernel(x_ref, a_ref, o_ref, wa_ref, wo_ref, g_ref, wq_ref, qn_ref, x1_ref, q_ref):
    x1, q = _outproj_apply(x_ref[...], a_ref, o_ref, wa_ref, wo_ref, g_ref, wq_ref, qn_ref)
    x1_ref[...] = x1
    for hd in range(XA_HEADS):
        q_ref[:, hd * XA_HEAD_DIM:(hd + 1) * XA_HEAD_DIM] = q[hd]


def _outproj(x, layer, a, o, w_a, w_o, g, wq, qn):
    n, d = x.shape
    tm = min(ROW_TILE, n)
    row = lambda i: (i, 0)
    return pl.pallas_call(
        _outproj_kernel,
        grid=(n // tm,),
        in_specs=[
            pl.BlockSpec((tm, d), row),
            pl.BlockSpec((tm, SWA_Q_W), row),
            pl.BlockSpec((tm, GLA_V_W), row),
        ] + [_layer_resident(p, layer) for p in (w_a, w_o, g, wq, qn)],
        out_specs=[pl.BlockSpec((tm, d), row), pl.BlockSpec((tm, XA_W), row)],
        out_shape=[jax.ShapeDtypeStruct((n, d), F32), jax.ShapeDtypeStruct((n, XA_W), BF16)],
        compiler_params=_params("parallel"),
        name="outproj",
    )(x, a, o, w_a, w_o, g, wq, qn)


def _memkv_kernel(m_ref, g_ref, wk_ref, wv_ref, kn_ref, k_ref, v_ref):
    m = _rms(m_ref[...], g_ref[...]).astype(BF16)
    kn = kn_ref[...]
    for hd in range(XA_HEADS):
        sl = slice(hd * XA_HEAD_DIM, (hd + 1) * XA_HEAD_DIM)
        k_ref[:, sl] = _rms(_dot(m, wk_ref[:, sl]), kn)
    v_ref[...] = _dot(m, wv_ref[...])


def _memkv(mem, g, wk, wv, kn):
    depth = wk.shape[0]
    n, d = mem.shape
    tm = min(ROW_TILE, n)
    per_layer = lambda l, i: (l, 0, 0)
    out = lambda l, i: (l, i, 0)
    return pl.pallas_call(
        _memkv_kernel,
        grid=(depth, n // tm),
        in_specs=[
            pl.BlockSpec((tm, d), lambda l, i: (i, 0)),
            pl.BlockSpec((None, 1, d), per_layer),
            pl.BlockSpec((None, d, XA_W), per_layer),
            pl.BlockSpec((None, d, XA_W), per_layer),
            pl.BlockSpec((None, 1, XA_HEAD_DIM), per_layer),
        ],
        out_specs=[pl.BlockSpec((None, tm, XA_W), out), pl.BlockSpec((None, tm, XA_W), out)],
        out_shape=[jax.ShapeDtypeStruct((depth, n, XA_W), F32)] * 2,
        compiler_params=_params("parallel", "parallel"),
        name="memkv",
    )(mem, g, wk, wv, kn)


def _mixout_prompt_kernel(x_ref, a_ref, o_ref, wa_ref, wo_ref, g_ref, wq_ref, qn_ref, mk_ref, mv_ref,
                          xwo_ref, fg_ref, wg_ref, wu_ref, wd_ref, out_ref, att_scr, a_scr):
    x1, q = _outproj_apply(x_ref[...], a_ref, o_ref, wa_ref, wo_ref, g_ref, wq_ref, qn_ref)
    mk = mk_ref[...].astype(BF16)
    mv = mv_ref[...].astype(BF16)
    heads = range(XA_HEADS)
    cols = lambda hd: slice(hd * XA_HEAD_DIM, (hd + 1) * XA_HEAD_DIM)
    scores = [_dot_tb(q[hd], mk[:, cols(hd)]) * (XA_HEAD_DIM ** -0.5 * LOG2E) for hd in heads]
    probs = []
    for s in scores:
        e = jnp.exp2(s - jnp.max(s, axis=-1, keepdims=True))
        probs.append((e.astype(BF16), jnp.sum(e, axis=-1, keepdims=True)))
    for hd in heads:
        att_scr[:, cols(hd)] = (_dot(probs[hd][0], mv[:, cols(hd)]) / probs[hd][1]).astype(BF16)
    x2 = x1 + _dot(att_scr[...], xwo_ref[...])
    out_ref[...] = _ffn_apply(x2, fg_ref, wg_ref, wu_ref, wd_ref, a_scr)


def _mixout_prompt(x, layer, a, o, outproj, mk, mv, wo, ffn, seq):
    n, d = x.shape
    dff = ffn[1].shape[2]
    tm = min(ROW_TILE, seq)
    per_seq = seq // tm
    row = lambda i: (i, 0)
    mem_spec = pl.BlockSpec((None, None, mk.shape[2], XA_W), lambda i: (layer, i // per_seq, 0, 0))
    return pl.pallas_call(
        _mixout_prompt_kernel,
        grid=(n // tm,),
        in_specs=([pl.BlockSpec((tm, d), row), pl.BlockSpec((tm, SWA_Q_W), row), pl.BlockSpec((tm, GLA_V_W), row)]
                  + [_layer_resident(p, layer) for p in outproj]
                  + [mem_spec, mem_spec]
                  + [_layer_resident(p, layer) for p in (wo,) + ffn]),
        out_specs=pl.BlockSpec((tm, d), row),
        out_shape=jax.ShapeDtypeStruct((n, d), F32),
        scratch_shapes=[pltpu.VMEM((tm, XA_W), BF16), pltpu.VMEM((tm, dff), BF16)],
        compiler_params=_params("parallel"),
        name="mixout_prompt",
    )(x, a, o, *outproj, mk, mv, wo, *ffn)


def _xattn_decode_kernel(q_ref, mk_ref, mv_ref, o_ref, *, steps):
    grp, nkeys = mk_ref.shape[0], mk_ref.shape[1]
    rows = grp * steps
    q = jnp.concatenate([q_ref[:, hd * XA_HEAD_DIM:(hd + 1) * XA_HEAD_DIM] for hd in range(XA_HEADS)], axis=0)
    r = lax.broadcasted_iota(jnp.int32, (XA_HEADS * rows, 1), 0)
    own = (r % rows) // steps
    same_head = (r // rows) == (lax.broadcasted_iota(jnp.int32, (1, nkeys), 1) % XA_HEADS)
    s = None
    for j in range(grp):
        sj = _dot_tb(q, mk_ref[j].astype(BF16))
        s = sj if s is None else jnp.where(own == j, sj, s)
    s = jnp.where(same_head, s * (XA_HEAD_DIM ** -0.5 * LOG2E), -jnp.inf)
    m = jnp.max(s, axis=-1, keepdims=True)
    e = jnp.exp2(s - m)
    p = e.astype(BF16)
    o = None
    for j in range(grp):
        oj = _dot(p, mv_ref[j].astype(BF16))
        o = oj if o is None else jnp.where(own == j, oj, o)
    o = o / jnp.sum(e, axis=-1, keepdims=True)
    for hd in range(XA_HEADS):
        o_ref[:, hd * XA_HEAD_DIM:(hd + 1) * XA_HEAD_DIM] = o[hd * rows:(hd + 1) * rows, :].astype(BF16)


def _xattn_decode(q, layer, mk, mv, steps):
    nseq, nkeys = mk.shape[1], mk.shape[2]
    grp = XA_DEC_GROUP
    rows = grp * steps
    row = lambda i: (i, 0)
    mem_spec = pl.BlockSpec((None, grp, nkeys, XA_HEAD_DIM), lambda i: (layer, i, 0, 0))
    return pl.pallas_call(
        functools.partial(_xattn_decode_kernel, steps=steps),
        grid=(nseq // grp,),
        in_specs=[pl.BlockSpec((rows, XA_W), row), mem_spec, mem_spec],
        out_specs=pl.BlockSpec((rows, XA_W), row),
        out_shape=jax.ShapeDtypeStruct((nseq * steps, XA_W), BF16),
        compiler_params=_params("parallel"),
        name="xattn_decode",
    )(q, mk, mv)


def _proj_res_kernel(x_ref, a_ref, w_ref, o_ref):
    o_ref[...] = x_ref[...] + _dot(a_ref[...], w_ref[...])


def _proj_res(x, layer, a, w):
    n, d = x.shape
    tm = min(ROW_TILE, n)
    row = lambda i: (i, 0)
    return pl.pallas_call(
        _proj_res_kernel,
        grid=(n // tm,),
        in_specs=[pl.BlockSpec((tm, d), row), pl.BlockSpec((tm, a.shape[1]), row), _layer_resident(w, layer)],
        out_specs=pl.BlockSpec((tm, d), row),
        out_shape=jax.ShapeDtypeStruct((n, d), F32),
        compiler_params=_params("parallel"),
        name="proj_res",
    )(x, a, w)


def _rope_tables(pos):
    half = HEAD_DIM // 2
    inv = ROPE_THETA ** (-jnp.arange(half, dtype=F32) / half)
    ang = pos.astype(F32)[:, None] * inv[None, :]
    cos, sin = jnp.cos(ang), jnp.sin(ang)
    reps = LANES // HEAD_DIM
    return jnp.tile(cos, (1, 2 * reps)), jnp.tile(jnp.concatenate([-sin, sin], axis=-1), (1, reps))


def _block_tril(n_blocks, size):
    i = jnp.arange(n_blocks * size)
    return ((i[:, None] // size == i[None, :] // size) & (i[:, None] >= i[None, :])).astype(BF16)


def _permute_heads(w, axis):
    blocks = jnp.split(w, SWA_Q_HEADS, axis=axis)
    return jnp.concatenate([blocks[h] for h in SWA_HEAD_ORDER], axis=axis)


def kernel(x_prompt, x_sample, cache_swa_k, cache_swa_v, state_gla, cache_mem_k, cache_mem_v, mem_prompt, ffn1_norm, ffn1_wg, ffn1_wu, ffn1_wd, mix_norm, w_in, swa_q_norm, swa_k_norm, swa_sinks, gla_w_gate, gla_b_gate, gla_out_norm, w_out, xa_norm, mem_norm, xa_wq, xa_wk, xa_wv, xa_q_norm, xa_k_norm, xa_wo, ffn2_norm, ffn2_wg, ffn2_wu, ffn2_wd):
    batch, seq, d = x_prompt.shape
    nseq, steps, _ = x_sample.shape
    depth = w_in.shape[0]
    mem_len = mem_prompt.shape[1]

    bf = lambda w: w.astype(BF16)
    vec = lambda p: p[:, None, :]
    ffn1 = (vec(ffn1_norm), bf(ffn1_wg), bf(ffn1_wu), bf(ffn1_wd))
    ffn2 = (vec(ffn2_norm), bf(ffn2_wg), bf(ffn2_wu), bf(ffn2_wd))
    w_all = jnp.pad(bf(w_in), ((0, 0), (0, 0), (0, MXU_TILE - GLA_LOWRANK)))
    gate_w = jnp.pad(bf(gla_w_gate), ((0, 0), (0, LANES - GLA_LOWRANK), (0, 0)))
    w_a = bf(_permute_heads(w_out[:, :SWA_Q_W], 1))
    w_o = bf(w_out[:, SWA_Q_W:])
    wq_b, wk_b, wv_b, wo_b = bf(xa_wq), bf(xa_wk), bf(xa_wv), bf(xa_wo)
    qn = vec(jnp.tile(swa_q_norm * (HEAD_DIM ** -0.5 * LOG2E), (1, LANES // HEAD_DIM)))
    kn = vec(jnp.tile(swa_k_norm, (1, LANES // HEAD_DIM)))
    inproj_params = (vec(mix_norm), w_all, gate_w, vec(gla_b_gate), qn, kn)
    outproj_params = (w_a, w_o, vec(xa_norm), wq_b, vec(xa_q_norm))
    gla_gain = vec(gla_out_norm)

    lane = jnp.arange(MXU_TILE)
    bd = (lane[:, None] // HEAD_DIM == lane[None, :] // HEAD_DIM).astype(BF16)
    cos_p, sin_p = _rope_tables(jnp.arange(seq))
    cos_s, sin_s = _rope_tables(PAST_LEN + jnp.arange(nseq * steps) % steps)
    tril_p = _block_tril(min(GLA_BLOCK, seq) // min(GLA_CHUNK, seq), min(GLA_CHUNK, seq))

    mk_p, mv_p = _memkv(mem_prompt.reshape(batch * mem_len, d), vec(mem_norm), wk_b, wv_b, vec(xa_k_norm))
    mk_p = mk_p.reshape(depth, batch, mem_len, XA_W)
    mv_p = mv_p.reshape(depth, batch, mem_len, XA_W)
    mk_s = cache_mem_k.reshape(depth, nseq, mem_len * XA_HEADS, XA_HEAD_DIM)
    mv_s = cache_mem_v.reshape(depth, nseq, mem_len * XA_HEADS, XA_HEAD_DIM)
    native = lambda c: jnp.transpose(c, (0, 1, 3, 4, 2)).reshape(depth, nseq, SWA_KV_W, WINDOW)
    kc_s, vc_s = native(cache_swa_k), native(cache_swa_v)

    xp = x_prompt.reshape(batch * seq, d)
    xs = x_sample.reshape(nseq * steps, d)
    kp_l, vp_l, sp_l = [], [], []
    swa_new = gla_new = None
    for l in range(depth):
        xp, q_s, k_s, v_s, q_g, k_g, v_g, g_g, la = _ffn_inproj(xp, l, ffn1, inproj_params, cos_p, sin_p, bd)
        a_p = _swa_prompt(swa_sinks, l, q_s, k_s, v_s, batch, seq)
        seqs = lambda a: a.reshape(batch, seq, a.shape[-1])
        o_p, s_p = _gla_prompt(l, seqs(q_g), seqs(k_g), seqs(la), seqs(v_g), seqs(g_g), gla_gain, tril_p,
                               batch, seq)
        o_p = o_p.reshape(batch * seq, GLA_V_W)
        last = lambda a: seqs(a)[:, seq - WINDOW:].reshape(batch, WINDOW, SWA_KV_HEADS, HEAD_DIM)
        kp_l.append(last(k_s))
        vp_l.append(last(v_s))
        sp_l.append(s_p.reshape(batch, GLA_HEADS, GLA_DK, GLA_DV))
        xp = _mixout_prompt(xp, l, a_p, o_p, outproj_params, mk_p, mv_p, wo_b, ffn2, seq)

        xs, q_s, k_s, v_s, q_g, k_g, v_g, g_g, la, k_t, v_t = _ffn_inproj(
            xs, l, ffn1, inproj_params, cos_s, sin_s, bd, transposed_kv=True)
        a_s, *swa_new = _swa_decode(swa_sinks, l, q_s, k_s, v_s, k_t, v_t, kc_s, vc_s, steps, swa_new)
        o_s, *gla_new = _gla_decode(l, q_g, k_g, la, v_g, g_g, gla_gain, state_gla, steps, gla_new)
        xs, q_x = _outproj(xs, l, a_s, o_s, *outproj_params)
        xs = _proj_res(xs, l, _xattn_decode(q_x, l, mk_s, mv_s, steps), wo_b)
        xs = _ffn(xs, l, *ffn2)

    unnative = lambda c: jnp.transpose(c.reshape(depth, nseq, SWA_KV_HEADS, HEAD_DIM, WINDOW), (0, 1, 4, 2, 3))
    return (xp.reshape(batch, seq, d), xs.reshape(nseq, steps, d),
            jnp.stack(kp_l), jnp.stack(vp_l), jnp.stack(sp_l),
            mk_p.reshape(depth, batch, mem_len, XA_HEADS, XA_HEAD_DIM),
            mv_p.reshape(depth, batch, mem_len, XA_HEADS, XA_HEAD_DIM),
            unnative(swa_new[0]), unnative(swa_new[1]), gla_new[0])
```

```python
import functools

import jax
import jax.numpy as jnp
from jax import lax
from jax.experimental import pallas as pl
from jax.experimental.pallas import tpu as pltpu

F32 = jnp.float32
BF16 = jnp.bfloat16

EPS = 1e-6
LOG2E = 1.4426950408889634
PAST_LEN = 16384
WINDOW = 128
ROPE_THETA = 10000.0
HEAD_DIM = 64
SWA_Q_HEADS = 8
SWA_KV_HEADS = 2
SWA_GROUP = SWA_Q_HEADS // SWA_KV_HEADS
GLA_HEADS = 4
GLA_DK = 64
GLA_DV = 128
GLA_LOWRANK = 16
GLA_GATE_TEMP = 16.0
GLA_CHUNK = 64
XA_HEADS = 4
XA_HEAD_DIM = 128

SWA_Q_W = SWA_Q_HEADS * HEAD_DIM
SWA_KV_W = SWA_KV_HEADS * HEAD_DIM
GLA_K_W = GLA_HEADS * GLA_DK
GLA_V_W = GLA_HEADS * GLA_DV
XA_W = XA_HEADS * XA_HEAD_DIM
MAIN_W = SWA_Q_W + 2 * SWA_KV_W + 2 * GLA_K_W + 2 * GLA_V_W

LANES = 128
MXU_TILE = 256
VMEM_LIMIT = 56 * 1024 * 1024

ROW_TILE = 512
ROW_SPLIT = 2
FFN_CHUNK = 256
SWA_BLOCK = 256
GLA_BLOCK = 256
GLA_SEQS = 4
DEC_GROUP = 4
SWA_DEC_SUBGROUPS = 8
XA_DEC_GROUP = 8
GLA_DEC_GROUP = 8

assert SWA_KV_HEADS * HEAD_DIM == LANES
SWA_HEAD_ORDER = tuple(kv * SWA_GROUP + g for g in range(SWA_GROUP) for kv in range(SWA_KV_HEADS))


def _dot(a, b):
    return jnp.dot(a, b, preferred_element_type=F32)


def _dot_tb(a, b):
    return lax.dot_general(a, b, (((1,), (1,)), ((), ())), preferred_element_type=F32)


def _dot_ta(a, b):
    return lax.dot_general(a, b, (((0,), (0,)), ((), ())), preferred_element_type=F32)


def _split_bf16(x):
    hi = x.astype(BF16)
    lo = (x - hi.astype(F32)).astype(BF16)
    return hi, lo


def _rms(x, g):
    ms = jnp.mean(x * x, axis=-1, keepdims=True)
    return x * lax.rsqrt(ms + EPS) * g


def _params(*sem):
    return pltpu.CompilerParams(dimension_semantics=sem, vmem_limit_bytes=VMEM_LIMIT)


def _resident(arr):
    nd = arr.ndim
    return pl.BlockSpec(arr.shape, lambda *_: (0,) * nd, pipeline_mode=pl.Buffered(1))


def _layer_resident(arr, layer):
    nd = arr.ndim
    return pl.BlockSpec((None,) + arr.shape[1:], lambda *_: (layer,) + (0,) * (nd - 1),
                        pipeline_mode=pl.Buffered(1))


def _ffn_hidden(x, g_ref, wg_ref, wu_ref, a_scr):
    h = _rms(x, g_ref[...]).astype(BF16)
    dff = wg_ref.shape[1]
    for c in range(dff // FFN_CHUNK):
        sl = slice(c * FFN_CHUNK, (c + 1) * FFN_CHUNK)
        g = _dot(h, wg_ref[:, sl])
        u = _dot(h, wu_ref[:, sl])
        a_scr[:, sl] = (g * jax.nn.sigmoid(g) * u).astype(BF16)


def _ffn_apply(x, g_ref, wg_ref, wu_ref, wd_ref, a_scr):
    _ffn_hidden(x, g_ref, wg_ref, wu_ref, a_scr)
    return x + 0.5 * _dot(a_scr[...], wd_ref[...])


def _ffn_kernel(x_ref, g_ref, wg_ref, wu_ref, wd_ref, o_ref, a_scr):
    o_ref[...] = _ffn_apply(x_ref[...], g_ref, wg_ref, wu_ref, wd_ref, a_scr)


def _ffn(x, layer, g, wg, wu, wd):
    n, d = x.shape
    dff = wg.shape[2]
    tm = min(ROW_TILE, n)
    return pl.pallas_call(
        _ffn_kernel,
        grid=(n // tm,),
        in_specs=[pl.BlockSpec((tm, d), lambda i: (i, 0))] + [_layer_resident(a, layer) for a in (g, wg, wu, wd)],
        out_specs=pl.BlockSpec((tm, d), lambda i: (i, 0)),
        out_shape=jax.ShapeDtypeStruct((n, d), F32),
        scratch_shapes=[pltpu.VMEM((tm, dff), BF16)],
        compiler_params=_params("parallel"),
        name="ffn",
    )(x, g, wg, wu, wd)


def _ffn_inproj_kernel(x_ref, fg_ref, wg_ref, wu_ref, wd_ref, g_ref, w_ref, gw_ref, gb_ref, qn_ref, kn_ref,
                       cos_ref, sin_ref, bd_ref, xo_ref, qs_ref, ks_ref, vs_ref, qg_ref, kg_ref, vg_ref,
                       gg_ref, la_ref, *rest):
    a_scr = rest[-1]
    kt_ref, vt_ref = rest[:-1] if len(rest) > 1 else (None, None)
    x = x_ref[...]
    _ffn_hidden(x, fg_ref, wg_ref, wu_ref, a_scr)
    n = x.shape[0] // ROW_SPLIT
    groups = [slice(r * n, (r + 1) * n) for r in range(ROW_SPLIT)]
    mid = [x[rs] + 0.5 * _dot(a_scr[rs, :], wd_ref[...]) for rs in groups]
    for rs, xm in zip(groups, mid):
        xo_ref[rs, :] = xm
    split = SWA_Q_W + 2 * SWA_KV_W + 2 * GLA_K_W
    h = [_rms(xm, g_ref[...]).astype(BF16) for xm in mid]
    z1 = [_dot(hr, w_ref[:, :split]) for hr in h]
    z2 = []
    for rs, hr, z in zip(groups, h, z1):
        _inproj_finish_swa(z, rs, qn_ref, kn_ref, cos_ref, sin_ref, bd_ref, qs_ref, ks_ref, vs_ref, qg_ref,
                           kg_ref, kt_ref, vt_ref)
        z2.append(_dot(hr, w_ref[:, split:]))
    for rs, z in zip(groups, z2):
        _inproj_finish_gla(z, rs, gw_ref, gb_ref, vg_ref, gg_ref, la_ref)


def _inproj_finish_swa(z1, rs, qn_ref, kn_ref, cos_ref, sin_ref, bd_ref, qs_ref, ks_ref, vs_ref, qg_ref,
                       kg_ref, kt_ref, vt_ref):
    bd = bd_ref[...]
    cos = cos_ref[rs, :]
    sin = sin_ref[rs, :]
    lane = lax.broadcasted_iota(jnp.int32, cos.shape, 1)
    lane_lo = (lane % HEAD_DIM) < (HEAD_DIM // 2)

    def head_scale(z):
        ss = _dot((z * z).astype(BF16), bd)
        return lax.rsqrt(ss * (1.0 / HEAD_DIM) + EPS)

    def rope(y):
        swapped = jnp.where(lane_lo, pltpu.roll(y, LANES - HEAD_DIM // 2, axis=1),
                            pltpu.roll(y, HEAD_DIM // 2, axis=1))
        return y * cos + swapped * sin

    qn = qn_ref[...]
    nat = []
    for t in range(SWA_Q_W // MXU_TILE):
        z = z1[:, t * MXU_TILE:(t + 1) * MXU_TILE]
        y = z * head_scale(z)
        nat += [rope(y[:, c * LANES:(c + 1) * LANES] * qn) for c in range(MXU_TILE // LANES)]
    low = _low_half()
    per_col = LANES // HEAD_DIM
    for c in range(SWA_Q_W // LANES):
        halves = []
        for half, head in enumerate(SWA_HEAD_ORDER[per_col * c:per_col * (c + 1)]):
            col = nat[head // per_col]
            halves.append(col if head % per_col == half else pltpu.roll(col, HEAD_DIM, axis=1))
        qs_ref[rs, c * LANES:(c + 1) * LANES] = jnp.where(low, halves[0], halves[1]).astype(BF16)
    o = SWA_Q_W
    z = z1[:, o:o + 2 * SWA_KV_W]
    k = rope(z[:, :SWA_KV_W] * head_scale(z)[:, :SWA_KV_W] * kn_ref[...])
    ks_ref[rs, :] = k
    vs_ref[rs, :] = z[:, SWA_KV_W:]
    if kt_ref is not None:
        kt_ref[:, rs] = k.T
        vt_ref[:, rs] = z[:, SWA_KV_W:].T
    o += 2 * SWA_KV_W
    qg_ref[rs, :] = z1[:, o:o + GLA_K_W] * (GLA_DK ** -0.5)
    o += GLA_K_W
    kg_ref[rs, :] = z1[:, o:o + GLA_K_W]


def _inproj_finish_gla(z2, rs, gw_ref, gb_ref, vg_ref, gg_ref, la_ref):
    vg_ref[rs, :] = z2[:, :GLA_V_W].astype(BF16)
    gg_ref[rs, :] = z2[:, GLA_V_W:2 * GLA_V_W]
    lr = z2[:, 2 * GLA_V_W:2 * GLA_V_W + LANES].astype(BF16)
    t = _dot(lr, gw_ref[...]) + gb_ref[...]
    log_sig = jnp.minimum(t, 0.0) - jnp.log(1.0 + jnp.exp(-jnp.abs(t)))
    la_ref[rs, :] = log_sig * (1.0 / GLA_GATE_TEMP)


def _ffn_inproj(x, layer, ffn, inproj, cos, sin, bd, transposed_kv=False):
    n, d = x.shape
    dff = ffn[1].shape[2]
    tm = min(ROW_TILE, n)
    pos_blocks = cos.shape[0] // tm
    row = lambda i: (i, 0)
    pos = lambda i: (i % pos_blocks, 0)
    widths = (d, SWA_Q_W, SWA_KV_W, SWA_KV_W, GLA_K_W, GLA_K_W, GLA_V_W, GLA_V_W, GLA_K_W)
    dtypes = (F32, BF16, F32, F32, F32, F32, BF16, F32, F32)
    out_specs = [pl.BlockSpec((tm, w), row) for w in widths]
    out_shape = [jax.ShapeDtypeStruct((n, w), dt) for w, dt in zip(widths, dtypes)]
    if transposed_kv:
        out_specs += [pl.BlockSpec((SWA_KV_W, tm), lambda i: (0, i))] * 2
        out_shape += [jax.ShapeDtypeStruct((SWA_KV_W, n), F32)] * 2
    return pl.pallas_call(
        _ffn_inproj_kernel,
        grid=(n // tm,),
        in_specs=([pl.BlockSpec((tm, d), row)]
                  + [_layer_resident(a, layer) for a in ffn + inproj]
                  + [pl.BlockSpec((tm, LANES), pos), pl.BlockSpec((tm, LANES), pos), _resident(bd)]),
        out_specs=out_specs,
        out_shape=out_shape,
        scratch_shapes=[pltpu.VMEM((tm, dff), BF16)],
        compiler_params=_params("parallel"),
        name="ffn_inproj",
    )(x, *ffn, *inproj, cos, sin, bd)


def _swa_attend(q, k, v, valid, sinks):
    return _swa_out(*_swa_probs(_swa_scores(q, k), valid, sinks), v)


def _low_half():
    return lax.broadcasted_iota(jnp.int32, (1, LANES), 1) < HEAD_DIM


def _swa_head_rows(q):
    low = _low_half()
    zero = jnp.zeros((), q.dtype)
    pieces = []
    for c in range(SWA_Q_W // LANES):
        qc = q[:, c * LANES:(c + 1) * LANES]
        pieces += [jnp.where(low, qc, zero), jnp.where(low, zero, qc)]
    return jnp.concatenate(pieces, axis=0)


def _swa_scores(q, k):
    return _dot_tb(_swa_head_rows(q), k)


def _swa_probs(s, valid, sinks):
    rows, keys = valid.shape
    s = jnp.where(valid[None], s.reshape(SWA_Q_HEADS, rows, keys), -jnp.inf)
    m = jnp.maximum(jnp.max(s, axis=-1, keepdims=True), sinks)
    e = jnp.exp2(s - m)
    den = jnp.sum(e, axis=-1, keepdims=True) + jnp.exp2(sinks - m)
    return e.astype(BF16).reshape(SWA_Q_HEADS * rows, keys), den


def _swa_out(e, den, v):
    return _swa_finish(_dot(e, v), den)


def _swa_finish(o, den):
    low = _low_half()
    o = o.reshape(den.shape[0], den.shape[1], LANES) / den
    return [jnp.where(low, o[2 * c], o[2 * c + 1]) for c in range(SWA_Q_W // LANES)]


def _sink_column(sink_ref, layer):
    idx = lax.broadcasted_iota(jnp.int32, (SWA_Q_HEADS, 1, 1), 0)
    col = jnp.zeros((SWA_Q_HEADS, 1, 1), F32)
    for p, head in enumerate(SWA_HEAD_ORDER):
        col = jnp.where(idx == p, sink_ref[layer, head] * LOG2E, col)
    return col


def _swa_prompt_kernel(sink_ref, q_ref, kp_ref, kc_ref, vp_ref, vc_ref, o_ref, *, layer):
    j = pl.program_id(1)
    kall = jnp.concatenate([kp_ref[...], kc_ref[...]], axis=0).astype(BF16)
    vall = jnp.concatenate([vp_ref[...], vc_ref[...]], axis=0).astype(BF16)
    sinks = _sink_column(sink_ref, layer)
    upper = (lax.broadcasted_iota(jnp.int32, (WINDOW, WINDOW), 1)
             > lax.broadcasted_iota(jnp.int32, (WINDOW, WINDOW), 0))[None]

    def banded_probs(s, has_prev):
        s = s.reshape(SWA_Q_HEADS, WINDOW, 2 * WINDOW)
        prev = s[:, :, :WINDOW]
        if has_prev is not True:
            prev = jnp.where(has_prev, prev, -jnp.inf)
        logits = jnp.where(upper, prev, s[:, :, WINDOW:])
        m = jnp.maximum(jnp.max(logits, axis=-1, keepdims=True), sinks)
        e = jnp.exp2(logits - m)
        den = jnp.sum(e, axis=-1, keepdims=True) + jnp.exp2(sinks - m)
        e = e.astype(BF16)
        zero = jnp.zeros((), BF16)
        e = jnp.concatenate([jnp.where(upper, e, zero), jnp.where(upper, zero, e)], axis=-1)
        return e.reshape(SWA_Q_HEADS * WINDOW, 2 * WINDOW), den

    windows = range(q_ref.shape[0] // WINDOW)
    rows = lambda w: slice(w * WINDOW, (w + 1) * WINDOW)
    keys = lambda w: slice(w * WINDOW, (w + 2) * WINDOW)
    scores = [_swa_scores(q_ref[rows(w), :], kall[keys(w)]) for w in windows]
    probs = [banded_probs(scores[w], True if w > 0 else j > 0) for w in windows]
    for w in windows:
        for c, col in enumerate(_swa_out(*probs[w], vall[keys(w)])):
            o_ref[rows(w), c * LANES:(c + 1) * LANES] = col.astype(BF16)


def _swa_prompt(sinks, layer, q, k, v, batch, seq):
    qb = min(SWA_BLOCK, seq)
    nb = seq // qb
    per = qb // WINDOW
    cur = lambda b, j: (b * nb + j, 0)
    prev = lambda b, j: ((b * nb + j) * per - jnp.minimum(j, 1), 0)
    return pl.pallas_call(
        functools.partial(_swa_prompt_kernel, layer=layer),
        grid=(batch, nb),
        in_specs=[
            pl.BlockSpec(memory_space=pltpu.SMEM),
            pl.BlockSpec((qb, SWA_Q_W), cur),
            pl.BlockSpec((WINDOW, SWA_KV_W), prev),
            pl.BlockSpec((qb, SWA_KV_W), cur),
            pl.BlockSpec((WINDOW, SWA_KV_W), prev),
            pl.BlockSpec((qb, SWA_KV_W), cur),
        ],
        out_specs=pl.BlockSpec((qb, SWA_Q_W), cur),
        out_shape=jax.ShapeDtypeStruct((batch * seq, SWA_Q_W), BF16),
        compiler_params=_params("parallel", "parallel"),
        name="swa_prompt",
    )(sinks, q, k, k, v, v)


def _layer_view(ref, layer, first):
    if not first:
        return ref
    for other in range(ref.shape[0]):
        if other != layer:
            ref[other] = jnp.zeros(ref.shape[1:], ref.dtype)
    return ref.at[layer]


def _swa_decode_kernel(sink_ref, q_ref, kn_ref, vn_ref, knt_ref, vnt_ref, kc_ref, vc_ref, *rest, steps, layer):
    o_ref, ko_ref, vo_ref = rest[-3:]
    first = len(rest) == 3
    ko_ref = _layer_view(ko_ref, layer, first)
    vo_ref = _layer_view(vo_ref, layer, first)
    n_seq = kc_ref.shape[0]
    grp = DEC_GROUP
    rows = grp * steps
    keep = WINDOW - steps
    lane = lax.broadcasted_iota(jnp.int32, (1, WINDOW), 1)
    for s in range(n_seq):
        tile = slice((s * steps) // LANES * LANES, (s * steps) // LANES * LANES + LANES)
        shift = (keep - s * steps) % LANES
        for cache_ref, new_ref, out_ref in ((kc_ref, knt_ref, ko_ref), (vc_ref, vnt_ref, vo_ref)):
            out_ref[s] = jnp.where(lane >= keep, pltpu.roll(new_ref[:, tile], shift, axis=1),
                                   pltpu.roll(cache_ref[s], keep, axis=1))
    nk = grp * WINDOW + rows
    r = lax.broadcasted_iota(jnp.int32, (rows, nk), 0)
    q_seq, q_step = r // steps, r % steps
    c = lax.broadcasted_iota(jnp.int32, (rows, nk), 1)
    is_new = c >= grp * WINDOW
    cn = c - grp * WINDOW
    k_seq = jnp.where(is_new, cn // steps, c // WINDOW)
    k_idx = jnp.where(is_new, WINDOW + cn % steps, c % WINDOW)
    rel = WINDOW + q_step - k_idx
    valid = (q_seq == k_seq) & (rel >= 0) & (rel < WINDOW)
    sinks = _sink_column(sink_ref, layer)
    groups = range(n_seq // grp)
    new_rows = lambda g: slice(g * rows, (g + 1) * rows)
    cached = lambda ref, g: jnp.concatenate([ref[g * grp + s] for s in range(grp)], axis=1).astype(BF16)
    scores = []
    for g in groups:
        qm = _swa_head_rows(q_ref[new_rows(g), :])
        scores.append(jnp.concatenate(
            [_dot(qm, cached(kc_ref, g)), _dot_tb(qm, kn_ref[new_rows(g), :].astype(BF16))], axis=1))
    probs = [_swa_probs(scores[g], valid, sinks) for g in groups]
    for g in groups:
        e, den = probs[g]
        o = (_dot_tb(e[:, :grp * WINDOW], cached(vc_ref, g))
             + _dot(e[:, grp * WINDOW:], vn_ref[new_rows(g), :].astype(BF16)))
        for c, col in enumerate(_swa_finish(o, den)):
            o_ref[new_rows(g), c * LANES:(c + 1) * LANES] = col.astype(BF16)


def _stacked_out(shape, block, layer, prev):
    tail = (0,) * (len(block) - 1)
    if prev is None:
        spec = pl.BlockSpec((shape[0],) + block, lambda i: (0, i) + tail)
        extra_inputs = []
    else:
        spec = pl.BlockSpec((None,) + block, lambda i: (layer, i) + tail)
        extra_inputs = list(prev)
    extra_specs = [pl.BlockSpec(memory_space=pl.ANY) for _ in extra_inputs]
    return extra_inputs, extra_specs, spec, jax.ShapeDtypeStruct(shape, F32)


def _swa_decode(sinks, layer, q, kn, vn, knt, vnt, kc, vc, steps, prev):
    depth, nseq = kc.shape[:2]
    grp = min(DEC_GROUP * SWA_DEC_SUBGROUPS, nseq)
    rows = grp * steps
    assert rows % LANES == 0, "a grid step's new keys must fill whole 128-lane tiles"
    row = lambda i: (i, 0)
    cache_spec = pl.BlockSpec((None, grp, SWA_KV_W, WINDOW), lambda i: (layer, i, 0, 0))
    new_t = pl.BlockSpec((SWA_KV_W, rows), lambda i: (0, i))
    extra_in, extra_specs, out_spec, stacked = _stacked_out(kc.shape, (grp, SWA_KV_W, WINDOW), layer, prev)
    n_in = 8
    return pl.pallas_call(
        functools.partial(_swa_decode_kernel, steps=steps, layer=layer),
        grid=(nseq // grp,),
        in_specs=[
            pl.BlockSpec(memory_space=pltpu.SMEM),
            pl.BlockSpec((rows, SWA_Q_W), row),
            pl.BlockSpec((rows, SWA_KV_W), row),
            pl.BlockSpec((rows, SWA_KV_W), row),
            new_t,
            new_t,
            cache_spec,
            cache_spec,
        ] + extra_specs,
        out_specs=[pl.BlockSpec((rows, SWA_Q_W), row), out_spec, out_spec],
        out_shape=[jax.ShapeDtypeStruct((nseq * steps, SWA_Q_W), BF16), stacked, stacked],
        input_output_aliases={n_in + i: 1 + i for i in range(len(extra_in))},
        compiler_params=_params("parallel"),
        name="swa_decode",
    )(sinks, q, kn, vn, knt, vnt, kc, vc, *extra_in)


def _gla_out(o, gain, gate):
    return _rms(o, gain) * (gate * jax.nn.sigmoid(gate))


def _head_stack(x, width):
    return jnp.concatenate([x[:, h * width:(h + 1) * width] for h in range(GLA_HEADS)], axis=0)


def _head_masked_stack(x, head_of_lane):
    zero = jnp.zeros((), x.dtype)
    return jnp.concatenate([jnp.where(head_of_lane == h, x, zero) for h in range(GLA_HEADS)], axis=0)


def _gla_prompt_kernel(q_ref, k_ref, la_ref, v_ref, gg_ref, gn_ref, tril_ref, o_ref, s_ref, st_scr):
    tb = pl.program_id(1)

    @pl.when(tb == 0)
    def _():
        st_scr[...] = jnp.zeros_like(st_scr)

    n_tok = q_ref.shape[1]
    c_len = min(GLA_CHUNK, n_tok)
    tril = tril_ref[...]
    head_of_lane = lax.broadcasted_iota(jnp.int32, (1, GLA_K_W), 1) // GLA_DK
    ri = lax.broadcasted_iota(jnp.int32, (GLA_HEADS * c_len, c_len), 0) % c_len
    ci = lax.broadcasted_iota(jnp.int32, (GLA_HEADS * c_len, c_len), 1)
    causal = ri >= ci
    gain = gn_ref[...]
    n_seq = q_ref.shape[0]
    items = [(i, c) for i in range(n_seq) for c in range(n_tok // c_len)]
    rows = lambda c: slice(c * c_len, (c + 1) * c_len)
    b_all = []
    for i in range(n_seq):
        g_hi, g_lo = _split_bf16(la_ref[i])
        b_all.append(_dot(tril, g_hi) + _dot(tril, g_lo))
    qm, kdm, dec, a_raw = {}, {}, {}, {}
    for it in items:
        i, c = it
        b = b_all[i][rows(c), :]
        dec[it] = jnp.exp(b[c_len - 1:c_len, :])
        q_t = q_ref[i, rows(c), :] * jnp.exp(b)
        k_t = k_ref[i, rows(c), :] * jnp.exp(-b)
        qm[it] = _head_masked_stack(q_t, head_of_lane).astype(BF16)
        kdm[it] = _head_masked_stack(k_t * dec[it], head_of_lane).astype(BF16)
        a_raw[it] = _dot_tb(qm[it], k_t.astype(BF16))
    upd = {it: _dot_ta(_head_stack(v_ref[it[0], rows(it[1]), :], GLA_DV), kdm[it]) for it in items}
    inter = {}
    for i in range(n_seq):
        st = st_scr[i]
        for c in range(n_tok // c_len):
            inter[(i, c)] = _dot_tb(qm[(i, c)], st.astype(BF16))
            st = dec[(i, c)] * st + upd[(i, c)]
        st_scr[i] = st
    for it in items:
        i, c = it
        a = jnp.where(causal, a_raw[it], 0.0).astype(BF16)
        intra = jnp.concatenate(
            [_dot(a[h * c_len:(h + 1) * c_len, :], v_ref[i, rows(c), h * GLA_DV:(h + 1) * GLA_DV])
             for h in range(GLA_HEADS)], axis=0)
        y = _gla_out(inter[it] + intra, gain, _head_stack(gg_ref[i, rows(c), :], GLA_DV)).astype(BF16)
        for h in range(GLA_HEADS):
            o_ref[i, rows(c), h * GLA_DV:(h + 1) * GLA_DV] = y[h * c_len:(h + 1) * c_len, :]

    @pl.when(tb == pl.num_programs(1) - 1)
    def _():
        for i in range(q_ref.shape[0]):
            s_ref[i] = st_scr[i].T


def _gla_prompt(layer, q, k, la, v, gg, gain, tril, batch, seq):
    tb = tril.shape[0]
    per = min(GLA_SEQS, batch)
    blk = lambda b, t: (b, t, 0)
    kw = pl.BlockSpec((per, tb, GLA_K_W), blk)
    vw = pl.BlockSpec((per, tb, GLA_V_W), blk)
    return pl.pallas_call(
        _gla_prompt_kernel,
        grid=(batch // per, seq // tb),
        in_specs=[kw, kw, kw, vw, vw, _layer_resident(gain, layer), _resident(tril)],
        out_specs=[vw, pl.BlockSpec((per, GLA_K_W, GLA_DV), lambda b, t: (b, 0, 0))],
        out_shape=[
            jax.ShapeDtypeStruct((batch, seq, GLA_V_W), BF16),
            jax.ShapeDtypeStruct((batch, GLA_K_W, GLA_DV), F32),
        ],
        scratch_shapes=[pltpu.VMEM((per, GLA_DV, GLA_K_W), F32)],
        compiler_params=_params("parallel", "arbitrary"),
        name="gla_prompt",
    )(q, k, la, v, gg, gain, tril)


def _gla_decode_kernel(q_ref, k_ref, la_ref, v_ref, gg_ref, gn_ref, s_ref, *rest, steps, layer):
    o_ref, so_ref = rest[-2:]
    so_ref = _layer_view(so_ref, layer, len(rest) == 2)
    grp = s_ref.shape[0]
    rows = grp * steps
    stacked = GLA_HEADS * rows
    ri = lax.broadcasted_iota(jnp.int32, (rows, rows), 0)
    ci = lax.broadcasted_iota(jnp.int32, (rows, rows), 1)
    same_seq = ri // steps == ci // steps
    g_hi, g_lo = _split_bf16(la_ref[...])
    tril = (same_seq & (ri >= ci)).astype(BF16)
    total = same_seq.astype(BF16)
    b = _dot(tril, g_hi) + _dot(tril, g_lo)
    b_last = _dot(total, g_hi) + _dot(total, g_lo)
    head_of_lane = lax.broadcasted_iota(jnp.int32, (1, GLA_K_W), 1) // GLA_DK
    k_t = k_ref[...] * jnp.exp(-b)
    qm = _head_masked_stack(q_ref[...] * jnp.exp(b), head_of_lane).astype(BF16)
    km = _head_masked_stack(k_t, head_of_lane).astype(BF16)
    kdm = _head_masked_stack(k_t * jnp.exp(b_last), head_of_lane).astype(BF16)
    v_st = _head_stack(v_ref[...], GLA_DV)
    seq_of_row = (lax.broadcasted_iota(jnp.int32, (stacked, 1), 0) % rows) // steps
    seq_of_g = lax.broadcasted_iota(jnp.int32, (rows, 1), 0) // steps
    zero = jnp.zeros((), BF16)
    rhs = jnp.concatenate([
        jnp.concatenate([v_st, jnp.zeros((stacked, GLA_DV), BF16)], axis=1),
        jnp.concatenate([jnp.zeros((2 * rows, GLA_DV), BF16), jnp.ones((2 * rows, GLA_DV), BF16)], axis=1),
    ], axis=0)
    a_raw = _dot_tb(qm, km)
    states = [s_ref[s].reshape(GLA_K_W, GLA_DV) for s in range(grp)]
    inter_all = [_dot(qm, st.astype(BF16)) for st in states]
    upd_all = []
    for s in range(grp):
        lhs = jnp.concatenate([jnp.where(seq_of_row == s, kdm, zero),
                               jnp.where(seq_of_g == s, g_hi, zero),
                               jnp.where(seq_of_g == s, g_lo, zero)], axis=0)
        upd_all.append(_dot_ta(lhs, rhs))
    rr = lax.broadcasted_iota(jnp.int32, (stacked, stacked), 0) % rows
    cc = lax.broadcasted_iota(jnp.int32, (stacked, stacked), 1) % rows
    causal = (rr // steps == cc // steps) & (rr >= cc)
    o = _dot(jnp.where(causal, a_raw, 0.0).astype(BF16), v_st)
    inter = inter_all[0]
    for s in range(1, grp):
        inter = jnp.where(seq_of_row == s, inter_all[s], inter)
    y = _gla_out(o + inter, gn_ref[...], _head_stack(gg_ref[...], GLA_DV)).astype(BF16)
    for h in range(GLA_HEADS):
        o_ref[:, h * GLA_DV:(h + 1) * GLA_DV] = y[h * rows:(h + 1) * rows, :]
    for s in range(grp):
        new = jnp.exp(upd_all[s][:, GLA_DV:]) * states[s] + upd_all[s][:, :GLA_DV]
        so_ref[s] = new.reshape(GLA_HEADS, GLA_DK, GLA_DV)


def _gla_decode(layer, q, k, la, v, gg, gain, state, steps, prev):
    nseq = state.shape[1]
    grp = min(GLA_DEC_GROUP, nseq)
    rows = grp * steps
    row = lambda i: (i, 0)
    st_spec = pl.BlockSpec((None, grp, GLA_HEADS, GLA_DK, GLA_DV), lambda i: (layer, i, 0, 0, 0))
    extra_in, extra_specs, out_spec, stacked = _stacked_out(
        state.shape, (grp, GLA_HEADS, GLA_DK, GLA_DV), layer, prev)
    n_in = 7
    return pl.pallas_call(
        functools.partial(_gla_decode_kernel, steps=steps, layer=layer),
        grid=(nseq // grp,),
        in_specs=[
            pl.BlockSpec((rows, GLA_K_W), row),
            pl.BlockSpec((rows, GLA_K_W), row),
            pl.BlockSpec((rows, GLA_K_W), row),
            pl.BlockSpec((rows, GLA_V_W), row),
            pl.BlockSpec((rows, GLA_V_W), row),
            _layer_resident(gain, layer),
            st_spec,
        ] + extra_specs,
        out_specs=[pl.BlockSpec((rows, GLA_V_W), row), out_spec],
        out_shape=[jax.ShapeDtypeStruct((nseq * steps, GLA_V_W), BF16), stacked],
        input_output_aliases={n_in + i: 1 + i for i in range(len(extra_in))},
        compiler_params=_params("parallel"),
        name="gla_decode",
    )(q, k, la, v, gg, gain, state, *extra_in)


def _outproj_apply(x, a_ref, o_ref, wa_ref, wo_ref, g_ref, wq_ref, qn_ref):
    x1 = x + _dot(a_ref[...], wa_ref[...]) + _dot(o_ref[...], wo_ref[...])
    h = _rms(x1, g_ref[...]).astype(BF16)
    qn = qn_ref[...]
    q = [_rms(_dot(h, wq_ref[:, hd * XA_HEAD_DIM:(hd + 1) * XA_HEAD_DIM]), qn).astype(BF16)
         for hd in range(XA_HEADS)]
    return x1, q


def _outproj_kernel(x_ref, a_ref, o_ref, wa_ref, wo_ref, g_ref, wq_ref, qn_ref, x1_ref, q_ref):
    x1, q = _outproj_apply(x_ref[...], a_ref, o_ref, wa_ref, wo_ref, g_ref, wq_ref, qn_ref)
    x1_ref[...] = x1
    for hd in range(XA_HEADS):
        q_ref[:, hd * XA_HEAD_DIM:(hd + 1) * XA_HEAD_DIM] = q[hd]


def _outproj(x, layer, a, o, w_a, w_o, g, wq, qn):
    n, d = x.shape
    tm = min(ROW_TILE, n)
    row = lambda i: (i, 0)
    return pl.pallas_call(
        _outproj_kernel,
        grid=(n // tm,),
        in_specs=[
            pl.BlockSpec((tm, d), row),
            pl.BlockSpec((tm, SWA_Q_W), row),
            pl.BlockSpec((tm, GLA_V_W), row),
        ] + [_layer_resident(p, layer) for p in (w_a, w_o, g, wq, qn)],
        out_specs=[pl.BlockSpec((tm, d), row), pl.BlockSpec((tm, XA_W), row)],
        out_shape=[jax.ShapeDtypeStruct((n, d), F32), jax.ShapeDtypeStruct((n, XA_W), BF16)],
        compiler_params=_params("parallel"),
        name="outproj",
    )(x, a, o, w_a, w_o, g, wq, qn)


def _memkv_kernel(m_ref, g_ref, wk_ref, wv_ref, kn_ref, k_ref, v_ref):
    m = _rms(m_ref[...], g_ref[...]).astype(BF16)
    kn = kn_ref[...]
    for hd in range(XA_HEADS):
        sl = slice(hd * XA_HEAD_DIM, (hd + 1) * XA_HEAD_DIM)
        k_ref[:, sl] = _rms(_dot(m, wk_ref[:, sl]), kn)
    v_ref[...] = _dot(m, wv_ref[...])


def _memkv(mem, g, wk, wv, kn):
    depth = wk.shape[0]
    n, d = mem.shape
    tm = min(ROW_TILE, n)
    per_layer = lambda l, i: (l, 0, 0)
    out = lambda l, i: (l, i, 0)
    return pl.pallas_call(
        _memkv_kernel,
        grid=(depth, n // tm),
        in_specs=[
            pl.BlockSpec((tm, d), lambda l, i: (i, 0)),
            pl.BlockSpec((None, 1, d), per_layer),
            pl.BlockSpec((None, d, XA_W), per_layer),
            pl.BlockSpec((None, d, XA_W), per_layer),
            pl.BlockSpec((None, 1, XA_HEAD_DIM), per_layer),
        ],
        out_specs=[pl.BlockSpec((None, tm, XA_W), out), pl.BlockSpec((None, tm, XA_W), out)],
        out_shape=[jax.ShapeDtypeStruct((depth, n, XA_W), F32)] * 2,
        compiler_params=_params("parallel", "parallel"),
        name="memkv",
    )(mem, g, wk, wv, kn)


def _mixout_prompt_kernel(x_ref, a_ref, o_ref, wa_ref, wo_ref, g_ref, wq_ref, qn_ref, mk_ref, mv_ref,
                          xwo_ref, fg_ref, wg_ref, wu_ref, wd_ref, out_ref, att_scr, a_scr):
    x1, q = _outproj_apply(x_ref[...], a_ref, o_ref, wa_ref, wo_ref, g_ref, wq_ref, qn_ref)
    mk = mk_ref[...].astype(BF16)
    mv = mv_ref[...].astype(BF16)
    heads = range(XA_HEADS)
    cols = lambda hd: slice(hd * XA_HEAD_DIM, (hd + 1) * XA_HEAD_DIM)
    scores = [_dot_tb(q[hd], mk[:, cols(hd)]) * (XA_HEAD_DIM ** -0.5 * LOG2E) for hd in heads]
    probs = []
    for s in scores:
        e = jnp.exp2(s - jnp.max(s, axis=-1, keepdims=True))
        probs.append((e.astype(BF16), jnp.sum(e, axis=-1, keepdims=True)))
    for hd in heads:
        att_scr[:, cols(hd)] = (_dot(probs[hd][0], mv[:, cols(hd)]) / probs[hd][1]).astype(BF16)
    x2 = x1 + _dot(att_scr[...], xwo_ref[...])
    out_ref[...] = _ffn_apply(x2, fg_ref, wg_ref, wu_ref, wd_ref, a_scr)


def _mixout_prompt(x, layer, a, o, outproj, mk, mv, wo, ffn, seq):
    n, d = x.shape
    dff = ffn[1].shape[2]
    tm = min(ROW_TILE, seq)
    per_seq = seq // tm
    row = lambda i: (i, 0)
    mem_spec = pl.BlockSpec((None, None, mk.shape[2], XA_W), lambda i: (layer, i // per_seq, 0, 0))
    return pl.pallas_call(
        _mixout_prompt_kernel,
        grid=(n // tm,),
        in_specs=([pl.BlockSpec((tm, d), row), pl.BlockSpec((tm, SWA_Q_W), row), pl.BlockSpec((tm, GLA_V_W), row)]
                  + [_layer_resident(p, layer) for p in outproj]
                  + [mem_spec, mem_spec]
                  + [_layer_resident(p, layer) for p in (wo,) + ffn]),
        out_specs=pl.BlockSpec((tm, d), row),
        out_shape=jax.ShapeDtypeStruct((n, d), F32),
        scratch_shapes=[pltpu.VMEM((tm, XA_W), BF16), pltpu.VMEM((tm, dff), BF16)],
        compiler_params=_params("parallel"),
        name="mixout_prompt",
    )(x, a, o, *outproj, mk, mv, wo, *ffn)


def _xattn_decode_kernel(q_ref, mk_ref, mv_ref, o_ref, *, steps):
    grp, nkeys = mk_ref.shape[0], mk_ref.shape[1]
    rows = grp * steps
    q = jnp.concatenate([q_ref[:, hd * XA_HEAD_DIM:(hd + 1) * XA_HEAD_DIM] for hd in range(XA_HEADS)], axis=0)
    r = lax.broadcasted_iota(jnp.int32, (XA_HEADS * rows, 1), 0)
    own = (r % rows) // steps
    same_head = (r // rows) == (lax.broadcasted_iota(jnp.int32, (1, nkeys), 1) % XA_HEADS)
    s = None
    for j in range(grp):
        sj = _dot_tb(q, mk_ref[j].astype(BF16))
        s = sj if s is None else jnp.where(own == j, sj, s)
    s = jnp.where(same_head, s * (XA_HEAD_DIM ** -0.5 * LOG2E), -jnp.inf)
    m = jnp.max(s, axis=-1, keepdims=True)
    e = jnp.exp2(s - m)
    p = e.astype(BF16)
    o = None
    for j in range(grp):
        oj = _dot(p, mv_ref[j].astype(BF16))
        o = oj if o is None else jnp.where(own == j, oj, o)
    o = o / jnp.sum(e, axis=-1, keepdims=True)
    for hd in range(XA_HEADS):
        o_ref[:, hd * XA_HEAD_DIM:(hd + 1) * XA_HEAD_DIM] = o[hd * rows:(hd + 1) * rows, :].astype(BF16)


def _xattn_decode(q, layer, mk, mv, steps):
    nseq, nkeys = mk.shape[1], mk.shape[2]
    grp = XA_DEC_GROUP
    rows = grp * steps
    row = lambda i: (i, 0)
    mem_spec = pl.BlockSpec((None, grp, nkeys, XA_HEAD_DIM), lambda i: (layer, i, 0, 0))
    return pl.pallas_call(
        functools.partial(_xattn_decode_kernel, steps=steps),
        grid=(nseq // grp,),
        in_specs=[pl.BlockSpec((rows, XA_W), row), mem_spec, mem_spec],
        out_specs=pl.BlockSpec((rows, XA_W), row),
        out_shape=jax.ShapeDtypeStruct((nseq * steps, XA_W), BF16),
        compiler_params=_params("parallel"),
        name="xattn_decode",
    )(q, mk, mv)


def _proj_res_kernel(x_ref, a_ref, w_ref, o_ref):
    o_ref[...] = x_ref[...] + _dot(a_ref[...], w_ref[...])


def _proj_res(x, layer, a, w):
    n, d = x.shape
    tm = min(ROW_TILE, n)
    row = lambda i: (i, 0)
    return pl.pallas_call(
        _proj_res_kernel,
        grid=(n // tm,),
        in_specs=[pl.BlockSpec((tm, d), row), pl.BlockSpec((tm, a.shape[1]), row), _layer_resident(w, layer)],
        out_specs=pl.BlockSpec((tm, d), row),
        out_shape=jax.ShapeDtypeStruct((n, d), F32),
        compiler_params=_params("parallel"),
        name="proj_res",
    )(x, a, w)


def _rope_tables(pos):
    half = HEAD_DIM // 2
    inv = ROPE_THETA ** (-jnp.arange(half, dtype=F32) / half)
    ang = pos.astype(F32)[:, None] * inv[None, :]
    cos, sin = jnp.cos(ang), jnp.sin(ang)
    reps = LANES // HEAD_DIM
    return jnp.tile(cos, (1, 2 * reps)), jnp.tile(jnp.concatenate([-sin, sin], axis=-1), (1, reps))


def _block_tril(n_blocks, size):
    i = jnp.arange(n_blocks * size)
    return ((i[:, None] // size == i[None, :] // size) & (i[:, None] >= i[None, :])).astype(BF16)


def _permute_heads(w, axis):
    blocks = jnp.split(w, SWA_Q_HEADS, axis=axis)
    return jnp.concatenate([blocks[h] for h in SWA_HEAD_ORDER], axis=axis)


def kernel(x_prompt, x_sample, cache_swa_k, cache_swa_v, state_gla, cache_mem_k, cache_mem_v, mem_prompt, ffn1_norm, ffn1_wg, ffn1_wu, ffn1_wd, mix_norm, w_in, swa_q_norm, swa_k_norm, swa_sinks, gla_w_gate, gla_b_gate, gla_out_norm, w_out, xa_norm, mem_norm, xa_wq, xa_wk, xa_wv, xa_q_norm, xa_k_norm, xa_wo, ffn2_norm, ffn2_wg, ffn2_wu, ffn2_wd):
    batch, seq, d = x_prompt.shape
    nseq, steps, _ = x_sample.shape
    depth = w_in.shape[0]
    mem_len = mem_prompt.shape[1]

    bf = lambda w: w.astype(BF16)
    vec = lambda p: p[:, None, :]
    ffn1 = (vec(ffn1_norm), bf(ffn1_wg), bf(ffn1_wu), bf(ffn1_wd))
    ffn2 = (vec(ffn2_norm), bf(ffn2_wg), bf(ffn2_wu), bf(ffn2_wd))
    w_all = jnp.pad(bf(w_in), ((0, 0), (0, 0), (0, MXU_TILE - GLA_LOWRANK)))
    gate_w = jnp.pad(bf(gla_w_gate), ((0, 0), (0, LANES - GLA_LOWRANK), (0, 0)))
    w_a = bf(_permute_heads(w_out[:, :SWA_Q_W], 1))
    w_o = bf(w_out[:, SWA_Q_W:])
    wq_b, wk_b, wv_b, wo_b = bf(xa_wq), bf(xa_wk), bf(xa_wv), bf(xa_wo)
    qn = vec(jnp.tile(swa_q_norm * (HEAD_DIM ** -0.5 * LOG2E), (1, LANES // HEAD_DIM)))
    kn = vec(jnp.tile(swa_k_norm, (1, LANES // HEAD_DIM)))
    inproj_params = (vec(mix_norm), w_all, gate_w, vec(gla_b_gate), qn, kn)
    outproj_params = (w_a, w_o, vec(xa_norm), wq_b, vec(xa_q_norm))
    gla_gain = vec(gla_out_norm)

    lane = jnp.arange(MXU_TILE)
    bd = (lane[:, None] // HEAD_DIM == lane[None, :] // HEAD_DIM).astype(BF16)
    cos_p, sin_p = _rope_tables(jnp.arange(seq))
    cos_s, sin_s = _rope_tables(PAST_LEN + jnp.arange(nseq * steps) % steps)
    tril_p = _block_tril(min(GLA_BLOCK, seq) // min(GLA_CHUNK, seq), min(GLA_CHUNK, seq))

    mk_p, mv_p = _memkv(mem_prompt.reshape(batch * mem_len, d), vec(mem_norm), wk_b, wv_b, vec(xa_k_norm))
    mk_p = mk_p.reshape(depth, batch, mem_len, XA_W)
    mv_p = mv_p.reshape(depth, batch, mem_len, XA_W)
    mk_s = cache_mem_k.reshape(depth, nseq, mem_len * XA_HEADS, XA_HEAD_DIM)
    mv_s = cache_mem_v.reshape(depth, nseq, mem_len * XA_HEADS, XA_HEAD_DIM)
    native = lambda c: jnp.transpose(c, (0, 1, 3, 4, 2)).reshape(depth, nseq, SWA_KV_W, WINDOW)
    kc_s, vc_s = native(cache_swa_k), native(cache_swa_v)

    xp = x_prompt.reshape(batch * seq, d)
    xs = x_sample.reshape(nseq * steps, d)
    kp_l, vp_l, sp_l = [], [], []
    swa_new = gla_new = None
    for l in range(depth):
        xp, q_s, k_s, v_s, q_g, k_g, v_g, g_g, la = _ffn_inproj(xp, l, ffn1, inproj_params, cos_p, sin_p, bd)
        a_p = _swa_prompt(swa_sinks, l, q_s, k_s, v_s, batch, seq)
        seqs = lambda a: a.reshape(batch, seq, a.shape[-1])
        o_p, s_p = _gla_prompt(l, seqs(q_g), seqs(k_g), seqs(la), seqs(v_g), seqs(g_g), gla_gain, tril_p,
                               batch, seq)
        o_p = o_p.reshape(batch * seq, GLA_V_W)
        last = lambda a: seqs(a)[:, seq - WINDOW:].reshape(batch, WINDOW, SWA_KV_HEADS, HEAD_DIM)
        kp_l.append(last(k_s))
        vp_l.append(last(v_s))
        sp_l.append(s_p.reshape(batch, GLA_HEADS, GLA_DK, GLA_DV))
        xp = _mixout_prompt(xp, l, a_p, o_p, outproj_params, mk_p, mv_p, wo_b, ffn2, seq)

        xs, q_s, k_s, v_s, q_g, k_g, v_g, g_g, la, k_t, v_t = _ffn_inproj(
            xs, l, ffn1, inproj_params, cos_s, sin_s, bd, transposed_kv=True)
        a_s, *swa_new = _swa_decode(swa_sinks, l, q_s, k_s, v_s, k_t, v_t, kc_s, vc_s, steps, swa_new)
        o_s, *gla_new = _gla_decode(l, q_g, k_g, la, v_g, g_g, gla_gain, state_gla, steps, gla_new)
        xs, q_x = _outproj(xs, l, a_s, o_s, *outproj_params)
        xs = _proj_res(xs, l, _xattn_decode(q_x, l, mk_s, mv_s, steps), wo_b)
        xs = _ffn(xs, l, *ffn2)

    unnative = lambda c: jnp.transpose(c.reshape(depth, nseq, SWA_KV_HEADS, HEAD_DIM, WINDOW), (0, 1, 4, 2, 3))
    return (xp.reshape(batch, seq, d), xs.reshape(nseq, steps, d),
            jnp.stack(kp_l), jnp.stack(vp_l), jnp.stack(sp_l),
            mk_p.reshape(depth, batch, mem_len, XA_HEADS, XA_HEAD_DIM),
            mv_p.reshape(depth, batch, mem_len, XA_HEADS, XA_HEAD_DIM),
            unnative(swa_new[0]), unnative(swa_new[1]), gla_new[0])
```

```python
import functools

import jax
import jax.numpy as jnp
from jax import lax
from jax.experimental import pallas as pl
from jax.experimental.pallas import tpu as pltpu

F32 = jnp.float32
BF16 = jnp.bfloat16

EPS = 1e-6
LOG2E = 1.4426950408889634
PAST_LEN = 16384
WINDOW = 128
ROPE_THETA = 10000.0
HEAD_DIM = 64
SWA_Q_HEADS = 8
SWA_KV_HEADS = 2
SWA_GROUP = SWA_Q_HEADS // SWA_KV_HEADS
GLA_HEADS = 4
GLA_DK = 64
GLA_DV = 128
GLA_LOWRANK = 16
GLA_GATE_TEMP = 16.0
GLA_CHUNK = 64
XA_HEADS = 4
XA_HEAD_DIM = 128

SWA_Q_W = SWA_Q_HEADS * HEAD_DIM
SWA_KV_W = SWA_KV_HEADS * HEAD_DIM
GLA_K_W = GLA_HEADS * GLA_DK
GLA_V_W = GLA_HEADS * GLA_DV
XA_W = XA_HEADS * XA_HEAD_DIM
MAIN_W = SWA_Q_W + 2 * SWA_KV_W + 2 * GLA_K_W + 2 * GLA_V_W

LANES = 128
BF16_SUBLANES = 16
MXU_TILE = 256
VMEM_LIMIT = 56 * 1024 * 1024

ROW_TILE = 512
ROW_SPLIT = 2
FFN_CHUNK = 256
SWA_BLOCK = 256
GLA_BLOCK = 256
GLA_SEQS = 4
DEC_GROUP = 4
SWA_DEC_SUBGROUPS = 8
XA_DEC_GROUP = 8
GLA_DEC_GROUP = 8

assert SWA_KV_HEADS * HEAD_DIM == LANES
SWA_HEAD_ORDER = tuple(kv * SWA_GROUP + g for g in range(SWA_GROUP) for kv in range(SWA_KV_HEADS))


def _dot(a, b):
    return jnp.dot(a, b, preferred_element_type=F32)


def _dot_tb(a, b):
    return lax.dot_general(a, b, (((1,), (1,)), ((), ())), preferred_element_type=F32)


def _dot_ta(a, b):
    return lax.dot_general(a, b, (((0,), (0,)), ((), ())), preferred_element_type=F32)


def _split_bf16(x):
    hi = x.astype(BF16)
    lo = (x - hi.astype(F32)).astype(BF16)
    return hi, lo


def _rms(x, g):
    ms = jnp.mean(x * x, axis=-1, keepdims=True)
    return x * lax.rsqrt(ms + EPS) * g


def _params(*sem):
    return pltpu.CompilerParams(dimension_semantics=sem, vmem_limit_bytes=VMEM_LIMIT)


def _resident(arr):
    nd = arr.ndim
    return pl.BlockSpec(arr.shape, lambda *_: (0,) * nd, pipeline_mode=pl.Buffered(1))


def _layer_resident(arr, layer):
    nd = arr.ndim
    index = (layer if arr.shape[0] > 1 else 0,) + (0,) * (nd - 1)
    return pl.BlockSpec((None,) + arr.shape[1:], lambda *_: index, pipeline_mode=pl.Buffered(1))


def _ffn_hidden(x, g_ref, wg_ref, wu_ref, a_scr):
    h = _rms(x, g_ref[...]).astype(BF16)
    dff = wg_ref.shape[1]
    for c in range(dff // FFN_CHUNK):
        sl = slice(c * FFN_CHUNK, (c + 1) * FFN_CHUNK)
        g = _dot(h, wg_ref[:, sl])
        u = _dot(h, wu_ref[:, sl])
        a_scr[:, sl] = (g * jax.nn.sigmoid(g) * u).astype(BF16)


def _ffn_apply(x, g_ref, wg_ref, wu_ref, wd_ref, a_scr):
    _ffn_hidden(x, g_ref, wg_ref, wu_ref, a_scr)
    return x + 0.5 * _dot(a_scr[...], wd_ref[...])


def _ffn_kernel(x_ref, g_ref, wg_ref, wu_ref, wd_ref, o_ref, a_scr):
    o_ref[...] = _ffn_apply(x_ref[...], g_ref, wg_ref, wu_ref, wd_ref, a_scr)


def _ffn(x, layer, g, wg, wu, wd):
    n, d = x.shape
    dff = wg.shape[2]
    tm = min(ROW_TILE, n)
    return pl.pallas_call(
        _ffn_kernel,
        grid=(n // tm,),
        in_specs=[pl.BlockSpec((tm, d), lambda i: (i, 0))] + [_layer_resident(a, layer) for a in (g, wg, wu, wd)],
        out_specs=pl.BlockSpec((tm, d), lambda i: (i, 0)),
        out_shape=jax.ShapeDtypeStruct((n, d), F32),
        scratch_shapes=[pltpu.VMEM((tm, dff), BF16)],
        compiler_params=_params("parallel"),
        name="ffn",
    )(x, g, wg, wu, wd)


def _ffn_inproj_kernel(x_ref, fg_ref, wg_ref, wu_ref, wd_ref, g_ref, w_ref, gw_ref, gb_ref, qn_ref, kn_ref,
                       cos_ref, sin_ref, bd_ref, xo_ref, qs_ref, ks_ref, vs_ref, qg_ref, kg_ref, vg_ref,
                       gg_ref, la_ref, *rest):
    a_scr = rest[-1]
    kt_ref, vt_ref = rest[:-1] if len(rest) > 1 else (None, None)
    x = x_ref[...]
    _ffn_hidden(x, fg_ref, wg_ref, wu_ref, a_scr)
    n = x.shape[0] // ROW_SPLIT
    groups = [slice(r * n, (r + 1) * n) for r in range(ROW_SPLIT)]
    mid = [x[rs] + 0.5 * _dot(a_scr[rs, :], wd_ref[...]) for rs in groups]
    for rs, xm in zip(groups, mid):
        xo_ref[rs, :] = xm
    split = SWA_Q_W + 2 * SWA_KV_W + 2 * GLA_K_W
    h = [_rms(xm, g_ref[...]).astype(BF16) for xm in mid]
    z1 = [_dot(hr, w_ref[:, :split]) for hr in h]
    z2 = []
    for rs, hr, z in zip(groups, h, z1):
        _inproj_finish_swa(z, rs, qn_ref, kn_ref, cos_ref, sin_ref, bd_ref, qs_ref, ks_ref, vs_ref, qg_ref,
                           kg_ref, kt_ref, vt_ref)
        z2.append(_dot(hr, w_ref[:, split:]))
    for rs, z in zip(groups, z2):
        _inproj_finish_gla(z, rs, gw_ref, gb_ref, vg_ref, gg_ref, la_ref)


def _inproj_finish_swa(z1, rs, qn_ref, kn_ref, cos_ref, sin_ref, bd_ref, qs_ref, ks_ref, vs_ref, qg_ref,
                       kg_ref, kt_ref, vt_ref):
    bd = bd_ref[...]
    cos = cos_ref[rs, :]
    sin = sin_ref[rs, :]
    lane = lax.broadcasted_iota(jnp.int32, cos.shape, 1)
    lane_lo = (lane % HEAD_DIM) < (HEAD_DIM // 2)

    def head_scale(z):
        ss = _dot((z * z).astype(BF16), bd)
        return lax.rsqrt(ss * (1.0 / HEAD_DIM) + EPS)

    def rope(y):
        swapped = jnp.where(lane_lo, pltpu.roll(y, LANES - HEAD_DIM // 2, axis=1),
                            pltpu.roll(y, HEAD_DIM // 2, axis=1))
        return y * cos + swapped * sin

    qn = qn_ref[...]
    nat = []
    for t in range(SWA_Q_W // MXU_TILE):
        z = z1[:, t * MXU_TILE:(t + 1) * MXU_TILE]
        y = z * head_scale(z)
        nat += [rope(y[:, c * LANES:(c + 1) * LANES] * qn) for c in range(MXU_TILE // LANES)]
    low = _low_half()
    per_col = LANES // HEAD_DIM
    for c in range(SWA_Q_W // LANES):
        halves = []
        for half, head in enumerate(SWA_HEAD_ORDER[per_col * c:per_col * (c + 1)]):
            col = nat[head // per_col]
            halves.append(col if head % per_col == half else pltpu.roll(col, HEAD_DIM, axis=1))
        qs_ref[rs, c * LANES:(c + 1) * LANES] = jnp.where(low, halves[0], halves[1]).astype(BF16)
    o = SWA_Q_W
    z = z1[:, o:o + 2 * SWA_KV_W]
    k = rope(z[:, :SWA_KV_W] * head_scale(z)[:, :SWA_KV_W] * kn_ref[...])
    ks_ref[rs, :] = k
    vs_ref[rs, :] = z[:, SWA_KV_W:]
    if kt_ref is not None:
        kt_ref[:, rs] = k.T
        vt_ref[:, rs] = z[:, SWA_KV_W:].T
    o += 2 * SWA_KV_W
    qg_ref[rs, :] = z1[:, o:o + GLA_K_W] * (GLA_DK ** -0.5)
    o += GLA_K_W
    kg_ref[rs, :] = z1[:, o:o + GLA_K_W]


def _inproj_finish_gla(z2, rs, gw_ref, gb_ref, vg_ref, gg_ref, la_ref):
    vg_ref[rs, :] = z2[:, :GLA_V_W].astype(BF16)
    gg_ref[rs, :] = z2[:, GLA_V_W:2 * GLA_V_W]
    lr = z2[:, 2 * GLA_V_W:2 * GLA_V_W + LANES].astype(BF16)
    t = _dot(lr, gw_ref[...]) + gb_ref[...]
    log_sig = jnp.minimum(t, 0.0) - jnp.log(1.0 + jnp.exp(-jnp.abs(t)))
    la_ref[rs, :] = log_sig * (1.0 / GLA_GATE_TEMP)


def _ffn_inproj(x, layer, ffn, inproj, cos, sin, bd, transposed_kv=False):
    n, d = x.shape
    dff = ffn[1].shape[2]
    tm = min(ROW_TILE, n)
    pos_blocks = cos.shape[0] // tm
    row = lambda i: (i, 0)
    pos = lambda i: (i % pos_blocks, 0)
    widths = (d, SWA_Q_W, SWA_KV_W, SWA_KV_W, GLA_K_W, GLA_K_W, GLA_V_W, GLA_V_W, GLA_K_W)
    dtypes = (F32, BF16, F32, F32, F32, F32, BF16, F32, F32)
    out_specs = [pl.BlockSpec((tm, w), row) for w in widths]
    out_shape = [jax.ShapeDtypeStruct((n, w), dt) for w, dt in zip(widths, dtypes)]
    if transposed_kv:
        out_specs += [pl.BlockSpec((SWA_KV_W, tm), lambda i: (0, i))] * 2
        out_shape += [jax.ShapeDtypeStruct((SWA_KV_W, n), F32)] * 2
    return pl.pallas_call(
        _ffn_inproj_kernel,
        grid=(n // tm,),
        in_specs=([pl.BlockSpec((tm, d), row)]
                  + [_layer_resident(a, layer) for a in ffn + inproj]
                  + [pl.BlockSpec((tm, LANES), pos), pl.BlockSpec((tm, LANES), pos), _resident(bd)]),
        out_specs=out_specs,
        out_shape=out_shape,
        scratch_shapes=[pltpu.VMEM((tm, dff), BF16)],
        compiler_params=_params("parallel"),
        name="ffn_inproj",
    )(x, *ffn, *inproj, cos, sin, bd)


def _swa_attend(q, k, v, valid, sinks):
    return _swa_out(*_swa_probs(_swa_scores(q, k), valid, sinks), v)


def _low_half():
    return lax.broadcasted_iota(jnp.int32, (1, LANES), 1) < HEAD_DIM


def _swa_head_rows(q):
    low = _low_half()
    zero = jnp.zeros((), q.dtype)
    pieces = []
    for c in range(SWA_Q_W // LANES):
        qc = q[:, c * LANES:(c + 1) * LANES]
        pieces += [jnp.where(low, qc, zero), jnp.where(low, zero, qc)]
    return jnp.concatenate(pieces, axis=0)


def _swa_scores(q, k):
    return _dot_tb(_swa_head_rows(q), k)


def _swa_probs(s, valid, sinks):
    rows, keys = valid.shape
    s = jnp.where(valid[None], s.reshape(SWA_Q_HEADS, rows, keys), -jnp.inf)
    m = jnp.maximum(jnp.max(s, axis=-1, keepdims=True), sinks)
    e = jnp.exp2(s - m)
    den = jnp.sum(e, axis=-1, keepdims=True) + jnp.exp2(sinks - m)
    return e.astype(BF16).reshape(SWA_Q_HEADS * rows, keys), den


def _swa_out(e, den, v):
    return _swa_finish(_dot(e, v), den)


def _swa_finish(o, den):
    low = _low_half()
    o = o.reshape(den.shape[0], den.shape[1], LANES) / den
    return [jnp.where(low, o[2 * c], o[2 * c + 1]) for c in range(SWA_Q_W // LANES)]


def _sink_column(sink_ref, layer):
    idx = lax.broadcasted_iota(jnp.int32, (SWA_Q_HEADS, 1, 1), 0)
    col = jnp.zeros((SWA_Q_HEADS, 1, 1), F32)
    for p, head in enumerate(SWA_HEAD_ORDER):
        col = jnp.where(idx == p, sink_ref[layer, head] * LOG2E, col)
    return col


def _swa_prompt_kernel(sink_ref, q_ref, kp_ref, kc_ref, vp_ref, vc_ref, *rest, layer):
    n_conv = len(rest) // 2
    o_ref = rest[n_conv]
    for src, dst in zip(rest[:n_conv], rest[n_conv + 1:]):
        dst[0] = src[...].astype(BF16)
    j = pl.program_id(1)
    kall = jnp.concatenate([kp_ref[...], kc_ref[...]], axis=0).astype(BF16)
    vall = jnp.concatenate([vp_ref[...], vc_ref[...]], axis=0).astype(BF16)
    sinks = _sink_column(sink_ref, layer)
    upper = (lax.broadcasted_iota(jnp.int32, (WINDOW, WINDOW), 1)
             > lax.broadcasted_iota(jnp.int32, (WINDOW, WINDOW), 0))[None]

    def banded_probs(s, has_prev):
        s = s.reshape(SWA_Q_HEADS, WINDOW, 2 * WINDOW)
        prev = s[:, :, :WINDOW]
        if has_prev is not True:
            prev = jnp.where(has_prev, prev, -jnp.inf)
        logits = jnp.where(upper, prev, s[:, :, WINDOW:])
        m = jnp.maximum(jnp.max(logits, axis=-1, keepdims=True), sinks)
        e = jnp.exp2(logits - m)
        den = jnp.sum(e, axis=-1, keepdims=True) + jnp.exp2(sinks - m)
        e = e.astype(BF16)
        zero = jnp.zeros((), BF16)
        e = jnp.concatenate([jnp.where(upper, e, zero), jnp.where(upper, zero, e)], axis=-1)
        return e.reshape(SWA_Q_HEADS * WINDOW, 2 * WINDOW), den

    windows = range(q_ref.shape[0] // WINDOW)
    rows = lambda w: slice(w * WINDOW, (w + 1) * WINDOW)
    keys = lambda w: slice(w * WINDOW, (w + 2) * WINDOW)
    scores = [_swa_scores(q_ref[rows(w), :], kall[keys(w)]) for w in windows]
    probs = [banded_probs(scores[w], True if w > 0 else j > 0) for w in windows]
    for w in windows:
        for c, col in enumerate(_swa_out(*probs[w], vall[keys(w)])):
            o_ref[rows(w), c * LANES:(c + 1) * LANES] = col.astype(BF16)


def _swa_prompt(sinks, layer, q, k, v, batch, seq, convert=()):
    qb = min(SWA_BLOCK, seq)
    nb = seq // qb
    per = qb // WINDOW
    cur = lambda b, j: (b * nb + j, 0)
    prev = lambda b, j: ((b * nb + j) * per - jnp.minimum(j, 1), 0)
    conv_in, conv_out, conv_shape = [], [], []
    for arr, src_layer in convert:
        _, rows, cols = arr.shape
        share = 1
        while rows % (batch * nb // share) or (rows * share // (batch * nb)) % BF16_SUBLANES:
            share *= 2
            assert (batch * nb) % share == 0
        chunk = rows * share // (batch * nb)
        conv_in.append(pl.BlockSpec((None, chunk, cols),
                                    lambda b, j, l=src_layer, s=share: (l, (b * nb + j) // s, 0)))
        conv_out.append(pl.BlockSpec((1, chunk, cols), lambda b, j, s=share: (0, (b * nb + j) // s, 0)))
        conv_shape.append(jax.ShapeDtypeStruct((1, rows, cols), BF16))
    return pl.pallas_call(
        functools.partial(_swa_prompt_kernel, layer=layer),
        grid=(batch, nb),
        in_specs=[
            pl.BlockSpec(memory_space=pltpu.SMEM),
            pl.BlockSpec((qb, SWA_Q_W), cur),
            pl.BlockSpec((WINDOW, SWA_KV_W), prev),
            pl.BlockSpec((qb, SWA_KV_W), cur),
            pl.BlockSpec((WINDOW, SWA_KV_W), prev),
            pl.BlockSpec((qb, SWA_KV_W), cur),
        ] + conv_in,
        out_specs=[pl.BlockSpec((qb, SWA_Q_W), cur)] + conv_out,
        out_shape=[jax.ShapeDtypeStruct((batch * seq, SWA_Q_W), BF16)] + conv_shape,
        compiler_params=_params("parallel", "parallel"),
        name="swa_prompt",
    )(sinks, q, k, k, v, v, *[arr for arr, _ in convert])


def _layer_view(ref, layer, first):
    if not first:
        return ref
    for other in range(ref.shape[0]):
        if other != layer:
            ref[other] = jnp.zeros(ref.shape[1:], ref.dtype)
    return ref.at[layer]


def _swa_decode_kernel(sink_ref, q_ref, kn_ref, vn_ref, knt_ref, vnt_ref, kc_ref, vc_ref, *rest, steps, layer):
    o_ref, ko_ref, vo_ref = rest[-3:]
    first = len(rest) == 3
    ko_ref = _layer_view(ko_ref, layer, first)
    vo_ref = _layer_view(vo_ref, layer, first)
    n_seq = kc_ref.shape[0]
    grp = DEC_GROUP
    rows = grp * steps
    keep = WINDOW - steps
    lane = lax.broadcasted_iota(jnp.int32, (1, WINDOW), 1)
    for s in range(n_seq):
        tile = slice((s * steps) // LANES * LANES, (s * steps) // LANES * LANES + LANES)
        shift = (keep - s * steps) % LANES
        for cache_ref, new_ref, out_ref in ((kc_ref, knt_ref, ko_ref), (vc_ref, vnt_ref, vo_ref)):
            out_ref[s] = jnp.where(lane >= keep, pltpu.roll(new_ref[:, tile], shift, axis=1),
                                   pltpu.roll(cache_ref[s], keep, axis=1))
    nk = grp * WINDOW + rows
    r = lax.broadcasted_iota(jnp.int32, (rows, nk), 0)
    q_seq, q_step = r // steps, r % steps
    c = lax.broadcasted_iota(jnp.int32, (rows, nk), 1)
    is_new = c >= grp * WINDOW
    cn = c - grp * WINDOW
    k_seq = jnp.where(is_new, cn // steps, c // WINDOW)
    k_idx = jnp.where(is_new, WINDOW + cn % steps, c % WINDOW)
    rel = WINDOW + q_step - k_idx
    valid = (q_seq == k_seq) & (rel >= 0) & (rel < WINDOW)
    sinks = _sink_column(sink_ref, layer)
    groups = range(n_seq // grp)
    new_rows = lambda g: slice(g * rows, (g + 1) * rows)
    cached = lambda ref, g: jnp.concatenate([ref[g * grp + s] for s in range(grp)], axis=1).astype(BF16)
    scores = []
    for g in groups:
        qm = _swa_head_rows(q_ref[new_rows(g), :])
        scores.append(jnp.concatenate(
            [_dot(qm, cached(kc_ref, g)), _dot_tb(qm, kn_ref[new_rows(g), :].astype(BF16))], axis=1))
    probs = [_swa_probs(scores[g], valid, sinks) for g in groups]
    for g in groups:
        e, den = probs[g]
        o = (_dot_tb(e[:, :grp * WINDOW], cached(vc_ref, g))
             + _dot(e[:, grp * WINDOW:], vn_ref[new_rows(g), :].astype(BF16)))
        for c, col in enumerate(_swa_finish(o, den)):
            o_ref[new_rows(g), c * LANES:(c + 1) * LANES] = col.astype(BF16)


def _stacked_out(shape, block, layer, prev):
    tail = (0,) * (len(block) - 1)
    if prev is None:
        spec = pl.BlockSpec((shape[0],) + block, lambda i: (0, i) + tail)
        extra_inputs = []
    else:
        spec = pl.BlockSpec((None,) + block, lambda i: (layer, i) + tail)
        extra_inputs = list(prev)
    extra_specs = [pl.BlockSpec(memory_space=pl.ANY) for _ in extra_inputs]
    return extra_inputs, extra_specs, spec, jax.ShapeDtypeStruct(shape, F32)


def _swa_decode(sinks, layer, q, kn, vn, knt, vnt, kc, vc, steps, prev):
    depth, nseq = kc.shape[:2]
    grp = min(DEC_GROUP * SWA_DEC_SUBGROUPS, nseq)
    rows = grp * steps
    assert rows % LANES == 0, "a grid step's new keys must fill whole 128-lane tiles"
    row = lambda i: (i, 0)
    cache_spec = pl.BlockSpec((None, grp, SWA_KV_W, WINDOW), lambda i: (layer, i, 0, 0))
    new_t = pl.BlockSpec((SWA_KV_W, rows), lambda i: (0, i))
    extra_in, extra_specs, out_spec, stacked = _stacked_out(kc.shape, (grp, SWA_KV_W, WINDOW), layer, prev)
    n_in = 8
    return pl.pallas_call(
        functools.partial(_swa_decode_kernel, steps=steps, layer=layer),
        grid=(nseq // grp,),
        in_specs=[
            pl.BlockSpec(memory_space=pltpu.SMEM),
            pl.BlockSpec((rows, SWA_Q_W), row),
            pl.BlockSpec((rows, SWA_KV_W), row),
            pl.BlockSpec((rows, SWA_KV_W), row),
            new_t,
            new_t,
            cache_spec,
            cache_spec,
        ] + extra_specs,
        out_specs=[pl.BlockSpec((rows, SWA_Q_W), row), out_spec, out_spec],
        out_shape=[jax.ShapeDtypeStruct((nseq * steps, SWA_Q_W), BF16), stacked, stacked],
        input_output_aliases={n_in + i: 1 + i for i in range(len(extra_in))},
        compiler_params=_params("parallel"),
        name="swa_decode",
    )(sinks, q, kn, vn, knt, vnt, kc, vc, *extra_in)


def _gla_out(o, gain, gate):
    return _rms(o, gain) * (gate * jax.nn.sigmoid(gate))


def _head_stack(x, width):
    return jnp.concatenate([x[:, h * width:(h + 1) * width] for h in range(GLA_HEADS)], axis=0)


def _head_masked_stack(x, head_of_lane):
    zero = jnp.zeros((), x.dtype)
    return jnp.concatenate([jnp.where(head_of_lane == h, x, zero) for h in range(GLA_HEADS)], axis=0)


def _gla_prompt_kernel(q_ref, k_ref, la_ref, v_ref, gg_ref, gn_ref, tril_ref, o_ref, s_ref, st_scr):
    tb = pl.program_id(1)

    @pl.when(tb == 0)
    def _():
        st_scr[...] = jnp.zeros_like(st_scr)

    n_tok = q_ref.shape[1]
    c_len = min(GLA_CHUNK, n_tok)
    tril = tril_ref[...]
    head_of_lane = lax.broadcasted_iota(jnp.int32, (1, GLA_K_W), 1) // GLA_DK
    ri = lax.broadcasted_iota(jnp.int32, (GLA_HEADS * c_len, c_len), 0) % c_len
    ci = lax.broadcasted_iota(jnp.int32, (GLA_HEADS * c_len, c_len), 1)
    causal = ri >= ci
    gain = gn_ref[...]
    n_seq = q_ref.shape[0]
    items = [(i, c) for i in range(n_seq) for c in range(n_tok // c_len)]
    rows = lambda c: slice(c * c_len, (c + 1) * c_len)
    b_all = []
    for i in range(n_seq):
        g_hi, g_lo = _split_bf16(la_ref[i])
        b_all.append(_dot(tril, g_hi) + _dot(tril, g_lo))
    qm, kdm, dec, a_raw = {}, {}, {}, {}
    for it in items:
        i, c = it
        b = b_all[i][rows(c), :]
        dec[it] = jnp.exp(b[c_len - 1:c_len, :])
        q_t = q_ref[i, rows(c), :] * jnp.exp(b)
        k_t = k_ref[i, rows(c), :] * jnp.exp(-b)
        qm[it] = _head_masked_stack(q_t, head_of_lane).astype(BF16)
        kdm[it] = _head_masked_stack(k_t * dec[it], head_of_lane).astype(BF16)
        a_raw[it] = _dot_tb(qm[it], k_t.astype(BF16))
    upd = {it: _dot_ta(_head_stack(v_ref[it[0], rows(it[1]), :], GLA_DV), kdm[it]) for it in items}
    inter = {}
    for i in range(n_seq):
        st = st_scr[i]
        for c in range(n_tok // c_len):
            inter[(i, c)] = _dot_tb(qm[(i, c)], st.astype(BF16))
            st = dec[(i, c)] * st + upd[(i, c)]
        st_scr[i] = st
    for it in items:
        i, c = it
        a = jnp.where(causal, a_raw[it], 0.0).astype(BF16)
        intra = jnp.concatenate(
            [_dot(a[h * c_len:(h + 1) * c_len, :], v_ref[i, rows(c), h * GLA_DV:(h + 1) * GLA_DV])
             for h in range(GLA_HEADS)], axis=0)
        y = _gla_out(inter[it] + intra, gain, _head_stack(gg_ref[i, rows(c), :], GLA_DV)).astype(BF16)
        for h in range(GLA_HEADS):
            o_ref[i, rows(c), h * GLA_DV:(h + 1) * GLA_DV] = y[h * c_len:(h + 1) * c_len, :]

    @pl.when(tb == pl.num_programs(1) - 1)
    def _():
        for i in range(q_ref.shape[0]):
            s_ref[i] = st_scr[i].T


def _gla_prompt(layer, q, k, la, v, gg, gain, tril, batch, seq):
    tb = tril.shape[0]
    per = min(GLA_SEQS, batch)
    blk = lambda b, t: (b, t, 0)
    kw = pl.BlockSpec((per, tb, GLA_K_W), blk)
    vw = pl.BlockSpec((per, tb, GLA_V_W), blk)
    return pl.pallas_call(
        _gla_prompt_kernel,
        grid=(batch // per, seq // tb),
        in_specs=[kw, kw, kw, vw, vw, _layer_resident(gain, layer), _resident(tril)],
        out_specs=[vw, pl.BlockSpec((per, GLA_K_W, GLA_DV), lambda b, t: (b, 0, 0))],
        out_shape=[
            jax.ShapeDtypeStruct((batch, seq, GLA_V_W), BF16),
            jax.ShapeDtypeStruct((batch, GLA_K_W, GLA_DV), F32),
        ],
        scratch_shapes=[pltpu.VMEM((per, GLA_DV, GLA_K_W), F32)],
        compiler_params=_params("parallel", "arbitrary"),
        name="gla_prompt",
    )(q, k, la, v, gg, gain, tril)


def _gla_decode_kernel(q_ref, k_ref, la_ref, v_ref, gg_ref, gn_ref, s_ref, *rest, steps, layer):
    o_ref, so_ref = rest[-2:]
    so_ref = _layer_view(so_ref, layer, len(rest) == 2)
    grp = s_ref.shape[0]
    rows = grp * steps
    stacked = GLA_HEADS * rows
    ri = lax.broadcasted_iota(jnp.int32, (rows, rows), 0)
    ci = lax.broadcasted_iota(jnp.int32, (rows, rows), 1)
    same_seq = ri // steps == ci // steps
    g_hi, g_lo = _split_bf16(la_ref[...])
    tril = (same_seq & (ri >= ci)).astype(BF16)
    total = same_seq.astype(BF16)
    b = _dot(tril, g_hi) + _dot(tril, g_lo)
    b_last = _dot(total, g_hi) + _dot(total, g_lo)
    head_of_lane = lax.broadcasted_iota(jnp.int32, (1, GLA_K_W), 1) // GLA_DK
    k_t = k_ref[...] * jnp.exp(-b)
    qm = _head_masked_stack(q_ref[...] * jnp.exp(b), head_of_lane).astype(BF16)
    km = _head_masked_stack(k_t, head_of_lane).astype(BF16)
    kdm = _head_masked_stack(k_t * jnp.exp(b_last), head_of_lane).astype(BF16)
    v_st = _head_stack(v_ref[...], GLA_DV)
    seq_of_row = (lax.broadcasted_iota(jnp.int32, (stacked, 1), 0) % rows) // steps
    seq_of_g = lax.broadcasted_iota(jnp.int32, (rows, 1), 0) // steps
    zero = jnp.zeros((), BF16)
    rhs = jnp.concatenate([
        jnp.concatenate([v_st, jnp.zeros((stacked, GLA_DV), BF16)], axis=1),
        jnp.concatenate([jnp.zeros((2 * rows, GLA_DV), BF16), jnp.ones((2 * rows, GLA_DV), BF16)], axis=1),
    ], axis=0)
    a_raw = _dot_tb(qm, km)
    states = [s_ref[s].reshape(GLA_K_W, GLA_DV) for s in range(grp)]
    inter_all = [_dot(qm, st.astype(BF16)) for st in states]
    upd_all = []
    for s in range(grp):
        lhs = jnp.concatenate([jnp.where(seq_of_row == s, kdm, zero),
                               jnp.where(seq_of_g == s, g_hi, zero),
                               jnp.where(seq_of_g == s, g_lo, zero)], axis=0)
        upd_all.append(_dot_ta(lhs, rhs))
    rr = lax.broadcasted_iota(jnp.int32, (stacked, stacked), 0) % rows
    cc = lax.broadcasted_iota(jnp.int32, (stacked, stacked), 1) % rows
    causal = (rr // steps == cc // steps) & (rr >= cc)
    o = _dot(jnp.where(causal, a_raw, 0.0).astype(BF16), v_st)
    inter = inter_all[0]
    for s in range(1, grp):
        inter = jnp.where(seq_of_row == s, inter_all[s], inter)
    y = _gla_out(o + inter, gn_ref[...], _head_stack(gg_ref[...], GLA_DV)).astype(BF16)
    for h in range(GLA_HEADS):
        o_ref[:, h * GLA_DV:(h + 1) * GLA_DV] = y[h * rows:(h + 1) * rows, :]
    for s in range(grp):
        new = jnp.exp(upd_all[s][:, GLA_DV:]) * states[s] + upd_all[s][:, :GLA_DV]
        so_ref[s] = new.reshape(GLA_HEADS, GLA_DK, GLA_DV)


def _gla_decode(layer, q, k, la, v, gg, gain, state, steps, prev):
    nseq = state.shape[1]
    grp = min(GLA_DEC_GROUP, nseq)
    rows = grp * steps
    row = lambda i: (i, 0)
    st_spec = pl.BlockSpec((None, grp, GLA_HEADS, GLA_DK, GLA_DV), lambda i: (layer, i, 0, 0, 0))
    extra_in, extra_specs, out_spec, stacked = _stacked_out(
        state.shape, (grp, GLA_HEADS, GLA_DK, GLA_DV), layer, prev)
    n_in = 7
    return pl.pallas_call(
        functools.partial(_gla_decode_kernel, steps=steps, layer=layer),
        grid=(nseq // grp,),
        in_specs=[
            pl.BlockSpec((rows, GLA_K_W), row),
            pl.BlockSpec((rows, GLA_K_W), row),
            pl.BlockSpec((rows, GLA_K_W), row),
            pl.BlockSpec((rows, GLA_V_W), row),
            pl.BlockSpec((rows, GLA_V_W), row),
            _layer_resident(gain, layer),
            st_spec,
        ] + extra_specs,
        out_specs=[pl.BlockSpec((rows, GLA_V_W), row), out_spec],
        out_shape=[jax.ShapeDtypeStruct((nseq * steps, GLA_V_W), BF16), stacked],
        input_output_aliases={n_in + i: 1 + i for i in range(len(extra_in))},
        compiler_params=_params("parallel"),
        name="gla_decode",
    )(q, k, la, v, gg, gain, state, *extra_in)


def _outproj_apply(x, a_ref, o_ref, wa_ref, wo_ref, g_ref, wq_ref, qn_ref):
    x1 = x + _dot(a_ref[...], wa_ref[...]) + _dot(o_ref[...], wo_ref[...])
    h = _rms(x1, g_ref[...]).astype(BF16)
    qn = qn_ref[...]
    q = [_rms(_dot(h, wq_ref[:, hd * XA_HEAD_DIM:(hd + 1) * XA_HEAD_DIM]), qn).astype(BF16)
         for hd in range(XA_HEADS)]
    return x1, q


def _outproj_kernel(x_ref, a_ref, o_ref, wa_ref, wo_ref, g_ref, wq_ref, qn_ref, x1_ref, q_ref):
    x1, q = _outproj_apply(x_ref[...], a_ref, o_ref, wa_ref, wo_ref, g_ref, wq_ref, qn_ref)
    x1_ref[...] = x1
    for hd in range(XA_HEADS):
        q_ref[:, hd * XA_HEAD_DIM:(hd + 1) * XA_HEAD_DIM] = q[hd]


def _outproj(x, layer, a, o, w_a, w_o, g, wq, qn):
    n, d = x.shape
    tm = min(ROW_TILE, n)
    row = lambda i: (i, 0)
    return pl.pallas_call(
        _outproj_kernel,
        grid=(n // tm,),
        in_specs=[
            pl.BlockSpec((tm, d), row),
            pl.BlockSpec((tm, SWA_Q_W), row),
            pl.BlockSpec((tm, GLA_V_W), row),
        ] + [_layer_resident(p, layer) for p in (w_a, w_o, g, wq, qn)],
        out_specs=[pl.BlockSpec((tm, d), row), pl.BlockSpec((tm, XA_W), row)],
        out_shape=[jax.ShapeDtypeStruct((n, d), F32), jax.ShapeDtypeStruct((n, XA_W), BF16)],
        compiler_params=_params("parallel"),
        name="outproj",
    )(x, a, o, w_a, w_o, g, wq, qn)


def _memkv_kernel(m_ref, g_ref, wk_ref, wv_ref, kn_ref, k_ref, v_ref):
    m = _rms(m_ref[...], g_ref[...]).astype(BF16)
    kn = kn_ref[...]
    for hd in range(XA_HEADS):
        sl = slice(hd * XA_HEAD_DIM, (hd + 1) * XA_HEAD_DIM)
        k_ref[:, sl] = _rms(_dot(m, wk_ref[:, sl]), kn)
    v_ref[...] = _dot(m, wv_ref[...])


def _memkv(mem, g, wk, wv, kn):
    depth = wk.shape[0]
    n, d = mem.shape
    tm = min(ROW_TILE, n)
    per_layer = lambda l, i: (l, 0, 0)
    out = lambda l, i: (l, i, 0)
    return pl.pallas_call(
        _memkv_kernel,
        grid=(depth, n // tm),
        in_specs=[
            pl.BlockSpec((tm, d), lambda l, i: (i, 0)),
            pl.BlockSpec((None, 1, d), per_layer),
            pl.BlockSpec((None, d, XA_W), per_layer),
            pl.BlockSpec((None, d, XA_W), per_layer),
            pl.BlockSpec((None, 1, XA_HEAD_DIM), per_layer),
        ],
        out_specs=[pl.BlockSpec((None, tm, XA_W), out), pl.BlockSpec((None, tm, XA_W), out)],
        out_shape=[jax.ShapeDtypeStruct((depth, n, XA_W), F32)] * 2,
        compiler_params=_params("parallel", "parallel"),
        name="memkv",
    )(mem, g, wk, wv, kn)


def _mixout_prompt_kernel(x_ref, a_ref, o_ref, wa_ref, wo_ref, g_ref, wq_ref, qn_ref, mk_ref, mv_ref,
                          xwo_ref, fg_ref, wg_ref, wu_ref, wd_ref, out_ref, att_scr, a_scr):
    x1, q = _outproj_apply(x_ref[...], a_ref, o_ref, wa_ref, wo_ref, g_ref, wq_ref, qn_ref)
    mk = mk_ref[...].astype(BF16)
    mv = mv_ref[...].astype(BF16)
    heads = range(XA_HEADS)
    cols = lambda hd: slice(hd * XA_HEAD_DIM, (hd + 1) * XA_HEAD_DIM)
    scores = [_dot_tb(q[hd], mk[:, cols(hd)]) * (XA_HEAD_DIM ** -0.5 * LOG2E) for hd in heads]
    probs = []
    for s in scores:
        e = jnp.exp2(s - jnp.max(s, axis=-1, keepdims=True))
        probs.append((e.astype(BF16), jnp.sum(e, axis=-1, keepdims=True)))
    for hd in heads:
        att_scr[:, cols(hd)] = (_dot(probs[hd][0], mv[:, cols(hd)]) / probs[hd][1]).astype(BF16)
    x2 = x1 + _dot(att_scr[...], xwo_ref[...])
    out_ref[...] = _ffn_apply(x2, fg_ref, wg_ref, wu_ref, wd_ref, a_scr)


def _mixout_prompt(x, layer, a, o, outproj, mk, mv, wo, ffn, seq):
    n, d = x.shape
    dff = ffn[1].shape[2]
    tm = min(ROW_TILE, seq)
    per_seq = seq // tm
    row = lambda i: (i, 0)
    mem_spec = pl.BlockSpec((None, None, mk.shape[2], XA_W), lambda i: (layer, i // per_seq, 0, 0))
    return pl.pallas_call(
        _mixout_prompt_kernel,
        grid=(n // tm,),
        in_specs=([pl.BlockSpec((tm, d), row), pl.BlockSpec((tm, SWA_Q_W), row), pl.BlockSpec((tm, GLA_V_W), row)]
                  + [_layer_resident(p, layer) for p in outproj]
                  + [mem_spec, mem_spec]
                  + [_layer_resident(p, layer) for p in (wo,) + ffn]),
        out_specs=pl.BlockSpec((tm, d), row),
        out_shape=jax.ShapeDtypeStruct((n, d), F32),
        scratch_shapes=[pltpu.VMEM((tm, XA_W), BF16), pltpu.VMEM((tm, dff), BF16)],
        compiler_params=_params("parallel"),
        name="mixout_prompt",
    )(x, a, o, *outproj, mk, mv, wo, *ffn)


def _xattn_decode_kernel(q_ref, mk_ref, mv_ref, o_ref, *, steps):
    grp, nkeys = mk_ref.shape[0], mk_ref.shape[1]
    rows = grp * steps
    q = jnp.concatenate([q_ref[:, hd * XA_HEAD_DIM:(hd + 1) * XA_HEAD_DIM] for hd in range(XA_HEADS)], axis=0)
    r = lax.broadcasted_iota(jnp.int32, (XA_HEADS * rows, 1), 0)
    own = (r % rows) // steps
    same_head = (r // rows) == (lax.broadcasted_iota(jnp.int32, (1, nkeys), 1) % XA_HEADS)
    s = None
    for j in range(grp):
        sj = _dot_tb(q, mk_ref[j].astype(BF16))
        s = sj if s is None else jnp.where(own == j, sj, s)
    s = jnp.where(same_head, s * (XA_HEAD_DIM ** -0.5 * LOG2E), -jnp.inf)
    m = jnp.max(s, axis=-1, keepdims=True)
    e = jnp.exp2(s - m)
    p = e.astype(BF16)
    o = None
    for j in range(grp):
        oj = _dot(p, mv_ref[j].astype(BF16))
        o = oj if o is None else jnp.where(own == j, oj, o)
    o = o / jnp.sum(e, axis=-1, keepdims=True)
    for hd in range(XA_HEADS):
        o_ref[:, hd * XA_HEAD_DIM:(hd + 1) * XA_HEAD_DIM] = o[hd * rows:(hd + 1) * rows, :].astype(BF16)


def _xattn_decode(q, layer, mk, mv, steps):
    nseq, nkeys = mk.shape[1], mk.shape[2]
    grp = XA_DEC_GROUP
    rows = grp * steps
    row = lambda i: (i, 0)
    mem_spec = pl.BlockSpec((None, grp, nkeys, XA_HEAD_DIM), lambda i: (layer, i, 0, 0))
    return pl.pallas_call(
        functools.partial(_xattn_decode_kernel, steps=steps),
        grid=(nseq // grp,),
        in_specs=[pl.BlockSpec((rows, XA_W), row), mem_spec, mem_spec],
        out_specs=pl.BlockSpec((rows, XA_W), row),
        out_shape=jax.ShapeDtypeStruct((nseq * steps, XA_W), BF16),
        compiler_params=_params("parallel"),
        name="xattn_decode",
    )(q, mk, mv)


def _proj_res_kernel(x_ref, a_ref, w_ref, o_ref):
    o_ref[...] = x_ref[...] + _dot(a_ref[...], w_ref[...])


def _proj_res(x, layer, a, w):
    n, d = x.shape
    tm = min(ROW_TILE, n)
    row = lambda i: (i, 0)
    return pl.pallas_call(
        _proj_res_kernel,
        grid=(n // tm,),
        in_specs=[pl.BlockSpec((tm, d), row), pl.BlockSpec((tm, a.shape[1]), row), _layer_resident(w, layer)],
        out_specs=pl.BlockSpec((tm, d), row),
        out_shape=jax.ShapeDtypeStruct((n, d), F32),
        compiler_params=_params("parallel"),
        name="proj_res",
    )(x, a, w)


def _rope_tables(pos):
    half = HEAD_DIM // 2
    inv = ROPE_THETA ** (-jnp.arange(half, dtype=F32) / half)
    ang = pos.astype(F32)[:, None] * inv[None, :]
    cos, sin = jnp.cos(ang), jnp.sin(ang)
    reps = LANES // HEAD_DIM
    return jnp.tile(cos, (1, 2 * reps)), jnp.tile(jnp.concatenate([-sin, sin], axis=-1), (1, reps))


def _block_tril(n_blocks, size):
    i = jnp.arange(n_blocks * size)
    return ((i[:, None] // size == i[None, :] // size) & (i[:, None] >= i[None, :])).astype(BF16)


def _permute_heads(w, axis):
    blocks = jnp.split(w, SWA_Q_HEADS, axis=axis)
    return jnp.concatenate([blocks[h] for h in SWA_HEAD_ORDER], axis=axis)


def kernel(x_prompt, x_sample, cache_swa_k, cache_swa_v, state_gla, cache_mem_k, cache_mem_v, mem_prompt, ffn1_norm, ffn1_wg, ffn1_wu, ffn1_wd, mix_norm, w_in, swa_q_norm, swa_k_norm, swa_sinks, gla_w_gate, gla_b_gate, gla_out_norm, w_out, xa_norm, mem_norm, xa_wq, xa_wk, xa_wv, xa_q_norm, xa_k_norm, xa_wo, ffn2_norm, ffn2_wg, ffn2_wu, ffn2_wd):
    batch, seq, d = x_prompt.shape
    nseq, steps, _ = x_sample.shape
    depth = w_in.shape[0]
    mem_len = mem_prompt.shape[1]

    bf = lambda w: w.astype(BF16)
    vec = lambda p: p[:, None, :]
    ffn1_w = [tuple(bf(w[:1]) for w in (ffn1_wg, ffn1_wu, ffn1_wd))] + [None] * (depth - 1)
    ffn2_w = [None] * depth
    to_convert = ([(w, l) for l in range(depth) for w in (ffn2_wg, ffn2_wu, ffn2_wd)]
                  + [(w, l) for l in range(1, depth) for w in (ffn1_wg, ffn1_wu, ffn1_wd)])
    w_all = jnp.pad(bf(w_in), ((0, 0), (0, 0), (0, MXU_TILE - GLA_LOWRANK)))
    gate_w = jnp.pad(bf(gla_w_gate), ((0, 0), (0, LANES - GLA_LOWRANK), (0, 0)))
    w_a = bf(_permute_heads(w_out[:, :SWA_Q_W], 1))
    w_o = bf(w_out[:, SWA_Q_W:])
    wq_b, wk_b, wv_b, wo_b = bf(xa_wq), bf(xa_wk), bf(xa_wv), bf(xa_wo)
    qn = vec(jnp.tile(swa_q_norm * (HEAD_DIM ** -0.5 * LOG2E), (1, LANES // HEAD_DIM)))
    kn = vec(jnp.tile(swa_k_norm, (1, LANES // HEAD_DIM)))
    inproj_params = (vec(mix_norm), w_all, gate_w, vec(gla_b_gate), qn, kn)
    outproj_params = (w_a, w_o, vec(xa_norm), wq_b, vec(xa_q_norm))
    gla_gain = vec(gla_out_norm)

    lane = jnp.arange(MXU_TILE)
    bd = (lane[:, None] // HEAD_DIM == lane[None, :] // HEAD_DIM).astype(BF16)
    cos_p, sin_p = _rope_tables(jnp.arange(seq))
    cos_s, sin_s = _rope_tables(PAST_LEN + jnp.arange(nseq * steps) % steps)
    tril_p = _block_tril(min(GLA_BLOCK, seq) // min(GLA_CHUNK, seq), min(GLA_CHUNK, seq))

    mk_p, mv_p = _memkv(mem_prompt.reshape(batch * mem_len, d), vec(mem_norm), wk_b, wv_b, vec(xa_k_norm))
    mk_p = mk_p.reshape(depth, batch, mem_len, XA_W)
    mv_p = mv_p.reshape(depth, batch, mem_len, XA_W)
    mk_s = cache_mem_k.reshape(depth, nseq, mem_len * XA_HEADS, XA_HEAD_DIM)
    mv_s = cache_mem_v.reshape(depth, nseq, mem_len * XA_HEADS, XA_HEAD_DIM)
    native = lambda c: jnp.transpose(c, (0, 1, 3, 4, 2)).reshape(depth, nseq, SWA_KV_W, WINDOW)
    kc_s, vc_s = native(cache_swa_k), native(cache_swa_v)

    xp = x_prompt.reshape(batch * seq, d)
    xs = x_sample.reshape(nseq * steps, d)
    kp_l, vp_l, sp_l = [], [], []
    swa_new = gla_new = None
    for l in range(depth):
        ffn1 = (vec(ffn1_norm),) + tuple(ffn1_w[l])
        xp, q_s, k_s, v_s, q_g, k_g, v_g, g_g, la = _ffn_inproj(xp, l, ffn1, inproj_params, cos_p, sin_p, bd)
        a_p, *cast = _swa_prompt(swa_sinks, l, q_s, k_s, v_s, batch, seq, to_convert if l == 0 else ())
        if l == 0:
            for m in range(depth):
                ffn2_w[m] = cast[3 * m:3 * m + 3]
            for m in range(1, depth):
                ffn1_w[m] = cast[3 * (depth + m - 1):3 * (depth + m)]
        ffn2 = (vec(ffn2_norm),) + tuple(ffn2_w[l])
        seqs = lambda a: a.reshape(batch, seq, a.shape[-1])
        o_p, s_p = _gla_prompt(l, seqs(q_g), seqs(k_g), seqs(la), seqs(v_g), seqs(g_g), gla_gain, tril_p,
                               batch, seq)
        o_p = o_p.reshape(batch * seq, GLA_V_W)
        last = lambda a: seqs(a)[:, seq - WINDOW:].reshape(batch, WINDOW, SWA_KV_HEADS, HEAD_DIM)
        kp_l.append(last(k_s))
        vp_l.append(last(v_s))
        sp_l.append(s_p.reshape(batch, GLA_HEADS, GLA_DK, GLA_DV))
        xp = _mixout_prompt(xp, l, a_p, o_p, outproj_params, mk_p, mv_p, wo_b, ffn2, seq)

        xs, q_s, k_s, v_s, q_g, k_g, v_g, g_g, la, k_t, v_t = _ffn_inproj(
            xs, l, ffn1, inproj_params, cos_s, sin_s, bd, transposed_kv=True)
        a_s, *swa_new = _swa_decode(swa_sinks, l, q_s, k_s, v_s, k_t, v_t, kc_s, vc_s, steps, swa_new)
        o_s, *gla_new = _gla_decode(l, q_g, k_g, la, v_g, g_g, gla_gain, state_gla, steps, gla_new)
        xs, q_x = _outproj(xs, l, a_s, o_s, *outproj_params)
        xs = _proj_res(xs, l, _xattn_decode(q_x, l, mk_s, mv_s, steps), wo_b)
        xs = _ffn(xs, l, *ffn2)

    unnative = lambda c: jnp.transpose(c.reshape(depth, nseq, SWA_KV_HEADS, HEAD_DIM, WINDOW), (0, 1, 4, 2, 3))
    return (xp.reshape(batch, seq, d), xs.reshape(nseq, steps, d),
            jnp.stack(kp_l), jnp.stack(vp_l), jnp.stack(sp_l),
            mk_p.reshape(depth, batch, mem_len, XA_HEADS, XA_HEAD_DIM),
            mv_p.reshape(depth, batch, mem_len, XA_HEADS, XA_HEAD_DIM),
            unnative(swa_new[0]), unnative(swa_new[1]), gla_new[0])
```

```python
import functools

import jax
import jax.numpy as jnp
from jax import lax
from jax.experimental import pallas as pl
from jax.experimental.pallas import tpu as pltpu

F32 = jnp.float32
BF16 = jnp.bfloat16

EPS = 1e-6
LOG2E = 1.4426950408889634
PAST_LEN = 16384
WINDOW = 128
ROPE_THETA = 10000.0
HEAD_DIM = 64
SWA_Q_HEADS = 8
SWA_KV_HEADS = 2
SWA_GROUP = SWA_Q_HEADS // SWA_KV_HEADS
GLA_HEADS = 4
GLA_DK = 64
GLA_DV = 128
GLA_LOWRANK = 16
GLA_GATE_TEMP = 16.0
GLA_CHUNK = 64
XA_HEADS = 4
XA_HEAD_DIM = 128

SWA_Q_W = SWA_Q_HEADS * HEAD_DIM
SWA_KV_W = SWA_KV_HEADS * HEAD_DIM
GLA_K_W = GLA_HEADS * GLA_DK
GLA_V_W = GLA_HEADS * GLA_DV
XA_W = XA_HEADS * XA_HEAD_DIM
MAIN_W = SWA_Q_W + 2 * SWA_KV_W + 2 * GLA_K_W + 2 * GLA_V_W

LANES = 128
BF16_SUBLANES = 16
MXU_TILE = 256
VMEM_LIMIT = 56 * 1024 * 1024

ROW_TILE = 512
ROW_SPLIT = 2
FFN_CHUNK = 256
SWA_BLOCK = 256
GLA_BLOCK = 256
GLA_SEQS = 4
DEC_GROUP = 4
SWA_DEC_SUBGROUPS = 8
XA_DEC_GROUP = 8
GLA_DEC_GROUP = 8

assert SWA_KV_HEADS * HEAD_DIM == LANES
SWA_HEAD_ORDER = tuple(kv * SWA_GROUP + g for g in range(SWA_GROUP) for kv in range(SWA_KV_HEADS))


def _dot(a, b):
    return jnp.dot(a, b, preferred_element_type=F32)


def _dot_tb(a, b):
    return lax.dot_general(a, b, (((1,), (1,)), ((), ())), preferred_element_type=F32)


def _dot_ta(a, b):
    return lax.dot_general(a, b, (((0,), (0,)), ((), ())), preferred_element_type=F32)


def _split_bf16(x):
    hi = x.astype(BF16)
    lo = (x - hi.astype(F32)).astype(BF16)
    return hi, lo


def _rms(x, g):
    ms = jnp.mean(x * x, axis=-1, keepdims=True)
    return x * lax.rsqrt(ms + EPS) * g


def _params(*sem):
    return pltpu.CompilerParams(dimension_semantics=sem, vmem_limit_bytes=VMEM_LIMIT)


def _resident(arr):
    nd = arr.ndim
    return pl.BlockSpec(arr.shape, lambda *_: (0,) * nd, pipeline_mode=pl.Buffered(1))


def _layer_resident(arr, layer):
    nd = arr.ndim
    index = (layer if arr.shape[0] > 1 else 0,) + (0,) * (nd - 1)
    return pl.BlockSpec((None,) + arr.shape[1:], lambda *_: index, pipeline_mode=pl.Buffered(1))


def _cast_specs(cast, n_steps):
    ins, outs, shapes = [], [], []
    for arr, src_layer in cast:
        _, rows, cols = arr.shape
        share = 1
        while rows % (n_steps // share) or (rows * share // n_steps) % BF16_SUBLANES:
            share *= 2
            assert n_steps % share == 0
        chunk = rows * share // n_steps
        ins.append(pl.BlockSpec((None, chunk, cols), lambda i, l=src_layer, s=share: (l, i // s, 0)))
        outs.append(pl.BlockSpec((1, chunk, cols), lambda i, s=share: (0, i // s, 0)))
        shapes.append(jax.ShapeDtypeStruct((1, rows, cols), BF16))
    return ins, outs, shapes


def _cast_chunks(srcs, dsts):
    for src, dst in zip(srcs, dsts):
        dst[0] = src[...].astype(BF16)


def _ffn_hidden(x, g_ref, wg_ref, wu_ref, a_scr):
    h = _rms(x, g_ref[...]).astype(BF16)
    dff = wg_ref.shape[1]
    for c in range(dff // FFN_CHUNK):
        sl = slice(c * FFN_CHUNK, (c + 1) * FFN_CHUNK)
        g = _dot(h, wg_ref[:, sl])
        u = _dot(h, wu_ref[:, sl])
        a_scr[:, sl] = (g * jax.nn.sigmoid(g) * u).astype(BF16)


def _ffn_apply(x, g_ref, wg_ref, wu_ref, wd_ref, a_scr):
    _ffn_hidden(x, g_ref, wg_ref, wu_ref, a_scr)
    return x + 0.5 * _dot(a_scr[...], wd_ref[...])


def _ffn_kernel(x_ref, g_ref, wg_ref, wu_ref, wd_ref, o_ref, a_scr):
    o_ref[...] = _ffn_apply(x_ref[...], g_ref, wg_ref, wu_ref, wd_ref, a_scr)


def _ffn(x, layer, g, wg, wu, wd):
    n, d = x.shape
    dff = wg.shape[2]
    tm = min(ROW_TILE, n)
    return pl.pallas_call(
        _ffn_kernel,
        grid=(n // tm,),
        in_specs=[pl.BlockSpec((tm, d), lambda i: (i, 0))] + [_layer_resident(a, layer) for a in (g, wg, wu, wd)],
        out_specs=pl.BlockSpec((tm, d), lambda i: (i, 0)),
        out_shape=jax.ShapeDtypeStruct((n, d), F32),
        scratch_shapes=[pltpu.VMEM((tm, dff), BF16)],
        compiler_params=_params("parallel"),
        name="ffn",
    )(x, g, wg, wu, wd)


def _ffn_inproj_kernel(*refs, n_cast, transposed_kv):
    (x_ref, fg_ref, wg_ref, wu_ref, wd_ref, g_ref, w_ref, gw_ref, gb_ref, qn_ref, kn_ref, cos_ref, sin_ref,
     bd_ref) = refs[:14]
    outs, a_scr = refs[14 + n_cast:-1], refs[-1]
    xo_ref, qs_ref, ks_ref, vs_ref, qg_ref, kg_ref, vg_ref, gg_ref, la_ref = outs[:9]
    kt_ref, vt_ref = outs[9:11] if transposed_kv else (None, None)
    _cast_chunks(refs[14:14 + n_cast], outs[len(outs) - n_cast:])
    x = x_ref[...]
    _ffn_hidden(x, fg_ref, wg_ref, wu_ref, a_scr)
    n = x.shape[0] // ROW_SPLIT
    groups = [slice(r * n, (r + 1) * n) for r in range(ROW_SPLIT)]
    mid = [x[rs] + 0.5 * _dot(a_scr[rs, :], wd_ref[...]) for rs in groups]
    for rs, xm in zip(groups, mid):
        xo_ref[rs, :] = xm
    split = SWA_Q_W + 2 * SWA_KV_W + 2 * GLA_K_W
    h = [_rms(xm, g_ref[...]).astype(BF16) for xm in mid]
    z1 = [_dot(hr, w_ref[:, :split]) for hr in h]
    z2 = []
    for rs, hr, z in zip(groups, h, z1):
        _inproj_finish_swa(z, rs, qn_ref, kn_ref, cos_ref, sin_ref, bd_ref, qs_ref, ks_ref, vs_ref, qg_ref,
                           kg_ref, kt_ref, vt_ref)
        z2.append(_dot(hr, w_ref[:, split:]))
    for rs, z in zip(groups, z2):
        _inproj_finish_gla(z, rs, gw_ref, gb_ref, vg_ref, gg_ref, la_ref)


def _inproj_finish_swa(z1, rs, qn_ref, kn_ref, cos_ref, sin_ref, bd_ref, qs_ref, ks_ref, vs_ref, qg_ref,
                       kg_ref, kt_ref, vt_ref):
    bd = bd_ref[...]
    cos = cos_ref[rs, :]
    sin = sin_ref[rs, :]
    lane = lax.broadcasted_iota(jnp.int32, cos.shape, 1)
    lane_lo = (lane % HEAD_DIM) < (HEAD_DIM // 2)

    def head_scale(z):
        ss = _dot((z * z).astype(BF16), bd)
        return lax.rsqrt(ss * (1.0 / HEAD_DIM) + EPS)

    def rope(y):
        swapped = jnp.where(lane_lo, pltpu.roll(y, LANES - HEAD_DIM // 2, axis=1),
                            pltpu.roll(y, HEAD_DIM // 2, axis=1))
        return y * cos + swapped * sin

    qn = qn_ref[...]
    nat = []
    for t in range(SWA_Q_W // MXU_TILE):
        z = z1[:, t * MXU_TILE:(t + 1) * MXU_TILE]
        y = z * head_scale(z)
        nat += [rope(y[:, c * LANES:(c + 1) * LANES] * qn) for c in range(MXU_TILE // LANES)]
    low = _low_half()
    per_col = LANES // HEAD_DIM
    for c in range(SWA_Q_W // LANES):
        halves = []
        for half, head in enumerate(SWA_HEAD_ORDER[per_col * c:per_col * (c + 1)]):
            col = nat[head // per_col]
            halves.append(col if head % per_col == half else pltpu.roll(col, HEAD_DIM, axis=1))
        qs_ref[rs, c * LANES:(c + 1) * LANES] = jnp.where(low, halves[0], halves[1]).astype(BF16)
    o = SWA_Q_W
    z = z1[:, o:o + 2 * SWA_KV_W]
    k = rope(z[:, :SWA_KV_W] * head_scale(z)[:, :SWA_KV_W] * kn_ref[...])
    ks_ref[rs, :] = k
    vs_ref[rs, :] = z[:, SWA_KV_W:]
    if kt_ref is not None:
        kt_ref[:, rs] = k.T
        vt_ref[:, rs] = z[:, SWA_KV_W:].T
    o += 2 * SWA_KV_W
    qg_ref[rs, :] = z1[:, o:o + GLA_K_W] * (GLA_DK ** -0.5)
    o += GLA_K_W
    kg_ref[rs, :] = z1[:, o:o + GLA_K_W]


def _inproj_finish_gla(z2, rs, gw_ref, gb_ref, vg_ref, gg_ref, la_ref):
    vg_ref[rs, :] = z2[:, :GLA_V_W].astype(BF16)
    gg_ref[rs, :] = z2[:, GLA_V_W:2 * GLA_V_W]
    lr = z2[:, 2 * GLA_V_W:2 * GLA_V_W + LANES].astype(BF16)
    t = _dot(lr, gw_ref[...]) + gb_ref[...]
    log_sig = jnp.minimum(t, 0.0) - jnp.log(1.0 + jnp.exp(-jnp.abs(t)))
    la_ref[rs, :] = log_sig * (1.0 / GLA_GATE_TEMP)


def _ffn_inproj(x, layer, ffn, inproj, cos, sin, bd, transposed_kv=False, cast=()):
    n, d = x.shape
    dff = ffn[1].shape[2]
    tm = min(ROW_TILE, n)
    pos_blocks = cos.shape[0] // tm
    row = lambda i: (i, 0)
    pos = lambda i: (i % pos_blocks, 0)
    widths = (d, SWA_Q_W, SWA_KV_W, SWA_KV_W, GLA_K_W, GLA_K_W, GLA_V_W, GLA_V_W, GLA_K_W)
    dtypes = (F32, BF16, F32, F32, F32, F32, BF16, F32, F32)
    out_specs = [pl.BlockSpec((tm, w), row) for w in widths]
    out_shape = [jax.ShapeDtypeStruct((n, w), dt) for w, dt in zip(widths, dtypes)]
    if transposed_kv:
        out_specs += [pl.BlockSpec((SWA_KV_W, tm), lambda i: (0, i))] * 2
        out_shape += [jax.ShapeDtypeStruct((SWA_KV_W, n), F32)] * 2
    cast_in, cast_out, cast_shape = _cast_specs(cast, n // tm)
    return pl.pallas_call(
        functools.partial(_ffn_inproj_kernel, n_cast=len(cast), transposed_kv=transposed_kv),
        grid=(n // tm,),
        in_specs=([pl.BlockSpec((tm, d), row)]
                  + [_layer_resident(a, layer) for a in ffn + inproj]
                  + [pl.BlockSpec((tm, LANES), pos), pl.BlockSpec((tm, LANES), pos), _resident(bd)]
                  + cast_in),
        out_specs=out_specs + cast_out,
        out_shape=out_shape + cast_shape,
        scratch_shapes=[pltpu.VMEM((tm, dff), BF16)],
        compiler_params=_params("parallel"),
        name="ffn_inproj",
    )(x, *ffn, *inproj, cos, sin, bd, *[arr for arr, _ in cast])


def _swa_attend(q, k, v, valid, sinks):
    return _swa_out(*_swa_probs(_swa_scores(q, k), valid, sinks), v)


def _low_half():
    return lax.broadcasted_iota(jnp.int32, (1, LANES), 1) < HEAD_DIM


def _swa_head_rows(q):
    low = _low_half()
    zero = jnp.zeros((), q.dtype)
    pieces = []
    for c in range(SWA_Q_W // LANES):
        qc = q[:, c * LANES:(c + 1) * LANES]
        pieces += [jnp.where(low, qc, zero), jnp.where(low, zero, qc)]
    return jnp.concatenate(pieces, axis=0)


def _swa_scores(q, k):
    return _dot_tb(_swa_head_rows(q), k)


def _swa_probs(s, valid, sinks):
    rows, keys = valid.shape
    s = jnp.where(valid[None], s.reshape(SWA_Q_HEADS, rows, keys), -jnp.inf)
    m = jnp.maximum(jnp.max(s, axis=-1, keepdims=True), sinks)
    e = jnp.exp2(s - m)
    den = jnp.sum(e, axis=-1, keepdims=True) + jnp.exp2(sinks - m)
    return e.astype(BF16).reshape(SWA_Q_HEADS * rows, keys), den


def _swa_out(e, den, v):
    return _swa_finish(_dot(e, v), den)


def _swa_finish(o, den):
    low = _low_half()
    o = o.reshape(den.shape[0], den.shape[1], LANES) / den
    return [jnp.where(low, o[2 * c], o[2 * c + 1]) for c in range(SWA_Q_W // LANES)]


def _sink_column(sink_ref, layer):
    idx = lax.broadcasted_iota(jnp.int32, (SWA_Q_HEADS, 1, 1), 0)
    col = jnp.zeros((SWA_Q_HEADS, 1, 1), F32)
    for p, head in enumerate(SWA_HEAD_ORDER):
        col = jnp.where(idx == p, sink_ref[layer, head] * LOG2E, col)
    return col


def _swa_prompt_kernel(sink_ref, q_ref, kp_ref, kc_ref, vp_ref, vc_ref, o_ref, *, layer):
    j = pl.program_id(1)
    kall = jnp.concatenate([kp_ref[...], kc_ref[...]], axis=0).astype(BF16)
    vall = jnp.concatenate([vp_ref[...], vc_ref[...]], axis=0).astype(BF16)
    sinks = _sink_column(sink_ref, layer)
    upper = (lax.broadcasted_iota(jnp.int32, (WINDOW, WINDOW), 1)
             > lax.broadcasted_iota(jnp.int32, (WINDOW, WINDOW), 0))[None]

    def banded_probs(s, has_prev):
        s = s.reshape(SWA_Q_HEADS, WINDOW, 2 * WINDOW)
        prev = s[:, :, :WINDOW]
        if has_prev is not True:
            prev = jnp.where(has_prev, prev, -jnp.inf)
        logits = jnp.where(upper, prev, s[:, :, WINDOW:])
        m = jnp.maximum(jnp.max(logits, axis=-1, keepdims=True), sinks)
        e = jnp.exp2(logits - m)
        den = jnp.sum(e, axis=-1, keepdims=True) + jnp.exp2(sinks - m)
        e = e.astype(BF16)
        zero = jnp.zeros((), BF16)
        e = jnp.concatenate([jnp.where(upper, e, zero), jnp.where(upper, zero, e)], axis=-1)
        return e.reshape(SWA_Q_HEADS * WINDOW, 2 * WINDOW), den

    windows = range(q_ref.shape[0] // WINDOW)
    rows = lambda w: slice(w * WINDOW, (w + 1) * WINDOW)
    keys = lambda w: slice(w * WINDOW, (w + 2) * WINDOW)
    scores = [_swa_scores(q_ref[rows(w), :], kall[keys(w)]) for w in windows]
    probs = [banded_probs(scores[w], True if w > 0 else j > 0) for w in windows]
    for w in windows:
        for c, col in enumerate(_swa_out(*probs[w], vall[keys(w)])):
            o_ref[rows(w), c * LANES:(c + 1) * LANES] = col.astype(BF16)


def _swa_prompt(sinks, layer, q, k, v, batch, seq):
    qb = min(SWA_BLOCK, seq)
    nb = seq // qb
    per = qb // WINDOW
    cur = lambda b, j: (b * nb + j, 0)
    prev = lambda b, j: ((b * nb + j) * per - jnp.minimum(j, 1), 0)
    return pl.pallas_call(
        functools.partial(_swa_prompt_kernel, layer=layer),
        grid=(batch, nb),
        in_specs=[
            pl.BlockSpec(memory_space=pltpu.SMEM),
            pl.BlockSpec((qb, SWA_Q_W), cur),
            pl.BlockSpec((WINDOW, SWA_KV_W), prev),
            pl.BlockSpec((qb, SWA_KV_W), cur),
            pl.BlockSpec((WINDOW, SWA_KV_W), prev),
            pl.BlockSpec((qb, SWA_KV_W), cur),
        ],
        out_specs=pl.BlockSpec((qb, SWA_Q_W), cur),
        out_shape=jax.ShapeDtypeStruct((batch * seq, SWA_Q_W), BF16),
        compiler_params=_params("parallel", "parallel"),
        name="swa_prompt",
    )(sinks, q, k, k, v, v)


def _layer_view(ref, layer, first):
    if not first:
        return ref
    for other in range(ref.shape[0]):
        if other != layer:
            ref[other] = jnp.zeros(ref.shape[1:], ref.dtype)
    return ref.at[layer]


def _swa_decode_kernel(sink_ref, q_ref, kn_ref, vn_ref, knt_ref, vnt_ref, kc_ref, vc_ref, *rest, steps, layer):
    o_ref, ko_ref, vo_ref = rest[-3:]
    first = len(rest) == 3
    ko_ref = _layer_view(ko_ref, layer, first)
    vo_ref = _layer_view(vo_ref, layer, first)
    n_seq = kc_ref.shape[0]
    grp = DEC_GROUP
    rows = grp * steps
    keep = WINDOW - steps
    lane = lax.broadcasted_iota(jnp.int32, (1, WINDOW), 1)
    for s in range(n_seq):
        tile = slice((s * steps) // LANES * LANES, (s * steps) // LANES * LANES + LANES)
        shift = (keep - s * steps) % LANES
        for cache_ref, new_ref, out_ref in ((kc_ref, knt_ref, ko_ref), (vc_ref, vnt_ref, vo_ref)):
            out_ref[s] = jnp.where(lane >= keep, pltpu.roll(new_ref[:, tile], shift, axis=1),
                                   pltpu.roll(cache_ref[s], keep, axis=1))
    nk = grp * WINDOW + rows
    r = lax.broadcasted_iota(jnp.int32, (rows, nk), 0)
    q_seq, q_step = r // steps, r % steps
    c = lax.broadcasted_iota(jnp.int32, (rows, nk), 1)
    is_new = c >= grp * WINDOW
    cn = c - grp * WINDOW
    k_seq = jnp.where(is_new, cn // steps, c // WINDOW)
    k_idx = jnp.where(is_new, WINDOW + cn % steps, c % WINDOW)
    rel = WINDOW + q_step - k_idx
    valid = (q_seq == k_seq) & (rel >= 0) & (rel < WINDOW)
    sinks = _sink_column(sink_ref, layer)
    groups = range(n_seq // grp)
    new_rows = lambda g: slice(g * rows, (g + 1) * rows)
    cached = lambda ref, g: jnp.concatenate([ref[g * grp + s] for s in range(grp)], axis=1).astype(BF16)
    scores = []
    for g in groups:
        qm = _swa_head_rows(q_ref[new_rows(g), :])
        scores.append(jnp.concatenate(
            [_dot(qm, cached(kc_ref, g)), _dot_tb(qm, kn_ref[new_rows(g), :].astype(BF16))], axis=1))
    probs = [_swa_probs(scores[g], valid, sinks) for g in groups]
    for g in groups:
        e, den = probs[g]
        o = (_dot_tb(e[:, :grp * WINDOW], cached(vc_ref, g))
             + _dot(e[:, grp * WINDOW:], vn_ref[new_rows(g), :].astype(BF16)))
        for c, col in enumerate(_swa_finish(o, den)):
            o_ref[new_rows(g), c * LANES:(c + 1) * LANES] = col.astype(BF16)


def _stacked_out(shape, block, layer, prev):
    tail = (0,) * (len(block) - 1)
    if prev is None:
        spec = pl.BlockSpec((shape[0],) + block, lambda i: (0, i) + tail)
        extra_inputs = []
    else:
        spec = pl.BlockSpec((None,) + block, lambda i: (layer, i) + tail)
        extra_inputs = list(prev)
    extra_specs = [pl.BlockSpec(memory_space=pl.ANY) for _ in extra_inputs]
    return extra_inputs, extra_specs, spec, jax.ShapeDtypeStruct(shape, F32)


def _swa_decode(sinks, layer, q, kn, vn, knt, vnt, kc, vc, steps, prev):
    depth, nseq = kc.shape[:2]
    grp = min(DEC_GROUP * SWA_DEC_SUBGROUPS, nseq)
    rows = grp * steps
    assert rows % LANES == 0, "a grid step's new keys must fill whole 128-lane tiles"
    row = lambda i: (i, 0)
    cache_spec = pl.BlockSpec((None, grp, SWA_KV_W, WINDOW), lambda i: (layer, i, 0, 0))
    new_t = pl.BlockSpec((SWA_KV_W, rows), lambda i: (0, i))
    extra_in, extra_specs, out_spec, stacked = _stacked_out(kc.shape, (grp, SWA_KV_W, WINDOW), layer, prev)
    n_in = 8
    return pl.pallas_call(
        functools.partial(_swa_decode_kernel, steps=steps, layer=layer),
        grid=(nseq // grp,),
        in_specs=[
            pl.BlockSpec(memory_space=pltpu.SMEM),
            pl.BlockSpec((rows, SWA_Q_W), row),
            pl.BlockSpec((rows, SWA_KV_W), row),
            pl.BlockSpec((rows, SWA_KV_W), row),
            new_t,
            new_t,
            cache_spec,
            cache_spec,
        ] + extra_specs,
        out_specs=[pl.BlockSpec((rows, SWA_Q_W), row), out_spec, out_spec],
        out_shape=[jax.ShapeDtypeStruct((nseq * steps, SWA_Q_W), BF16), stacked, stacked],
        input_output_aliases={n_in + i: 1 + i for i in range(len(extra_in))},
        compiler_params=_params("parallel"),
        name="swa_decode",
    )(sinks, q, kn, vn, knt, vnt, kc, vc, *extra_in)


def _gla_out(o, gain, gate):
    return _rms(o, gain) * (gate * jax.nn.sigmoid(gate))


def _head_stack(x, width):
    return jnp.concatenate([x[:, h * width:(h + 1) * width] for h in range(GLA_HEADS)], axis=0)


def _head_masked_stack(x, head_of_lane):
    zero = jnp.zeros((), x.dtype)
    return jnp.concatenate([jnp.where(head_of_lane == h, x, zero) for h in range(GLA_HEADS)], axis=0)


def _gla_prompt_kernel(q_ref, k_ref, la_ref, v_ref, gg_ref, gn_ref, tril_ref, o_ref, s_ref, st_scr):
    tb = pl.program_id(1)

    @pl.when(tb == 0)
    def _():
        st_scr[...] = jnp.zeros_like(st_scr)

    n_tok = q_ref.shape[1]
    c_len = min(GLA_CHUNK, n_tok)
    tril = tril_ref[...]
    head_of_lane = lax.broadcasted_iota(jnp.int32, (1, GLA_K_W), 1) // GLA_DK
    ri = lax.broadcasted_iota(jnp.int32, (GLA_HEADS * c_len, c_len), 0) % c_len
    ci = lax.broadcasted_iota(jnp.int32, (GLA_HEADS * c_len, c_len), 1)
    causal = ri >= ci
    gain = gn_ref[...]
    n_seq = q_ref.shape[0]
    items = [(i, c) for i in range(n_seq) for c in range(n_tok // c_len)]
    rows = lambda c: slice(c * c_len, (c + 1) * c_len)
    b_all = []
    for i in range(n_seq):
        g_hi, g_lo = _split_bf16(la_ref[i])
        b_all.append(_dot(tril, g_hi) + _dot(tril, g_lo))
    qm, kdm, dec, a_raw = {}, {}, {}, {}
    for it in items:
        i, c = it
        b = b_all[i][rows(c), :]
        dec[it] = jnp.exp(b[c_len - 1:c_len, :])
        q_t = q_ref[i, rows(c), :] * jnp.exp(b)
        k_t = k_ref[i, rows(c), :] * jnp.exp(-b)
        qm[it] = _head_masked_stack(q_t, head_of_lane).astype(BF16)
        kdm[it] = _head_masked_stack(k_t * dec[it], head_of_lane).astype(BF16)
        a_raw[it] = _dot_tb(qm[it], k_t.astype(BF16))
    upd = {it: _dot_ta(_head_stack(v_ref[it[0], rows(it[1]), :], GLA_DV), kdm[it]) for it in items}
    inter = {}
    for i in range(n_seq):
        st = st_scr[i]
        for c in range(n_tok // c_len):
            inter[(i, c)] = _dot_tb(qm[(i, c)], st.astype(BF16))
            st = dec[(i, c)] * st + upd[(i, c)]
        st_scr[i] = st
    for it in items:
        i, c = it
        a = jnp.where(causal, a_raw[it], 0.0).astype(BF16)
        intra = jnp.concatenate(
            [_dot(a[h * c_len:(h + 1) * c_len, :], v_ref[i, rows(c), h * GLA_DV:(h + 1) * GLA_DV])
             for h in range(GLA_HEADS)], axis=0)
        y = _gla_out(inter[it] + intra, gain, _head_stack(gg_ref[i, rows(c), :], GLA_DV)).astype(BF16)
        for h in range(GLA_HEADS):
            o_ref[i, rows(c), h * GLA_DV:(h + 1) * GLA_DV] = y[h * c_len:(h + 1) * c_len, :]

    @pl.when(tb == pl.num_programs(1) - 1)
    def _():
        for i in range(q_ref.shape[0]):
            s_ref[i] = st_scr[i].T


def _gla_prompt(layer, q, k, la, v, gg, gain, tril, batch, seq):
    tb = tril.shape[0]
    per = min(GLA_SEQS, batch)
    blk = lambda b, t: (b, t, 0)
    kw = pl.BlockSpec((per, tb, GLA_K_W), blk)
    vw = pl.BlockSpec((per, tb, GLA_V_W), blk)
    return pl.pallas_call(
        _gla_prompt_kernel,
        grid=(batch // per, seq // tb),
        in_specs=[kw, kw, kw, vw, vw, _layer_resident(gain, layer), _resident(tril)],
        out_specs=[vw, pl.BlockSpec((per, GLA_K_W, GLA_DV), lambda b, t: (b, 0, 0))],
        out_shape=[
            jax.ShapeDtypeStruct((batch, seq, GLA_V_W), BF16),
            jax.ShapeDtypeStruct((batch, GLA_K_W, GLA_DV), F32),
        ],
        scratch_shapes=[pltpu.VMEM((per, GLA_DV, GLA_K_W), F32)],
        compiler_params=_params("parallel", "arbitrary"),
        name="gla_prompt",
    )(q, k, la, v, gg, gain, tril)


def _gla_decode_kernel(q_ref, k_ref, la_ref, v_ref, gg_ref, gn_ref, s_ref, *rest, steps, layer):
    o_ref, so_ref = rest[-2:]
    so_ref = _layer_view(so_ref, layer, len(rest) == 2)
    grp = s_ref.shape[0]
    rows = grp * steps
    stacked = GLA_HEADS * rows
    ri = lax.broadcasted_iota(jnp.int32, (rows, rows), 0)
    ci = lax.broadcasted_iota(jnp.int32, (rows, rows), 1)
    same_seq = ri // steps == ci // steps
    g_hi, g_lo = _split_bf16(la_ref[...])
    tril = (same_seq & (ri >= ci)).astype(BF16)
    total = same_seq.astype(BF16)
    b = _dot(tril, g_hi) + _dot(tril, g_lo)
    b_last = _dot(total, g_hi) + _dot(total, g_lo)
    head_of_lane = lax.broadcasted_iota(jnp.int32, (1, GLA_K_W), 1) // GLA_DK
    k_t = k_ref[...] * jnp.exp(-b)
    qm = _head_masked_stack(q_ref[...] * jnp.exp(b), head_of_lane).astype(BF16)
    km = _head_masked_stack(k_t, head_of_lane).astype(BF16)
    kdm = _head_masked_stack(k_t * jnp.exp(b_last), head_of_lane).astype(BF16)
    v_st = _head_stack(v_ref[...], GLA_DV)
    seq_of_row = (lax.broadcasted_iota(jnp.int32, (stacked, 1), 0) % rows) // steps
    seq_of_g = lax.broadcasted_iota(jnp.int32, (rows, 1), 0) // steps
    zero = jnp.zeros((), BF16)
    rhs = jnp.concatenate([
        jnp.concatenate([v_st, jnp.zeros((stacked, GLA_DV), BF16)], axis=1),
        jnp.concatenate([jnp.zeros((2 * rows, GLA_DV), BF16), jnp.ones((2 * rows, GLA_DV), BF16)], axis=1),
    ], axis=0)
    a_raw = _dot_tb(qm, km)
    states = [s_ref[s].reshape(GLA_K_W, GLA_DV) for s in range(grp)]
    inter_all = [_dot(qm, st.astype(BF16)) for st in states]
    upd_all = []
    for s in range(grp):
        lhs = jnp.concatenate([jnp.where(seq_of_row == s, kdm, zero),
                               jnp.where(seq_of_g == s, g_hi, zero),
                               jnp.where(seq_of_g == s, g_lo, zero)], axis=0)
        upd_all.append(_dot_ta(lhs, rhs))
    rr = lax.broadcasted_iota(jnp.int32, (stacked, stacked), 0) % rows
    cc = lax.broadcasted_iota(jnp.int32, (stacked, stacked), 1) % rows
    causal = (rr // steps == cc // steps) & (rr >= cc)
    o = _dot(jnp.where(causal, a_raw, 0.0).astype(BF16), v_st)
    inter = inter_all[0]
    for s in range(1, grp):
        inter = jnp.where(seq_of_row == s, inter_all[s], inter)
    y = _gla_out(o + inter, gn_ref[...], _head_stack(gg_ref[...], GLA_DV)).astype(BF16)
    for h in range(GLA_HEADS):
        o_ref[:, h * GLA_DV:(h + 1) * GLA_DV] = y[h * rows:(h + 1) * rows, :]
    for s in range(grp):
        new = jnp.exp(upd_all[s][:, GLA_DV:]) * states[s] + upd_all[s][:, :GLA_DV]
        so_ref[s] = new.reshape(GLA_HEADS, GLA_DK, GLA_DV)


def _gla_decode(layer, q, k, la, v, gg, gain, state, steps, prev):
    nseq = state.shape[1]
    grp = min(GLA_DEC_GROUP, nseq)
    rows = grp * steps
    row = lambda i: (i, 0)
    st_spec = pl.BlockSpec((None, grp, GLA_HEADS, GLA_DK, GLA_DV), lambda i: (layer, i, 0, 0, 0))
    extra_in, extra_specs, out_spec, stacked = _stacked_out(
        state.shape, (grp, GLA_HEADS, GLA_DK, GLA_DV), layer, prev)
    n_in = 7
    return pl.pallas_call(
        functools.partial(_gla_decode_kernel, steps=steps, layer=layer),
        grid=(nseq // grp,),
        in_specs=[
            pl.BlockSpec((rows, GLA_K_W), row),
            pl.BlockSpec((rows, GLA_K_W), row),
            pl.BlockSpec((rows, GLA_K_W), row),
            pl.BlockSpec((rows, GLA_V_W), row),
            pl.BlockSpec((rows, GLA_V_W), row),
            _layer_resident(gain, layer),
            st_spec,
        ] + extra_specs,
        out_specs=[pl.BlockSpec((rows, GLA_V_W), row), out_spec],
        out_shape=[jax.ShapeDtypeStruct((nseq * steps, GLA_V_W), BF16), stacked],
        input_output_aliases={n_in + i: 1 + i for i in range(len(extra_in))},
        compiler_params=_params("parallel"),
        name="gla_decode",
    )(q, k, la, v, gg, gain, state, *extra_in)


def _outproj_apply(x, a_ref, o_ref, wa_ref, wo_ref, g_ref, wq_ref, qn_ref):
    x1 = x + _dot(a_ref[...], wa_ref[...]) + _dot(o_ref[...], wo_ref[...])
    h = _rms(x1, g_ref[...]).astype(BF16)
    qn = qn_ref[...]
    q = [_rms(_dot(h, wq_ref[:, hd * XA_HEAD_DIM:(hd + 1) * XA_HEAD_DIM]), qn).astype(BF16)
         for hd in range(XA_HEADS)]
    return x1, q


def _outproj_kernel(x_ref, a_ref, o_ref, wa_ref, wo_ref, g_ref, wq_ref, qn_ref, x1_ref, q_ref):
    x1, q = _outproj_apply(x_ref[...], a_ref, o_ref, wa_ref, wo_ref, g_ref, wq_ref, qn_ref)
    x1_ref[...] = x1
    for hd in range(XA_HEADS):
        q_ref[:, hd * XA_HEAD_DIM:(hd + 1) * XA_HEAD_DIM] = q[hd]


def _outproj(x, layer, a, o, w_a, w_o, g, wq, qn):
    n, d = x.shape
    tm = min(ROW_TILE, n)
    row = lambda i: (i, 0)
    return pl.pallas_call(
        _outproj_kernel,
        grid=(n // tm,),
        in_specs=[
            pl.BlockSpec((tm, d), row),
            pl.BlockSpec((tm, SWA_Q_W), row),
            pl.BlockSpec((tm, GLA_V_W), row),
        ] + [_layer_resident(p, layer) for p in (w_a, w_o, g, wq, qn)],
        out_specs=[pl.BlockSpec((tm, d), row), pl.BlockSpec((tm, XA_W), row)],
        out_shape=[jax.ShapeDtypeStruct((n, d), F32), jax.ShapeDtypeStruct((n, XA_W), BF16)],
        compiler_params=_params("parallel"),
        name="outproj",
    )(x, a, o, w_a, w_o, g, wq, qn)


def _memkv_kernel(m_ref, g_ref, wk_ref, wv_ref, kn_ref, k_ref, v_ref):
    m = _rms(m_ref[...], g_ref[...]).astype(BF16)
    kn = kn_ref[...]
    for hd in range(XA_HEADS):
        sl = slice(hd * XA_HEAD_DIM, (hd + 1) * XA_HEAD_DIM)
        k_ref[:, sl] = _rms(_dot(m, wk_ref[:, sl]), kn)
    v_ref[...] = _dot(m, wv_ref[...])


def _memkv(mem, g, wk, wv, kn):
    depth = wk.shape[0]
    n, d = mem.shape
    tm = min(ROW_TILE, n)
    per_layer = lambda l, i: (l, 0, 0)
    out = lambda l, i: (l, i, 0)
    return pl.pallas_call(
        _memkv_kernel,
        grid=(depth, n // tm),
        in_specs=[
            pl.BlockSpec((tm, d), lambda l, i: (i, 0)),
            pl.BlockSpec((None, 1, d), per_layer),
            pl.BlockSpec((None, d, XA_W), per_layer),
            pl.BlockSpec((None, d, XA_W), per_layer),
            pl.BlockSpec((None, 1, XA_HEAD_DIM), per_layer),
        ],
        out_specs=[pl.BlockSpec((None, tm, XA_W), out), pl.BlockSpec((None, tm, XA_W), out)],
        out_shape=[jax.ShapeDtypeStruct((depth, n, XA_W), F32)] * 2,
        compiler_params=_params("parallel", "parallel"),
        name="memkv",
    )(mem, g, wk, wv, kn)


def _mixout_prompt_kernel(*refs, n_cast):
    (x_ref, a_ref, o_ref, wa_ref, wo_ref, g_ref, wq_ref, qn_ref, mk_ref, mv_ref, xwo_ref, fg_ref, wg_ref, wu_ref,
     wd_ref) = refs[:15]
    out_ref, att_scr, a_scr = refs[15 + n_cast], refs[-2], refs[-1]
    _cast_chunks(refs[15:15 + n_cast], refs[16 + n_cast:-2])
    x1, q = _outproj_apply(x_ref[...], a_ref, o_ref, wa_ref, wo_ref, g_ref, wq_ref, qn_ref)
    mk = mk_ref[...].astype(BF16)
    mv = mv_ref[...].astype(BF16)
    heads = range(XA_HEADS)
    cols = lambda hd: slice(hd * XA_HEAD_DIM, (hd + 1) * XA_HEAD_DIM)
    scores = [_dot_tb(q[hd], mk[:, cols(hd)]) * (XA_HEAD_DIM ** -0.5 * LOG2E) for hd in heads]
    probs = []
    for s in scores:
        e = jnp.exp2(s - jnp.max(s, axis=-1, keepdims=True))
        probs.append((e.astype(BF16), jnp.sum(e, axis=-1, keepdims=True)))
    for hd in heads:
        att_scr[:, cols(hd)] = (_dot(probs[hd][0], mv[:, cols(hd)]) / probs[hd][1]).astype(BF16)
    x2 = x1 + _dot(att_scr[...], xwo_ref[...])
    out_ref[...] = _ffn_apply(x2, fg_ref, wg_ref, wu_ref, wd_ref, a_scr)


def _mixout_prompt(x, layer, a, o, outproj, mk, mv, wo, ffn, seq, cast=()):
    n, d = x.shape
    dff = ffn[1].shape[2]
    tm = min(ROW_TILE, seq)
    per_seq = seq // tm
    row = lambda i: (i, 0)
    mem_spec = pl.BlockSpec((None, None, mk.shape[2], XA_W), lambda i: (layer, i // per_seq, 0, 0))
    cast_in, cast_out, cast_shape = _cast_specs(cast, n // tm)
    return pl.pallas_call(
        functools.partial(_mixout_prompt_kernel, n_cast=len(cast)),
        grid=(n // tm,),
        in_specs=([pl.BlockSpec((tm, d), row), pl.BlockSpec((tm, SWA_Q_W), row), pl.BlockSpec((tm, GLA_V_W), row)]
                  + [_layer_resident(p, layer) for p in outproj]
                  + [mem_spec, mem_spec]
                  + [_layer_resident(p, layer) for p in (wo,) + ffn]
                  + cast_in),
        out_specs=[pl.BlockSpec((tm, d), row)] + cast_out,
        out_shape=[jax.ShapeDtypeStruct((n, d), F32)] + cast_shape,
        scratch_shapes=[pltpu.VMEM((tm, XA_W), BF16), pltpu.VMEM((tm, dff), BF16)],
        compiler_params=_params("parallel"),
        name="mixout_prompt",
    )(x, a, o, *outproj, mk, mv, wo, *ffn, *[arr for arr, _ in cast])


def _xattn_decode_kernel(q_ref, mk_ref, mv_ref, o_ref, *, steps):
    grp, nkeys = mk_ref.shape[0], mk_ref.shape[1]
    rows = grp * steps
    q = jnp.concatenate([q_ref[:, hd * XA_HEAD_DIM:(hd + 1) * XA_HEAD_DIM] for hd in range(XA_HEADS)], axis=0)
    r = lax.broadcasted_iota(jnp.int32, (XA_HEADS * rows, 1), 0)
    own = (r % rows) // steps
    same_head = (r // rows) == (lax.broadcasted_iota(jnp.int32, (1, nkeys), 1) % XA_HEADS)
    s = None
    for j in range(grp):
        sj = _dot_tb(q, mk_ref[j].astype(BF16))
        s = sj if s is None else jnp.where(own == j, sj, s)
    s = jnp.where(same_head, s * (XA_HEAD_DIM ** -0.5 * LOG2E), -jnp.inf)
    m = jnp.max(s, axis=-1, keepdims=True)
    e = jnp.exp2(s - m)
    p = e.astype(BF16)
    o = None
    for j in range(grp):
        oj = _dot(p, mv_ref[j].astype(BF16))
        o = oj if o is None else jnp.where(own == j, oj, o)
    o = o / jnp.sum(e, axis=-1, keepdims=True)
    for hd in range(XA_HEADS):
        o_ref[:, hd * XA_HEAD_DIM:(hd + 1) * XA_HEAD_DIM] = o[hd * rows:(hd + 1) * rows, :].astype(BF16)


def _xattn_decode(q, layer, mk, mv, steps):
    nseq, nkeys = mk.shape[1], mk.shape[2]
    grp = XA_DEC_GROUP
    rows = grp * steps
    row = lambda i: (i, 0)
    mem_spec = pl.BlockSpec((None, grp, nkeys, XA_HEAD_DIM), lambda i: (layer, i, 0, 0))
    return pl.pallas_call(
        functools.partial(_xattn_decode_kernel, steps=steps),
        grid=(nseq // grp,),
        in_specs=[pl.BlockSpec((rows, XA_W), row), mem_spec, mem_spec],
        out_specs=pl.BlockSpec((rows, XA_W), row),
        out_shape=jax.ShapeDtypeStruct((nseq * steps, XA_W), BF16),
        compiler_params=_params("parallel"),
        name="xattn_decode",
    )(q, mk, mv)


def _proj_res_kernel(x_ref, a_ref, w_ref, o_ref):
    o_ref[...] = x_ref[...] + _dot(a_ref[...], w_ref[...])


def _proj_res(x, layer, a, w):
    n, d = x.shape
    tm = min(ROW_TILE, n)
    row = lambda i: (i, 0)
    return pl.pallas_call(
        _proj_res_kernel,
        grid=(n // tm,),
        in_specs=[pl.BlockSpec((tm, d), row), pl.BlockSpec((tm, a.shape[1]), row), _layer_resident(w, layer)],
        out_specs=pl.BlockSpec((tm, d), row),
        out_shape=jax.ShapeDtypeStruct((n, d), F32),
        compiler_params=_params("parallel"),
        name="proj_res",
    )(x, a, w)


def _rope_tables(pos):
    half = HEAD_DIM // 2
    inv = ROPE_THETA ** (-jnp.arange(half, dtype=F32) / half)
    ang = pos.astype(F32)[:, None] * inv[None, :]
    cos, sin = jnp.cos(ang), jnp.sin(ang)
    reps = LANES // HEAD_DIM
    return jnp.tile(cos, (1, 2 * reps)), jnp.tile(jnp.concatenate([-sin, sin], axis=-1), (1, reps))


def _block_tril(n_blocks, size):
    i = jnp.arange(n_blocks * size)
    return ((i[:, None] // size == i[None, :] // size) & (i[:, None] >= i[None, :])).astype(BF16)


def _permute_heads(w, axis):
    blocks = jnp.split(w, SWA_Q_HEADS, axis=axis)
    return jnp.concatenate([blocks[h] for h in SWA_HEAD_ORDER], axis=axis)


def kernel(x_prompt, x_sample, cache_swa_k, cache_swa_v, state_gla, cache_mem_k, cache_mem_v, mem_prompt, ffn1_norm, ffn1_wg, ffn1_wu, ffn1_wd, mix_norm, w_in, swa_q_norm, swa_k_norm, swa_sinks, gla_w_gate, gla_b_gate, gla_out_norm, w_out, xa_norm, mem_norm, xa_wq, xa_wk, xa_wv, xa_q_norm, xa_k_norm, xa_wo, ffn2_norm, ffn2_wg, ffn2_wu, ffn2_wd):
    batch, seq, d = x_prompt.shape
    nseq, steps, _ = x_sample.shape
    depth = w_in.shape[0]
    mem_len = mem_prompt.shape[1]

    bf = lambda w: w.astype(BF16)
    vec = lambda p: p[:, None, :]
    ffn1_w = [tuple(bf(w[:1]) for w in (ffn1_wg, ffn1_wu, ffn1_wd))] + [None] * (depth - 1)
    w_all = jnp.pad(bf(w_in), ((0, 0), (0, 0), (0, MXU_TILE - GLA_LOWRANK)))
    gate_w = jnp.pad(bf(gla_w_gate), ((0, 0), (0, LANES - GLA_LOWRANK), (0, 0)))
    w_a = bf(_permute_heads(w_out[:, :SWA_Q_W], 1))
    w_o = bf(w_out[:, SWA_Q_W:])
    wq_b, wk_b, wv_b, wo_b = bf(xa_wq), bf(xa_wk), bf(xa_wv), bf(xa_wo)
    qn = vec(jnp.tile(swa_q_norm * (HEAD_DIM ** -0.5 * LOG2E), (1, LANES // HEAD_DIM)))
    kn = vec(jnp.tile(swa_k_norm, (1, LANES // HEAD_DIM)))
    inproj_params = (vec(mix_norm), w_all, gate_w, vec(gla_b_gate), qn, kn)
    outproj_params = (w_a, w_o, vec(xa_norm), wq_b, vec(xa_q_norm))
    gla_gain = vec(gla_out_norm)

    lane = jnp.arange(MXU_TILE)
    bd = (lane[:, None] // HEAD_DIM == lane[None, :] // HEAD_DIM).astype(BF16)
    cos_p, sin_p = _rope_tables(jnp.arange(seq))
    cos_s, sin_s = _rope_tables(PAST_LEN + jnp.arange(nseq * steps) % steps)
    tril_p = _block_tril(min(GLA_BLOCK, seq) // min(GLA_CHUNK, seq), min(GLA_CHUNK, seq))

    mk_p, mv_p = _memkv(mem_prompt.reshape(batch * mem_len, d), vec(mem_norm), wk_b, wv_b, vec(xa_k_norm))
    mk_p = mk_p.reshape(depth, batch, mem_len, XA_W)
    mv_p = mv_p.reshape(depth, batch, mem_len, XA_W)
    mk_s = cache_mem_k.reshape(depth, nseq, mem_len * XA_HEADS, XA_HEAD_DIM)
    mv_s = cache_mem_v.reshape(depth, nseq, mem_len * XA_HEADS, XA_HEAD_DIM)
    native = lambda c: jnp.transpose(c, (0, 1, 3, 4, 2)).reshape(depth, nseq, SWA_KV_W, WINDOW)
    kc_s, vc_s = native(cache_swa_k), native(cache_swa_v)

    xp = x_prompt.reshape(batch * seq, d)
    xs = x_sample.reshape(nseq * steps, d)
    kp_l, vp_l, sp_l = [], [], []
    swa_new = gla_new = None
    for l in range(depth):
        ffn1 = (vec(ffn1_norm),) + tuple(ffn1_w[l])
        xp, q_s, k_s, v_s, q_g, k_g, v_g, g_g, la, *ffn2_w = _ffn_inproj(
            xp, l, ffn1, inproj_params, cos_p, sin_p, bd, cast=[(w, l) for w in (ffn2_wg, ffn2_wu, ffn2_wd)])
        ffn2 = (vec(ffn2_norm),) + tuple(ffn2_w)
        a_p = _swa_prompt(swa_sinks, l, q_s, k_s, v_s, batch, seq)
        seqs = lambda a: a.reshape(batch, seq, a.shape[-1])
        o_p, s_p = _gla_prompt(l, seqs(q_g), seqs(k_g), seqs(la), seqs(v_g), seqs(g_g), gla_gain, tril_p,
                               batch, seq)
        o_p = o_p.reshape(batch * seq, GLA_V_W)
        last = lambda a: seqs(a)[:, seq - WINDOW:].reshape(batch, WINDOW, SWA_KV_HEADS, HEAD_DIM)
        kp_l.append(last(k_s))
        vp_l.append(last(v_s))
        sp_l.append(s_p.reshape(batch, GLA_HEADS, GLA_DK, GLA_DV))
        next_ffn1 = [(w, l + 1) for w in (ffn1_wg, ffn1_wu, ffn1_wd)] if l + 1 < depth else []
        xp, *cast = _mixout_prompt(xp, l, a_p, o_p, outproj_params, mk_p, mv_p, wo_b, ffn2, seq, cast=next_ffn1)
        if next_ffn1:
            ffn1_w[l + 1] = cast

        xs, q_s, k_s, v_s, q_g, k_g, v_g, g_g, la, k_t, v_t = _ffn_inproj(
            xs, l, ffn1, inproj_params, cos_s, sin_s, bd, transposed_kv=True)
        a_s, *swa_new = _swa_decode(swa_sinks, l, q_s, k_s, v_s, k_t, v_t, kc_s, vc_s, steps, swa_new)
        o_s, *gla_new = _gla_decode(l, q_g, k_g, la, v_g, g_g, gla_gain, state_gla, steps, gla_new)
        xs, q_x = _outproj(xs, l, a_s, o_s, *outproj_params)
        xs = _proj_res(xs, l, _xattn_decode(q_x, l, mk_s, mv_s, steps), wo_b)
        xs = _ffn(xs, l, *ffn2)

    unnative = lambda c: jnp.transpose(c.reshape(depth, nseq, SWA_KV_HEADS, HEAD_DIM, WINDOW), (0, 1, 4, 2, 3))
    return (xp.reshape(batch, seq, d), xs.reshape(nseq, steps, d),
            jnp.stack(kp_l), jnp.stack(vp_l), jnp.stack(sp_l),
            mk_p.reshape(depth, batch, mem_len, XA_HEADS, XA_HEAD_DIM),
            mv_p.reshape(depth, batch, mem_len, XA_HEADS, XA_HEAD_DIM),
            unnative(swa_new[0]), unnative(swa_new[1]), gla_new[0])
```

```python
import functools

import jax
import jax.numpy as jnp
from jax import lax
from jax.experimental import pallas as pl
from jax.experimental.pallas import tpu as pltpu

F32 = jnp.float32
BF16 = jnp.bfloat16

EPS = 1e-6
LOG2E = 1.4426950408889634
PAST_LEN = 16384
WINDOW = 128
ROPE_THETA = 10000.0
HEAD_DIM = 64
SWA_Q_HEADS = 8
SWA_KV_HEADS = 2
SWA_GROUP = SWA_Q_HEADS // SWA_KV_HEADS
GLA_HEADS = 4
GLA_DK = 64
GLA_DV = 128
GLA_LOWRANK = 16
GLA_GATE_TEMP = 16.0
GLA_CHUNK = 64
XA_HEADS = 4
XA_HEAD_DIM = 128

SWA_Q_W = SWA_Q_HEADS * HEAD_DIM
SWA_KV_W = SWA_KV_HEADS * HEAD_DIM
GLA_K_W = GLA_HEADS * GLA_DK
GLA_V_W = GLA_HEADS * GLA_DV
XA_W = XA_HEADS * XA_HEAD_DIM
MAIN_W = SWA_Q_W + 2 * SWA_KV_W + 2 * GLA_K_W + 2 * GLA_V_W

LANES = 128
BF16_SUBLANES = 16
MXU_TILE = 256
VMEM_LIMIT = 56 * 1024 * 1024

ROW_TILE = 512
ROW_SPLIT = 2
FFN_CHUNK = 256
SWA_BLOCK = 256
GLA_BLOCK = 256
GLA_SEQS = 4
DEC_GROUP = 4
SWA_DEC_SUBGROUPS = 8
XA_DEC_GROUP = 8
GLA_DEC_GROUP = 8

assert SWA_KV_HEADS * HEAD_DIM == LANES
SWA_HEAD_ORDER = tuple(kv * SWA_GROUP + g for g in range(SWA_GROUP) for kv in range(SWA_KV_HEADS))


def _dot(a, b):
    return jnp.dot(a, b, preferred_element_type=F32)


def _dot_tb(a, b):
    return lax.dot_general(a, b, (((1,), (1,)), ((), ())), preferred_element_type=F32)


def _dot_ta(a, b):
    return lax.dot_general(a, b, (((0,), (0,)), ((), ())), preferred_element_type=F32)


def _split_bf16(x):
    hi = x.astype(BF16)
    lo = (x - hi.astype(F32)).astype(BF16)
    return hi, lo


def _rms(x, g):
    ms = jnp.mean(x * x, axis=-1, keepdims=True)
    return x * lax.rsqrt(ms + EPS) * g


def _params(*sem):
    return pltpu.CompilerParams(dimension_semantics=sem, vmem_limit_bytes=VMEM_LIMIT)


def _resident(arr):
    nd = arr.ndim
    return pl.BlockSpec(arr.shape, lambda *_: (0,) * nd, pipeline_mode=pl.Buffered(1))


def _layer_resident(arr, layer):
    nd = arr.ndim
    index = (layer if arr.shape[0] > 1 else 0,) + (0,) * (nd - 1)
    return pl.BlockSpec((None,) + arr.shape[1:], lambda *_: index, pipeline_mode=pl.Buffered(1))


def _cast_specs(cast, n_steps, step_of=lambda i: i):
    ins, outs, shapes = [], [], []
    for arr, src_layer, *padded in cast:
        _, rows, cols = arr.shape
        out_cols = padded[0] if padded else cols
        share = 1
        while rows % (n_steps // share) or (rows * share // n_steps) % BF16_SUBLANES:
            share *= 2
            assert n_steps % share == 0
        chunk = rows * share // n_steps
        ins.append(pl.BlockSpec((None, chunk, cols), lambda *g, l=src_layer, s=share: (l, step_of(*g) // s, 0)))
        outs.append(pl.BlockSpec((1, chunk, out_cols), lambda *g, s=share: (0, step_of(*g) // s, 0)))
        shapes.append(jax.ShapeDtypeStruct((1, rows, out_cols), BF16))
    return ins, outs, shapes


def _cast_chunks(srcs, dsts):
    for src, dst in zip(srcs, dsts):
        cols = src.shape[-1]
        dst[0, :, :cols] = src[...].astype(BF16)
        if dst.shape[-1] > cols:
            dst[0, :, cols:] = jnp.zeros((dst.shape[1], dst.shape[-1] - cols), BF16)


def _ffn_hidden(x, g_ref, wg_ref, wu_ref, a_scr):
    h = _rms(x, g_ref[...]).astype(BF16)
    dff = wg_ref.shape[1]
    for c in range(dff // FFN_CHUNK):
        sl = slice(c * FFN_CHUNK, (c + 1) * FFN_CHUNK)
        g = _dot(h, wg_ref[:, sl])
        u = _dot(h, wu_ref[:, sl])
        a_scr[:, sl] = (g * jax.nn.sigmoid(g) * u).astype(BF16)


def _ffn_apply(x, g_ref, wg_ref, wu_ref, wd_ref, a_scr):
    _ffn_hidden(x, g_ref, wg_ref, wu_ref, a_scr)
    return x + 0.5 * _dot(a_scr[...], wd_ref[...])


def _ffn_kernel(x_ref, g_ref, wg_ref, wu_ref, wd_ref, o_ref, a_scr):
    o_ref[...] = _ffn_apply(x_ref[...], g_ref, wg_ref, wu_ref, wd_ref, a_scr)


def _ffn(x, layer, g, wg, wu, wd):
    n, d = x.shape
    dff = wg.shape[2]
    tm = min(ROW_TILE, n)
    return pl.pallas_call(
        _ffn_kernel,
        grid=(n // tm,),
        in_specs=[pl.BlockSpec((tm, d), lambda i: (i, 0))] + [_layer_resident(a, layer) for a in (g, wg, wu, wd)],
        out_specs=pl.BlockSpec((tm, d), lambda i: (i, 0)),
        out_shape=jax.ShapeDtypeStruct((n, d), F32),
        scratch_shapes=[pltpu.VMEM((tm, dff), BF16)],
        compiler_params=_params("parallel"),
        name="ffn",
    )(x, g, wg, wu, wd)


def _ffn_inproj_kernel(*refs, n_cast, transposed_kv):
    (x_ref, fg_ref, wg_ref, wu_ref, wd_ref, g_ref, w_ref, gw_ref, gb_ref, qn_ref, kn_ref, cos_ref, sin_ref,
     bd_ref) = refs[:14]
    outs, a_scr = refs[14 + n_cast:-1], refs[-1]
    xo_ref, qs_ref, ks_ref, vs_ref, qg_ref, kg_ref, vg_ref, gg_ref, la_ref = outs[:9]
    kt_ref, vt_ref = outs[9:11] if transposed_kv else (None, None)
    _cast_chunks(refs[14:14 + n_cast], outs[len(outs) - n_cast:])
    x = x_ref[...]
    _ffn_hidden(x, fg_ref, wg_ref, wu_ref, a_scr)
    n = x.shape[0] // ROW_SPLIT
    groups = [slice(r * n, (r + 1) * n) for r in range(ROW_SPLIT)]
    mid = [x[rs] + 0.5 * _dot(a_scr[rs, :], wd_ref[...]) for rs in groups]
    for rs, xm in zip(groups, mid):
        xo_ref[rs, :] = xm
    split = SWA_Q_W + 2 * SWA_KV_W + 2 * GLA_K_W
    h = [_rms(xm, g_ref[...]).astype(BF16) for xm in mid]
    z1 = [_dot(hr, w_ref[:, :split]) for hr in h]
    z2 = []
    for rs, hr, z in zip(groups, h, z1):
        _inproj_finish_swa(z, rs, qn_ref, kn_ref, cos_ref, sin_ref, bd_ref, qs_ref, ks_ref, vs_ref, qg_ref,
                           kg_ref, kt_ref, vt_ref)
        z2.append(_dot(hr, w_ref[:, split:]))
    for rs, z in zip(groups, z2):
        _inproj_finish_gla(z, rs, gw_ref, gb_ref, vg_ref, gg_ref, la_ref)


def _inproj_finish_swa(z1, rs, qn_ref, kn_ref, cos_ref, sin_ref, bd_ref, qs_ref, ks_ref, vs_ref, qg_ref,
                       kg_ref, kt_ref, vt_ref):
    bd = bd_ref[...]
    cos = cos_ref[rs, :]
    sin = sin_ref[rs, :]
    lane = lax.broadcasted_iota(jnp.int32, cos.shape, 1)
    lane_lo = (lane % HEAD_DIM) < (HEAD_DIM // 2)

    def head_scale(z):
        ss = _dot((z * z).astype(BF16), bd)
        return lax.rsqrt(ss * (1.0 / HEAD_DIM) + EPS)

    def rope(y):
        swapped = jnp.where(lane_lo, pltpu.roll(y, LANES - HEAD_DIM // 2, axis=1),
                            pltpu.roll(y, HEAD_DIM // 2, axis=1))
        return y * cos + swapped * sin

    qn = qn_ref[...]
    nat = []
    for t in range(SWA_Q_W // MXU_TILE):
        z = z1[:, t * MXU_TILE:(t + 1) * MXU_TILE]
        y = z * head_scale(z)
        nat += [rope(y[:, c * LANES:(c + 1) * LANES] * qn) for c in range(MXU_TILE // LANES)]
    low = _low_half()
    per_col = LANES // HEAD_DIM
    for c in range(SWA_Q_W // LANES):
        halves = []
        for half, head in enumerate(SWA_HEAD_ORDER[per_col * c:per_col * (c + 1)]):
            col = nat[head // per_col]
            halves.append(col if head % per_col == half else pltpu.roll(col, HEAD_DIM, axis=1))
        qs_ref[rs, c * LANES:(c + 1) * LANES] = jnp.where(low, halves[0], halves[1]).astype(BF16)
    o = SWA_Q_W
    z = z1[:, o:o + 2 * SWA_KV_W]
    k = rope(z[:, :SWA_KV_W] * head_scale(z)[:, :SWA_KV_W] * kn_ref[...])
    ks_ref[rs, :] = k
    vs_ref[rs, :] = z[:, SWA_KV_W:]
    if kt_ref is not None:
        kt_ref[:, rs] = k.T
        vt_ref[:, rs] = z[:, SWA_KV_W:].T
    o += 2 * SWA_KV_W
    qg_ref[rs, :] = z1[:, o:o + GLA_K_W] * (GLA_DK ** -0.5)
    o += GLA_K_W
    kg_ref[rs, :] = z1[:, o:o + GLA_K_W]


def _inproj_finish_gla(z2, rs, gw_ref, gb_ref, vg_ref, gg_ref, la_ref):
    vg_ref[rs, :] = z2[:, :GLA_V_W].astype(BF16)
    gg_ref[rs, :] = z2[:, GLA_V_W:2 * GLA_V_W]
    lr = z2[:, 2 * GLA_V_W:2 * GLA_V_W + LANES].astype(BF16)
    t = _dot(lr, gw_ref[...]) + gb_ref[...]
    log_sig = jnp.minimum(t, 0.0) - jnp.log(1.0 + jnp.exp(-jnp.abs(t)))
    la_ref[rs, :] = log_sig * (1.0 / GLA_GATE_TEMP)


def _ffn_inproj(x, layer, ffn, inproj, cos, sin, bd, transposed_kv=False, cast=()):
    n, d = x.shape
    dff = ffn[1].shape[2]
    tm = min(ROW_TILE, n)
    pos_blocks = cos.shape[0] // tm
    row = lambda i: (i, 0)
    pos = lambda i: (i % pos_blocks, 0)
    widths = (d, SWA_Q_W, SWA_KV_W, SWA_KV_W, GLA_K_W, GLA_K_W, GLA_V_W, GLA_V_W, GLA_K_W)
    dtypes = (F32, BF16, F32, F32, F32, F32, BF16, F32, F32)
    out_specs = [pl.BlockSpec((tm, w), row) for w in widths]
    out_shape = [jax.ShapeDtypeStruct((n, w), dt) for w, dt in zip(widths, dtypes)]
    if transposed_kv:
        out_specs += [pl.BlockSpec((SWA_KV_W, tm), lambda i: (0, i))] * 2
        out_shape += [jax.ShapeDtypeStruct((SWA_KV_W, n), F32)] * 2
    cast_in, cast_out, cast_shape = _cast_specs(cast, n // tm)
    return pl.pallas_call(
        functools.partial(_ffn_inproj_kernel, n_cast=len(cast), transposed_kv=transposed_kv),
        grid=(n // tm,),
        in_specs=([pl.BlockSpec((tm, d), row)]
                  + [_layer_resident(a, layer) for a in ffn + inproj]
                  + [pl.BlockSpec((tm, LANES), pos), pl.BlockSpec((tm, LANES), pos), _resident(bd)]
                  + cast_in),
        out_specs=out_specs + cast_out,
        out_shape=out_shape + cast_shape,
        scratch_shapes=[pltpu.VMEM((tm, dff), BF16)],
        compiler_params=_params("parallel"),
        name="ffn_inproj",
    )(x, *ffn, *inproj, cos, sin, bd, *[c[0] for c in cast])


def _swa_attend(q, k, v, valid, sinks):
    return _swa_out(*_swa_probs(_swa_scores(q, k), valid, sinks), v)


def _low_half():
    return lax.broadcasted_iota(jnp.int32, (1, LANES), 1) < HEAD_DIM


def _swa_head_rows(q):
    low = _low_half()
    zero = jnp.zeros((), q.dtype)
    pieces = []
    for c in range(SWA_Q_W // LANES):
        qc = q[:, c * LANES:(c + 1) * LANES]
        pieces += [jnp.where(low, qc, zero), jnp.where(low, zero, qc)]
    return jnp.concatenate(pieces, axis=0)


def _swa_scores(q, k):
    return _dot_tb(_swa_head_rows(q), k)


def _swa_probs(s, valid, sinks):
    rows, keys = valid.shape
    s = jnp.where(valid[None], s.reshape(SWA_Q_HEADS, rows, keys), -jnp.inf)
    m = jnp.maximum(jnp.max(s, axis=-1, keepdims=True), sinks)
    e = jnp.exp2(s - m)
    den = jnp.sum(e, axis=-1, keepdims=True) + jnp.exp2(sinks - m)
    return e.astype(BF16).reshape(SWA_Q_HEADS * rows, keys), den


def _swa_out(e, den, v):
    return _swa_finish(_dot(e, v), den)


def _swa_finish(o, den):
    low = _low_half()
    o = o.reshape(den.shape[0], den.shape[1], LANES) / den
    return [jnp.where(low, o[2 * c], o[2 * c + 1]) for c in range(SWA_Q_W // LANES)]


def _sink_column(sink_ref, layer):
    idx = lax.broadcasted_iota(jnp.int32, (SWA_Q_HEADS, 1, 1), 0)
    col = jnp.zeros((SWA_Q_HEADS, 1, 1), F32)
    for p, head in enumerate(SWA_HEAD_ORDER):
        col = jnp.where(idx == p, sink_ref[layer, head] * LOG2E, col)
    return col


def _swa_prompt_kernel(sink_ref, q_ref, kp_ref, kc_ref, vp_ref, vc_ref, o_ref, *, layer):
    j = pl.program_id(1)
    kall = jnp.concatenate([kp_ref[...], kc_ref[...]], axis=0).astype(BF16)
    vall = jnp.concatenate([vp_ref[...], vc_ref[...]], axis=0).astype(BF16)
    sinks = _sink_column(sink_ref, layer)
    upper = (lax.broadcasted_iota(jnp.int32, (WINDOW, WINDOW), 1)
             > lax.broadcasted_iota(jnp.int32, (WINDOW, WINDOW), 0))[None]

    def banded_probs(s, has_prev):
        s = s.reshape(SWA_Q_HEADS, WINDOW, 2 * WINDOW)
        prev = s[:, :, :WINDOW]
        if has_prev is not True:
            prev = jnp.where(has_prev, prev, -jnp.inf)
        logits = jnp.where(upper, prev, s[:, :, WINDOW:])
        m = jnp.maximum(jnp.max(logits, axis=-1, keepdims=True), sinks)
        e = jnp.exp2(logits - m)
        den = jnp.sum(e, axis=-1, keepdims=True) + jnp.exp2(sinks - m)
        e = e.astype(BF16)
        zero = jnp.zeros((), BF16)
        e = jnp.concatenate([jnp.where(upper, e, zero), jnp.where(upper, zero, e)], axis=-1)
        return e.reshape(SWA_Q_HEADS * WINDOW, 2 * WINDOW), den

    windows = range(q_ref.shape[0] // WINDOW)
    rows = lambda w: slice(w * WINDOW, (w + 1) * WINDOW)
    keys = lambda w: slice(w * WINDOW, (w + 2) * WINDOW)
    scores = [_swa_scores(q_ref[rows(w), :], kall[keys(w)]) for w in windows]
    probs = [banded_probs(scores[w], True if w > 0 else j > 0) for w in windows]
    for w in windows:
        for c, col in enumerate(_swa_out(*probs[w], vall[keys(w)])):
            o_ref[rows(w), c * LANES:(c + 1) * LANES] = col.astype(BF16)


def _swa_prompt(sinks, layer, q, k, v, batch, seq):
    qb = min(SWA_BLOCK, seq)
    nb = seq // qb
    per = qb // WINDOW
    cur = lambda b, j: (b * nb + j, 0)
    prev = lambda b, j: ((b * nb + j) * per - jnp.minimum(j, 1), 0)
    return pl.pallas_call(
        functools.partial(_swa_prompt_kernel, layer=layer),
        grid=(batch, nb),
        in_specs=[
            pl.BlockSpec(memory_space=pltpu.SMEM),
            pl.BlockSpec((qb, SWA_Q_W), cur),
            pl.BlockSpec((WINDOW, SWA_KV_W), prev),
            pl.BlockSpec((qb, SWA_KV_W), cur),
            pl.BlockSpec((WINDOW, SWA_KV_W), prev),
            pl.BlockSpec((qb, SWA_KV_W), cur),
        ],
        out_specs=pl.BlockSpec((qb, SWA_Q_W), cur),
        out_shape=jax.ShapeDtypeStruct((batch * seq, SWA_Q_W), BF16),
        compiler_params=_params("parallel", "parallel"),
        name="swa_prompt",
    )(sinks, q, k, k, v, v)


def _layer_view(ref, layer, first):
    if not first:
        return ref
    for other in range(ref.shape[0]):
        if other != layer:
            ref[other] = jnp.zeros(ref.shape[1:], ref.dtype)
    return ref.at[layer]


def _swa_decode_kernel(sink_ref, q_ref, kn_ref, vn_ref, knt_ref, vnt_ref, kc_ref, vc_ref, *rest, steps, layer):
    o_ref, ko_ref, vo_ref = rest[-3:]
    first = len(rest) == 3
    ko_ref = _layer_view(ko_ref, layer, first)
    vo_ref = _layer_view(vo_ref, layer, first)
    n_seq = kc_ref.shape[0]
    grp = DEC_GROUP
    rows = grp * steps
    keep = WINDOW - steps
    lane = lax.broadcasted_iota(jnp.int32, (1, WINDOW), 1)
    for s in range(n_seq):
        tile = slice((s * steps) // LANES * LANES, (s * steps) // LANES * LANES + LANES)
        shift = (keep - s * steps) % LANES
        for cache_ref, new_ref, out_ref in ((kc_ref, knt_ref, ko_ref), (vc_ref, vnt_ref, vo_ref)):
            out_ref[s] = jnp.where(lane >= keep, pltpu.roll(new_ref[:, tile], shift, axis=1),
                                   pltpu.roll(cache_ref[s], keep, axis=1))
    nk = grp * WINDOW + rows
    r = lax.broadcasted_iota(jnp.int32, (rows, nk), 0)
    q_seq, q_step = r // steps, r % steps
    c = lax.broadcasted_iota(jnp.int32, (rows, nk), 1)
    is_new = c >= grp * WINDOW
    cn = c - grp * WINDOW
    k_seq = jnp.where(is_new, cn // steps, c // WINDOW)
    k_idx = jnp.where(is_new, WINDOW + cn % steps, c % WINDOW)
    rel = WINDOW + q_step - k_idx
    valid = (q_seq == k_seq) & (rel >= 0) & (rel < WINDOW)
    sinks = _sink_column(sink_ref, layer)
    groups = range(n_seq // grp)
    new_rows = lambda g: slice(g * rows, (g + 1) * rows)
    cached = lambda ref, g: jnp.concatenate([ref[g * grp + s] for s in range(grp)], axis=1).astype(BF16)
    scores = []
    for g in groups:
        qm = _swa_head_rows(q_ref[new_rows(g), :])
        scores.append(jnp.concatenate(
            [_dot(qm, cached(kc_ref, g)), _dot_tb(qm, kn_ref[new_rows(g), :].astype(BF16))], axis=1))
    probs = [_swa_probs(scores[g], valid, sinks) for g in groups]
    for g in groups:
        e, den = probs[g]
        o = (_dot_tb(e[:, :grp * WINDOW], cached(vc_ref, g))
             + _dot(e[:, grp * WINDOW:], vn_ref[new_rows(g), :].astype(BF16)))
        for c, col in enumerate(_swa_finish(o, den)):
            o_ref[new_rows(g), c * LANES:(c + 1) * LANES] = col.astype(BF16)


def _stacked_out(shape, block, layer, prev):
    tail = (0,) * (len(block) - 1)
    if prev is None:
        spec = pl.BlockSpec((shape[0],) + block, lambda i: (0, i) + tail)
        extra_inputs = []
    else:
        spec = pl.BlockSpec((None,) + block, lambda i: (layer, i) + tail)
        extra_inputs = list(prev)
    extra_specs = [pl.BlockSpec(memory_space=pl.ANY) for _ in extra_inputs]
    return extra_inputs, extra_specs, spec, jax.ShapeDtypeStruct(shape, F32)


def _swa_decode(sinks, layer, q, kn, vn, knt, vnt, kc, vc, steps, prev):
    depth, nseq = kc.shape[:2]
    grp = min(DEC_GROUP * SWA_DEC_SUBGROUPS, nseq)
    rows = grp * steps
    assert rows % LANES == 0, "a grid step's new keys must fill whole 128-lane tiles"
    row = lambda i: (i, 0)
    cache_spec = pl.BlockSpec((None, grp, SWA_KV_W, WINDOW), lambda i: (layer, i, 0, 0))
    new_t = pl.BlockSpec((SWA_KV_W, rows), lambda i: (0, i))
    extra_in, extra_specs, out_spec, stacked = _stacked_out(kc.shape, (grp, SWA_KV_W, WINDOW), layer, prev)
    n_in = 8
    return pl.pallas_call(
        functools.partial(_swa_decode_kernel, steps=steps, layer=layer),
        grid=(nseq // grp,),
        in_specs=[
            pl.BlockSpec(memory_space=pltpu.SMEM),
            pl.BlockSpec((rows, SWA_Q_W), row),
            pl.BlockSpec((rows, SWA_KV_W), row),
            pl.BlockSpec((rows, SWA_KV_W), row),
            new_t,
            new_t,
            cache_spec,
            cache_spec,
        ] + extra_specs,
        out_specs=[pl.BlockSpec((rows, SWA_Q_W), row), out_spec, out_spec],
        out_shape=[jax.ShapeDtypeStruct((nseq * steps, SWA_Q_W), BF16), stacked, stacked],
        input_output_aliases={n_in + i: 1 + i for i in range(len(extra_in))},
        compiler_params=_params("parallel"),
        name="swa_decode",
    )(sinks, q, kn, vn, knt, vnt, kc, vc, *extra_in)


def _gla_out(o, gain, gate):
    return _rms(o, gain) * (gate * jax.nn.sigmoid(gate))


def _head_stack(x, width):
    return jnp.concatenate([x[:, h * width:(h + 1) * width] for h in range(GLA_HEADS)], axis=0)


def _head_masked_stack(x, head_of_lane):
    zero = jnp.zeros((), x.dtype)
    return jnp.concatenate([jnp.where(head_of_lane == h, x, zero) for h in range(GLA_HEADS)], axis=0)


def _gla_prompt_kernel(q_ref, k_ref, la_ref, v_ref, gg_ref, gn_ref, tril_ref, o_ref, s_ref, st_scr):
    tb = pl.program_id(1)

    @pl.when(tb == 0)
    def _():
        st_scr[...] = jnp.zeros_like(st_scr)

    n_tok = q_ref.shape[1]
    c_len = min(GLA_CHUNK, n_tok)
    tril = tril_ref[...]
    head_of_lane = lax.broadcasted_iota(jnp.int32, (1, GLA_K_W), 1) // GLA_DK
    ri = lax.broadcasted_iota(jnp.int32, (GLA_HEADS * c_len, c_len), 0) % c_len
    ci = lax.broadcasted_iota(jnp.int32, (GLA_HEADS * c_len, c_len), 1)
    causal = ri >= ci
    gain = gn_ref[...]
    n_seq = q_ref.shape[0]
    items = [(i, c) for i in range(n_seq) for c in range(n_tok // c_len)]
    rows = lambda c: slice(c * c_len, (c + 1) * c_len)
    b_all = []
    for i in range(n_seq):
        g_hi, g_lo = _split_bf16(la_ref[i])
        b_all.append(_dot(tril, g_hi) + _dot(tril, g_lo))
    qm, kdm, dec, a_raw = {}, {}, {}, {}
    for it in items:
        i, c = it
        b = b_all[i][rows(c), :]
        dec[it] = jnp.exp(b[c_len - 1:c_len, :])
        q_t = q_ref[i, rows(c), :] * jnp.exp(b)
        k_t = k_ref[i, rows(c), :] * jnp.exp(-b)
        qm[it] = _head_masked_stack(q_t, head_of_lane).astype(BF16)
        kdm[it] = _head_masked_stack(k_t * dec[it], head_of_lane).astype(BF16)
        a_raw[it] = _dot_tb(qm[it], k_t.astype(BF16))
    upd = {it: _dot_ta(_head_stack(v_ref[it[0], rows(it[1]), :], GLA_DV), kdm[it]) for it in items}
    inter = {}
    for i in range(n_seq):
        st = st_scr[i]
        for c in range(n_tok // c_len):
            inter[(i, c)] = _dot_tb(qm[(i, c)], st.astype(BF16))
            st = dec[(i, c)] * st + upd[(i, c)]
        st_scr[i] = st
    for it in items:
        i, c = it
        a = jnp.where(causal, a_raw[it], 0.0).astype(BF16)
        intra = jnp.concatenate(
            [_dot(a[h * c_len:(h + 1) * c_len, :], v_ref[i, rows(c), h * GLA_DV:(h + 1) * GLA_DV])
             for h in range(GLA_HEADS)], axis=0)
        y = _gla_out(inter[it] + intra, gain, _head_stack(gg_ref[i, rows(c), :], GLA_DV)).astype(BF16)
        for h in range(GLA_HEADS):
            o_ref[i, rows(c), h * GLA_DV:(h + 1) * GLA_DV] = y[h * c_len:(h + 1) * c_len, :]

    @pl.when(tb == pl.num_programs(1) - 1)
    def _():
        for i in range(q_ref.shape[0]):
            s_ref[i] = st_scr[i].T


def _gla_prompt(layer, q, k, la, v, gg, gain, tril, batch, seq):
    tb = tril.shape[0]
    per = min(GLA_SEQS, batch)
    blk = lambda b, t: (b, t, 0)
    kw = pl.BlockSpec((per, tb, GLA_K_W), blk)
    vw = pl.BlockSpec((per, tb, GLA_V_W), blk)
    return pl.pallas_call(
        _gla_prompt_kernel,
        grid=(batch // per, seq // tb),
        in_specs=[kw, kw, kw, vw, vw, _layer_resident(gain, layer), _resident(tril)],
        out_specs=[vw, pl.BlockSpec((per, GLA_K_W, GLA_DV), lambda b, t: (b, 0, 0))],
        out_shape=[
            jax.ShapeDtypeStruct((batch, seq, GLA_V_W), BF16),
            jax.ShapeDtypeStruct((batch, GLA_K_W, GLA_DV), F32),
        ],
        scratch_shapes=[pltpu.VMEM((per, GLA_DV, GLA_K_W), F32)],
        compiler_params=_params("parallel", "arbitrary"),
        name="gla_prompt",
    )(q, k, la, v, gg, gain, tril)


def _gla_decode_kernel(q_ref, k_ref, la_ref, v_ref, gg_ref, gn_ref, s_ref, *rest, steps, layer):
    o_ref, so_ref = rest[-2:]
    so_ref = _layer_view(so_ref, layer, len(rest) == 2)
    grp = s_ref.shape[0]
    rows = grp * steps
    stacked = GLA_HEADS * rows
    ri = lax.broadcasted_iota(jnp.int32, (rows, rows), 0)
    ci = lax.broadcasted_iota(jnp.int32, (rows, rows), 1)
    same_seq = ri // steps == ci // steps
    g_hi, g_lo = _split_bf16(la_ref[...])
    tril = (same_seq & (ri >= ci)).astype(BF16)
    total = same_seq.astype(BF16)
    b = _dot(tril, g_hi) + _dot(tril, g_lo)
    b_last = _dot(total, g_hi) + _dot(total, g_lo)
    head_of_lane = lax.broadcasted_iota(jnp.int32, (1, GLA_K_W), 1) // GLA_DK
    k_t = k_ref[...] * jnp.exp(-b)
    qm = _head_masked_stack(q_ref[...] * jnp.exp(b), head_of_lane).astype(BF16)
    km = _head_masked_stack(k_t, head_of_lane).astype(BF16)
    kdm = _head_masked_stack(k_t * jnp.exp(b_last), head_of_lane).astype(BF16)
    v_st = _head_stack(v_ref[...], GLA_DV)
    seq_of_row = (lax.broadcasted_iota(jnp.int32, (stacked, 1), 0) % rows) // steps
    seq_of_g = lax.broadcasted_iota(jnp.int32, (rows, 1), 0) // steps
    zero = jnp.zeros((), BF16)
    rhs = jnp.concatenate([
        jnp.concatenate([v_st, jnp.zeros((stacked, GLA_DV), BF16)], axis=1),
        jnp.concatenate([jnp.zeros((2 * rows, GLA_DV), BF16), jnp.ones((2 * rows, GLA_DV), BF16)], axis=1),
    ], axis=0)
    a_raw = _dot_tb(qm, km)
    states = [s_ref[s].reshape(GLA_K_W, GLA_DV) for s in range(grp)]
    inter_all = [_dot(qm, st.astype(BF16)) for st in states]
    upd_all = []
    for s in range(grp):
        lhs = jnp.concatenate([jnp.where(seq_of_row == s, kdm, zero),
                               jnp.where(seq_of_g == s, g_hi, zero),
                               jnp.where(seq_of_g == s, g_lo, zero)], axis=0)
        upd_all.append(_dot_ta(lhs, rhs))
    rr = lax.broadcasted_iota(jnp.int32, (stacked, stacked), 0) % rows
    cc = lax.broadcasted_iota(jnp.int32, (stacked, stacked), 1) % rows
    causal = (rr // steps == cc // steps) & (rr >= cc)
    o = _dot(jnp.where(causal, a_raw, 0.0).astype(BF16), v_st)
    inter = inter_all[0]
    for s in range(1, grp):
        inter = jnp.where(seq_of_row == s, inter_all[s], inter)
    y = _gla_out(o + inter, gn_ref[...], _head_stack(gg_ref[...], GLA_DV)).astype(BF16)
    for h in range(GLA_HEADS):
        o_ref[:, h * GLA_DV:(h + 1) * GLA_DV] = y[h * rows:(h + 1) * rows, :]
    for s in range(grp):
        new = jnp.exp(upd_all[s][:, GLA_DV:]) * states[s] + upd_all[s][:, :GLA_DV]
        so_ref[s] = new.reshape(GLA_HEADS, GLA_DK, GLA_DV)


def _gla_decode(layer, q, k, la, v, gg, gain, state, steps, prev):
    nseq = state.shape[1]
    grp = min(GLA_DEC_GROUP, nseq)
    rows = grp * steps
    row = lambda i: (i, 0)
    st_spec = pl.BlockSpec((None, grp, GLA_HEADS, GLA_DK, GLA_DV), lambda i: (layer, i, 0, 0, 0))
    extra_in, extra_specs, out_spec, stacked = _stacked_out(
        state.shape, (grp, GLA_HEADS, GLA_DK, GLA_DV), layer, prev)
    n_in = 7
    return pl.pallas_call(
        functools.partial(_gla_decode_kernel, steps=steps, layer=layer),
        grid=(nseq // grp,),
        in_specs=[
            pl.BlockSpec((rows, GLA_K_W), row),
            pl.BlockSpec((rows, GLA_K_W), row),
            pl.BlockSpec((rows, GLA_K_W), row),
            pl.BlockSpec((rows, GLA_V_W), row),
            pl.BlockSpec((rows, GLA_V_W), row),
            _layer_resident(gain, layer),
            st_spec,
        ] + extra_specs,
        out_specs=[pl.BlockSpec((rows, GLA_V_W), row), out_spec],
        out_shape=[jax.ShapeDtypeStruct((nseq * steps, GLA_V_W), BF16), stacked],
        input_output_aliases={n_in + i: 1 + i for i in range(len(extra_in))},
        compiler_params=_params("parallel"),
        name="gla_decode",
    )(q, k, la, v, gg, gain, state, *extra_in)


def _outproj_apply(x, a_ref, o_ref, wa_ref, wo_ref, g_ref, wq_ref, qn_ref):
    x1 = x + _dot(a_ref[...], wa_ref[...]) + _dot(o_ref[...], wo_ref[...])
    h = _rms(x1, g_ref[...]).astype(BF16)
    qn = qn_ref[...]
    q = [_rms(_dot(h, wq_ref[:, hd * XA_HEAD_DIM:(hd + 1) * XA_HEAD_DIM]), qn).astype(BF16)
         for hd in range(XA_HEADS)]
    return x1, q


def _outproj_kernel(x_ref, a_ref, o_ref, wa_ref, wo_ref, g_ref, wq_ref, qn_ref, x1_ref, q_ref):
    x1, q = _outproj_apply(x_ref[...], a_ref, o_ref, wa_ref, wo_ref, g_ref, wq_ref, qn_ref)
    x1_ref[...] = x1
    for hd in range(XA_HEADS):
        q_ref[:, hd * XA_HEAD_DIM:(hd + 1) * XA_HEAD_DIM] = q[hd]


def _outproj(x, layer, a, o, w_a, w_o, g, wq, qn):
    n, d = x.shape
    tm = min(ROW_TILE, n)
    row = lambda i: (i, 0)
    return pl.pallas_call(
        _outproj_kernel,
        grid=(n // tm,),
        in_specs=[
            pl.BlockSpec((tm, d), row),
            pl.BlockSpec((tm, SWA_Q_W), row),
            pl.BlockSpec((tm, GLA_V_W), row),
        ] + [_layer_resident(p, layer) for p in (w_a, w_o, g, wq, qn)],
        out_specs=[pl.BlockSpec((tm, d), row), pl.BlockSpec((tm, XA_W), row)],
        out_shape=[jax.ShapeDtypeStruct((n, d), F32), jax.ShapeDtypeStruct((n, XA_W), BF16)],
        compiler_params=_params("parallel"),
        name="outproj",
    )(x, a, o, w_a, w_o, g, wq, qn)


def _memkv_kernel(*refs, n_cast):
    m_ref, g_ref, wk_ref, wv_ref, kn_ref = refs[:5]
    k_ref, v_ref = refs[5 + n_cast:7 + n_cast]
    _cast_chunks(refs[5:5 + n_cast], refs[7 + n_cast:])
    m = _rms(m_ref[...], g_ref[...]).astype(BF16)
    kn = kn_ref[...]
    for hd in range(XA_HEADS):
        sl = slice(hd * XA_HEAD_DIM, (hd + 1) * XA_HEAD_DIM)
        k_ref[:, sl] = _rms(_dot(m, wk_ref[:, sl]), kn)
    v_ref[...] = _dot(m, wv_ref[...])


def _memkv(mem, g, wk, wv, kn, cast=()):
    depth = wk.shape[0]
    n, d = mem.shape
    tm = min(ROW_TILE, n)
    per_layer = lambda l, i: (l, 0, 0)
    out = lambda l, i: (l, i, 0)
    tiles = n // tm
    cast_in, cast_out, cast_shape = _cast_specs(cast, depth * tiles, lambda l, i: l * tiles + i)
    return pl.pallas_call(
        functools.partial(_memkv_kernel, n_cast=len(cast)),
        grid=(depth, tiles),
        in_specs=[
            pl.BlockSpec((tm, d), lambda l, i: (i, 0)),
            pl.BlockSpec((None, 1, d), per_layer),
            pl.BlockSpec((None, d, XA_W), per_layer),
            pl.BlockSpec((None, d, XA_W), per_layer),
            pl.BlockSpec((None, 1, XA_HEAD_DIM), per_layer),
        ] + cast_in,
        out_specs=[pl.BlockSpec((None, tm, XA_W), out), pl.BlockSpec((None, tm, XA_W), out)] + cast_out,
        out_shape=[jax.ShapeDtypeStruct((depth, n, XA_W), F32)] * 2 + cast_shape,
        compiler_params=_params("parallel", "parallel"),
        name="memkv",
    )(mem, g, wk, wv, kn, *[c[0] for c in cast])


def _mixout_prompt_kernel(*refs, n_cast):
    (x_ref, a_ref, o_ref, wa_ref, wo_ref, g_ref, wq_ref, qn_ref, mk_ref, mv_ref, xwo_ref, fg_ref, wg_ref, wu_ref,
     wd_ref) = refs[:15]
    out_ref, att_scr, a_scr = refs[15 + n_cast], refs[-2], refs[-1]
    _cast_chunks(refs[15:15 + n_cast], refs[16 + n_cast:-2])
    x1, q = _outproj_apply(x_ref[...], a_ref, o_ref, wa_ref, wo_ref, g_ref, wq_ref, qn_ref)
    mk = mk_ref[...].astype(BF16)
    mv = mv_ref[...].astype(BF16)
    heads = range(XA_HEADS)
    cols = lambda hd: slice(hd * XA_HEAD_DIM, (hd + 1) * XA_HEAD_DIM)
    scores = [_dot_tb(q[hd], mk[:, cols(hd)]) * (XA_HEAD_DIM ** -0.5 * LOG2E) for hd in heads]
    probs = []
    for s in scores:
        e = jnp.exp2(s - jnp.max(s, axis=-1, keepdims=True))
        probs.append((e.astype(BF16), jnp.sum(e, axis=-1, keepdims=True)))
    for hd in heads:
        att_scr[:, cols(hd)] = (_dot(probs[hd][0], mv[:, cols(hd)]) / probs[hd][1]).astype(BF16)
    x2 = x1 + _dot(att_scr[...], xwo_ref[...])
    out_ref[...] = _ffn_apply(x2, fg_ref, wg_ref, wu_ref, wd_ref, a_scr)


def _mixout_prompt(x, layer, a, o, outproj, mk, mv, wo, ffn, seq, cast=()):
    n, d = x.shape
    dff = ffn[1].shape[2]
    tm = min(ROW_TILE, seq)
    per_seq = seq // tm
    row = lambda i: (i, 0)
    mem_spec = pl.BlockSpec((None, None, mk.shape[2], XA_W), lambda i: (layer, i // per_seq, 0, 0))
    cast_in, cast_out, cast_shape = _cast_specs(cast, n // tm)
    return pl.pallas_call(
        functools.partial(_mixout_prompt_kernel, n_cast=len(cast)),
        grid=(n // tm,),
        in_specs=([pl.BlockSpec((tm, d), row), pl.BlockSpec((tm, SWA_Q_W), row), pl.BlockSpec((tm, GLA_V_W), row)]
                  + [_layer_resident(p, layer) for p in outproj]
                  + [mem_spec, mem_spec]
                  + [_layer_resident(p, layer) for p in (wo,) + ffn]
                  + cast_in),
        out_specs=[pl.BlockSpec((tm, d), row)] + cast_out,
        out_shape=[jax.ShapeDtypeStruct((n, d), F32)] + cast_shape,
        scratch_shapes=[pltpu.VMEM((tm, XA_W), BF16), pltpu.VMEM((tm, dff), BF16)],
        compiler_params=_params("parallel"),
        name="mixout_prompt",
    )(x, a, o, *outproj, mk, mv, wo, *ffn, *[c[0] for c in cast])


def _xattn_decode_kernel(q_ref, mk_ref, mv_ref, o_ref, *, steps):
    grp, nkeys = mk_ref.shape[0], mk_ref.shape[1]
    rows = grp * steps
    q = jnp.concatenate([q_ref[:, hd * XA_HEAD_DIM:(hd + 1) * XA_HEAD_DIM] for hd in range(XA_HEADS)], axis=0)
    r = lax.broadcasted_iota(jnp.int32, (XA_HEADS * rows, 1), 0)
    own = (r % rows) // steps
    same_head = (r // rows) == (lax.broadcasted_iota(jnp.int32, (1, nkeys), 1) % XA_HEADS)
    s = None
    for j in range(grp):
        sj = _dot_tb(q, mk_ref[j].astype(BF16))
        s = sj if s is None else jnp.where(own == j, sj, s)
    s = jnp.where(same_head, s * (XA_HEAD_DIM ** -0.5 * LOG2E), -jnp.inf)
    m = jnp.max(s, axis=-1, keepdims=True)
    e = jnp.exp2(s - m)
    p = e.astype(BF16)
    o = None
    for j in range(grp):
        oj = _dot(p, mv_ref[j].astype(BF16))
        o = oj if o is None else jnp.where(own == j, oj, o)
    o = o / jnp.sum(e, axis=-1, keepdims=True)
    for hd in range(XA_HEADS):
        o_ref[:, hd * XA_HEAD_DIM:(hd + 1) * XA_HEAD_DIM] = o[hd * rows:(hd + 1) * rows, :].astype(BF16)


def _xattn_decode(q, layer, mk, mv, steps):
    nseq, nkeys = mk.shape[1], mk.shape[2]
    grp = XA_DEC_GROUP
    rows = grp * steps
    row = lambda i: (i, 0)
    mem_spec = pl.BlockSpec((None, grp, nkeys, XA_HEAD_DIM), lambda i: (layer, i, 0, 0))
    return pl.pallas_call(
        functools.partial(_xattn_decode_kernel, steps=steps),
        grid=(nseq // grp,),
        in_specs=[pl.BlockSpec((rows, XA_W), row), mem_spec, mem_spec],
        out_specs=pl.BlockSpec((rows, XA_W), row),
        out_shape=jax.ShapeDtypeStruct((nseq * steps, XA_W), BF16),
        compiler_params=_params("parallel"),
        name="xattn_decode",
    )(q, mk, mv)


def _proj_res_kernel(x_ref, a_ref, w_ref, o_ref):
    o_ref[...] = x_ref[...] + _dot(a_ref[...], w_ref[...])


def _proj_res(x, layer, a, w):
    n, d = x.shape
    tm = min(ROW_TILE, n)
    row = lambda i: (i, 0)
    return pl.pallas_call(
        _proj_res_kernel,
        grid=(n // tm,),
        in_specs=[pl.BlockSpec((tm, d), row), pl.BlockSpec((tm, a.shape[1]), row), _layer_resident(w, layer)],
        out_specs=pl.BlockSpec((tm, d), row),
        out_shape=jax.ShapeDtypeStruct((n, d), F32),
        compiler_params=_params("parallel"),
        name="proj_res",
    )(x, a, w)


def _rope_tables(pos):
    half = HEAD_DIM // 2
    inv = ROPE_THETA ** (-jnp.arange(half, dtype=F32) / half)
    ang = pos.astype(F32)[:, None] * inv[None, :]
    cos, sin = jnp.cos(ang), jnp.sin(ang)
    reps = LANES // HEAD_DIM
    return jnp.tile(cos, (1, 2 * reps)), jnp.tile(jnp.concatenate([-sin, sin], axis=-1), (1, reps))


def _block_tril(n_blocks, size):
    i = jnp.arange(n_blocks * size)
    return ((i[:, None] // size == i[None, :] // size) & (i[:, None] >= i[None, :])).astype(BF16)


def _permute_heads(w, axis):
    blocks = jnp.split(w, SWA_Q_HEADS, axis=axis)
    return jnp.concatenate([blocks[h] for h in SWA_HEAD_ORDER], axis=axis)


def kernel(x_prompt, x_sample, cache_swa_k, cache_swa_v, state_gla, cache_mem_k, cache_mem_v, mem_prompt, ffn1_norm, ffn1_wg, ffn1_wu, ffn1_wd, mix_norm, w_in, swa_q_norm, swa_k_norm, swa_sinks, gla_w_gate, gla_b_gate, gla_out_norm, w_out, xa_norm, mem_norm, xa_wq, xa_wk, xa_wv, xa_q_norm, xa_k_norm, xa_wo, ffn2_norm, ffn2_wg, ffn2_wu, ffn2_wd):
    batch, seq, d = x_prompt.shape
    nseq, steps, _ = x_sample.shape
    depth = w_in.shape[0]
    mem_len = mem_prompt.shape[1]

    bf = lambda w: w.astype(BF16)
    vec = lambda p: p[:, None, :]
    first_cast = ([(w, 0) for w in (ffn1_wg, ffn1_wu, ffn1_wd)]
                  + [(w_in, l, w_in.shape[2] + MXU_TILE - GLA_LOWRANK) for l in range(depth)])
    gate_w = jnp.pad(bf(gla_w_gate), ((0, 0), (0, LANES - GLA_LOWRANK), (0, 0)))
    w_a = bf(_permute_heads(w_out[:, :SWA_Q_W], 1))
    w_o = bf(w_out[:, SWA_Q_W:])
    wq_b, wk_b, wv_b, wo_b = bf(xa_wq), bf(xa_wk), bf(xa_wv), bf(xa_wo)
    qn = vec(jnp.tile(swa_q_norm * (HEAD_DIM ** -0.5 * LOG2E), (1, LANES // HEAD_DIM)))
    kn = vec(jnp.tile(swa_k_norm, (1, LANES // HEAD_DIM)))
    inproj_params = lambda l: (vec(mix_norm), w_all[l], gate_w, vec(gla_b_gate), qn, kn)
    outproj_params = (w_a, w_o, vec(xa_norm), wq_b, vec(xa_q_norm))
    gla_gain = vec(gla_out_norm)

    lane = jnp.arange(MXU_TILE)
    bd = (lane[:, None] // HEAD_DIM == lane[None, :] // HEAD_DIM).astype(BF16)
    cos_p, sin_p = _rope_tables(jnp.arange(seq))
    cos_s, sin_s = _rope_tables(PAST_LEN + jnp.arange(nseq * steps) % steps)
    tril_p = _block_tril(min(GLA_BLOCK, seq) // min(GLA_CHUNK, seq), min(GLA_CHUNK, seq))

    mk_p, mv_p, *cast = _memkv(mem_prompt.reshape(batch * mem_len, d), vec(mem_norm), wk_b, wv_b,
                               vec(xa_k_norm), first_cast)
    ffn1_w = [cast[:3]] + [None] * (depth - 1)
    w_all = cast[3:]
    mk_p = mk_p.reshape(depth, batch, mem_len, XA_W)
    mv_p = mv_p.reshape(depth, batch, mem_len, XA_W)
    mk_s = cache_mem_k.reshape(depth, nseq, mem_len * XA_HEADS, XA_HEAD_DIM)
    mv_s = cache_mem_v.reshape(depth, nseq, mem_len * XA_HEADS, XA_HEAD_DIM)
    native = lambda c: jnp.transpose(c, (0, 1, 3, 4, 2)).reshape(depth, nseq, SWA_KV_W, WINDOW)
    kc_s, vc_s = native(cache_swa_k), native(cache_swa_v)

    xp = x_prompt.reshape(batch * seq, d)
    xs = x_sample.reshape(nseq * steps, d)
    kp_l, vp_l, sp_l = [], [], []
    swa_new = gla_new = None
    for l in range(depth):
        ffn1 = (vec(ffn1_norm),) + tuple(ffn1_w[l])
        xp, q_s, k_s, v_s, q_g, k_g, v_g, g_g, la, *ffn2_w = _ffn_inproj(
            xp, l, ffn1, inproj_params(l), cos_p, sin_p, bd, cast=[(w, l) for w in (ffn2_wg, ffn2_wu, ffn2_wd)])
        ffn2 = (vec(ffn2_norm),) + tuple(ffn2_w)
        a_p = _swa_prompt(swa_sinks, l, q_s, k_s, v_s, batch, seq)
        seqs = lambda a: a.reshape(batch, seq, a.shape[-1])
        o_p, s_p = _gla_prompt(l, seqs(q_g), seqs(k_g), seqs(la), seqs(v_g), seqs(g_g), gla_gain, tril_p,
                               batch, seq)
        o_p = o_p.reshape(batch * seq, GLA_V_W)
        last = lambda a: seqs(a)[:, seq - WINDOW:].reshape(batch, WINDOW, SWA_KV_HEADS, HEAD_DIM)
        kp_l.append(last(k_s))
        vp_l.append(last(v_s))
        sp_l.append(s_p.reshape(batch, GLA_HEADS, GLA_DK, GLA_DV))
        next_ffn1 = [(w, l + 1) for w in (ffn1_wg, ffn1_wu, ffn1_wd)] if l + 1 < depth else []
        xp, *cast = _mixout_prompt(xp, l, a_p, o_p, outproj_params, mk_p, mv_p, wo_b, ffn2, seq, cast=next_ffn1)
        if next_ffn1:
            ffn1_w[l + 1] = cast

        xs, q_s, k_s, v_s, q_g, k_g, v_g, g_g, la, k_t, v_t = _ffn_inproj(
            xs, l, ffn1, inproj_params(l), cos_s, sin_s, bd, transposed_kv=True)
        a_s, *swa_new = _swa_decode(swa_sinks, l, q_s, k_s, v_s, k_t, v_t, kc_s, vc_s, steps, swa_new)
        o_s, *gla_new = _gla_decode(l, q_g, k_g, la, v_g, g_g, gla_gain, state_gla, steps, gla_new)
        xs, q_x = _outproj(xs, l, a_s, o_s, *outproj_params)
        xs = _proj_res(xs, l, _xattn_decode(q_x, l, mk_s, mv_s, steps), wo_b)
        xs = _ffn(xs, l, *ffn2)

    unnative = lambda c: jnp.transpose(c.reshape(depth, nseq, SWA_KV_HEADS, HEAD_DIM, WINDOW), (0, 1, 4, 2, 3))
    return (xp.reshape(batch, seq, d), xs.reshape(nseq, steps, d),
            jnp.stack(kp_l), jnp.stack(vp_l), jnp.stack(sp_l),
            mk_p.reshape(depth, batch, mem_len, XA_HEADS, XA_HEAD_DIM),
            mv_p.reshape(depth, batch, mem_len, XA_HEADS, XA_HEAD_DIM),
            unnative(swa_new[0]), unnative(swa_new[1]), gla_new[0])
```

```python
import functools

import jax
import jax.numpy as jnp
from jax import lax
from jax.experimental import pallas as pl
from jax.experimental.pallas import tpu as pltpu

F32 = jnp.float32
BF16 = jnp.bfloat16

EPS = 1e-6
LOG2E = 1.4426950408889634
PAST_LEN = 16384
WINDOW = 128
ROPE_THETA = 10000.0
HEAD_DIM = 64
SWA_Q_HEADS = 8
SWA_KV_HEADS = 2
SWA_GROUP = SWA_Q_HEADS // SWA_KV_HEADS
GLA_HEADS = 4
GLA_DK = 64
GLA_DV = 128
GLA_LOWRANK = 16
GLA_GATE_TEMP = 16.0
GLA_CHUNK = 64
XA_HEADS = 4
XA_HEAD_DIM = 128

SWA_Q_W = SWA_Q_HEADS * HEAD_DIM
SWA_KV_W = SWA_KV_HEADS * HEAD_DIM
GLA_K_W = GLA_HEADS * GLA_DK
GLA_V_W = GLA_HEADS * GLA_DV
XA_W = XA_HEADS * XA_HEAD_DIM
MAIN_W = SWA_Q_W + 2 * SWA_KV_W + 2 * GLA_K_W + 2 * GLA_V_W

LANES = 128
BF16_SUBLANES = 16
MXU_TILE = 256
VMEM_LIMIT = 56 * 1024 * 1024

ROW_TILE = 512
ROW_SPLIT = 2
FFN_CHUNK = 256
SWA_BLOCK = 512
GLA_BLOCK = 256
GLA_SEQS = 4
DEC_GROUP = 4
SWA_DEC_SUBGROUPS = 8
XA_DEC_GROUP = 8
GLA_DEC_GROUP = 8

assert SWA_KV_HEADS * HEAD_DIM == LANES
SWA_HEAD_ORDER = tuple(kv * SWA_GROUP + g for g in range(SWA_GROUP) for kv in range(SWA_KV_HEADS))


def _dot(a, b):
    return jnp.dot(a, b, preferred_element_type=F32)


def _dot_tb(a, b):
    return lax.dot_general(a, b, (((1,), (1,)), ((), ())), preferred_element_type=F32)


def _dot_ta(a, b):
    return lax.dot_general(a, b, (((0,), (0,)), ((), ())), preferred_element_type=F32)


def _split_bf16(x):
    hi = x.astype(BF16)
    lo = (x - hi.astype(F32)).astype(BF16)
    return hi, lo


def _rms(x, g):
    ms = jnp.mean(x * x, axis=-1, keepdims=True)
    return x * lax.rsqrt(ms + EPS) * g


def _params(*sem):
    return pltpu.CompilerParams(dimension_semantics=sem, vmem_limit_bytes=VMEM_LIMIT)


def _resident(arr):
    nd = arr.ndim
    return pl.BlockSpec(arr.shape, lambda *_: (0,) * nd, pipeline_mode=pl.Buffered(1))


def _layer_resident(arr, layer):
    nd = arr.ndim
    index = (layer if arr.shape[0] > 1 else 0,) + (0,) * (nd - 1)
    return pl.BlockSpec((None,) + arr.shape[1:], lambda *_: index, pipeline_mode=pl.Buffered(1))


def _cast_specs(cast, n_steps, step_of=lambda i: i):
    ins, outs, shapes = [], [], []
    for arr, src_layer, *padded in cast:
        _, rows, cols = arr.shape
        out_cols = padded[0] if padded else cols
        share = 1
        while rows % (n_steps // share) or (rows * share // n_steps) % BF16_SUBLANES:
            share *= 2
            assert n_steps % share == 0
        chunk = rows * share // n_steps
        ins.append(pl.BlockSpec((None, chunk, cols), lambda *g, l=src_layer, s=share: (l, step_of(*g) // s, 0)))
        outs.append(pl.BlockSpec((1, chunk, out_cols), lambda *g, s=share: (0, step_of(*g) // s, 0)))
        shapes.append(jax.ShapeDtypeStruct((1, rows, out_cols), BF16))
    return ins, outs, shapes


def _cast_chunks(srcs, dsts):
    for src, dst in zip(srcs, dsts):
        cols = src.shape[-1]
        dst[0, :, :cols] = src[...].astype(BF16)
        if dst.shape[-1] > cols:
            dst[0, :, cols:] = jnp.zeros((dst.shape[1], dst.shape[-1] - cols), BF16)


def _ffn_hidden(x, g_ref, wg_ref, wu_ref, a_scr):
    h = _rms(x, g_ref[...]).astype(BF16)
    dff = wg_ref.shape[1]
    for c in range(dff // FFN_CHUNK):
        sl = slice(c * FFN_CHUNK, (c + 1) * FFN_CHUNK)
        g = _dot(h, wg_ref[:, sl])
        u = _dot(h, wu_ref[:, sl])
        a_scr[:, sl] = (g * jax.nn.sigmoid(g) * u).astype(BF16)


def _ffn_apply(x, g_ref, wg_ref, wu_ref, wd_ref, a_scr):
    _ffn_hidden(x, g_ref, wg_ref, wu_ref, a_scr)
    return x + 0.5 * _dot(a_scr[...], wd_ref[...])


def _ffn_inproj_kernel(*refs, n_cast, transposed_kv):
    (x_ref, fg_ref, wg_ref, wu_ref, wd_ref, g_ref, w_ref, gw_ref, gb_ref, qn_ref, kn_ref, cos_ref, sin_ref,
     bd_ref) = refs[:14]
    outs, a_scr = refs[14 + n_cast:-1], refs[-1]
    xo_ref, qs_ref, ks_ref, vs_ref, qg_ref, kg_ref, vg_ref, gg_ref, la_ref = outs[:9]
    kt_ref, vt_ref = outs[9:11] if transposed_kv else (None, None)
    _cast_chunks(refs[14:14 + n_cast], outs[len(outs) - n_cast:])
    x = x_ref[...]
    _ffn_hidden(x, fg_ref, wg_ref, wu_ref, a_scr)
    n = x.shape[0] // ROW_SPLIT
    groups = [slice(r * n, (r + 1) * n) for r in range(ROW_SPLIT)]
    mid = [x[rs] + 0.5 * _dot(a_scr[rs, :], wd_ref[...]) for rs in groups]
    for rs, xm in zip(groups, mid):
        xo_ref[rs, :] = xm
    split = SWA_Q_W + 2 * SWA_KV_W + 2 * GLA_K_W
    h = [_rms(xm, g_ref[...]).astype(BF16) for xm in mid]
    z1 = [_dot(hr, w_ref[:, :split]) for hr in h]
    z2 = []
    for rs, hr, z in zip(groups, h, z1):
        _inproj_finish_swa(z, rs, qn_ref, kn_ref, cos_ref, sin_ref, bd_ref, qs_ref, ks_ref, vs_ref, qg_ref,
                           kg_ref, kt_ref, vt_ref)
        z2.append(_dot(hr, w_ref[:, split:]))
    for rs, z in zip(groups, z2):
        _inproj_finish_gla(z, rs, gw_ref, gb_ref, vg_ref, gg_ref, la_ref)


def _inproj_finish_swa(z1, rs, qn_ref, kn_ref, cos_ref, sin_ref, bd_ref, qs_ref, ks_ref, vs_ref, qg_ref,
                       kg_ref, kt_ref, vt_ref):
    bd = bd_ref[...]
    cos = cos_ref[rs, :]
    sin = sin_ref[rs, :]
    lane = lax.broadcasted_iota(jnp.int32, cos.shape, 1)
    lane_lo = (lane % HEAD_DIM) < (HEAD_DIM // 2)

    def head_scale(z):
        ss = _dot((z * z).astype(BF16), bd)
        return lax.rsqrt(ss * (1.0 / HEAD_DIM) + EPS)

    def rope(y):
        swapped = jnp.where(lane_lo, pltpu.roll(y, LANES - HEAD_DIM // 2, axis=1),
                            pltpu.roll(y, HEAD_DIM // 2, axis=1))
        return y * cos + swapped * sin

    qn = qn_ref[...]
    nat = []
    for t in range(SWA_Q_W // MXU_TILE):
        z = z1[:, t * MXU_TILE:(t + 1) * MXU_TILE]
        y = z * head_scale(z)
        nat += [rope(y[:, c * LANES:(c + 1) * LANES] * qn) for c in range(MXU_TILE // LANES)]
    low = _low_half()
    per_col = LANES // HEAD_DIM
    for c in range(SWA_Q_W // LANES):
        halves = []
        for half, head in enumerate(SWA_HEAD_ORDER[per_col * c:per_col * (c + 1)]):
            col = nat[head // per_col]
            halves.append(col if head % per_col == half else pltpu.roll(col, HEAD_DIM, axis=1))
        qs_ref[rs, c * LANES:(c + 1) * LANES] = jnp.where(low, halves[0], halves[1]).astype(BF16)
    o = SWA_Q_W
    z = z1[:, o:o + 2 * SWA_KV_W]
    k = rope(z[:, :SWA_KV_W] * head_scale(z)[:, :SWA_KV_W] * kn_ref[...])
    ks_ref[rs, :] = k
    vs_ref[rs, :] = z[:, SWA_KV_W:]
    if kt_ref is not None:
        kt_ref[:, rs] = k.T
        vt_ref[:, rs] = z[:, SWA_KV_W:].T
    o += 2 * SWA_KV_W
    qg_ref[rs, :] = z1[:, o:o + GLA_K_W] * (GLA_DK ** -0.5)
    o += GLA_K_W
    kg_ref[rs, :] = z1[:, o:o + GLA_K_W]


def _inproj_finish_gla(z2, rs, gw_ref, gb_ref, vg_ref, gg_ref, la_ref):
    vg_ref[rs, :] = z2[:, :GLA_V_W].astype(BF16)
    gg_ref[rs, :] = z2[:, GLA_V_W:2 * GLA_V_W]
    lr = z2[:, 2 * GLA_V_W:2 * GLA_V_W + LANES].astype(BF16)
    t = _dot(lr, gw_ref[...]) + gb_ref[...]
    log_sig = jnp.minimum(t, 0.0) - jnp.log(1.0 + jnp.exp(-jnp.abs(t)))
    la_ref[rs, :] = log_sig * (1.0 / GLA_GATE_TEMP)


def _ffn_inproj(x, layer, ffn, inproj, cos, sin, bd, transposed_kv=False, cast=()):
    n, d = x.shape
    dff = ffn[1].shape[2]
    tm = min(ROW_TILE, n)
    pos_blocks = cos.shape[0] // tm
    row = lambda i: (i, 0)
    pos = lambda i: (i % pos_blocks, 0)
    widths = (d, SWA_Q_W, SWA_KV_W, SWA_KV_W, GLA_K_W, GLA_K_W, GLA_V_W, GLA_V_W, GLA_K_W)
    dtypes = (F32, BF16, F32, F32, F32, F32, BF16, F32, F32)
    out_specs = [pl.BlockSpec((tm, w), row) for w in widths]
    out_shape = [jax.ShapeDtypeStruct((n, w), dt) for w, dt in zip(widths, dtypes)]
    if transposed_kv:
        out_specs += [pl.BlockSpec((SWA_KV_W, tm), lambda i: (0, i))] * 2
        out_shape += [jax.ShapeDtypeStruct((SWA_KV_W, n), F32)] * 2
    cast_in, cast_out, cast_shape = _cast_specs(cast, n // tm)
    return pl.pallas_call(
        functools.partial(_ffn_inproj_kernel, n_cast=len(cast), transposed_kv=transposed_kv),
        grid=(n // tm,),
        in_specs=([pl.BlockSpec((tm, d), row)]
                  + [_layer_resident(a, layer) for a in ffn + inproj]
                  + [pl.BlockSpec((tm, LANES), pos), pl.BlockSpec((tm, LANES), pos), _resident(bd)]
                  + cast_in),
        out_specs=out_specs + cast_out,
        out_shape=out_shape + cast_shape,
        scratch_shapes=[pltpu.VMEM((tm, dff), BF16)],
        compiler_params=_params("parallel"),
        name="ffn_inproj",
    )(x, *ffn, *inproj, cos, sin, bd, *[c[0] for c in cast])


def _swa_attend(q, k, v, valid, sinks):
    return _swa_out(*_swa_probs(_swa_scores(q, k), valid, sinks), v)


def _low_half():
    return lax.broadcasted_iota(jnp.int32, (1, LANES), 1) < HEAD_DIM


def _swa_head_rows(q):
    low = _low_half()
    zero = jnp.zeros((), q.dtype)
    pieces = []
    for c in range(SWA_Q_W // LANES):
        qc = q[:, c * LANES:(c + 1) * LANES]
        pieces += [jnp.where(low, qc, zero), jnp.where(low, zero, qc)]
    return jnp.concatenate(pieces, axis=0)


def _swa_scores(q, k):
    return _dot_tb(_swa_head_rows(q), k)


def _swa_probs(s, valid, sinks):
    rows, keys = valid.shape
    s = jnp.where(valid[None], s.reshape(SWA_Q_HEADS, rows, keys), -jnp.inf)
    m = jnp.maximum(jnp.max(s, axis=-1, keepdims=True), sinks)
    e = jnp.exp2(s - m)
    den = jnp.sum(e, axis=-1, keepdims=True) + jnp.exp2(sinks - m)
    return e.astype(BF16).reshape(SWA_Q_HEADS * rows, keys), den


def _swa_out(e, den, v):
    return _swa_finish(_dot(e, v), den)


def _swa_finish(o, den):
    low = _low_half()
    o = o.reshape(den.shape[0], den.shape[1], LANES) / den
    return [jnp.where(low, o[2 * c], o[2 * c + 1]) for c in range(SWA_Q_W // LANES)]


def _sink_column(sink_ref, layer):
    idx = lax.broadcasted_iota(jnp.int32, (SWA_Q_HEADS, 1, 1), 0)
    col = jnp.zeros((SWA_Q_HEADS, 1, 1), F32)
    for p, head in enumerate(SWA_HEAD_ORDER):
        col = jnp.where(idx == p, sink_ref[layer, head] * LOG2E, col)
    return col


def _swa_prompt_kernel(sink_ref, q_ref, kp_ref, kc_ref, vp_ref, vc_ref, o_ref, *, layer):
    j = pl.program_id(1)
    kall = jnp.concatenate([kp_ref[...], kc_ref[...]], axis=0).astype(BF16)
    vall = jnp.concatenate([vp_ref[...], vc_ref[...]], axis=0).astype(BF16)
    sinks = _sink_column(sink_ref, layer)
    upper = (lax.broadcasted_iota(jnp.int32, (WINDOW, WINDOW), 1)
             > lax.broadcasted_iota(jnp.int32, (WINDOW, WINDOW), 0))[None]

    def banded_probs(s, has_prev):
        s = s.reshape(SWA_Q_HEADS, WINDOW, 2 * WINDOW)
        prev = s[:, :, :WINDOW]
        if has_prev is not True:
            prev = jnp.where(has_prev, prev, -jnp.inf)
        logits = jnp.where(upper, prev, s[:, :, WINDOW:])
        m = jnp.maximum(jnp.max(logits, axis=-1, keepdims=True), sinks)
        e = jnp.exp2(logits - m)
        den = jnp.sum(e, axis=-1, keepdims=True) + jnp.exp2(sinks - m)
        e = e.astype(BF16)
        zero = jnp.zeros((), BF16)
        e = jnp.concatenate([jnp.where(upper, e, zero), jnp.where(upper, zero, e)], axis=-1)
        return e.reshape(SWA_Q_HEADS * WINDOW, 2 * WINDOW), den

    windows = range(q_ref.shape[0] // WINDOW)
    rows = lambda w: slice(w * WINDOW, (w + 1) * WINDOW)
    keys = lambda w: slice(w * WINDOW, (w + 2) * WINDOW)
    scores = [_swa_scores(q_ref[rows(w), :], kall[keys(w)]) for w in windows]
    probs = [banded_probs(scores[w], True if w > 0 else j > 0) for w in windows]
    for w in windows:
        for c, col in enumerate(_swa_out(*probs[w], vall[keys(w)])):
            o_ref[rows(w), c * LANES:(c + 1) * LANES] = col.astype(BF16)


def _swa_prompt(sinks, layer, q, k, v, batch, seq):
    qb = min(SWA_BLOCK, seq)
    nb = seq // qb
    per = qb // WINDOW
    cur = lambda b, j: (b * nb + j, 0)
    prev = lambda b, j: ((b * nb + j) * per - jnp.minimum(j, 1), 0)
    return pl.pallas_call(
        functools.partial(_swa_prompt_kernel, layer=layer),
        grid=(batch, nb),
        in_specs=[
            pl.BlockSpec(memory_space=pltpu.SMEM),
            pl.BlockSpec((qb, SWA_Q_W), cur),
            pl.BlockSpec((WINDOW, SWA_KV_W), prev),
            pl.BlockSpec((qb, SWA_KV_W), cur),
            pl.BlockSpec((WINDOW, SWA_KV_W), prev),
            pl.BlockSpec((qb, SWA_KV_W), cur),
        ],
        out_specs=pl.BlockSpec((qb, SWA_Q_W), cur),
        out_shape=jax.ShapeDtypeStruct((batch * seq, SWA_Q_W), BF16),
        compiler_params=_params("parallel", "parallel"),
        name="swa_prompt",
    )(sinks, q, k, k, v, v)


def _layer_view(ref, layer, first):
    if not first:
        return ref
    for other in range(ref.shape[0]):
        if other != layer:
            ref[other] = jnp.zeros(ref.shape[1:], ref.dtype)
    return ref.at[layer]


def _swa_decode_kernel(sink_ref, q_ref, kn_ref, vn_ref, knt_ref, vnt_ref, kc_ref, vc_ref, *rest, steps, layer):
    o_ref, ko_ref, vo_ref = rest[-3:]
    first = len(rest) == 3
    ko_ref = _layer_view(ko_ref, layer, first)
    vo_ref = _layer_view(vo_ref, layer, first)
    n_seq = kc_ref.shape[0]
    grp = DEC_GROUP
    rows = grp * steps
    keep = WINDOW - steps
    lane = lax.broadcasted_iota(jnp.int32, (1, WINDOW), 1)
    for s in range(n_seq):
        tile = slice((s * steps) // LANES * LANES, (s * steps) // LANES * LANES + LANES)
        shift = (keep - s * steps) % LANES
        for cache_ref, new_ref, out_ref in ((kc_ref, knt_ref, ko_ref), (vc_ref, vnt_ref, vo_ref)):
            out_ref[s] = jnp.where(lane >= keep, pltpu.roll(new_ref[:, tile], shift, axis=1),
                                   pltpu.roll(cache_ref[s], keep, axis=1))
    nk = grp * WINDOW + rows
    r = lax.broadcasted_iota(jnp.int32, (rows, nk), 0)
    q_seq, q_step = r // steps, r % steps
    c = lax.broadcasted_iota(jnp.int32, (rows, nk), 1)
    is_new = c >= grp * WINDOW
    cn = c - grp * WINDOW
    k_seq = jnp.where(is_new, cn // steps, c // WINDOW)
    k_idx = jnp.where(is_new, WINDOW + cn % steps, c % WINDOW)
    rel = WINDOW + q_step - k_idx
    valid = (q_seq == k_seq) & (rel >= 0) & (rel < WINDOW)
    sinks = _sink_column(sink_ref, layer)
    groups = range(n_seq // grp)
    new_rows = lambda g: slice(g * rows, (g + 1) * rows)
    cached = lambda ref, g: jnp.concatenate([ref[g * grp + s] for s in range(grp)], axis=1).astype(BF16)
    scores = []
    for g in groups:
        qm = _swa_head_rows(q_ref[new_rows(g), :])
        scores.append(jnp.concatenate(
            [_dot(qm, cached(kc_ref, g)), _dot_tb(qm, kn_ref[new_rows(g), :].astype(BF16))], axis=1))
    probs = [_swa_probs(scores[g], valid, sinks) for g in groups]
    for g in groups:
        e, den = probs[g]
        o = (_dot_tb(e[:, :grp * WINDOW], cached(vc_ref, g))
             + _dot(e[:, grp * WINDOW:], vn_ref[new_rows(g), :].astype(BF16)))
        for c, col in enumerate(_swa_finish(o, den)):
            o_ref[new_rows(g), c * LANES:(c + 1) * LANES] = col.astype(BF16)


def _stacked_out(shape, block, layer, prev):
    tail = (0,) * (len(block) - 1)
    if prev is None:
        spec = pl.BlockSpec((shape[0],) + block, lambda i: (0, i) + tail)
        extra_inputs = []
    else:
        spec = pl.BlockSpec((None,) + block, lambda i: (layer, i) + tail)
        extra_inputs = list(prev)
    extra_specs = [pl.BlockSpec(memory_space=pl.ANY) for _ in extra_inputs]
    return extra_inputs, extra_specs, spec, jax.ShapeDtypeStruct(shape, F32)


def _swa_decode(sinks, layer, q, kn, vn, knt, vnt, kc, vc, steps, prev):
    depth, nseq = kc.shape[:2]
    grp = min(DEC_GROUP * SWA_DEC_SUBGROUPS, nseq)
    rows = grp * steps
    assert rows % LANES == 0, "a grid step's new keys must fill whole 128-lane tiles"
    row = lambda i: (i, 0)
    cache_spec = pl.BlockSpec((None, grp, SWA_KV_W, WINDOW), lambda i: (layer, i, 0, 0))
    new_t = pl.BlockSpec((SWA_KV_W, rows), lambda i: (0, i))
    extra_in, extra_specs, out_spec, stacked = _stacked_out(kc.shape, (grp, SWA_KV_W, WINDOW), layer, prev)
    n_in = 8
    return pl.pallas_call(
        functools.partial(_swa_decode_kernel, steps=steps, layer=layer),
        grid=(nseq // grp,),
        in_specs=[
            pl.BlockSpec(memory_space=pltpu.SMEM),
            pl.BlockSpec((rows, SWA_Q_W), row),
            pl.BlockSpec((rows, SWA_KV_W), row),
            pl.BlockSpec((rows, SWA_KV_W), row),
            new_t,
            new_t,
            cache_spec,
            cache_spec,
        ] + extra_specs,
        out_specs=[pl.BlockSpec((rows, SWA_Q_W), row), out_spec, out_spec],
        out_shape=[jax.ShapeDtypeStruct((nseq * steps, SWA_Q_W), BF16), stacked, stacked],
        input_output_aliases={n_in + i: 1 + i for i in range(len(extra_in))},
        compiler_params=_params("parallel"),
        name="swa_decode",
    )(sinks, q, kn, vn, knt, vnt, kc, vc, *extra_in)


def _gla_out(o, gain, gate):
    return _rms(o, gain) * (gate * jax.nn.sigmoid(gate))


def _head_stack(x, width):
    return jnp.concatenate([x[:, h * width:(h + 1) * width] for h in range(GLA_HEADS)], axis=0)


def _head_masked_stack(x, head_of_lane):
    zero = jnp.zeros((), x.dtype)
    return jnp.concatenate([jnp.where(head_of_lane == h, x, zero) for h in range(GLA_HEADS)], axis=0)


def _gla_prompt_kernel(q_ref, k_ref, la_ref, v_ref, gg_ref, gn_ref, tril_ref, o_ref, s_ref, st_scr):
    tb = pl.program_id(1)

    @pl.when(tb == 0)
    def _():
        st_scr[...] = jnp.zeros_like(st_scr)

    n_tok = q_ref.shape[1]
    c_len = min(GLA_CHUNK, n_tok)
    tril = tril_ref[...]
    head_of_lane = lax.broadcasted_iota(jnp.int32, (1, GLA_K_W), 1) // GLA_DK
    ri = lax.broadcasted_iota(jnp.int32, (GLA_HEADS * c_len, c_len), 0) % c_len
    ci = lax.broadcasted_iota(jnp.int32, (GLA_HEADS * c_len, c_len), 1)
    causal = ri >= ci
    gain = gn_ref[...]
    n_seq = q_ref.shape[0]
    items = [(i, c) for i in range(n_seq) for c in range(n_tok // c_len)]
    rows = lambda c: slice(c * c_len, (c + 1) * c_len)
    b_all = []
    for i in range(n_seq):
        g_hi, g_lo = _split_bf16(la_ref[i])
        b_all.append(_dot(tril, g_hi) + _dot(tril, g_lo))
    qm, kdm, dec, a_raw = {}, {}, {}, {}
    for it in items:
        i, c = it
        b = b_all[i][rows(c), :]
        dec[it] = jnp.exp(b[c_len - 1:c_len, :])
        q_t = q_ref[i, rows(c), :] * jnp.exp(b)
        k_t = k_ref[i, rows(c), :] * jnp.exp(-b)
        qm[it] = _head_masked_stack(q_t, head_of_lane).astype(BF16)
        kdm[it] = _head_masked_stack(k_t * dec[it], head_of_lane).astype(BF16)
        a_raw[it] = _dot_tb(qm[it], k_t.astype(BF16))
    upd = {it: _dot_ta(_head_stack(v_ref[it[0], rows(it[1]), :], GLA_DV), kdm[it]) for it in items}
    inter = {}
    for i in range(n_seq):
        st = st_scr[i]
        for c in range(n_tok // c_len):
            inter[(i, c)] = _dot_tb(qm[(i, c)], st.astype(BF16))
            st = dec[(i, c)] * st + upd[(i, c)]
        st_scr[i] = st
    for it in items:
        i, c = it
        a = jnp.where(causal, a_raw[it], 0.0).astype(BF16)
        intra = jnp.concatenate(
            [_dot(a[h * c_len:(h + 1) * c_len, :], v_ref[i, rows(c), h * GLA_DV:(h + 1) * GLA_DV])
             for h in range(GLA_HEADS)], axis=0)
        y = _gla_out(inter[it] + intra, gain, _head_stack(gg_ref[i, rows(c), :], GLA_DV)).astype(BF16)
        for h in range(GLA_HEADS):
            o_ref[i, rows(c), h * GLA_DV:(h + 1) * GLA_DV] = y[h * c_len:(h + 1) * c_len, :]

    @pl.when(tb == pl.num_programs(1) - 1)
    def _():
        for i in range(q_ref.shape[0]):
            s_ref[i] = st_scr[i].T


def _gla_prompt(layer, q, k, la, v, gg, gain, tril, batch, seq):
    tb = tril.shape[0]
    per = min(GLA_SEQS, batch)
    blk = lambda b, t: (b, t, 0)
    kw = pl.BlockSpec((per, tb, GLA_K_W), blk)
    vw = pl.BlockSpec((per, tb, GLA_V_W), blk)
    return pl.pallas_call(
        _gla_prompt_kernel,
        grid=(batch // per, seq // tb),
        in_specs=[kw, kw, kw, vw, vw, _layer_resident(gain, layer), _resident(tril)],
        out_specs=[vw, pl.BlockSpec((per, GLA_K_W, GLA_DV), lambda b, t: (b, 0, 0))],
        out_shape=[
            jax.ShapeDtypeStruct((batch, seq, GLA_V_W), BF16),
            jax.ShapeDtypeStruct((batch, GLA_K_W, GLA_DV), F32),
        ],
        scratch_shapes=[pltpu.VMEM((per, GLA_DV, GLA_K_W), F32)],
        compiler_params=_params("parallel", "arbitrary"),
        name="gla_prompt",
    )(q, k, la, v, gg, gain, tril)


def _gla_decode_kernel(q_ref, k_ref, la_ref, v_ref, gg_ref, gn_ref, s_ref, *rest, steps, layer):
    o_ref, so_ref = rest[-2:]
    so_ref = _layer_view(so_ref, layer, len(rest) == 2)
    grp = s_ref.shape[0]
    rows = grp * steps
    stacked = GLA_HEADS * rows
    ri = lax.broadcasted_iota(jnp.int32, (rows, rows), 0)
    ci = lax.broadcasted_iota(jnp.int32, (rows, rows), 1)
    same_seq = ri // steps == ci // steps
    g_hi, g_lo = _split_bf16(la_ref[...])
    tril = (same_seq & (ri >= ci)).astype(BF16)
    total = same_seq.astype(BF16)
    b = _dot(tril, g_hi) + _dot(tril, g_lo)
    b_last = _dot(total, g_hi) + _dot(total, g_lo)
    head_of_lane = lax.broadcasted_iota(jnp.int32, (1, GLA_K_W), 1) // GLA_DK
    k_t = k_ref[...] * jnp.exp(-b)
    qm = _head_masked_stack(q_ref[...] * jnp.exp(b), head_of_lane).astype(BF16)
    km = _head_masked_stack(k_t, head_of_lane).astype(BF16)
    kdm = _head_masked_stack(k_t * jnp.exp(b_last), head_of_lane).astype(BF16)
    v_st = _head_stack(v_ref[...], GLA_DV)
    seq_of_row = (lax.broadcasted_iota(jnp.int32, (stacked, 1), 0) % rows) // steps
    seq_of_g = lax.broadcasted_iota(jnp.int32, (rows, 1), 0) // steps
    zero = jnp.zeros((), BF16)
    rhs = jnp.concatenate([
        jnp.concatenate([v_st, jnp.zeros((stacked, GLA_DV), BF16)], axis=1),
        jnp.concatenate([jnp.zeros((2 * rows, GLA_DV), BF16), jnp.ones((2 * rows, GLA_DV), BF16)], axis=1),
    ], axis=0)
    a_raw = _dot_tb(qm, km)
    states = [s_ref[s].reshape(GLA_K_W, GLA_DV) for s in range(grp)]
    inter_all = [_dot(qm, st.astype(BF16)) for st in states]
    upd_all = []
    for s in range(grp):
        lhs = jnp.concatenate([jnp.where(seq_of_row == s, kdm, zero),
                               jnp.where(seq_of_g == s, g_hi, zero),
                               jnp.where(seq_of_g == s, g_lo, zero)], axis=0)
        upd_all.append(_dot_ta(lhs, rhs))
    rr = lax.broadcasted_iota(jnp.int32, (stacked, stacked), 0) % rows
    cc = lax.broadcasted_iota(jnp.int32, (stacked, stacked), 1) % rows
    causal = (rr // steps == cc // steps) & (rr >= cc)
    o = _dot(jnp.where(causal, a_raw, 0.0).astype(BF16), v_st)
    inter = inter_all[0]
    for s in range(1, grp):
        inter = jnp.where(seq_of_row == s, inter_all[s], inter)
    y = _gla_out(o + inter, gn_ref[...], _head_stack(gg_ref[...], GLA_DV)).astype(BF16)
    for h in range(GLA_HEADS):
        o_ref[:, h * GLA_DV:(h + 1) * GLA_DV] = y[h * rows:(h + 1) * rows, :]
    for s in range(grp):
        new = jnp.exp(upd_all[s][:, GLA_DV:]) * states[s] + upd_all[s][:, :GLA_DV]
        so_ref[s] = new.reshape(GLA_HEADS, GLA_DK, GLA_DV)


def _gla_decode(layer, q, k, la, v, gg, gain, state, steps, prev):
    nseq = state.shape[1]
    grp = min(GLA_DEC_GROUP, nseq)
    rows = grp * steps
    row = lambda i: (i, 0)
    st_spec = pl.BlockSpec((None, grp, GLA_HEADS, GLA_DK, GLA_DV), lambda i: (layer, i, 0, 0, 0))
    extra_in, extra_specs, out_spec, stacked = _stacked_out(
        state.shape, (grp, GLA_HEADS, GLA_DK, GLA_DV), layer, prev)
    n_in = 7
    return pl.pallas_call(
        functools.partial(_gla_decode_kernel, steps=steps, layer=layer),
        grid=(nseq // grp,),
        in_specs=[
            pl.BlockSpec((rows, GLA_K_W), row),
            pl.BlockSpec((rows, GLA_K_W), row),
            pl.BlockSpec((rows, GLA_K_W), row),
            pl.BlockSpec((rows, GLA_V_W), row),
            pl.BlockSpec((rows, GLA_V_W), row),
            _layer_resident(gain, layer),
            st_spec,
        ] + extra_specs,
        out_specs=[pl.BlockSpec((rows, GLA_V_W), row), out_spec],
        out_shape=[jax.ShapeDtypeStruct((nseq * steps, GLA_V_W), BF16), stacked],
        input_output_aliases={n_in + i: 1 + i for i in range(len(extra_in))},
        compiler_params=_params("parallel"),
        name="gla_decode",
    )(q, k, la, v, gg, gain, state, *extra_in)


def _outproj_apply(x, a_ref, o_ref, wa_ref, wo_ref, g_ref, wq_ref, qn_ref):
    x1 = x + _dot(a_ref[...], wa_ref[...]) + _dot(o_ref[...], wo_ref[...])
    h = _rms(x1, g_ref[...]).astype(BF16)
    qn = qn_ref[...]
    q = [_rms(_dot(h, wq_ref[:, hd * XA_HEAD_DIM:(hd + 1) * XA_HEAD_DIM]), qn).astype(BF16)
         for hd in range(XA_HEADS)]
    return x1, q


def _outproj_kernel(x_ref, a_ref, o_ref, wa_ref, wo_ref, g_ref, wq_ref, qn_ref, x1_ref, q_ref):
    x1, q = _outproj_apply(x_ref[...], a_ref, o_ref, wa_ref, wo_ref, g_ref, wq_ref, qn_ref)
    x1_ref[...] = x1
    for hd in range(XA_HEADS):
        q_ref[:, hd * XA_HEAD_DIM:(hd + 1) * XA_HEAD_DIM] = q[hd]


def _outproj(x, layer, a, o, w_a, w_o, g, wq, qn):
    n, d = x.shape
    tm = min(ROW_TILE, n)
    row = lambda i: (i, 0)
    return pl.pallas_call(
        _outproj_kernel,
        grid=(n // tm,),
        in_specs=[
            pl.BlockSpec((tm, d), row),
            pl.BlockSpec((tm, SWA_Q_W), row),
            pl.BlockSpec((tm, GLA_V_W), row),
        ] + [_layer_resident(p, layer) for p in (w_a, w_o, g, wq, qn)],
        out_specs=[pl.BlockSpec((tm, d), row), pl.BlockSpec((tm, XA_W), row)],
        out_shape=[jax.ShapeDtypeStruct((n, d), F32), jax.ShapeDtypeStruct((n, XA_W), BF16)],
        compiler_params=_params("parallel"),
        name="outproj",
    )(x, a, o, w_a, w_o, g, wq, qn)


def _memkv_kernel(*refs, n_cast):
    m_ref, g_ref, wk_ref, wv_ref, kn_ref = refs[:5]
    k_ref, v_ref = refs[5 + n_cast:7 + n_cast]
    _cast_chunks(refs[5:5 + n_cast], refs[7 + n_cast:])
    m = _rms(m_ref[...], g_ref[...]).astype(BF16)
    kn = kn_ref[...]
    for hd in range(XA_HEADS):
        sl = slice(hd * XA_HEAD_DIM, (hd + 1) * XA_HEAD_DIM)
        k_ref[:, sl] = _rms(_dot(m, wk_ref[:, sl]), kn)
    v_ref[...] = _dot(m, wv_ref[...])


def _memkv(mem, g, wk, wv, kn, cast=()):
    depth = wk.shape[0]
    n, d = mem.shape
    tm = min(ROW_TILE, n)
    per_layer = lambda l, i: (l, 0, 0)
    out = lambda l, i: (l, i, 0)
    tiles = n // tm
    cast_in, cast_out, cast_shape = _cast_specs(cast, depth * tiles, lambda l, i: l * tiles + i)
    return pl.pallas_call(
        functools.partial(_memkv_kernel, n_cast=len(cast)),
        grid=(depth, tiles),
        in_specs=[
            pl.BlockSpec((tm, d), lambda l, i: (i, 0)),
            pl.BlockSpec((None, 1, d), per_layer),
            pl.BlockSpec((None, d, XA_W), per_layer),
            pl.BlockSpec((None, d, XA_W), per_layer),
            pl.BlockSpec((None, 1, XA_HEAD_DIM), per_layer),
        ] + cast_in,
        out_specs=[pl.BlockSpec((None, tm, XA_W), out), pl.BlockSpec((None, tm, XA_W), out)] + cast_out,
        out_shape=[jax.ShapeDtypeStruct((depth, n, XA_W), F32)] * 2 + cast_shape,
        compiler_params=_params("parallel", "parallel"),
        name="memkv",
    )(mem, g, wk, wv, kn, *[c[0] for c in cast])


def _mixout_prompt_kernel(*refs, n_cast):
    (x_ref, a_ref, o_ref, wa_ref, wo_ref, g_ref, wq_ref, qn_ref, mk_ref, mv_ref, xwo_ref, fg_ref, wg_ref, wu_ref,
     wd_ref) = refs[:15]
    out_ref, att_scr, a_scr = refs[15 + n_cast], refs[-2], refs[-1]
    _cast_chunks(refs[15:15 + n_cast], refs[16 + n_cast:-2])
    x1, q = _outproj_apply(x_ref[...], a_ref, o_ref, wa_ref, wo_ref, g_ref, wq_ref, qn_ref)
    mk = mk_ref[...].astype(BF16)
    mv = mv_ref[...].astype(BF16)
    heads = range(XA_HEADS)
    cols = lambda hd: slice(hd * XA_HEAD_DIM, (hd + 1) * XA_HEAD_DIM)
    scores = [_dot_tb(q[hd], mk[:, cols(hd)]) * (XA_HEAD_DIM ** -0.5 * LOG2E) for hd in heads]
    probs = []
    for s in scores:
        e = jnp.exp2(s - jnp.max(s, axis=-1, keepdims=True))
        probs.append((e.astype(BF16), jnp.sum(e, axis=-1, keepdims=True)))
    for hd in heads:
        att_scr[:, cols(hd)] = (_dot(probs[hd][0], mv[:, cols(hd)]) / probs[hd][1]).astype(BF16)
    x2 = x1 + _dot(att_scr[...], xwo_ref[...])
    out_ref[...] = _ffn_apply(x2, fg_ref, wg_ref, wu_ref, wd_ref, a_scr)


def _mixout_prompt(x, layer, a, o, outproj, mk, mv, wo, ffn, seq, cast=()):
    n, d = x.shape
    dff = ffn[1].shape[2]
    tm = min(ROW_TILE, seq)
    per_seq = seq // tm
    row = lambda i: (i, 0)
    mem_spec = pl.BlockSpec((None, None, mk.shape[2], XA_W), lambda i: (layer, i // per_seq, 0, 0))
    cast_in, cast_out, cast_shape = _cast_specs(cast, n // tm)
    return pl.pallas_call(
        functools.partial(_mixout_prompt_kernel, n_cast=len(cast)),
        grid=(n // tm,),
        in_specs=([pl.BlockSpec((tm, d), row), pl.BlockSpec((tm, SWA_Q_W), row), pl.BlockSpec((tm, GLA_V_W), row)]
                  + [_layer_resident(p, layer) for p in outproj]
                  + [mem_spec, mem_spec]
                  + [_layer_resident(p, layer) for p in (wo,) + ffn]
                  + cast_in),
        out_specs=[pl.BlockSpec((tm, d), row)] + cast_out,
        out_shape=[jax.ShapeDtypeStruct((n, d), F32)] + cast_shape,
        scratch_shapes=[pltpu.VMEM((tm, XA_W), BF16), pltpu.VMEM((tm, dff), BF16)],
        compiler_params=_params("parallel"),
        name="mixout_prompt",
    )(x, a, o, *outproj, mk, mv, wo, *ffn, *[c[0] for c in cast])


def _xattn_decode_kernel(q_ref, mk_ref, mv_ref, o_ref, *, steps):
    grp, nkeys = mk_ref.shape[0], mk_ref.shape[1]
    rows = grp * steps
    q = jnp.concatenate([q_ref[:, hd * XA_HEAD_DIM:(hd + 1) * XA_HEAD_DIM] for hd in range(XA_HEADS)], axis=0)
    r = lax.broadcasted_iota(jnp.int32, (XA_HEADS * rows, 1), 0)
    own = (r % rows) // steps
    same_head = (r // rows) == (lax.broadcasted_iota(jnp.int32, (1, nkeys), 1) % XA_HEADS)
    s = None
    for j in range(grp):
        sj = _dot_tb(q, mk_ref[j].astype(BF16))
        s = sj if s is None else jnp.where(own == j, sj, s)
    s = jnp.where(same_head, s * (XA_HEAD_DIM ** -0.5 * LOG2E), -jnp.inf)
    m = jnp.max(s, axis=-1, keepdims=True)
    e = jnp.exp2(s - m)
    p = e.astype(BF16)
    o = None
    for j in range(grp):
        oj = _dot(p, mv_ref[j].astype(BF16))
        o = oj if o is None else jnp.where(own == j, oj, o)
    o = o / jnp.sum(e, axis=-1, keepdims=True)
    for hd in range(XA_HEADS):
        o_ref[:, hd * XA_HEAD_DIM:(hd + 1) * XA_HEAD_DIM] = o[hd * rows:(hd + 1) * rows, :].astype(BF16)


def _xattn_decode(q, layer, mk, mv, steps):
    nseq, nkeys = mk.shape[1], mk.shape[2]
    grp = XA_DEC_GROUP
    rows = grp * steps
    row = lambda i: (i, 0)
    mem_spec = pl.BlockSpec((None, grp, nkeys, XA_HEAD_DIM), lambda i: (layer, i, 0, 0))
    return pl.pallas_call(
        functools.partial(_xattn_decode_kernel, steps=steps),
        grid=(nseq // grp,),
        in_specs=[pl.BlockSpec((rows, XA_W), row), mem_spec, mem_spec],
        out_specs=pl.BlockSpec((rows, XA_W), row),
        out_shape=jax.ShapeDtypeStruct((nseq * steps, XA_W), BF16),
        compiler_params=_params("parallel"),
        name="xattn_decode",
    )(q, mk, mv)


def _proj_ffn_kernel(x_ref, a_ref, w_ref, g_ref, wg_ref, wu_ref, wd_ref, o_ref, a_scr):
    x = x_ref[...] + _dot(a_ref[...], w_ref[...])
    o_ref[...] = _ffn_apply(x, g_ref, wg_ref, wu_ref, wd_ref, a_scr)


def _proj_ffn(x, layer, a, w, ffn):
    n, d = x.shape
    dff = ffn[1].shape[2]
    tm = min(ROW_TILE, n)
    row = lambda i: (i, 0)
    return pl.pallas_call(
        _proj_ffn_kernel,
        grid=(n // tm,),
        in_specs=([pl.BlockSpec((tm, d), row), pl.BlockSpec((tm, a.shape[1]), row)]
                  + [_layer_resident(p, layer) for p in (w,) + ffn]),
        out_specs=pl.BlockSpec((tm, d), row),
        out_shape=jax.ShapeDtypeStruct((n, d), F32),
        scratch_shapes=[pltpu.VMEM((tm, dff), BF16)],
        compiler_params=_params("parallel"),
        name="proj_ffn",
    )(x, a, w, *ffn)


def _rope_tables(pos):
    half = HEAD_DIM // 2
    inv = ROPE_THETA ** (-jnp.arange(half, dtype=F32) / half)
    ang = pos.astype(F32)[:, None] * inv[None, :]
    cos, sin = jnp.cos(ang), jnp.sin(ang)
    reps = LANES // HEAD_DIM
    return jnp.tile(cos, (1, 2 * reps)), jnp.tile(jnp.concatenate([-sin, sin], axis=-1), (1, reps))


def _block_tril(n_blocks, size):
    i = jnp.arange(n_blocks * size)
    return ((i[:, None] // size == i[None, :] // size) & (i[:, None] >= i[None, :])).astype(BF16)


def _permute_heads(w, axis):
    blocks = jnp.split(w, SWA_Q_HEADS, axis=axis)
    return jnp.concatenate([blocks[h] for h in SWA_HEAD_ORDER], axis=axis)


def kernel(x_prompt, x_sample, cache_swa_k, cache_swa_v, state_gla, cache_mem_k, cache_mem_v, mem_prompt, ffn1_norm, ffn1_wg, ffn1_wu, ffn1_wd, mix_norm, w_in, swa_q_norm, swa_k_norm, swa_sinks, gla_w_gate, gla_b_gate, gla_out_norm, w_out, xa_norm, mem_norm, xa_wq, xa_wk, xa_wv, xa_q_norm, xa_k_norm, xa_wo, ffn2_norm, ffn2_wg, ffn2_wu, ffn2_wd):
    batch, seq, d = x_prompt.shape
    nseq, steps, _ = x_sample.shape
    depth = w_in.shape[0]
    mem_len = mem_prompt.shape[1]

    bf = lambda w: w.astype(BF16)
    vec = lambda p: p[:, None, :]
    first_cast = ([(w, 0) for w in (ffn1_wg, ffn1_wu, ffn1_wd)]
                  + [(w_in, l, w_in.shape[2] + MXU_TILE - GLA_LOWRANK) for l in range(depth)])
    gate_w = jnp.pad(bf(gla_w_gate), ((0, 0), (0, LANES - GLA_LOWRANK), (0, 0)))
    w_a = bf(_permute_heads(w_out[:, :SWA_Q_W], 1))
    w_o = bf(w_out[:, SWA_Q_W:])
    wq_b, wk_b, wv_b, wo_b = bf(xa_wq), bf(xa_wk), bf(xa_wv), bf(xa_wo)
    qn = vec(jnp.tile(swa_q_norm * (HEAD_DIM ** -0.5 * LOG2E), (1, LANES // HEAD_DIM)))
    kn = vec(jnp.tile(swa_k_norm, (1, LANES // HEAD_DIM)))
    inproj_params = lambda l: (vec(mix_norm), w_all[l], gate_w, vec(gla_b_gate), qn, kn)
    outproj_params = (w_a, w_o, vec(xa_norm), wq_b, vec(xa_q_norm))
    gla_gain = vec(gla_out_norm)

    lane = jnp.arange(MXU_TILE)
    bd = (lane[:, None] // HEAD_DIM == lane[None, :] // HEAD_DIM).astype(BF16)
    cos_p, sin_p = _rope_tables(jnp.arange(seq))
    cos_s, sin_s = _rope_tables(PAST_LEN + jnp.arange(nseq * steps) % steps)
    tril_p = _block_tril(min(GLA_BLOCK, seq) // min(GLA_CHUNK, seq), min(GLA_CHUNK, seq))

    mk_p, mv_p, *cast = _memkv(mem_prompt.reshape(batch * mem_len, d), vec(mem_norm), wk_b, wv_b,
                               vec(xa_k_norm), first_cast)
    ffn1_w = [cast[:3]] + [None] * (depth - 1)
    w_all = cast[3:]
    mk_p = mk_p.reshape(depth, batch, mem_len, XA_W)
    mv_p = mv_p.reshape(depth, batch, mem_len, XA_W)
    mk_s = cache_mem_k.reshape(depth, nseq, mem_len * XA_HEADS, XA_HEAD_DIM)
    mv_s = cache_mem_v.reshape(depth, nseq, mem_len * XA_HEADS, XA_HEAD_DIM)
    native = lambda c: jnp.transpose(c, (0, 1, 3, 4, 2)).reshape(depth, nseq, SWA_KV_W, WINDOW)
    kc_s, vc_s = native(cache_swa_k), native(cache_swa_v)

    xp = x_prompt.reshape(batch * seq, d)
    xs = x_sample.reshape(nseq * steps, d)
    kp_l, vp_l, sp_l = [], [], []
    swa_new = gla_new = None
    for l in range(depth):
        ffn1 = (vec(ffn1_norm),) + tuple(ffn1_w[l])
        xp, q_s, k_s, v_s, q_g, k_g, v_g, g_g, la, *ffn2_w = _ffn_inproj(
            xp, l, ffn1, inproj_params(l), cos_p, sin_p, bd, cast=[(w, l) for w in (ffn2_wg, ffn2_wu, ffn2_wd)])
        ffn2 = (vec(ffn2_norm),) + tuple(ffn2_w)
        a_p = _swa_prompt(swa_sinks, l, q_s, k_s, v_s, batch, seq)
        seqs = lambda a: a.reshape(batch, seq, a.shape[-1])
        o_p, s_p = _gla_prompt(l, seqs(q_g), seqs(k_g), seqs(la), seqs(v_g), seqs(g_g), gla_gain, tril_p,
                               batch, seq)
        o_p = o_p.reshape(batch * seq, GLA_V_W)
        last = lambda a: seqs(a)[:, seq - WINDOW:].reshape(batch, WINDOW, SWA_KV_HEADS, HEAD_DIM)
        kp_l.append(last(k_s))
        vp_l.append(last(v_s))
        sp_l.append(s_p.reshape(batch, GLA_HEADS, GLA_DK, GLA_DV))
        next_ffn1 = [(w, l + 1) for w in (ffn1_wg, ffn1_wu, ffn1_wd)] if l + 1 < depth else []
        xp, *cast = _mixout_prompt(xp, l, a_p, o_p, outproj_params, mk_p, mv_p, wo_b, ffn2, seq, cast=next_ffn1)
        if next_ffn1:
            ffn1_w[l + 1] = cast

        xs, q_s, k_s, v_s, q_g, k_g, v_g, g_g, la, k_t, v_t = _ffn_inproj(
            xs, l, ffn1, inproj_params(l), cos_s, sin_s, bd, transposed_kv=True)
        a_s, *swa_new = _swa_decode(swa_sinks, l, q_s, k_s, v_s, k_t, v_t, kc_s, vc_s, steps, swa_new)
        o_s, *gla_new = _gla_decode(l, q_g, k_g, la, v_g, g_g, gla_gain, state_gla, steps, gla_new)
        xs, q_x = _outproj(xs, l, a_s, o_s, *outproj_params)
        xs = _proj_ffn(xs, l, _xattn_decode(q_x, l, mk_s, mv_s, steps), wo_b, ffn2)

    unnative = lambda c: jnp.transpose(c.reshape(depth, nseq, SWA_KV_HEADS, HEAD_DIM, WINDOW), (0, 1, 4, 2, 3))
    return (xp.reshape(batch, seq, d), xs.reshape(nseq, steps, d),
            jnp.stack(kp_l), jnp.stack(vp_l), jnp.stack(sp_l),
            mk_p.reshape(depth, batch, mem_len, XA_HEADS, XA_HEAD_DIM),
            mv_p.reshape(depth, batch, mem_len, XA_HEADS, XA_HEAD_DIM),
            unnative(swa_new[0]), unnative(swa_new[1]), gla_new[0])
```

```python
import functools

import jax
import jax.numpy as jnp
from jax import lax
from jax.experimental import pallas as pl
from jax.experimental.pallas import tpu as pltpu

F32 = jnp.float32
BF16 = jnp.bfloat16

EPS = 1e-6
LOG2E = 1.4426950408889634
PAST_LEN = 16384
WINDOW = 128
ROPE_THETA = 10000.0
HEAD_DIM = 64
SWA_Q_HEADS = 8
SWA_KV_HEADS = 2
SWA_GROUP = SWA_Q_HEADS // SWA_KV_HEADS
GLA_HEADS = 4
GLA_DK = 64
GLA_DV = 128
GLA_LOWRANK = 16
GLA_GATE_TEMP = 16.0
GLA_CHUNK = 64
XA_HEADS = 4
XA_HEAD_DIM = 128

SWA_Q_W = SWA_Q_HEADS * HEAD_DIM
SWA_KV_W = SWA_KV_HEADS * HEAD_DIM
GLA_K_W = GLA_HEADS * GLA_DK
GLA_V_W = GLA_HEADS * GLA_DV
XA_W = XA_HEADS * XA_HEAD_DIM
MAIN_W = SWA_Q_W + 2 * SWA_KV_W + 2 * GLA_K_W + 2 * GLA_V_W

LANES = 128
BF16_SUBLANES = 16
MXU_TILE = 256
VMEM_LIMIT = 56 * 1024 * 1024

ROW_TILE = 512
ROW_SPLIT = 2
FFN_CHUNK = 256
SWA_BLOCK = 1024
GLA_BLOCK = 256
GLA_SEQS = 4
DEC_GROUP = 4
SWA_DEC_SUBGROUPS = 8
XA_DEC_GROUP = 8
GLA_DEC_GROUP = 8

assert SWA_KV_HEADS * HEAD_DIM == LANES
SWA_HEAD_ORDER = tuple(kv * SWA_GROUP + g for g in range(SWA_GROUP) for kv in range(SWA_KV_HEADS))


def _dot(a, b):
    return jnp.dot(a, b, preferred_element_type=F32)


def _dot_tb(a, b):
    return lax.dot_general(a, b, (((1,), (1,)), ((), ())), preferred_element_type=F32)


def _dot_ta(a, b):
    return lax.dot_general(a, b, (((0,), (0,)), ((), ())), preferred_element_type=F32)


def _split_bf16(x):
    hi = x.astype(BF16)
    lo = (x - hi.astype(F32)).astype(BF16)
    return hi, lo


def _rms(x, g):
    ms = jnp.mean(x * x, axis=-1, keepdims=True)
    return x * lax.rsqrt(ms + EPS) * g


def _params(*sem):
    return pltpu.CompilerParams(dimension_semantics=sem, vmem_limit_bytes=VMEM_LIMIT)


def _resident(arr):
    nd = arr.ndim
    return pl.BlockSpec(arr.shape, lambda *_: (0,) * nd, pipeline_mode=pl.Buffered(1))


def _layer_resident(arr, layer):
    nd = arr.ndim
    index = (layer if arr.shape[0] > 1 else 0,) + (0,) * (nd - 1)
    return pl.BlockSpec((None,) + arr.shape[1:], lambda *_: index, pipeline_mode=pl.Buffered(1))


def _cast_specs(cast, n_steps, step_of=lambda i: i):
    ins, outs, shapes = [], [], []
    for arr, src_layer, *padded in cast:
        _, rows, cols = arr.shape
        out_cols = padded[0] if padded else cols
        share = 1
        while rows % (n_steps // share) or (rows * share // n_steps) % BF16_SUBLANES:
            share *= 2
            assert n_steps % share == 0
        chunk = rows * share // n_steps
        ins.append(pl.BlockSpec((None, chunk, cols), lambda *g, l=src_layer, s=share: (l, step_of(*g) // s, 0)))
        outs.append(pl.BlockSpec((1, chunk, out_cols), lambda *g, s=share: (0, step_of(*g) // s, 0)))
        shapes.append(jax.ShapeDtypeStruct((1, rows, out_cols), BF16))
    return ins, outs, shapes


def _cast_chunks(srcs, dsts):
    for src, dst in zip(srcs, dsts):
        cols = src.shape[-1]
        dst[0, :, :cols] = src[...].astype(BF16)
        if dst.shape[-1] > cols:
            dst[0, :, cols:] = jnp.zeros((dst.shape[1], dst.shape[-1] - cols), BF16)


def _ffn_hidden(x, g_ref, wg_ref, wu_ref, a_scr):
    h = _rms(x, g_ref[...]).astype(BF16)
    dff = wg_ref.shape[1]
    for c in range(dff // FFN_CHUNK):
        sl = slice(c * FFN_CHUNK, (c + 1) * FFN_CHUNK)
        g = _dot(h, wg_ref[:, sl])
        u = _dot(h, wu_ref[:, sl])
        a_scr[:, sl] = (g * jax.nn.sigmoid(g) * u).astype(BF16)


def _ffn_apply(x, g_ref, wg_ref, wu_ref, wd_ref, a_scr):
    _ffn_hidden(x, g_ref, wg_ref, wu_ref, a_scr)
    return x + 0.5 * _dot(a_scr[...], wd_ref[...])


def _ffn_inproj_kernel(*refs, n_cast, transposed_kv):
    (x_ref, fg_ref, wg_ref, wu_ref, wd_ref, g_ref, w_ref, gw_ref, gb_ref, qn_ref, kn_ref, cos_ref, sin_ref,
     bd_ref) = refs[:14]
    outs, a_scr = refs[14 + n_cast:-1], refs[-1]
    xo_ref, qs_ref, ks_ref, vs_ref, qg_ref, kg_ref, vg_ref, gg_ref, la_ref = outs[:9]
    kt_ref, vt_ref = outs[9:11] if transposed_kv else (None, None)
    _cast_chunks(refs[14:14 + n_cast], outs[len(outs) - n_cast:])
    x = x_ref[...]
    _ffn_hidden(x, fg_ref, wg_ref, wu_ref, a_scr)
    n = x.shape[0] // ROW_SPLIT
    groups = [slice(r * n, (r + 1) * n) for r in range(ROW_SPLIT)]
    mid = [x[rs] + 0.5 * _dot(a_scr[rs, :], wd_ref[...]) for rs in groups]
    for rs, xm in zip(groups, mid):
        xo_ref[rs, :] = xm
    split = SWA_Q_W + 2 * SWA_KV_W + 2 * GLA_K_W
    h = [_rms(xm, g_ref[...]).astype(BF16) for xm in mid]
    z1 = [_dot(hr, w_ref[:, :split]) for hr in h]
    z2 = []
    for rs, hr, z in zip(groups, h, z1):
        _inproj_finish_swa(z, rs, qn_ref, kn_ref, cos_ref, sin_ref, bd_ref, qs_ref, ks_ref, vs_ref, qg_ref,
                           kg_ref, kt_ref, vt_ref)
        z2.append(_dot(hr, w_ref[:, split:]))
    for rs, z in zip(groups, z2):
        _inproj_finish_gla(z, rs, gw_ref, gb_ref, vg_ref, gg_ref, la_ref)


def _inproj_finish_swa(z1, rs, qn_ref, kn_ref, cos_ref, sin_ref, bd_ref, qs_ref, ks_ref, vs_ref, qg_ref,
                       kg_ref, kt_ref, vt_ref):
    bd = bd_ref[...]
    cos = cos_ref[rs, :]
    sin = sin_ref[rs, :]
    lane = lax.broadcasted_iota(jnp.int32, cos.shape, 1)
    lane_lo = (lane % HEAD_DIM) < (HEAD_DIM // 2)

    def head_scale(z):
        ss = _dot((z * z).astype(BF16), bd)
        return lax.rsqrt(ss * (1.0 / HEAD_DIM) + EPS)

    def rope(y):
        swapped = jnp.where(lane_lo, pltpu.roll(y, LANES - HEAD_DIM // 2, axis=1),
                            pltpu.roll(y, HEAD_DIM // 2, axis=1))
        return y * cos + swapped * sin

    qn = qn_ref[...]
    nat = []
    for t in range(SWA_Q_W // MXU_TILE):
        z = z1[:, t * MXU_TILE:(t + 1) * MXU_TILE]
        y = z * head_scale(z)
        nat += [rope(y[:, c * LANES:(c + 1) * LANES] * qn) for c in range(MXU_TILE // LANES)]
    low = _low_half()
    per_col = LANES // HEAD_DIM
    for c in range(SWA_Q_W // LANES):
        halves = []
        for half, head in enumerate(SWA_HEAD_ORDER[per_col * c:per_col * (c + 1)]):
            col = nat[head // per_col]
            halves.append(col if head % per_col == half else pltpu.roll(col, HEAD_DIM, axis=1))
        qs_ref[rs, c * LANES:(c + 1) * LANES] = jnp.where(low, halves[0], halves[1]).astype(BF16)
    o = SWA_Q_W
    z = z1[:, o:o + 2 * SWA_KV_W]
    k = rope(z[:, :SWA_KV_W] * head_scale(z)[:, :SWA_KV_W] * kn_ref[...])
    ks_ref[rs, :] = k
    vs_ref[rs, :] = z[:, SWA_KV_W:]
    if kt_ref is not None:
        kt_ref[:, rs] = k.T
        vt_ref[:, rs] = z[:, SWA_KV_W:].T
    o += 2 * SWA_KV_W
    qg_ref[rs, :] = z1[:, o:o + GLA_K_W] * (GLA_DK ** -0.5)
    o += GLA_K_W
    kg_ref[rs, :] = z1[:, o:o + GLA_K_W]


def _inproj_finish_gla(z2, rs, gw_ref, gb_ref, vg_ref, gg_ref, la_ref):
    vg_ref[rs, :] = z2[:, :GLA_V_W].astype(BF16)
    gg_ref[rs, :] = z2[:, GLA_V_W:2 * GLA_V_W]
    lr = z2[:, 2 * GLA_V_W:2 * GLA_V_W + LANES].astype(BF16)
    t = _dot(lr, gw_ref[...]) + gb_ref[...]
    log_sig = jnp.minimum(t, 0.0) - jnp.log(1.0 + jnp.exp(-jnp.abs(t)))
    la_ref[rs, :] = log_sig * (1.0 / GLA_GATE_TEMP)


def _ffn_inproj(x, layer, ffn, inproj, cos, sin, bd, transposed_kv=False, cast=()):
    n, d = x.shape
    dff = ffn[1].shape[2]
    tm = min(ROW_TILE, n)
    pos_blocks = cos.shape[0] // tm
    row = lambda i: (i, 0)
    pos = lambda i: (i % pos_blocks, 0)
    widths = (d, SWA_Q_W, SWA_KV_W, SWA_KV_W, GLA_K_W, GLA_K_W, GLA_V_W, GLA_V_W, GLA_K_W)
    dtypes = (F32, BF16, F32, F32, F32, F32, BF16, F32, F32)
    out_specs = [pl.BlockSpec((tm, w), row) for w in widths]
    out_shape = [jax.ShapeDtypeStruct((n, w), dt) for w, dt in zip(widths, dtypes)]
    if transposed_kv:
        out_specs += [pl.BlockSpec((SWA_KV_W, tm), lambda i: (0, i))] * 2
        out_shape += [jax.ShapeDtypeStruct((SWA_KV_W, n), F32)] * 2
    cast_in, cast_out, cast_shape = _cast_specs(cast, n // tm)
    return pl.pallas_call(
        functools.partial(_ffn_inproj_kernel, n_cast=len(cast), transposed_kv=transposed_kv),
        grid=(n // tm,),
        in_specs=([pl.BlockSpec((tm, d), row)]
                  + [_layer_resident(a, layer) for a in ffn + inproj]
                  + [pl.BlockSpec((tm, LANES), pos), pl.BlockSpec((tm, LANES), pos), _resident(bd)]
                  + cast_in),
        out_specs=out_specs + cast_out,
        out_shape=out_shape + cast_shape,
        scratch_shapes=[pltpu.VMEM((tm, dff), BF16)],
        compiler_params=_params("parallel"),
        name="ffn_inproj",
    )(x, *ffn, *inproj, cos, sin, bd, *[c[0] for c in cast])


def _swa_attend(q, k, v, valid, sinks):
    return _swa_out(*_swa_probs(_swa_scores(q, k), valid, sinks), v)


def _low_half():
    return lax.broadcasted_iota(jnp.int32, (1, LANES), 1) < HEAD_DIM


def _swa_head_rows(q):
    low = _low_half()
    zero = jnp.zeros((), q.dtype)
    pieces = []
    for c in range(SWA_Q_W // LANES):
        qc = q[:, c * LANES:(c + 1) * LANES]
        pieces += [jnp.where(low, qc, zero), jnp.where(low, zero, qc)]
    return jnp.concatenate(pieces, axis=0)


def _swa_scores(q, k):
    return _dot_tb(_swa_head_rows(q), k)


def _swa_probs(s, valid, sinks):
    rows, keys = valid.shape
    s = jnp.where(valid[None], s.reshape(SWA_Q_HEADS, rows, keys), -jnp.inf)
    m = jnp.maximum(jnp.max(s, axis=-1, keepdims=True), sinks)
    e = jnp.exp2(s - m)
    den = jnp.sum(e, axis=-1, keepdims=True) + jnp.exp2(sinks - m)
    return e.astype(BF16).reshape(SWA_Q_HEADS * rows, keys), den


def _swa_out(e, den, v):
    return _swa_finish(_dot(e, v), den)


def _swa_finish(o, den):
    low = _low_half()
    o = o.reshape(den.shape[0], den.shape[1], LANES) / den
    return [jnp.where(low, o[2 * c], o[2 * c + 1]) for c in range(SWA_Q_W // LANES)]


def _sink_column(sink_ref, layer):
    idx = lax.broadcasted_iota(jnp.int32, (SWA_Q_HEADS, 1, 1), 0)
    col = jnp.zeros((SWA_Q_HEADS, 1, 1), F32)
    for p, head in enumerate(SWA_HEAD_ORDER):
        col = jnp.where(idx == p, sink_ref[layer, head] * LOG2E, col)
    return col


def _swa_prompt_kernel(sink_ref, q_ref, kp_ref, kc_ref, vp_ref, vc_ref, o_ref, *, layer):
    j = pl.program_id(1)
    kall = jnp.concatenate([kp_ref[...], kc_ref[...]], axis=0).astype(BF16)
    vall = jnp.concatenate([vp_ref[...], vc_ref[...]], axis=0).astype(BF16)
    sinks = _sink_column(sink_ref, layer)
    upper = (lax.broadcasted_iota(jnp.int32, (WINDOW, WINDOW), 1)
             > lax.broadcasted_iota(jnp.int32, (WINDOW, WINDOW), 0))[None]

    def banded_probs(s, has_prev):
        s = s.reshape(SWA_Q_HEADS, WINDOW, 2 * WINDOW)
        prev = s[:, :, :WINDOW]
        if has_prev is not True:
            prev = jnp.where(has_prev, prev, -jnp.inf)
        logits = jnp.where(upper, prev, s[:, :, WINDOW:])
        m = jnp.maximum(jnp.max(logits, axis=-1, keepdims=True), sinks)
        e = jnp.exp2(logits - m)
        den = jnp.sum(e, axis=-1, keepdims=True) + jnp.exp2(sinks - m)
        e = e.astype(BF16)
        zero = jnp.zeros((), BF16)
        e = jnp.concatenate([jnp.where(upper, e, zero), jnp.where(upper, zero, e)], axis=-1)
        return e.reshape(SWA_Q_HEADS * WINDOW, 2 * WINDOW), den

    windows = range(q_ref.shape[0] // WINDOW)
    rows = lambda w: slice(w * WINDOW, (w + 1) * WINDOW)
    keys = lambda w: slice(w * WINDOW, (w + 2) * WINDOW)
    scores = [_swa_scores(q_ref[rows(w), :], kall[keys(w)]) for w in windows]
    probs = [banded_probs(scores[w], True if w > 0 else j > 0) for w in windows]
    for w in windows:
        for c, col in enumerate(_swa_out(*probs[w], vall[keys(w)])):
            o_ref[rows(w), c * LANES:(c + 1) * LANES] = col.astype(BF16)


def _swa_prompt(sinks, layer, q, k, v, batch, seq):
    qb = min(SWA_BLOCK, seq)
    nb = seq // qb
    per = qb // WINDOW
    cur = lambda b, j: (b * nb + j, 0)
    prev = lambda b, j: ((b * nb + j) * per - jnp.minimum(j, 1), 0)
    return pl.pallas_call(
        functools.partial(_swa_prompt_kernel, layer=layer),
        grid=(batch, nb),
        in_specs=[
            pl.BlockSpec(memory_space=pltpu.SMEM),
            pl.BlockSpec((qb, SWA_Q_W), cur),
            pl.BlockSpec((WINDOW, SWA_KV_W), prev),
            pl.BlockSpec((qb, SWA_KV_W), cur),
            pl.BlockSpec((WINDOW, SWA_KV_W), prev),
            pl.BlockSpec((qb, SWA_KV_W), cur),
        ],
        out_specs=pl.BlockSpec((qb, SWA_Q_W), cur),
        out_shape=jax.ShapeDtypeStruct((batch * seq, SWA_Q_W), BF16),
        compiler_params=_params("parallel", "parallel"),
        name="swa_prompt",
    )(sinks, q, k, k, v, v)


def _layer_view(ref, layer, first):
    if not first:
        return ref
    for other in range(ref.shape[0]):
        if other != layer:
            ref[other] = jnp.zeros(ref.shape[1:], ref.dtype)
    return ref.at[layer]


def _swa_decode_kernel(sink_ref, q_ref, kn_ref, vn_ref, knt_ref, vnt_ref, kc_ref, vc_ref, *rest, steps, layer):
    o_ref, ko_ref, vo_ref = rest[-3:]
    first = len(rest) == 3
    ko_ref = _layer_view(ko_ref, layer, first)
    vo_ref = _layer_view(vo_ref, layer, first)
    n_seq = kc_ref.shape[0]
    grp = DEC_GROUP
    rows = grp * steps
    keep = WINDOW - steps
    lane = lax.broadcasted_iota(jnp.int32, (1, WINDOW), 1)
    for s in range(n_seq):
        tile = slice((s * steps) // LANES * LANES, (s * steps) // LANES * LANES + LANES)
        shift = (keep - s * steps) % LANES
        for cache_ref, new_ref, out_ref in ((kc_ref, knt_ref, ko_ref), (vc_ref, vnt_ref, vo_ref)):
            out_ref[s] = jnp.where(lane >= keep, pltpu.roll(new_ref[:, tile], shift, axis=1),
                                   pltpu.roll(cache_ref[s], keep, axis=1))
    nk = grp * WINDOW + rows
    r = lax.broadcasted_iota(jnp.int32, (rows, nk), 0)
    q_seq, q_step = r // steps, r % steps
    c = lax.broadcasted_iota(jnp.int32, (rows, nk), 1)
    is_new = c >= grp * WINDOW
    cn = c - grp * WINDOW
    k_seq = jnp.where(is_new, cn // steps, c // WINDOW)
    k_idx = jnp.where(is_new, WINDOW + cn % steps, c % WINDOW)
    rel = WINDOW + q_step - k_idx
    valid = (q_seq == k_seq) & (rel >= 0) & (rel < WINDOW)
    sinks = _sink_column(sink_ref, layer)
    groups = range(n_seq // grp)
    new_rows = lambda g: slice(g * rows, (g + 1) * rows)
    cached = lambda ref, g: jnp.concatenate([ref[g * grp + s] for s in range(grp)], axis=1).astype(BF16)
    scores = []
    for g in groups:
        qm = _swa_head_rows(q_ref[new_rows(g), :])
        scores.append(jnp.concatenate(
            [_dot(qm, cached(kc_ref, g)), _dot_tb(qm, kn_ref[new_rows(g), :].astype(BF16))], axis=1))
    probs = [_swa_probs(scores[g], valid, sinks) for g in groups]
    for g in groups:
        e, den = probs[g]
        o = (_dot_tb(e[:, :grp * WINDOW], cached(vc_ref, g))
             + _dot(e[:, grp * WINDOW:], vn_ref[new_rows(g), :].astype(BF16)))
        for c, col in enumerate(_swa_finish(o, den)):
            o_ref[new_rows(g), c * LANES:(c + 1) * LANES] = col.astype(BF16)


def _stacked_out(shape, block, layer, prev):
    tail = (0,) * (len(block) - 1)
    if prev is None:
        spec = pl.BlockSpec((shape[0],) + block, lambda i: (0, i) + tail)
        extra_inputs = []
    else:
        spec = pl.BlockSpec((None,) + block, lambda i: (layer, i) + tail)
        extra_inputs = list(prev)
    extra_specs = [pl.BlockSpec(memory_space=pl.ANY) for _ in extra_inputs]
    return extra_inputs, extra_specs, spec, jax.ShapeDtypeStruct(shape, F32)


def _swa_decode(sinks, layer, q, kn, vn, knt, vnt, kc, vc, steps, prev):
    depth, nseq = kc.shape[:2]
    grp = min(DEC_GROUP * SWA_DEC_SUBGROUPS, nseq)
    rows = grp * steps
    assert rows % LANES == 0, "a grid step's new keys must fill whole 128-lane tiles"
    row = lambda i: (i, 0)
    cache_spec = pl.BlockSpec((None, grp, SWA_KV_W, WINDOW), lambda i: (layer, i, 0, 0))
    new_t = pl.BlockSpec((SWA_KV_W, rows), lambda i: (0, i))
    extra_in, extra_specs, out_spec, stacked = _stacked_out(kc.shape, (grp, SWA_KV_W, WINDOW), layer, prev)
    n_in = 8
    return pl.pallas_call(
        functools.partial(_swa_decode_kernel, steps=steps, layer=layer),
        grid=(nseq // grp,),
        in_specs=[
            pl.BlockSpec(memory_space=pltpu.SMEM),
            pl.BlockSpec((rows, SWA_Q_W), row),
            pl.BlockSpec((rows, SWA_KV_W), row),
            pl.BlockSpec((rows, SWA_KV_W), row),
            new_t,
            new_t,
            cache_spec,
            cache_spec,
        ] + extra_specs,
        out_specs=[pl.BlockSpec((rows, SWA_Q_W), row), out_spec, out_spec],
        out_shape=[jax.ShapeDtypeStruct((nseq * steps, SWA_Q_W), BF16), stacked, stacked],
        input_output_aliases={n_in + i: 1 + i for i in range(len(extra_in))},
        compiler_params=_params("parallel"),
        name="swa_decode",
    )(sinks, q, kn, vn, knt, vnt, kc, vc, *extra_in)


def _gla_out(o, gain, gate):
    return _rms(o, gain) * (gate * jax.nn.sigmoid(gate))


def _head_stack(x, width):
    return jnp.concatenate([x[:, h * width:(h + 1) * width] for h in range(GLA_HEADS)], axis=0)


def _head_masked_stack(x, head_of_lane):
    zero = jnp.zeros((), x.dtype)
    return jnp.concatenate([jnp.where(head_of_lane == h, x, zero) for h in range(GLA_HEADS)], axis=0)


def _gla_prompt_kernel(q_ref, k_ref, la_ref, v_ref, gg_ref, gn_ref, tril_ref, o_ref, s_ref, st_scr):
    tb = pl.program_id(1)

    @pl.when(tb == 0)
    def _():
        st_scr[...] = jnp.zeros_like(st_scr)

    n_tok = q_ref.shape[1]
    c_len = min(GLA_CHUNK, n_tok)
    tril = tril_ref[...]
    head_of_lane = lax.broadcasted_iota(jnp.int32, (1, GLA_K_W), 1) // GLA_DK
    ri = lax.broadcasted_iota(jnp.int32, (GLA_HEADS * c_len, c_len), 0) % c_len
    ci = lax.broadcasted_iota(jnp.int32, (GLA_HEADS * c_len, c_len), 1)
    causal = ri >= ci
    gain = gn_ref[...]
    n_seq = q_ref.shape[0]
    items = [(i, c) for i in range(n_seq) for c in range(n_tok // c_len)]
    rows = lambda c: slice(c * c_len, (c + 1) * c_len)
    b_all = []
    for i in range(n_seq):
        g_hi, g_lo = _split_bf16(la_ref[i])
        b_all.append(_dot(tril, g_hi) + _dot(tril, g_lo))
    qm, kdm, dec, a_raw = {}, {}, {}, {}
    for it in items:
        i, c = it
        b = b_all[i][rows(c), :]
        dec[it] = jnp.exp(b[c_len - 1:c_len, :])
        q_t = q_ref[i, rows(c), :] * jnp.exp(b)
        k_t = k_ref[i, rows(c), :] * jnp.exp(-b)
        qm[it] = _head_masked_stack(q_t, head_of_lane).astype(BF16)
        kdm[it] = _head_masked_stack(k_t * dec[it], head_of_lane).astype(BF16)
        a_raw[it] = _dot_tb(qm[it], k_t.astype(BF16))
    upd = {it: _dot_ta(_head_stack(v_ref[it[0], rows(it[1]), :], GLA_DV), kdm[it]) for it in items}
    inter = {}
    for i in range(n_seq):
        st = st_scr[i]
        for c in range(n_tok // c_len):
            inter[(i, c)] = _dot_tb(qm[(i, c)], st.astype(BF16))
            st = dec[(i, c)] * st + upd[(i, c)]
        st_scr[i] = st
    for it in items:
        i, c = it
        a = jnp.where(causal, a_raw[it], 0.0).astype(BF16)
        intra = jnp.concatenate(
            [_dot(a[h * c_len:(h + 1) * c_len, :], v_ref[i, rows(c), h * GLA_DV:(h + 1) * GLA_DV])
             for h in range(GLA_HEADS)], axis=0)
        y = _gla_out(inter[it] + intra, gain, _head_stack(gg_ref[i, rows(c), :], GLA_DV)).astype(BF16)
        for h in range(GLA_HEADS):
            o_ref[i, rows(c), h * GLA_DV:(h + 1) * GLA_DV] = y[h * c_len:(h + 1) * c_len, :]

    @pl.when(tb == pl.num_programs(1) - 1)
    def _():
        for i in range(q_ref.shape[0]):
            s_ref[i] = st_scr[i].T


def _gla_prompt(layer, q, k, la, v, gg, gain, tril, batch, seq):
    tb = tril.shape[0]
    per = min(GLA_SEQS, batch)
    blk = lambda b, t: (b, t, 0)
    kw = pl.BlockSpec((per, tb, GLA_K_W), blk)
    vw = pl.BlockSpec((per, tb, GLA_V_W), blk)
    return pl.pallas_call(
        _gla_prompt_kernel,
        grid=(batch // per, seq // tb),
        in_specs=[kw, kw, kw, vw, vw, _layer_resident(gain, layer), _resident(tril)],
        out_specs=[vw, pl.BlockSpec((per, GLA_K_W, GLA_DV), lambda b, t: (b, 0, 0))],
        out_shape=[
            jax.ShapeDtypeStruct((batch, seq, GLA_V_W), BF16),
            jax.ShapeDtypeStruct((batch, GLA_K_W, GLA_DV), F32),
        ],
        scratch_shapes=[pltpu.VMEM((per, GLA_DV, GLA_K_W), F32)],
        compiler_params=_params("parallel", "arbitrary"),
        name="gla_prompt",
    )(q, k, la, v, gg, gain, tril)


def _gla_decode_kernel(q_ref, k_ref, la_ref, v_ref, gg_ref, gn_ref, s_ref, *rest, steps, layer):
    o_ref, so_ref = rest[-2:]
    so_ref = _layer_view(so_ref, layer, len(rest) == 2)
    grp = s_ref.shape[0]
    rows = grp * steps
    stacked = GLA_HEADS * rows
    ri = lax.broadcasted_iota(jnp.int32, (rows, rows), 0)
    ci = lax.broadcasted_iota(jnp.int32, (rows, rows), 1)
    same_seq = ri // steps == ci // steps
    g_hi, g_lo = _split_bf16(la_ref[...])
    tril = (same_seq & (ri >= ci)).astype(BF16)
    total = same_seq.astype(BF16)
    b = _dot(tril, g_hi) + _dot(tril, g_lo)
    b_last = _dot(total, g_hi) + _dot(total, g_lo)
    head_of_lane = lax.broadcasted_iota(jnp.int32, (1, GLA_K_W), 1) // GLA_DK
    k_t = k_ref[...] * jnp.exp(-b)
    qm = _head_masked_stack(q_ref[...] * jnp.exp(b), head_of_lane).astype(BF16)
    km = _head_masked_stack(k_t, head_of_lane).astype(BF16)
    kdm = _head_masked_stack(k_t * jnp.exp(b_last), head_of_lane).astype(BF16)
    v_st = _head_stack(v_ref[...], GLA_DV)
    seq_of_row = (lax.broadcasted_iota(jnp.int32, (stacked, 1), 0) % rows) // steps
    seq_of_g = lax.broadcasted_iota(jnp.int32, (rows, 1), 0) // steps
    zero = jnp.zeros((), BF16)
    rhs = jnp.concatenate([
        jnp.concatenate([v_st, jnp.zeros((stacked, GLA_DV), BF16)], axis=1),
        jnp.concatenate([jnp.zeros((2 * rows, GLA_DV), BF16), jnp.ones((2 * rows, GLA_DV), BF16)], axis=1),
    ], axis=0)
    a_raw = _dot_tb(qm, km)
    states = [s_ref[s].reshape(GLA_K_W, GLA_DV) for s in range(grp)]
    inter_all = [_dot(qm, st.astype(BF16)) for st in states]
    upd_all = []
    for s in range(grp):
        lhs = jnp.concatenate([jnp.where(seq_of_row == s, kdm, zero),
                               jnp.where(seq_of_g == s, g_hi, zero),
                               jnp.where(seq_of_g == s, g_lo, zero)], axis=0)
        upd_all.append(_dot_ta(lhs, rhs))
    rr = lax.broadcasted_iota(jnp.int32, (stacked, stacked), 0) % rows
    cc = lax.broadcasted_iota(jnp.int32, (stacked, stacked), 1) % rows
    causal = (rr // steps == cc // steps) & (rr >= cc)
    o = _dot(jnp.where(causal, a_raw, 0.0).astype(BF16), v_st)
    inter = inter_all[0]
    for s in range(1, grp):
        inter = jnp.where(seq_of_row == s, inter_all[s], inter)
    y = _gla_out(o + inter, gn_ref[...], _head_stack(gg_ref[...], GLA_DV)).astype(BF16)
    for h in range(GLA_HEADS):
        o_ref[:, h * GLA_DV:(h + 1) * GLA_DV] = y[h * rows:(h + 1) * rows, :]
    for s in range(grp):
        new = jnp.exp(upd_all[s][:, GLA_DV:]) * states[s] + upd_all[s][:, :GLA_DV]
        so_ref[s] = new.reshape(GLA_HEADS, GLA_DK, GLA_DV)


def _gla_decode(layer, q, k, la, v, gg, gain, state, steps, prev):
    nseq = state.shape[1]
    grp = min(GLA_DEC_GROUP, nseq)
    rows = grp * steps
    row = lambda i: (i, 0)
    st_spec = pl.BlockSpec((None, grp, GLA_HEADS, GLA_DK, GLA_DV), lambda i: (layer, i, 0, 0, 0))
    extra_in, extra_specs, out_spec, stacked = _stacked_out(
        state.shape, (grp, GLA_HEADS, GLA_DK, GLA_DV), layer, prev)
    n_in = 7
    return pl.pallas_call(
        functools.partial(_gla_decode_kernel, steps=steps, layer=layer),
        grid=(nseq // grp,),
        in_specs=[
            pl.BlockSpec((rows, GLA_K_W), row),
            pl.BlockSpec((rows, GLA_K_W), row),
            pl.BlockSpec((rows, GLA_K_W), row),
            pl.BlockSpec((rows, GLA_V_W), row),
            pl.BlockSpec((rows, GLA_V_W), row),
            _layer_resident(gain, layer),
            st_spec,
        ] + extra_specs,
        out_specs=[pl.BlockSpec((rows, GLA_V_W), row), out_spec],
        out_shape=[jax.ShapeDtypeStruct((nseq * steps, GLA_V_W), BF16), stacked],
        input_output_aliases={n_in + i: 1 + i for i in range(len(extra_in))},
        compiler_params=_params("parallel"),
        name="gla_decode",
    )(q, k, la, v, gg, gain, state, *extra_in)


def _outproj_apply(x, a_ref, o_ref, wa_ref, wo_ref, g_ref, wq_ref, qn_ref):
    x1 = x + _dot(a_ref[...], wa_ref[...]) + _dot(o_ref[...], wo_ref[...])
    h = _rms(x1, g_ref[...]).astype(BF16)
    qn = qn_ref[...]
    q = [_rms(_dot(h, wq_ref[:, hd * XA_HEAD_DIM:(hd + 1) * XA_HEAD_DIM]), qn).astype(BF16)
         for hd in range(XA_HEADS)]
    return x1, q


def _outproj_kernel(x_ref, a_ref, o_ref, wa_ref, wo_ref, g_ref, wq_ref, qn_ref, x1_ref, q_ref):
    x1, q = _outproj_apply(x_ref[...], a_ref, o_ref, wa_ref, wo_ref, g_ref, wq_ref, qn_ref)
    x1_ref[...] = x1
    for hd in range(XA_HEADS):
        q_ref[:, hd * XA_HEAD_DIM:(hd + 1) * XA_HEAD_DIM] = q[hd]


def _outproj(x, layer, a, o, w_a, w_o, g, wq, qn):
    n, d = x.shape
    tm = min(ROW_TILE, n)
    row = lambda i: (i, 0)
    return pl.pallas_call(
        _outproj_kernel,
        grid=(n // tm,),
        in_specs=[
            pl.BlockSpec((tm, d), row),
            pl.BlockSpec((tm, SWA_Q_W), row),
            pl.BlockSpec((tm, GLA_V_W), row),
        ] + [_layer_resident(p, layer) for p in (w_a, w_o, g, wq, qn)],
        out_specs=[pl.BlockSpec((tm, d), row), pl.BlockSpec((tm, XA_W), row)],
        out_shape=[jax.ShapeDtypeStruct((n, d), F32), jax.ShapeDtypeStruct((n, XA_W), BF16)],
        compiler_params=_params("parallel"),
        name="outproj",
    )(x, a, o, w_a, w_o, g, wq, qn)


def _memkv_kernel(*refs, n_cast):
    m_ref, g_ref, wk_ref, wv_ref, kn_ref = refs[:5]
    k_ref, v_ref = refs[5 + n_cast:7 + n_cast]
    _cast_chunks(refs[5:5 + n_cast], refs[7 + n_cast:])
    m = _rms(m_ref[...], g_ref[...]).astype(BF16)
    kn = kn_ref[...]
    for hd in range(XA_HEADS):
        sl = slice(hd * XA_HEAD_DIM, (hd + 1) * XA_HEAD_DIM)
        k_ref[:, sl] = _rms(_dot(m, wk_ref[:, sl]), kn)
    v_ref[...] = _dot(m, wv_ref[...])


def _memkv(mem, g, wk, wv, kn, cast=()):
    depth = wk.shape[0]
    n, d = mem.shape
    tm = min(ROW_TILE, n)
    per_layer = lambda l, i: (l, 0, 0)
    out = lambda l, i: (l, i, 0)
    tiles = n // tm
    cast_in, cast_out, cast_shape = _cast_specs(cast, depth * tiles, lambda l, i: l * tiles + i)
    return pl.pallas_call(
        functools.partial(_memkv_kernel, n_cast=len(cast)),
        grid=(depth, tiles),
        in_specs=[
            pl.BlockSpec((tm, d), lambda l, i: (i, 0)),
            pl.BlockSpec((None, 1, d), per_layer),
            pl.BlockSpec((None, d, XA_W), per_layer),
            pl.BlockSpec((None, d, XA_W), per_layer),
            pl.BlockSpec((None, 1, XA_HEAD_DIM), per_layer),
        ] + cast_in,
        out_specs=[pl.BlockSpec((None, tm, XA_W), out), pl.BlockSpec((None, tm, XA_W), out)] + cast_out,
        out_shape=[jax.ShapeDtypeStruct((depth, n, XA_W), F32)] * 2 + cast_shape,
        compiler_params=_params("parallel", "parallel"),
        name="memkv",
    )(mem, g, wk, wv, kn, *[c[0] for c in cast])


def _mixout_prompt_kernel(*refs, n_cast):
    (x_ref, a_ref, o_ref, wa_ref, wo_ref, g_ref, wq_ref, qn_ref, mk_ref, mv_ref, xwo_ref, fg_ref, wg_ref, wu_ref,
     wd_ref) = refs[:15]
    out_ref, att_scr, a_scr = refs[15 + n_cast], refs[-2], refs[-1]
    _cast_chunks(refs[15:15 + n_cast], refs[16 + n_cast:-2])
    x1, q = _outproj_apply(x_ref[...], a_ref, o_ref, wa_ref, wo_ref, g_ref, wq_ref, qn_ref)
    mk = mk_ref[...].astype(BF16)
    mv = mv_ref[...].astype(BF16)
    heads = range(XA_HEADS)
    cols = lambda hd: slice(hd * XA_HEAD_DIM, (hd + 1) * XA_HEAD_DIM)
    scores = [_dot_tb(q[hd], mk[:, cols(hd)]) * (XA_HEAD_DIM ** -0.5 * LOG2E) for hd in heads]
    probs = []
    for s in scores:
        e = jnp.exp2(s - jnp.max(s, axis=-1, keepdims=True))
        probs.append((e.astype(BF16), jnp.sum(e, axis=-1, keepdims=True)))
    for hd in heads:
        att_scr[:, cols(hd)] = (_dot(probs[hd][0], mv[:, cols(hd)]) / probs[hd][1]).astype(BF16)
    x2 = x1 + _dot(att_scr[...], xwo_ref[...])
    out_ref[...] = _ffn_apply(x2, fg_ref, wg_ref, wu_ref, wd_ref, a_scr)


def _mixout_prompt(x, layer, a, o, outproj, mk, mv, wo, ffn, seq, cast=()):
    n, d = x.shape
    dff = ffn[1].shape[2]
    tm = min(ROW_TILE, seq)
    per_seq = seq // tm
    row = lambda i: (i, 0)
    mem_spec = pl.BlockSpec((None, None, mk.shape[2], XA_W), lambda i: (layer, i // per_seq, 0, 0))
    cast_in, cast_out, cast_shape = _cast_specs(cast, n // tm)
    return pl.pallas_call(
        functools.partial(_mixout_prompt_kernel, n_cast=len(cast)),
        grid=(n // tm,),
        in_specs=([pl.BlockSpec((tm, d), row), pl.BlockSpec((tm, SWA_Q_W), row), pl.BlockSpec((tm, GLA_V_W), row)]
                  + [_layer_resident(p, layer) for p in outproj]
                  + [mem_spec, mem_spec]
                  + [_layer_resident(p, layer) for p in (wo,) + ffn]
                  + cast_in),
        out_specs=[pl.BlockSpec((tm, d), row)] + cast_out,
        out_shape=[jax.ShapeDtypeStruct((n, d), F32)] + cast_shape,
        scratch_shapes=[pltpu.VMEM((tm, XA_W), BF16), pltpu.VMEM((tm, dff), BF16)],
        compiler_params=_params("parallel"),
        name="mixout_prompt",
    )(x, a, o, *outproj, mk, mv, wo, *ffn, *[c[0] for c in cast])


def _xattn_decode_kernel(q_ref, mk_ref, mv_ref, o_ref, *, steps):
    grp, nkeys = mk_ref.shape[0], mk_ref.shape[1]
    rows = grp * steps
    q = jnp.concatenate([q_ref[:, hd * XA_HEAD_DIM:(hd + 1) * XA_HEAD_DIM] for hd in range(XA_HEADS)], axis=0)
    r = lax.broadcasted_iota(jnp.int32, (XA_HEADS * rows, 1), 0)
    own = (r % rows) // steps
    same_head = (r // rows) == (lax.broadcasted_iota(jnp.int32, (1, nkeys), 1) % XA_HEADS)
    s = None
    for j in range(grp):
        sj = _dot_tb(q, mk_ref[j].astype(BF16))
        s = sj if s is None else jnp.where(own == j, sj, s)
    s = jnp.where(same_head, s * (XA_HEAD_DIM ** -0.5 * LOG2E), -jnp.inf)
    m = jnp.max(s, axis=-1, keepdims=True)
    e = jnp.exp2(s - m)
    p = e.astype(BF16)
    o = None
    for j in range(grp):
        oj = _dot(p, mv_ref[j].astype(BF16))
        o = oj if o is None else jnp.where(own == j, oj, o)
    o = o / jnp.sum(e, axis=-1, keepdims=True)
    for hd in range(XA_HEADS):
        o_ref[:, hd * XA_HEAD_DIM:(hd + 1) * XA_HEAD_DIM] = o[hd * rows:(hd + 1) * rows, :].astype(BF16)


def _xattn_decode(q, layer, mk, mv, steps):
    nseq, nkeys = mk.shape[1], mk.shape[2]
    grp = XA_DEC_GROUP
    rows = grp * steps
    row = lambda i: (i, 0)
    mem_spec = pl.BlockSpec((None, grp, nkeys, XA_HEAD_DIM), lambda i: (layer, i, 0, 0))
    return pl.pallas_call(
        functools.partial(_xattn_decode_kernel, steps=steps),
        grid=(nseq // grp,),
        in_specs=[pl.BlockSpec((rows, XA_W), row), mem_spec, mem_spec],
        out_specs=pl.BlockSpec((rows, XA_W), row),
        out_shape=jax.ShapeDtypeStruct((nseq * steps, XA_W), BF16),
        compiler_params=_params("parallel"),
        name="xattn_decode",
    )(q, mk, mv)


def _proj_ffn_kernel(x_ref, a_ref, w_ref, g_ref, wg_ref, wu_ref, wd_ref, o_ref, a_scr):
    x = x_ref[...] + _dot(a_ref[...], w_ref[...])
    o_ref[...] = _ffn_apply(x, g_ref, wg_ref, wu_ref, wd_ref, a_scr)


def _proj_ffn(x, layer, a, w, ffn):
    n, d = x.shape
    dff = ffn[1].shape[2]
    tm = min(ROW_TILE, n)
    row = lambda i: (i, 0)
    return pl.pallas_call(
        _proj_ffn_kernel,
        grid=(n // tm,),
        in_specs=([pl.BlockSpec((tm, d), row), pl.BlockSpec((tm, a.shape[1]), row)]
                  + [_layer_resident(p, layer) for p in (w,) + ffn]),
        out_specs=pl.BlockSpec((tm, d), row),
        out_shape=jax.ShapeDtypeStruct((n, d), F32),
        scratch_shapes=[pltpu.VMEM((tm, dff), BF16)],
        compiler_params=_params("parallel"),
        name="proj_ffn",
    )(x, a, w, *ffn)


def _rope_tables(pos):
    half = HEAD_DIM // 2
    inv = ROPE_THETA ** (-jnp.arange(half, dtype=F32) / half)
    ang = pos.astype(F32)[:, None] * inv[None, :]
    cos, sin = jnp.cos(ang), jnp.sin(ang)
    reps = LANES // HEAD_DIM
    return jnp.tile(cos, (1, 2 * reps)), jnp.tile(jnp.concatenate([-sin, sin], axis=-1), (1, reps))


def _block_tril(n_blocks, size):
    i = jnp.arange(n_blocks * size)
    return ((i[:, None] // size == i[None, :] // size) & (i[:, None] >= i[None, :])).astype(BF16)


def _permute_heads(w, axis):
    blocks = jnp.split(w, SWA_Q_HEADS, axis=axis)
    return jnp.concatenate([blocks[h] for h in SWA_HEAD_ORDER], axis=axis)


def kernel(x_prompt, x_sample, cache_swa_k, cache_swa_v, state_gla, cache_mem_k, cache_mem_v, mem_prompt, ffn1_norm, ffn1_wg, ffn1_wu, ffn1_wd, mix_norm, w_in, swa_q_norm, swa_k_norm, swa_sinks, gla_w_gate, gla_b_gate, gla_out_norm, w_out, xa_norm, mem_norm, xa_wq, xa_wk, xa_wv, xa_q_norm, xa_k_norm, xa_wo, ffn2_norm, ffn2_wg, ffn2_wu, ffn2_wd):
    batch, seq, d = x_prompt.shape
    nseq, steps, _ = x_sample.shape
    depth = w_in.shape[0]
    mem_len = mem_prompt.shape[1]

    bf = lambda w: w.astype(BF16)
    vec = lambda p: p[:, None, :]
    first_cast = ([(w, 0) for w in (ffn1_wg, ffn1_wu, ffn1_wd)]
                  + [(w_in, l, w_in.shape[2] + MXU_TILE - GLA_LOWRANK) for l in range(depth)])
    gate_w = jnp.pad(bf(gla_w_gate), ((0, 0), (0, LANES - GLA_LOWRANK), (0, 0)))
    w_a = bf(_permute_heads(w_out[:, :SWA_Q_W], 1))
    w_o = bf(w_out[:, SWA_Q_W:])
    wq_b, wk_b, wv_b, wo_b = bf(xa_wq), bf(xa_wk), bf(xa_wv), bf(xa_wo)
    qn = vec(jnp.tile(swa_q_norm * (HEAD_DIM ** -0.5 * LOG2E), (1, LANES // HEAD_DIM)))
    kn = vec(jnp.tile(swa_k_norm, (1, LANES // HEAD_DIM)))
    inproj_params = lambda l: (vec(mix_norm), w_all[l], gate_w, vec(gla_b_gate), qn, kn)
    outproj_params = (w_a, w_o, vec(xa_norm), wq_b, vec(xa_q_norm))
    gla_gain = vec(gla_out_norm)

    lane = jnp.arange(MXU_TILE)
    bd = (lane[:, None] // HEAD_DIM == lane[None, :] // HEAD_DIM).astype(BF16)
    cos_p, sin_p = _rope_tables(jnp.arange(seq))
    cos_s, sin_s = _rope_tables(PAST_LEN + jnp.arange(nseq * steps) % steps)
    tril_p = _block_tril(min(GLA_BLOCK, seq) // min(GLA_CHUNK, seq), min(GLA_CHUNK, seq))

    mk_p, mv_p, *cast = _memkv(mem_prompt.reshape(batch * mem_len, d), vec(mem_norm), wk_b, wv_b,
                               vec(xa_k_norm), first_cast)
    ffn1_w = [cast[:3]] + [None] * (depth - 1)
    w_all = cast[3:]
    mk_p = mk_p.reshape(depth, batch, mem_len, XA_W)
    mv_p = mv_p.reshape(depth, batch, mem_len, XA_W)
    mk_s = cache_mem_k.reshape(depth, nseq, mem_len * XA_HEADS, XA_HEAD_DIM)
    mv_s = cache_mem_v.reshape(depth, nseq, mem_len * XA_HEADS, XA_HEAD_DIM)
    native = lambda c: jnp.transpose(c, (0, 1, 3, 4, 2)).reshape(depth, nseq, SWA_KV_W, WINDOW)
    kc_s, vc_s = native(cache_swa_k), native(cache_swa_v)

    xp = x_prompt.reshape(batch * seq, d)
    xs = x_sample.reshape(nseq * steps, d)
    kp_l, vp_l, sp_l = [], [], []
    swa_new = gla_new = None
    for l in range(depth):
        ffn1 = (vec(ffn1_norm),) + tuple(ffn1_w[l])
        xp, q_s, k_s, v_s, q_g, k_g, v_g, g_g, la, *ffn2_w = _ffn_inproj(
            xp, l, ffn1, inproj_params(l), cos_p, sin_p, bd, cast=[(w, l) for w in (ffn2_wg, ffn2_wu, ffn2_wd)])
        ffn2 = (vec(ffn2_norm),) + tuple(ffn2_w)
        a_p = _swa_prompt(swa_sinks, l, q_s, k_s, v_s, batch, seq)
        seqs = lambda a: a.reshape(batch, seq, a.shape[-1])
        o_p, s_p = _gla_prompt(l, seqs(q_g), seqs(k_g), seqs(la), seqs(v_g), seqs(g_g), gla_gain, tril_p,
                               batch, seq)
        o_p = o_p.reshape(batch * seq, GLA_V_W)
        last = lambda a: seqs(a)[:, seq - WINDOW:].reshape(batch, WINDOW, SWA_KV_HEADS, HEAD_DIM)
        kp_l.append(last(k_s))
        vp_l.append(last(v_s))
        sp_l.append(s_p.reshape(batch, GLA_HEADS, GLA_DK, GLA_DV))
        next_ffn1 = [(w, l + 1) for w in (ffn1_wg, ffn1_wu, ffn1_wd)] if l + 1 < depth else []
        xp, *cast = _mixout_prompt(xp, l, a_p, o_p, outproj_params, mk_p, mv_p, wo_b, ffn2, seq, cast=next_ffn1)
        if next_ffn1:
            ffn1_w[l + 1] = cast

        xs, q_s, k_s, v_s, q_g, k_g, v_g, g_g, la, k_t, v_t = _ffn_inproj(
            xs, l, ffn1, inproj_params(l), cos_s, sin_s, bd, transposed_kv=True)
        a_s, *swa_new = _swa_decode(swa_sinks, l, q_s, k_s, v_s, k_t, v_t, kc_s, vc_s, steps, swa_new)
        o_s, *gla_new = _gla_decode(l, q_g, k_g, la, v_g, g_g, gla_gain, state_gla, steps, gla_new)
        xs, q_x = _outproj(xs, l, a_s, o_s, *outproj_params)
        xs = _proj_ffn(xs, l, _xattn_decode(q_x, l, mk_s, mv_s, steps), wo_b, ffn2)

    unnative = lambda c: jnp.transpose(c.reshape(depth, nseq, SWA_KV_HEADS, HEAD_DIM, WINDOW), (0, 1, 4, 2, 3))
    return (xp.reshape(batch, seq, d), xs.reshape(nseq, steps, d),
            jnp.stack(kp_l), jnp.stack(vp_l), jnp.stack(sp_l),
            mk_p.reshape(depth, batch, mem_len, XA_HEADS, XA_HEAD_DIM),
            mv_p.reshape(depth, batch, mem_len, XA_HEADS, XA_HEAD_DIM),
            unnative(swa_new[0]), unnative(swa_new[1]), gla_new[0])
```

```python
import functools

import jax
import jax.numpy as jnp
from jax import lax
from jax.experimental import pallas as pl
from jax.experimental.pallas import tpu as pltpu

F32 = jnp.float32
BF16 = jnp.bfloat16

EPS = 1e-6
LOG2E = 1.4426950408889634
PAST_LEN = 16384
WINDOW = 128
ROPE_THETA = 10000.0
HEAD_DIM = 64
SWA_Q_HEADS = 8
SWA_KV_HEADS = 2
SWA_GROUP = SWA_Q_HEADS // SWA_KV_HEADS
GLA_HEADS = 4
GLA_DK = 64
GLA_DV = 128
GLA_LOWRANK = 16
GLA_GATE_TEMP = 16.0
GLA_CHUNK = 64
XA_HEADS = 4
XA_HEAD_DIM = 128

SWA_Q_W = SWA_Q_HEADS * HEAD_DIM
SWA_KV_W = SWA_KV_HEADS * HEAD_DIM
GLA_K_W = GLA_HEADS * GLA_DK
GLA_V_W = GLA_HEADS * GLA_DV
XA_W = XA_HEADS * XA_HEAD_DIM
MAIN_W = SWA_Q_W + 2 * SWA_KV_W + 2 * GLA_K_W + 2 * GLA_V_W

LANES = 128
BF16_SUBLANES = 16
MXU_TILE = 256
VMEM_LIMIT = 56 * 1024 * 1024

ROW_TILE = 512
ROW_SPLIT = 2
FFN_CHUNK = 256
SWA_BLOCK = 1024
GLA_BLOCK = 256
GLA_SEQS = 4
DEC_GROUP = 4
SWA_DEC_SUBGROUPS = 8
XA_DEC_GROUP = 8
XA_RING = 3
GLA_DEC_GROUP = 8

assert SWA_KV_HEADS * HEAD_DIM == LANES
SWA_HEAD_ORDER = tuple(kv * SWA_GROUP + g for g in range(SWA_GROUP) for kv in range(SWA_KV_HEADS))


def _dot(a, b):
    return jnp.dot(a, b, preferred_element_type=F32)


def _dot_tb(a, b):
    return lax.dot_general(a, b, (((1,), (1,)), ((), ())), preferred_element_type=F32)


def _dot_ta(a, b):
    return lax.dot_general(a, b, (((0,), (0,)), ((), ())), preferred_element_type=F32)


def _split_bf16(x):
    hi = x.astype(BF16)
    lo = (x - hi.astype(F32)).astype(BF16)
    return hi, lo


def _rms(x, g):
    ms = jnp.mean(x * x, axis=-1, keepdims=True)
    return x * lax.rsqrt(ms + EPS) * g


def _params(*sem):
    return pltpu.CompilerParams(dimension_semantics=sem, vmem_limit_bytes=VMEM_LIMIT)


def _resident(arr):
    nd = arr.ndim
    return pl.BlockSpec(arr.shape, lambda *_: (0,) * nd, pipeline_mode=pl.Buffered(1))


def _layer_resident(arr, layer):
    nd = arr.ndim
    index = (layer if arr.shape[0] > 1 else 0,) + (0,) * (nd - 1)
    return pl.BlockSpec((None,) + arr.shape[1:], lambda *_: index, pipeline_mode=pl.Buffered(1))


def _cast_specs(cast, n_steps, step_of=lambda i: i):
    ins, outs, shapes = [], [], []
    for arr, src_layer, *padded in cast:
        _, rows, cols = arr.shape
        out_cols = padded[0] if padded else cols
        share = 1
        while rows % (n_steps // share) or (rows * share // n_steps) % BF16_SUBLANES:
            share *= 2
            assert n_steps % share == 0
        chunk = rows * share // n_steps
        ins.append(pl.BlockSpec((None, chunk, cols), lambda *g, l=src_layer, s=share: (l, step_of(*g) // s, 0)))
        outs.append(pl.BlockSpec((1, chunk, out_cols), lambda *g, s=share: (0, step_of(*g) // s, 0)))
        shapes.append(jax.ShapeDtypeStruct((1, rows, out_cols), BF16))
    return ins, outs, shapes


def _cast_chunks(srcs, dsts):
    for src, dst in zip(srcs, dsts):
        cols = src.shape[-1]
        dst[0, :, :cols] = src[...].astype(BF16)
        if dst.shape[-1] > cols:
            dst[0, :, cols:] = jnp.zeros((dst.shape[1], dst.shape[-1] - cols), BF16)


def _ffn_hidden(x, g_ref, wg_ref, wu_ref, a_scr):
    h = _rms(x, g_ref[...]).astype(BF16)
    dff = wg_ref.shape[1]
    for c in range(dff // FFN_CHUNK):
        sl = slice(c * FFN_CHUNK, (c + 1) * FFN_CHUNK)
        g = _dot(h, wg_ref[:, sl])
        u = _dot(h, wu_ref[:, sl])
        a_scr[:, sl] = (g * jax.nn.sigmoid(g) * u).astype(BF16)


def _ffn_apply(x, g_ref, wg_ref, wu_ref, wd_ref, a_scr):
    _ffn_hidden(x, g_ref, wg_ref, wu_ref, a_scr)
    return x + 0.5 * _dot(a_scr[...], wd_ref[...])


def _ffn_inproj_kernel(*refs, n_cast, transposed_kv):
    (x_ref, fg_ref, wg_ref, wu_ref, wd_ref, g_ref, w_ref, gw_ref, gb_ref, qn_ref, kn_ref, cos_ref, sin_ref,
     bd_ref) = refs[:14]
    outs, a_scr = refs[14 + n_cast:-1], refs[-1]
    xo_ref, qs_ref, ks_ref, vs_ref, qg_ref, kg_ref, vg_ref, gg_ref, la_ref = outs[:9]
    kt_ref, vt_ref = outs[9:11] if transposed_kv else (None, None)
    _cast_chunks(refs[14:14 + n_cast], outs[len(outs) - n_cast:])
    x = x_ref[...]
    _ffn_hidden(x, fg_ref, wg_ref, wu_ref, a_scr)
    n = x.shape[0] // ROW_SPLIT
    groups = [slice(r * n, (r + 1) * n) for r in range(ROW_SPLIT)]
    mid = [x[rs] + 0.5 * _dot(a_scr[rs, :], wd_ref[...]) for rs in groups]
    for rs, xm in zip(groups, mid):
        xo_ref[rs, :] = xm
    split = SWA_Q_W + 2 * SWA_KV_W + 2 * GLA_K_W
    h = [_rms(xm, g_ref[...]).astype(BF16) for xm in mid]
    z1 = [_dot(hr, w_ref[:, :split]) for hr in h]
    z2 = []
    for rs, hr, z in zip(groups, h, z1):
        _inproj_finish_swa(z, rs, qn_ref, kn_ref, cos_ref, sin_ref, bd_ref, qs_ref, ks_ref, vs_ref, qg_ref,
                           kg_ref, kt_ref, vt_ref)
        z2.append(_dot(hr, w_ref[:, split:]))
    for rs, z in zip(groups, z2):
        _inproj_finish_gla(z, rs, gw_ref, gb_ref, vg_ref, gg_ref, la_ref)


def _inproj_finish_swa(z1, rs, qn_ref, kn_ref, cos_ref, sin_ref, bd_ref, qs_ref, ks_ref, vs_ref, qg_ref,
                       kg_ref, kt_ref, vt_ref):
    bd = bd_ref[...]
    cos = cos_ref[rs, :]
    sin = sin_ref[rs, :]
    lane = lax.broadcasted_iota(jnp.int32, cos.shape, 1)
    lane_lo = (lane % HEAD_DIM) < (HEAD_DIM // 2)

    def head_scale(z):
        ss = _dot((z * z).astype(BF16), bd)
        return lax.rsqrt(ss * (1.0 / HEAD_DIM) + EPS)

    def rope(y):
        swapped = jnp.where(lane_lo, pltpu.roll(y, LANES - HEAD_DIM // 2, axis=1),
                            pltpu.roll(y, HEAD_DIM // 2, axis=1))
        return y * cos + swapped * sin

    qn = qn_ref[...]
    nat = []
    for t in range(SWA_Q_W // MXU_TILE):
        z = z1[:, t * MXU_TILE:(t + 1) * MXU_TILE]
        y = z * head_scale(z)
        nat += [rope(y[:, c * LANES:(c + 1) * LANES] * qn) for c in range(MXU_TILE // LANES)]
    low = _low_half()
    per_col = LANES // HEAD_DIM
    for c in range(SWA_Q_W // LANES):
        halves = []
        for half, head in enumerate(SWA_HEAD_ORDER[per_col * c:per_col * (c + 1)]):
            col = nat[head // per_col]
            halves.append(col if head % per_col == half else pltpu.roll(col, HEAD_DIM, axis=1))
        qs_ref[rs, c * LANES:(c + 1) * LANES] = jnp.where(low, halves[0], halves[1]).astype(BF16)
    o = SWA_Q_W
    z = z1[:, o:o + 2 * SWA_KV_W]
    k = rope(z[:, :SWA_KV_W] * head_scale(z)[:, :SWA_KV_W] * kn_ref[...])
    ks_ref[rs, :] = k
    vs_ref[rs, :] = z[:, SWA_KV_W:]
    if kt_ref is not None:
        kt_ref[:, rs] = k.T
        vt_ref[:, rs] = z[:, SWA_KV_W:].T
    o += 2 * SWA_KV_W
    qg_ref[rs, :] = z1[:, o:o + GLA_K_W] * (GLA_DK ** -0.5)
    o += GLA_K_W
    kg_ref[rs, :] = z1[:, o:o + GLA_K_W]


def _inproj_finish_gla(z2, rs, gw_ref, gb_ref, vg_ref, gg_ref, la_ref):
    vg_ref[rs, :] = z2[:, :GLA_V_W].astype(BF16)
    gg_ref[rs, :] = z2[:, GLA_V_W:2 * GLA_V_W]
    lr = z2[:, 2 * GLA_V_W:2 * GLA_V_W + LANES].astype(BF16)
    t = _dot(lr, gw_ref[...]) + gb_ref[...]
    log_sig = jnp.minimum(t, 0.0) - jnp.log(1.0 + jnp.exp(-jnp.abs(t)))
    la_ref[rs, :] = log_sig * (1.0 / GLA_GATE_TEMP)


def _ffn_inproj(x, layer, ffn, inproj, cos, sin, bd, transposed_kv=False, cast=()):
    n, d = x.shape
    dff = ffn[1].shape[2]
    tm = min(ROW_TILE, n)
    pos_blocks = cos.shape[0] // tm
    row = lambda i: (i, 0)
    pos = lambda i: (i % pos_blocks, 0)
    widths = (d, SWA_Q_W, SWA_KV_W, SWA_KV_W, GLA_K_W, GLA_K_W, GLA_V_W, GLA_V_W, GLA_K_W)
    dtypes = (F32, BF16, F32, F32, F32, F32, BF16, F32, F32)
    out_specs = [pl.BlockSpec((tm, w), row) for w in widths]
    out_shape = [jax.ShapeDtypeStruct((n, w), dt) for w, dt in zip(widths, dtypes)]
    if transposed_kv:
        out_specs += [pl.BlockSpec((SWA_KV_W, tm), lambda i: (0, i))] * 2
        out_shape += [jax.ShapeDtypeStruct((SWA_KV_W, n), F32)] * 2
    cast_in, cast_out, cast_shape = _cast_specs(cast, n // tm)
    return pl.pallas_call(
        functools.partial(_ffn_inproj_kernel, n_cast=len(cast), transposed_kv=transposed_kv),
        grid=(n // tm,),
        in_specs=([pl.BlockSpec((tm, d), row)]
                  + [_layer_resident(a, layer) for a in ffn + inproj]
                  + [pl.BlockSpec((tm, LANES), pos), pl.BlockSpec((tm, LANES), pos), _resident(bd)]
                  + cast_in),
        out_specs=out_specs + cast_out,
        out_shape=out_shape + cast_shape,
        scratch_shapes=[pltpu.VMEM((tm, dff), BF16)],
        compiler_params=_params("parallel"),
        name="ffn_inproj",
    )(x, *ffn, *inproj, cos, sin, bd, *[c[0] for c in cast])


def _swa_attend(q, k, v, valid, sinks):
    return _swa_out(*_swa_probs(_swa_scores(q, k), valid, sinks), v)


def _low_half():
    return lax.broadcasted_iota(jnp.int32, (1, LANES), 1) < HEAD_DIM


def _swa_head_rows(q):
    low = _low_half()
    zero = jnp.zeros((), q.dtype)
    pieces = []
    for c in range(SWA_Q_W // LANES):
        qc = q[:, c * LANES:(c + 1) * LANES]
        pieces += [jnp.where(low, qc, zero), jnp.where(low, zero, qc)]
    return jnp.concatenate(pieces, axis=0)


def _swa_scores(q, k):
    return _dot_tb(_swa_head_rows(q), k)


def _swa_probs(s, valid, sinks):
    rows, keys = valid.shape
    s = jnp.where(valid[None], s.reshape(SWA_Q_HEADS, rows, keys), -jnp.inf)
    m = jnp.maximum(jnp.max(s, axis=-1, keepdims=True), sinks)
    e = jnp.exp2(s - m)
    den = jnp.sum(e, axis=-1, keepdims=True) + jnp.exp2(sinks - m)
    return e.astype(BF16).reshape(SWA_Q_HEADS * rows, keys), den


def _swa_out(e, den, v):
    return _swa_finish(_dot(e, v), den)


def _swa_finish(o, den):
    low = _low_half()
    o = o.reshape(den.shape[0], den.shape[1], LANES) / den
    return [jnp.where(low, o[2 * c], o[2 * c + 1]) for c in range(SWA_Q_W // LANES)]


def _sink_column(sink_ref, layer):
    idx = lax.broadcasted_iota(jnp.int32, (SWA_Q_HEADS, 1, 1), 0)
    col = jnp.zeros((SWA_Q_HEADS, 1, 1), F32)
    for p, head in enumerate(SWA_HEAD_ORDER):
        col = jnp.where(idx == p, sink_ref[layer, head] * LOG2E, col)
    return col


def _swa_prompt_kernel(sink_ref, q_ref, kp_ref, kc_ref, vp_ref, vc_ref, o_ref, *, layer):
    j = pl.program_id(1)
    kall = jnp.concatenate([kp_ref[...], kc_ref[...]], axis=0).astype(BF16)
    vall = jnp.concatenate([vp_ref[...], vc_ref[...]], axis=0).astype(BF16)
    sinks = _sink_column(sink_ref, layer)
    upper = (lax.broadcasted_iota(jnp.int32, (WINDOW, WINDOW), 1)
             > lax.broadcasted_iota(jnp.int32, (WINDOW, WINDOW), 0))[None]

    def banded_probs(s, has_prev):
        s = s.reshape(SWA_Q_HEADS, WINDOW, 2 * WINDOW)
        prev = s[:, :, :WINDOW]
        if has_prev is not True:
            prev = jnp.where(has_prev, prev, -jnp.inf)
        logits = jnp.where(upper, prev, s[:, :, WINDOW:])
        m = jnp.maximum(jnp.max(logits, axis=-1, keepdims=True), sinks)
        e = jnp.exp2(logits - m)
        den = jnp.sum(e, axis=-1, keepdims=True) + jnp.exp2(sinks - m)
        e = e.astype(BF16)
        zero = jnp.zeros((), BF16)
        e = jnp.concatenate([jnp.where(upper, e, zero), jnp.where(upper, zero, e)], axis=-1)
        return e.reshape(SWA_Q_HEADS * WINDOW, 2 * WINDOW), den

    windows = range(q_ref.shape[0] // WINDOW)
    rows = lambda w: slice(w * WINDOW, (w + 1) * WINDOW)
    keys = lambda w: slice(w * WINDOW, (w + 2) * WINDOW)
    scores = [_swa_scores(q_ref[rows(w), :], kall[keys(w)]) for w in windows]
    probs = [banded_probs(scores[w], True if w > 0 else j > 0) for w in windows]
    for w in windows:
        for c, col in enumerate(_swa_out(*probs[w], vall[keys(w)])):
            o_ref[rows(w), c * LANES:(c + 1) * LANES] = col.astype(BF16)


def _swa_prompt(sinks, layer, q, k, v, batch, seq):
    qb = min(SWA_BLOCK, seq)
    nb = seq // qb
    per = qb // WINDOW
    cur = lambda b, j: (b * nb + j, 0)
    prev = lambda b, j: ((b * nb + j) * per - jnp.minimum(j, 1), 0)
    return pl.pallas_call(
        functools.partial(_swa_prompt_kernel, layer=layer),
        grid=(batch, nb),
        in_specs=[
            pl.BlockSpec(memory_space=pltpu.SMEM),
            pl.BlockSpec((qb, SWA_Q_W), cur),
            pl.BlockSpec((WINDOW, SWA_KV_W), prev),
            pl.BlockSpec((qb, SWA_KV_W), cur),
            pl.BlockSpec((WINDOW, SWA_KV_W), prev),
            pl.BlockSpec((qb, SWA_KV_W), cur),
        ],
        out_specs=pl.BlockSpec((qb, SWA_Q_W), cur),
        out_shape=jax.ShapeDtypeStruct((batch * seq, SWA_Q_W), BF16),
        compiler_params=_params("parallel", "parallel"),
        name="swa_prompt",
    )(sinks, q, k, k, v, v)


def _layer_view(ref, layer, first):
    if not first:
        return ref
    for other in range(ref.shape[0]):
        if other != layer:
            ref[other] = jnp.zeros(ref.shape[1:], ref.dtype)
    return ref.at[layer]


def _swa_decode_kernel(sink_ref, q_ref, kn_ref, vn_ref, knt_ref, vnt_ref, kc_ref, vc_ref, *rest, steps, layer):
    o_ref, ko_ref, vo_ref = rest[-3:]
    first = len(rest) == 3
    ko_ref = _layer_view(ko_ref, layer, first)
    vo_ref = _layer_view(vo_ref, layer, first)
    n_seq = kc_ref.shape[0]
    grp = DEC_GROUP
    rows = grp * steps
    keep = WINDOW - steps
    lane = lax.broadcasted_iota(jnp.int32, (1, WINDOW), 1)
    for s in range(n_seq):
        tile = slice((s * steps) // LANES * LANES, (s * steps) // LANES * LANES + LANES)
        shift = (keep - s * steps) % LANES
        for cache_ref, new_ref, out_ref in ((kc_ref, knt_ref, ko_ref), (vc_ref, vnt_ref, vo_ref)):
            out_ref[s] = jnp.where(lane >= keep, pltpu.roll(new_ref[:, tile], shift, axis=1),
                                   pltpu.roll(cache_ref[s], keep, axis=1))
    nk = grp * WINDOW + rows
    r = lax.broadcasted_iota(jnp.int32, (rows, nk), 0)
    q_seq, q_step = r // steps, r % steps
    c = lax.broadcasted_iota(jnp.int32, (rows, nk), 1)
    is_new = c >= grp * WINDOW
    cn = c - grp * WINDOW
    k_seq = jnp.where(is_new, cn // steps, c // WINDOW)
    k_idx = jnp.where(is_new, WINDOW + cn % steps, c % WINDOW)
    rel = WINDOW + q_step - k_idx
    valid = (q_seq == k_seq) & (rel >= 0) & (rel < WINDOW)
    sinks = _sink_column(sink_ref, layer)
    groups = range(n_seq // grp)
    new_rows = lambda g: slice(g * rows, (g + 1) * rows)
    cached = lambda ref, g: jnp.concatenate([ref[g * grp + s] for s in range(grp)], axis=1).astype(BF16)
    scores = []
    for g in groups:
        qm = _swa_head_rows(q_ref[new_rows(g), :])
        scores.append(jnp.concatenate(
            [_dot(qm, cached(kc_ref, g)), _dot_tb(qm, kn_ref[new_rows(g), :].astype(BF16))], axis=1))
    probs = [_swa_probs(scores[g], valid, sinks) for g in groups]
    for g in groups:
        e, den = probs[g]
        o = (_dot_tb(e[:, :grp * WINDOW], cached(vc_ref, g))
             + _dot(e[:, grp * WINDOW:], vn_ref[new_rows(g), :].astype(BF16)))
        for c, col in enumerate(_swa_finish(o, den)):
            o_ref[new_rows(g), c * LANES:(c + 1) * LANES] = col.astype(BF16)


def _stacked_out(shape, block, layer, prev):
    tail = (0,) * (len(block) - 1)
    if prev is None:
        spec = pl.BlockSpec((shape[0],) + block, lambda i: (0, i) + tail)
        extra_inputs = []
    else:
        spec = pl.BlockSpec((None,) + block, lambda i: (layer, i) + tail)
        extra_inputs = list(prev)
    extra_specs = [pl.BlockSpec(memory_space=pl.ANY) for _ in extra_inputs]
    return extra_inputs, extra_specs, spec, jax.ShapeDtypeStruct(shape, F32)


def _swa_decode(sinks, layer, q, kn, vn, knt, vnt, kc, vc, steps, prev):
    depth, nseq = kc.shape[:2]
    grp = min(DEC_GROUP * SWA_DEC_SUBGROUPS, nseq)
    rows = grp * steps
    assert rows % LANES == 0, "a grid step's new keys must fill whole 128-lane tiles"
    row = lambda i: (i, 0)
    cache_spec = pl.BlockSpec((None, grp, SWA_KV_W, WINDOW), lambda i: (layer, i, 0, 0))
    new_t = pl.BlockSpec((SWA_KV_W, rows), lambda i: (0, i))
    extra_in, extra_specs, out_spec, stacked = _stacked_out(kc.shape, (grp, SWA_KV_W, WINDOW), layer, prev)
    n_in = 8
    return pl.pallas_call(
        functools.partial(_swa_decode_kernel, steps=steps, layer=layer),
        grid=(nseq // grp,),
        in_specs=[
            pl.BlockSpec(memory_space=pltpu.SMEM),
            pl.BlockSpec((rows, SWA_Q_W), row),
            pl.BlockSpec((rows, SWA_KV_W), row),
            pl.BlockSpec((rows, SWA_KV_W), row),
            new_t,
            new_t,
            cache_spec,
            cache_spec,
        ] + extra_specs,
        out_specs=[pl.BlockSpec((rows, SWA_Q_W), row), out_spec, out_spec],
        out_shape=[jax.ShapeDtypeStruct((nseq * steps, SWA_Q_W), BF16), stacked, stacked],
        input_output_aliases={n_in + i: 1 + i for i in range(len(extra_in))},
        compiler_params=_params("parallel"),
        name="swa_decode",
    )(sinks, q, kn, vn, knt, vnt, kc, vc, *extra_in)


def _gla_out(o, gain, gate):
    return _rms(o, gain) * (gate * jax.nn.sigmoid(gate))


def _head_stack(x, width):
    return jnp.concatenate([x[:, h * width:(h + 1) * width] for h in range(GLA_HEADS)], axis=0)


def _head_masked_stack(x, head_of_lane):
    zero = jnp.zeros((), x.dtype)
    return jnp.concatenate([jnp.where(head_of_lane == h, x, zero) for h in range(GLA_HEADS)], axis=0)


def _gla_prompt_kernel(q_ref, k_ref, la_ref, v_ref, gg_ref, gn_ref, tril_ref, o_ref, s_ref, st_scr):
    tb = pl.program_id(1)

    @pl.when(tb == 0)
    def _():
        st_scr[...] = jnp.zeros_like(st_scr)

    n_tok = q_ref.shape[1]
    c_len = min(GLA_CHUNK, n_tok)
    tril = tril_ref[...]
    head_of_lane = lax.broadcasted_iota(jnp.int32, (1, GLA_K_W), 1) // GLA_DK
    ri = lax.broadcasted_iota(jnp.int32, (GLA_HEADS * c_len, c_len), 0) % c_len
    ci = lax.broadcasted_iota(jnp.int32, (GLA_HEADS * c_len, c_len), 1)
    causal = ri >= ci
    gain = gn_ref[...]
    n_seq = q_ref.shape[0]
    items = [(i, c) for i in range(n_seq) for c in range(n_tok // c_len)]
    rows = lambda c: slice(c * c_len, (c + 1) * c_len)
    b_all = []
    for i in range(n_seq):
        g_hi, g_lo = _split_bf16(la_ref[i])
        b_all.append(_dot(tril, g_hi) + _dot(tril, g_lo))
    qm, kdm, dec, a_raw = {}, {}, {}, {}
    for it in items:
        i, c = it
        b = b_all[i][rows(c), :]
        dec[it] = jnp.exp(b[c_len - 1:c_len, :])
        q_t = q_ref[i, rows(c), :] * jnp.exp(b)
        k_t = k_ref[i, rows(c), :] * jnp.exp(-b)
        qm[it] = _head_masked_stack(q_t, head_of_lane).astype(BF16)
        kdm[it] = _head_masked_stack(k_t * dec[it], head_of_lane).astype(BF16)
        a_raw[it] = _dot_tb(qm[it], k_t.astype(BF16))
    upd = {it: _dot_ta(_head_stack(v_ref[it[0], rows(it[1]), :], GLA_DV), kdm[it]) for it in items}
    inter = {}
    for i in range(n_seq):
        st = st_scr[i]
        for c in range(n_tok // c_len):
            inter[(i, c)] = _dot_tb(qm[(i, c)], st.astype(BF16))
            st = dec[(i, c)] * st + upd[(i, c)]
        st_scr[i] = st
    for it in items:
        i, c = it
        a = jnp.where(causal, a_raw[it], 0.0).astype(BF16)
        intra = jnp.concatenate(
            [_dot(a[h * c_len:(h + 1) * c_len, :], v_ref[i, rows(c), h * GLA_DV:(h + 1) * GLA_DV])
             for h in range(GLA_HEADS)], axis=0)
        y = _gla_out(inter[it] + intra, gain, _head_stack(gg_ref[i, rows(c), :], GLA_DV)).astype(BF16)
        for h in range(GLA_HEADS):
            o_ref[i, rows(c), h * GLA_DV:(h + 1) * GLA_DV] = y[h * c_len:(h + 1) * c_len, :]

    @pl.when(tb == pl.num_programs(1) - 1)
    def _():
        for i in range(q_ref.shape[0]):
            s_ref[i] = st_scr[i].T


def _gla_prompt(layer, q, k, la, v, gg, gain, tril, batch, seq):
    tb = tril.shape[0]
    per = min(GLA_SEQS, batch)
    blk = lambda b, t: (b, t, 0)
    kw = pl.BlockSpec((per, tb, GLA_K_W), blk)
    vw = pl.BlockSpec((per, tb, GLA_V_W), blk)
    return pl.pallas_call(
        _gla_prompt_kernel,
        grid=(batch // per, seq // tb),
        in_specs=[kw, kw, kw, vw, vw, _layer_resident(gain, layer), _resident(tril)],
        out_specs=[vw, pl.BlockSpec((per, GLA_K_W, GLA_DV), lambda b, t: (b, 0, 0))],
        out_shape=[
            jax.ShapeDtypeStruct((batch, seq, GLA_V_W), BF16),
            jax.ShapeDtypeStruct((batch, GLA_K_W, GLA_DV), F32),
        ],
        scratch_shapes=[pltpu.VMEM((per, GLA_DV, GLA_K_W), F32)],
        compiler_params=_params("parallel", "arbitrary"),
        name="gla_prompt",
    )(q, k, la, v, gg, gain, tril)


def _gla_decode_kernel(q_ref, k_ref, la_ref, v_ref, gg_ref, gn_ref, s_ref, *rest, steps, layer):
    o_ref, so_ref = rest[-2:]
    so_ref = _layer_view(so_ref, layer, len(rest) == 2)
    grp = s_ref.shape[0]
    rows = grp * steps
    stacked = GLA_HEADS * rows
    ri = lax.broadcasted_iota(jnp.int32, (rows, rows), 0)
    ci = lax.broadcasted_iota(jnp.int32, (rows, rows), 1)
    same_seq = ri // steps == ci // steps
    g_hi, g_lo = _split_bf16(la_ref[...])
    tril = (same_seq & (ri >= ci)).astype(BF16)
    total = same_seq.astype(BF16)
    b = _dot(tril, g_hi) + _dot(tril, g_lo)
    b_last = _dot(total, g_hi) + _dot(total, g_lo)
    head_of_lane = lax.broadcasted_iota(jnp.int32, (1, GLA_K_W), 1) // GLA_DK
    k_t = k_ref[...] * jnp.exp(-b)
    qm = _head_masked_stack(q_ref[...] * jnp.exp(b), head_of_lane).astype(BF16)
    km = _head_masked_stack(k_t, head_of_lane).astype(BF16)
    kdm = _head_masked_stack(k_t * jnp.exp(b_last), head_of_lane).astype(BF16)
    v_st = _head_stack(v_ref[...], GLA_DV)
    seq_of_row = (lax.broadcasted_iota(jnp.int32, (stacked, 1), 0) % rows) // steps
    seq_of_g = lax.broadcasted_iota(jnp.int32, (rows, 1), 0) // steps
    zero = jnp.zeros((), BF16)
    rhs = jnp.concatenate([
        jnp.concatenate([v_st, jnp.zeros((stacked, GLA_DV), BF16)], axis=1),
        jnp.concatenate([jnp.zeros((2 * rows, GLA_DV), BF16), jnp.ones((2 * rows, GLA_DV), BF16)], axis=1),
    ], axis=0)
    a_raw = _dot_tb(qm, km)
    states = [s_ref[s].reshape(GLA_K_W, GLA_DV) for s in range(grp)]
    inter_all = [_dot(qm, st.astype(BF16)) for st in states]
    upd_all = []
    for s in range(grp):
        lhs = jnp.concatenate([jnp.where(seq_of_row == s, kdm, zero),
                               jnp.where(seq_of_g == s, g_hi, zero),
                               jnp.where(seq_of_g == s, g_lo, zero)], axis=0)
        upd_all.append(_dot_ta(lhs, rhs))
    rr = lax.broadcasted_iota(jnp.int32, (stacked, stacked), 0) % rows
    cc = lax.broadcasted_iota(jnp.int32, (stacked, stacked), 1) % rows
    causal = (rr // steps == cc // steps) & (rr >= cc)
    o = _dot(jnp.where(causal, a_raw, 0.0).astype(BF16), v_st)
    inter = inter_all[0]
    for s in range(1, grp):
        inter = jnp.where(seq_of_row == s, inter_all[s], inter)
    y = _gla_out(o + inter, gn_ref[...], _head_stack(gg_ref[...], GLA_DV)).astype(BF16)
    for h in range(GLA_HEADS):
        o_ref[:, h * GLA_DV:(h + 1) * GLA_DV] = y[h * rows:(h + 1) * rows, :]
    for s in range(grp):
        new = jnp.exp(upd_all[s][:, GLA_DV:]) * states[s] + upd_all[s][:, :GLA_DV]
        so_ref[s] = new.reshape(GLA_HEADS, GLA_DK, GLA_DV)


def _gla_decode(layer, q, k, la, v, gg, gain, state, steps, prev):
    nseq = state.shape[1]
    grp = min(GLA_DEC_GROUP, nseq)
    rows = grp * steps
    row = lambda i: (i, 0)
    st_spec = pl.BlockSpec((None, grp, GLA_HEADS, GLA_DK, GLA_DV), lambda i: (layer, i, 0, 0, 0))
    extra_in, extra_specs, out_spec, stacked = _stacked_out(
        state.shape, (grp, GLA_HEADS, GLA_DK, GLA_DV), layer, prev)
    n_in = 7
    return pl.pallas_call(
        functools.partial(_gla_decode_kernel, steps=steps, layer=layer),
        grid=(nseq // grp,),
        in_specs=[
            pl.BlockSpec((rows, GLA_K_W), row),
            pl.BlockSpec((rows, GLA_K_W), row),
            pl.BlockSpec((rows, GLA_K_W), row),
            pl.BlockSpec((rows, GLA_V_W), row),
            pl.BlockSpec((rows, GLA_V_W), row),
            _layer_resident(gain, layer),
            st_spec,
        ] + extra_specs,
        out_specs=[pl.BlockSpec((rows, GLA_V_W), row), out_spec],
        out_shape=[jax.ShapeDtypeStruct((nseq * steps, GLA_V_W), BF16), stacked],
        input_output_aliases={n_in + i: 1 + i for i in range(len(extra_in))},
        compiler_params=_params("parallel"),
        name="gla_decode",
    )(q, k, la, v, gg, gain, state, *extra_in)


def _outproj_apply(x, a_ref, o_ref, wa_ref, wo_ref, g_ref, wq_ref, qn_ref):
    x1 = x + _dot(a_ref[...], wa_ref[...]) + _dot(o_ref[...], wo_ref[...])
    h = _rms(x1, g_ref[...]).astype(BF16)
    qn = qn_ref[...]
    q = [_rms(_dot(h, wq_ref[:, hd * XA_HEAD_DIM:(hd + 1) * XA_HEAD_DIM]), qn).astype(BF16)
         for hd in range(XA_HEADS)]
    return x1, q


def _outproj_kernel(x_ref, a_ref, o_ref, wa_ref, wo_ref, g_ref, wq_ref, qn_ref, x1_ref, q_ref):
    x1, q = _outproj_apply(x_ref[...], a_ref, o_ref, wa_ref, wo_ref, g_ref, wq_ref, qn_ref)
    x1_ref[...] = x1
    for hd in range(XA_HEADS):
        q_ref[:, hd * XA_HEAD_DIM:(hd + 1) * XA_HEAD_DIM] = q[hd]


def _outproj(x, layer, a, o, w_a, w_o, g, wq, qn):
    n, d = x.shape
    tm = min(ROW_TILE, n)
    row = lambda i: (i, 0)
    return pl.pallas_call(
        _outproj_kernel,
        grid=(n // tm,),
        in_specs=[
            pl.BlockSpec((tm, d), row),
            pl.BlockSpec((tm, SWA_Q_W), row),
            pl.BlockSpec((tm, GLA_V_W), row),
        ] + [_layer_resident(p, layer) for p in (w_a, w_o, g, wq, qn)],
        out_specs=[pl.BlockSpec((tm, d), row), pl.BlockSpec((tm, XA_W), row)],
        out_shape=[jax.ShapeDtypeStruct((n, d), F32), jax.ShapeDtypeStruct((n, XA_W), BF16)],
        compiler_params=_params("parallel"),
        name="outproj",
    )(x, a, o, w_a, w_o, g, wq, qn)


def _memkv_kernel(*refs, n_cast):
    m_ref, g_ref, wk_ref, wv_ref, kn_ref = refs[:5]
    k_ref, v_ref = refs[5 + n_cast:7 + n_cast]
    _cast_chunks(refs[5:5 + n_cast], refs[7 + n_cast:])
    m = _rms(m_ref[...], g_ref[...]).astype(BF16)
    kn = kn_ref[...]
    for hd in range(XA_HEADS):
        sl = slice(hd * XA_HEAD_DIM, (hd + 1) * XA_HEAD_DIM)
        k_ref[:, sl] = _rms(_dot(m, wk_ref[:, sl]), kn)
    v_ref[...] = _dot(m, wv_ref[...])


def _memkv(mem, g, wk, wv, kn, cast=()):
    depth = wk.shape[0]
    n, d = mem.shape
    tm = min(ROW_TILE, n)
    per_layer = lambda l, i: (l, 0, 0)
    out = lambda l, i: (l, i, 0)
    tiles = n // tm
    cast_in, cast_out, cast_shape = _cast_specs(cast, depth * tiles, lambda l, i: l * tiles + i)
    return pl.pallas_call(
        functools.partial(_memkv_kernel, n_cast=len(cast)),
        grid=(depth, tiles),
        in_specs=[
            pl.BlockSpec((tm, d), lambda l, i: (i, 0)),
            pl.BlockSpec((None, 1, d), per_layer),
            pl.BlockSpec((None, d, XA_W), per_layer),
            pl.BlockSpec((None, d, XA_W), per_layer),
            pl.BlockSpec((None, 1, XA_HEAD_DIM), per_layer),
        ] + cast_in,
        out_specs=[pl.BlockSpec((None, tm, XA_W), out), pl.BlockSpec((None, tm, XA_W), out)] + cast_out,
        out_shape=[jax.ShapeDtypeStruct((depth, n, XA_W), F32)] * 2 + cast_shape,
        compiler_params=_params("parallel", "parallel"),
        name="memkv",
    )(mem, g, wk, wv, kn, *[c[0] for c in cast])


def _mixout_prompt_kernel(*refs, n_cast):
    (x_ref, a_ref, o_ref, wa_ref, wo_ref, g_ref, wq_ref, qn_ref, mk_ref, mv_ref, xwo_ref, fg_ref, wg_ref, wu_ref,
     wd_ref) = refs[:15]
    out_ref, att_scr, a_scr = refs[15 + n_cast], refs[-2], refs[-1]
    _cast_chunks(refs[15:15 + n_cast], refs[16 + n_cast:-2])
    x1, q = _outproj_apply(x_ref[...], a_ref, o_ref, wa_ref, wo_ref, g_ref, wq_ref, qn_ref)
    mk = mk_ref[...].astype(BF16)
    mv = mv_ref[...].astype(BF16)
    heads = range(XA_HEADS)
    cols = lambda hd: slice(hd * XA_HEAD_DIM, (hd + 1) * XA_HEAD_DIM)
    scores = [_dot_tb(q[hd], mk[:, cols(hd)]) * (XA_HEAD_DIM ** -0.5 * LOG2E) for hd in heads]
    probs = []
    for s in scores:
        e = jnp.exp2(s - jnp.max(s, axis=-1, keepdims=True))
        probs.append((e.astype(BF16), jnp.sum(e, axis=-1, keepdims=True)))
    for hd in heads:
        att_scr[:, cols(hd)] = (_dot(probs[hd][0], mv[:, cols(hd)]) / probs[hd][1]).astype(BF16)
    x2 = x1 + _dot(att_scr[...], xwo_ref[...])
    out_ref[...] = _ffn_apply(x2, fg_ref, wg_ref, wu_ref, wd_ref, a_scr)


def _mixout_prompt(x, layer, a, o, outproj, mk, mv, wo, ffn, seq, cast=()):
    n, d = x.shape
    dff = ffn[1].shape[2]
    tm = min(ROW_TILE, seq)
    per_seq = seq // tm
    row = lambda i: (i, 0)
    mem_spec = pl.BlockSpec((None, None, mk.shape[2], XA_W), lambda i: (layer, i // per_seq, 0, 0))
    cast_in, cast_out, cast_shape = _cast_specs(cast, n // tm)
    return pl.pallas_call(
        functools.partial(_mixout_prompt_kernel, n_cast=len(cast)),
        grid=(n // tm,),
        in_specs=([pl.BlockSpec((tm, d), row), pl.BlockSpec((tm, SWA_Q_W), row), pl.BlockSpec((tm, GLA_V_W), row)]
                  + [_layer_resident(p, layer) for p in outproj]
                  + [mem_spec, mem_spec]
                  + [_layer_resident(p, layer) for p in (wo,) + ffn]
                  + cast_in),
        out_specs=[pl.BlockSpec((tm, d), row)] + cast_out,
        out_shape=[jax.ShapeDtypeStruct((n, d), F32)] + cast_shape,
        scratch_shapes=[pltpu.VMEM((tm, XA_W), BF16), pltpu.VMEM((tm, dff), BF16)],
        compiler_params=_params("parallel"),
        name="mixout_prompt",
    )(x, a, o, *outproj, mk, mv, wo, *ffn, *[c[0] for c in cast])


def _xattn_decode_kernel(q_ref, mk_hbm, mv_hbm, o_ref, k_buf, v_buf, sem, *, steps, layer, n_steps):
    i = pl.program_id(0)
    grp, nkeys = k_buf.shape[1], k_buf.shape[2]

    def block_copies(step, slot):
        seqs = pl.ds(step * grp, grp)
        return (pltpu.make_async_copy(mk_hbm.at[layer, seqs], k_buf.at[slot], sem.at[0, slot]),
                pltpu.make_async_copy(mv_hbm.at[layer, seqs], v_buf.at[slot], sem.at[1, slot]))

    @pl.when(i == 0)
    def _():
        for step in range(min(XA_RING - 1, n_steps)):
            for copy in block_copies(step, step):
                copy.start()

    ahead = i + (XA_RING - 1)

    @pl.when(ahead < n_steps)
    def _():
        for copy in block_copies(ahead, ahead % XA_RING):
            copy.start()

    slot = i % XA_RING
    for copy in block_copies(i, slot):
        copy.wait()
    mk_ref, mv_ref = k_buf.at[slot], v_buf.at[slot]
    rows = grp * steps
    q = jnp.concatenate([q_ref[:, hd * XA_HEAD_DIM:(hd + 1) * XA_HEAD_DIM] for hd in range(XA_HEADS)], axis=0)
    r = lax.broadcasted_iota(jnp.int32, (XA_HEADS * rows, 1), 0)
    own = (r % rows) // steps
    same_head = (r // rows) == (lax.broadcasted_iota(jnp.int32, (1, nkeys), 1) % XA_HEADS)
    s = None
    for j in range(grp):
        sj = _dot_tb(q, mk_ref[j].astype(BF16))
        s = sj if s is None else jnp.where(own == j, sj, s)
    s = jnp.where(same_head, s * (XA_HEAD_DIM ** -0.5 * LOG2E), -jnp.inf)
    m = jnp.max(s, axis=-1, keepdims=True)
    e = jnp.exp2(s - m)
    p = e.astype(BF16)
    o = None
    for j in range(grp):
        oj = _dot(p, mv_ref[j].astype(BF16))
        o = oj if o is None else jnp.where(own == j, oj, o)
    o = o / jnp.sum(e, axis=-1, keepdims=True)
    for hd in range(XA_HEADS):
        o_ref[:, hd * XA_HEAD_DIM:(hd + 1) * XA_HEAD_DIM] = o[hd * rows:(hd + 1) * rows, :].astype(BF16)


def _xattn_decode(q, layer, mk, mv, steps):
    nseq, nkeys = mk.shape[1], mk.shape[2]
    grp = XA_DEC_GROUP
    rows = grp * steps
    row = lambda i: (i, 0)
    n_steps = nseq // grp
    ring = (XA_RING, grp, nkeys, XA_HEAD_DIM)
    return pl.pallas_call(
        functools.partial(_xattn_decode_kernel, steps=steps, layer=layer, n_steps=n_steps),
        grid=(n_steps,),
        in_specs=[pl.BlockSpec((rows, XA_W), row), pl.BlockSpec(memory_space=pl.ANY),
                  pl.BlockSpec(memory_space=pl.ANY)],
        out_specs=pl.BlockSpec((rows, XA_W), row),
        out_shape=jax.ShapeDtypeStruct((nseq * steps, XA_W), BF16),
        scratch_shapes=[pltpu.VMEM(ring, F32), pltpu.VMEM(ring, F32), pltpu.SemaphoreType.DMA((2, XA_RING))],
        compiler_params=_params("arbitrary"),
        name="xattn_decode",
    )(q, mk, mv)


def _proj_ffn_kernel(x_ref, a_ref, w_ref, g_ref, wg_ref, wu_ref, wd_ref, o_ref, a_scr):
    x = x_ref[...] + _dot(a_ref[...], w_ref[...])
    o_ref[...] = _ffn_apply(x, g_ref, wg_ref, wu_ref, wd_ref, a_scr)


def _proj_ffn(x, layer, a, w, ffn):
    n, d = x.shape
    dff = ffn[1].shape[2]
    tm = min(ROW_TILE, n)
    row = lambda i: (i, 0)
    return pl.pallas_call(
        _proj_ffn_kernel,
        grid=(n // tm,),
        in_specs=([pl.BlockSpec((tm, d), row), pl.BlockSpec((tm, a.shape[1]), row)]
                  + [_layer_resident(p, layer) for p in (w,) + ffn]),
        out_specs=pl.BlockSpec((tm, d), row),
        out_shape=jax.ShapeDtypeStruct((n, d), F32),
        scratch_shapes=[pltpu.VMEM((tm, dff), BF16)],
        compiler_params=_params("parallel"),
        name="proj_ffn",
    )(x, a, w, *ffn)


def _rope_tables(pos):
    half = HEAD_DIM // 2
    inv = ROPE_THETA ** (-jnp.arange(half, dtype=F32) / half)
    ang = pos.astype(F32)[:, None] * inv[None, :]
    cos, sin = jnp.cos(ang), jnp.sin(ang)
    reps = LANES // HEAD_DIM
    return jnp.tile(cos, (1, 2 * reps)), jnp.tile(jnp.concatenate([-sin, sin], axis=-1), (1, reps))


def _block_tril(n_blocks, size):
    i = jnp.arange(n_blocks * size)
    return ((i[:, None] // size == i[None, :] // size) & (i[:, None] >= i[None, :])).astype(BF16)


def _permute_heads(w, axis):
    blocks = jnp.split(w, SWA_Q_HEADS, axis=axis)
    return jnp.concatenate([blocks[h] for h in SWA_HEAD_ORDER], axis=axis)


def kernel(x_prompt, x_sample, cache_swa_k, cache_swa_v, state_gla, cache_mem_k, cache_mem_v, mem_prompt, ffn1_norm, ffn1_wg, ffn1_wu, ffn1_wd, mix_norm, w_in, swa_q_norm, swa_k_norm, swa_sinks, gla_w_gate, gla_b_gate, gla_out_norm, w_out, xa_norm, mem_norm, xa_wq, xa_wk, xa_wv, xa_q_norm, xa_k_norm, xa_wo, ffn2_norm, ffn2_wg, ffn2_wu, ffn2_wd):
    batch, seq, d = x_prompt.shape
    nseq, steps, _ = x_sample.shape
    depth = w_in.shape[0]
    mem_len = mem_prompt.shape[1]

    bf = lambda w: w.astype(BF16)
    vec = lambda p: p[:, None, :]
    first_cast = ([(w, 0) for w in (ffn1_wg, ffn1_wu, ffn1_wd)]
                  + [(w_in, l, w_in.shape[2] + MXU_TILE - GLA_LOWRANK) for l in range(depth)])
    gate_w = jnp.pad(bf(gla_w_gate), ((0, 0), (0, LANES - GLA_LOWRANK), (0, 0)))
    w_a = bf(_permute_heads(w_out[:, :SWA_Q_W], 1))
    w_o = bf(w_out[:, SWA_Q_W:])
    wq_b, wk_b, wv_b, wo_b = bf(xa_wq), bf(xa_wk), bf(xa_wv), bf(xa_wo)
    qn = vec(jnp.tile(swa_q_norm * (HEAD_DIM ** -0.5 * LOG2E), (1, LANES // HEAD_DIM)))
    kn = vec(jnp.tile(swa_k_norm, (1, LANES // HEAD_DIM)))
    inproj_params = lambda l: (vec(mix_norm), w_all[l], gate_w, vec(gla_b_gate), qn, kn)
    outproj_params = (w_a, w_o, vec(xa_norm), wq_b, vec(xa_q_norm))
    gla_gain = vec(gla_out_norm)

    lane = jnp.arange(MXU_TILE)
    bd = (lane[:, None] // HEAD_DIM == lane[None, :] // HEAD_DIM).astype(BF16)
    cos_p, sin_p = _rope_tables(jnp.arange(seq))
    cos_s, sin_s = _rope_tables(PAST_LEN + jnp.arange(nseq * steps) % steps)
    tril_p = _block_tril(min(GLA_BLOCK, seq) // min(GLA_CHUNK, seq), min(GLA_CHUNK, seq))

    mk_p, mv_p, *cast = _memkv(mem_prompt.reshape(batch * mem_len, d), vec(mem_norm), wk_b, wv_b,
                               vec(xa_k_norm), first_cast)
    ffn1_w = [cast[:3]] + [None] * (depth - 1)
    w_all = cast[3:]
    mk_p = mk_p.reshape(depth, batch, mem_len, XA_W)
    mv_p = mv_p.reshape(depth, batch, mem_len, XA_W)
    mk_s = cache_mem_k.reshape(depth, nseq, mem_len * XA_HEADS, XA_HEAD_DIM)
    mv_s = cache_mem_v.reshape(depth, nseq, mem_len * XA_HEADS, XA_HEAD_DIM)
    native = lambda c: jnp.transpose(c, (0, 1, 3, 4, 2)).reshape(depth, nseq, SWA_KV_W, WINDOW)
    kc_s, vc_s = native(cache_swa_k), native(cache_swa_v)

    xp = x_prompt.reshape(batch * seq, d)
    xs = x_sample.reshape(nseq * steps, d)
    kp_l, vp_l, sp_l = [], [], []
    swa_new = gla_new = None
    for l in range(depth):
        ffn1 = (vec(ffn1_norm),) + tuple(ffn1_w[l])
        xp, q_s, k_s, v_s, q_g, k_g, v_g, g_g, la, *ffn2_w = _ffn_inproj(
            xp, l, ffn1, inproj_params(l), cos_p, sin_p, bd, cast=[(w, l) for w in (ffn2_wg, ffn2_wu, ffn2_wd)])
        ffn2 = (vec(ffn2_norm),) + tuple(ffn2_w)
        a_p = _swa_prompt(swa_sinks, l, q_s, k_s, v_s, batch, seq)
        seqs = lambda a: a.reshape(batch, seq, a.shape[-1])
        o_p, s_p = _gla_prompt(l, seqs(q_g), seqs(k_g), seqs(la), seqs(v_g), seqs(g_g), gla_gain, tril_p,
                               batch, seq)
        o_p = o_p.reshape(batch * seq, GLA_V_W)
        last = lambda a: seqs(a)[:, seq - WINDOW:].reshape(batch, WINDOW, SWA_KV_HEADS, HEAD_DIM)
        kp_l.append(last(k_s))
        vp_l.append(last(v_s))
        sp_l.append(s_p.reshape(batch, GLA_HEADS, GLA_DK, GLA_DV))
        next_ffn1 = [(w, l + 1) for w in (ffn1_wg, ffn1_wu, ffn1_wd)] if l + 1 < depth else []
        xp, *cast = _mixout_prompt(xp, l, a_p, o_p, outproj_params, mk_p, mv_p, wo_b, ffn2, seq, cast=next_ffn1)
        if next_ffn1:
            ffn1_w[l + 1] = cast

        xs, q_s, k_s, v_s, q_g, k_g, v_g, g_g, la, k_t, v_t = _ffn_inproj(
            xs, l, ffn1, inproj_params(l), cos_s, sin_s, bd, transposed_kv=True)
        a_s, *swa_new = _swa_decode(swa_sinks, l, q_s, k_s, v_s, k_t, v_t, kc_s, vc_s, steps, swa_new)
        o_s, *gla_new = _gla_decode(l, q_g, k_g, la, v_g, g_g, gla_gain, state_gla, steps, gla_new)
        xs, q_x = _outproj(xs, l, a_s, o_s, *outproj_params)
        xs = _proj_ffn(xs, l, _xattn_decode(q_x, l, mk_s, mv_s, steps), wo_b, ffn2)

    unnative = lambda c: jnp.transpose(c.reshape(depth, nseq, SWA_KV_HEADS, HEAD_DIM, WINDOW), (0, 1, 4, 2, 3))
    return (xp.reshape(batch, seq, d), xs.reshape(nseq, steps, d),
            jnp.stack(kp_l), jnp.stack(vp_l), jnp.stack(sp_l),
            mk_p.reshape(depth, batch, mem_len, XA_HEADS, XA_HEAD_DIM),
            mv_p.reshape(depth, batch, mem_len, XA_HEADS, XA_HEAD_DIM),
            unnative(swa_new[0]), unnative(swa_new[1]), gla_new[0])
```
